```python
import math
import jax, jax.numpy as jnp
from jax import lax
import numpy as np

D_MODEL = 1024
BATCH = 8
SEQ = 2048
DEPTH = 2

PE_DIM = 256
EPS = 1e-6
NEG = -1e30
F32 = jnp.float32

CHUNK = 128
RET_HEADS = 4
RET_DK = 128
RET_DV = 256
ML_HEADS = 4
ML_DK = 128
ML_DV = 256
ML_CONV = 4
ML_F_BIAS_LO = 3.0
ML_F_BIAS_HI = 6.0
AB_SIZES = (RET_HEADS * RET_DK, RET_HEADS * RET_DK, RET_HEADS * RET_DV, RET_HEADS * RET_DV,
            ML_HEADS * ML_DK, ML_HEADS * ML_DK, ML_HEADS * ML_DV, ML_HEADS, ML_HEADS,
            ML_HEADS * ML_DV, ML_HEADS * ML_DV)
AB_IN = sum(AB_SIZES)
AB_MIX = RET_HEADS * RET_DV + ML_HEADS * ML_DV

NSA_HEADS = 16
NSA_GROUPS = 4
NSA_HPG = NSA_HEADS // NSA_GROUPS
NSA_DH = 64
CMP_STRIDE = 16
CMP_LEN = 2 * CMP_STRIDE
CMP_HIDDEN = 128
SEL_BLOCK = 64
SEL_TOPN = 4
WINDOW = 512
Q_BLOCK = 64
NSA_W = NSA_HEADS * NSA_DH
NSA_KV = NSA_GROUPS * NSA_DH
S5_GROUPS = 32
S5_GROUP_CH = 16
S5_STATE = 64
S5_WIDTH = S5_GROUPS * S5_GROUP_CH
CD_SIZES = (NSA_W, NSA_KV, NSA_KV, NSA_KV, NSA_KV, NSA_KV, NSA_KV, 3 * NSA_HEADS, NSA_W,
            S5_WIDTH, S5_WIDTH)
CD_IN = sum(CD_SIZES)
CD_MIX = NSA_W + S5_WIDTH

kernel_name = "hybrid_ret_mlstm_nsa_s5_sandwich"


def _split(a, sizes):
    offs = np.cumsum(sizes)[:-1].tolist()
    return jnp.split(a, offs, axis=-1)


def rmsnorm(x, g):
    xf = x.astype(F32)
    y = xf * lax.rsqrt(jnp.mean(xf * xf, -1, keepdims=True) + EPS)
    return (y * g.astype(F32)).astype(x.dtype)


def head_norm(y, g):
    yc = y - y.mean(-1, keepdims=True)
    yn = yc * lax.rsqrt(jnp.mean(yc * yc, -1, keepdims=True) + EPS)
    b, s, h, d = y.shape
    return yn.reshape(b, s, h * d) * g.astype(F32)


def causal_dwconv(x, w, b):
    k, c = w.shape
    out = lax.conv_general_dilated(x, w[:, None, :], window_strides=(1,), padding=[(k - 1, 0)],
                                   dimension_numbers=('NWC', 'WIO', 'NWC'), feature_group_count=c)
    return out + b


def alibi_slopes(n):
    return 2.0 ** (-8.0 * jnp.arange(1, n + 1, dtype=F32) / n)


def retention(q, k, v):
    b_, s_, h, dk = q.shape
    dv = v.shape[-1]
    L = CHUNK
    nc = s_ // L
    log_g = jnp.log1p(-(2.0 ** (-5.0 - jnp.arange(h, dtype=F32))))
    qc = (q * dk ** -0.5).reshape(b_, nc, L, h, dk)
    kc = k.reshape(b_, nc, L, h, dk)
    vc = v.reshape(b_, nc, L, h, dv)
    pos = jnp.arange(L, dtype=F32)
    diff = pos[:, None] - pos[None, :]
    decay = jnp.where(diff >= 0, jnp.exp(jnp.maximum(diff, 0.0)[None] * log_g[:, None, None]), 0.0)
    sc = jnp.einsum('bcihd,bcjhd->bchij', qc, kc) * decay
    y_intra = jnp.einsum('bchij,bcjhe->bcihe', sc, vc)
    zeta = jnp.exp((L - 1 - pos)[:, None] * log_g)
    kv = jnp.einsum('bcjhd,bcjhe->bchde', kc * zeta[:, :, None], vc)
    chunk_decay = jnp.exp(L * log_g)[:, None, None]

    def step(r, kv_c):
        return r * chunk_decay + kv_c, r

    _, r_prev = lax.scan(step, jnp.zeros((b_, h, dk, dv), F32), jnp.moveaxis(kv, 1, 0))
    r_prev = jnp.moveaxis(r_prev, 0, 1)
    xi = jnp.exp((pos + 1.0)[:, None] * log_g)
    y_cross = jnp.einsum('bcihd,bchde->bcihe', qc * xi[:, :, None], r_prev)
    return (y_intra + y_cross).reshape(b_, s_, h, dv)


def mlstm(q, k, v, i_pre, f_pre, o_pre):
    b_, s_, h, dk = q.shape
    dv = v.shape[-1]
    L = CHUNK
    nc = s_ // L
    qc = q.reshape(b_, nc, L, h, dk)
    kc = (k * dk ** -0.5).reshape(b_, nc, L, h, dk)
    vc = v.reshape(b_, nc, L, h, dv)
    logf = jax.nn.log_sigmoid(f_pre).reshape(b_, nc, L, h).transpose(0, 1, 3, 2)
    ig = i_pre.reshape(b_, nc, L, h).transpose(0, 1, 3, 2)
    bcum = jnp.cumsum(logf, axis=-1)
    b_last = bcum[..., -1]
    a = b_last[..., None] - bcum + ig
    mu = a.max(-1)
    w = jnp.exp(a - mu[..., None])
    kw = kc * w.transpose(0, 1, 3, 2)[..., None]
    kv = jnp.einsum('bcshd,bcshe->bchde', kw, vc)
    ksum = kw.sum(2)

    def step(carry, inp):
        c_st, n_st, m_st = carry
        kv_c, ks_c, mu_c, bl_c = inp
        m_new = jnp.maximum(bl_c + m_st, mu_c)
        sp = jnp.exp(bl_c + m_st - m_new)
        sc = jnp.exp(mu_c - m_new)
        c_new = sp[..., None, None] * c_st + sc[..., None, None] * kv_c
        n_new = sp[..., None] * n_st + sc[..., None] * ks_c
        return (c_new, n_new, m_new), (c_st, n_st, m_st)

    init = (jnp.zeros((b_, h, dk, dv), F32), jnp.zeros((b_, h, dk), F32), jnp.full((b_, h), NEG, F32))
    _, (c_prev, n_prev, m_prev) = lax.scan(
        step, init, (jnp.moveaxis(kv, 1, 0), jnp.moveaxis(ksum, 1, 0), jnp.moveaxis(mu, 1, 0), jnp.moveaxis(b_last, 1, 0)))
    c_prev = jnp.moveaxis(c_prev, 0, 1)
    n_prev = jnp.moveaxis(n_prev, 0, 1)
    m_prev = jnp.moveaxis(m_prev, 0, 1)
    causal = jnp.tril(jnp.ones((L, L), bool))
    log_d = bcum[..., :, None] - bcum[..., None, :] + ig[..., None, :]
    log_d = jnp.where(causal, log_d, -jnp.inf)
    inter = bcum + m_prev[..., None]
    m_t = jnp.maximum(inter, log_d.max(-1))
    dmat = jnp.exp(log_d - m_t[..., None])
    w_inter = jnp.exp(inter - m_t).transpose(0, 1, 3, 2)
    s = jnp.einsum('bcthd,bcshd->bchts', qc, kc) * dmat
    num = jnp.einsum('bchts,bcshe->bcthe', s, vc) + w_inter[..., None] * jnp.einsum('bcthd,bchde->bcthe', qc, c_prev)
    den = s.sum(-1).transpose(0, 1, 3, 2) + w_inter * jnp.einsum('bcthd,bchd->bcth', qc, n_prev)
    den = jnp.maximum(jnp.abs(den), jnp.exp(-m_t.transpose(0, 1, 3, 2)))
    hcell = (num / den[..., None]).reshape(b_, s_, h, dv)
    return jax.nn.sigmoid(o_pre) * hcell


def compress(kx, pos, w1, w2):
    b_, s_, g, dh = kx.shape
    ch = kx.reshape(b_, s_ // CMP_STRIDE, CMP_STRIDE, g, dh)
    blocks = jnp.concatenate([ch[:, :-1], ch[:, 1:]], axis=2) + pos[None, None, :, None, :]
    nb = blocks.shape[1]
    flat = blocks.transpose(0, 1, 3, 2, 4).reshape(b_, nb, g, CMP_LEN * dh)
    return jax.nn.gelu(flat @ w1) @ w2


def nsa(q, k_cmp, v_cmp, k_slc, v_slc, k_win, v_win, gate_pre, pos_k, pos_v, w1_k, w2_k, w1_v, w2_v):
    b_, s_, h, dh = q.shape
    g, hpg = NSA_GROUPS, NSA_HPG
    t = jnp.arange(s_)
    slopes = alibi_slopes(h).reshape(g, hpg)
    qg = (q * dh ** -0.5).reshape(b_, s_, g, hpg, dh)
    kc = compress(k_cmp, pos_k, w1_k, w2_k)
    vc = compress(v_cmp, pos_v, w1_v, w2_v)
    nb = kc.shape[1]
    cstart = jnp.arange(nb) * CMP_STRIDE
    dist_c = (t[:, None] - (cstart + CMP_LEN - 1)[None, :]).astype(F32)
    s_c = jnp.einsum('btgkd,bjgd->bgktj', qg, kc) - slopes[:, :, None, None] * dist_c
    s_c = jnp.where(dist_c >= 0, s_c, NEG)
    p_c = jax.nn.softmax(s_c, axis=-1)
    p_c = jnp.where((t >= CMP_LEN - 1)[:, None], p_c, 0.0)
    o_c = jnp.einsum('bgktj,bjgd->btgkd', p_c, vc)
    nsel = s_ // SEL_BLOCK
    topn = min(SEL_TOPN, nsel)
    sel = jnp.arange(nsel)
    overlap = ((cstart[:, None] < (sel[None, :] + 1) * SEL_BLOCK)
               & (cstart[:, None] + CMP_LEN > sel[None, :] * SEL_BLOCK)).astype(F32)
    imp = jnp.einsum('bgktj,jn->bgtn', p_c, overlap)
    valid = sel[None, :] * SEL_BLOCK <= t[:, None]
    forced = (sel[None, :] == 0) | (sel[None, :] == (t // SEL_BLOCK)[:, None])
    score = jnp.where(forced, jnp.inf, jnp.where(valid, imp, -1.0))
    _, idx = lax.top_k(score, topn)
    ks_blk = k_slc.reshape(b_, nsel, SEL_BLOCK, g, dh).transpose(0, 3, 1, 2, 4)
    vs_blk = v_slc.reshape(b_, nsel, SEL_BLOCK, g, dh).transpose(0, 3, 1, 2, 4)
    kw_pad = jnp.pad(k_win, ((0, 0), (WINDOW, 0), (0, 0), (0, 0)))
    vw_pad = jnp.pad(v_win, ((0, 0), (WINDOW, 0), (0, 0), (0, 0)))
    nq = s_ // Q_BLOCK
    q_blocks = qg.reshape(b_, nq, Q_BLOCK, g, hpg, dh).transpose(1, 0, 2, 3, 4, 5)
    idx_blocks = idx.reshape(b_, g, nq, Q_BLOCK, topn).transpose(2, 0, 1, 3, 4)
    bi = jnp.arange(b_)[:, None, None, None]
    gi = jnp.arange(g)[None, :, None, None]
    span = WINDOW + Q_BLOCK

    def block(args):
        qb, ib, nblk = args
        t0 = nblk * Q_BLOCK
        tq = t0 + jnp.arange(Q_BLOCK)
        k_sel = ks_blk[bi, gi, ib]
        v_sel = vs_blk[bi, gi, ib]
        kpos = ib[..., None] * SEL_BLOCK + jnp.arange(SEL_BLOCK)
        dist = (tq[None, None, :, None, None] - kpos).astype(F32)
        s = jnp.einsum('bqgkd,bgqnsd->bgkqns', qb, k_sel) - slopes[None, :, :, None, None, None] * dist[:, :, None]
        s = jnp.where(dist[:, :, None] >= 0, s, NEG)
        p = jax.nn.softmax(s.reshape(b_, g, hpg, Q_BLOCK, topn * SEL_BLOCK), axis=-1).reshape(s.shape)
        o_s = jnp.einsum('bgkqns,bgqnsd->bqgkd', p, v_sel)
        kwb = lax.dynamic_slice_in_dim(kw_pad, t0, span, axis=1)
        vwb = lax.dynamic_slice_in_dim(vw_pad, t0, span, axis=1)
        kpos_w = t0 - WINDOW + jnp.arange(span)
        dist_w = tq[:, None] - kpos_w[None, :]
        valid_w = (dist_w >= 0) & (dist_w < WINDOW) & (kpos_w[None, :] >= 0)
        s_w = jnp.einsum('bqgkd,bsgd->bgkqs', qb, kwb) - slopes[:, :, None, None] * dist_w.astype(F32)
        s_w = jnp.where(valid_w, s_w, NEG)
        p_w = jax.nn.softmax(s_w, axis=-1)
        o_w = jnp.einsum('bgkqs,bsgd->bqgkd', p_w, vwb)
        return o_s, o_w

    o_s, o_w = lax.map(block, (q_blocks, idx_blocks, jnp.arange(nq)))
    o_s = o_s.transpose(1, 0, 2, 3, 4, 5).reshape(b_, s_, g, hpg, dh)
    o_w = o_w.transpose(1, 0, 2, 3, 4, 5).reshape(b_, s_, g, hpg, dh)
    gt = jax.nn.sigmoid(gate_pre).reshape(b_, s_, g, hpg, 3)
    o = gt[..., 0:1] * o_c + gt[..., 1:2] * o_s + gt[..., 2:3] * o_w
    return o.reshape(b_, s_, h * dh)


def s5(u, a_re, a_im, log_dt, b_re, b_im, c_re, c_im, d_skip, w_glu):
    b_, s_, wdt = u.shape
    g, c, pst = S5_GROUPS, S5_GROUP_CH, S5_STATE
    dt = jnp.exp(log_dt)[:, None]
    er = jnp.exp(a_re * dt)
    abar_re = er * jnp.cos(a_im * dt)
    abar_im = er * jnp.sin(a_im * dt)
    lam2 = a_re * a_re + a_im * a_im
    cr = ((abar_re - 1.0) * a_re + abar_im * a_im) / lam2
    ci = (abar_im * a_re - (abar_re - 1.0) * a_im) / lam2
    bb_re = cr[..., None] * b_re - ci[..., None] * b_im
    bb_im = cr[..., None] * b_im + ci[..., None] * b_re
    ug = u.reshape(b_, s_, g, c)
    bu_re = jnp.einsum('bsgc,gpc->bsgp', ug, bb_re)
    bu_im = jnp.einsum('bsgc,gpc->bsgp', ug, bb_im)
    ar = jnp.broadcast_to(abar_re, (1, s_, g, pst))
    ai = jnp.broadcast_to(abar_im, (1, s_, g, pst))

    def combine(e1, e2):
        a1r, a1i, b1r, b1i = e1
        a2r, a2i, b2r, b2i = e2
        return (a2r * a1r - a2i * a1i, a2r * a1i + a2i * a1r,
                a2r * b1r - a2i * b1i + b2r, a2r * b1i + a2i * b1r + b2i)

    _, _, xr, xi = lax.associative_scan(combine, (ar, ai, bu_re, bu_im), axis=1)
    y = jnp.einsum('bsgp,gcp->bsgc', xr, c_re) - jnp.einsum('bsgp,gcp->bsgc', xi, c_im)
    y = jax.nn.gelu(y.reshape(b_, s_, wdt) + d_skip * u)
    z1, z2 = jnp.split(y @ w_glu, 2, axis=-1)
    return z1 * jax.nn.sigmoid(z2)


def ab_mixer(u, w_in, ret_norm, conv_w, conv_b, gate_b, ml_norm, w_out):
    b_, s_, _ = u.shape
    proj = (u @ w_in).astype(F32)
    rq, rk, rv, rg, mq, mk, mv, mi, mf, mo, mz = _split(proj, AB_SIZES)
    hd = lambda a, n: a.reshape(b_, s_, n, -1)
    y_ret = retention(hd(rq, RET_HEADS), hd(rk, RET_HEADS), hd(rv, RET_HEADS))
    y_ret = head_norm(y_ret, ret_norm) * jax.nn.silu(rg)
    mqk = jax.nn.silu(causal_dwconv(jnp.concatenate([mq, mk], -1), conv_w.astype(F32), conv_b.astype(F32)))
    mq, mk = jnp.split(mqk, 2, axis=-1)
    gb = gate_b.astype(F32)
    y_ml = mlstm(hd(mq, ML_HEADS), hd(mk, ML_HEADS), hd(mv, ML_HEADS),
                 mi + gb[:ML_HEADS], mf + gb[ML_HEADS:], hd(mo, ML_HEADS))
    y_ml = head_norm(y_ml, ml_norm) * jax.nn.silu(mz)
    mix = jnp.concatenate([y_ret, y_ml], -1).astype(u.dtype)
    return mix @ w_out


def cd_mixer(u, w_in, pos_k, pos_v, w1_k, w2_k, w1_v, w2_v, a_re, a_im, log_dt,
             b_re, b_im, c_re, c_im, d_skip, w_glu, w_out):
    b_, s_, _ = u.shape
    proj = (u @ w_in).astype(F32)
    nq, kc, vc, ks, vs, kw, vw, ng, nz, su, sz = _split(proj, CD_SIZES)
    kvh = lambda a: a.reshape(b_, s_, NSA_GROUPS, NSA_DH)
    y_nsa = nsa(nq.reshape(b_, s_, NSA_HEADS, NSA_DH), kvh(kc), kvh(vc), kvh(ks), kvh(vs), kvh(kw), kvh(vw), ng,
                pos_k.astype(F32), pos_v.astype(F32), w1_k.astype(F32), w2_k.astype(F32),
                w1_v.astype(F32), w2_v.astype(F32))
    y_nsa = y_nsa * jax.nn.silu(nz)
    y_s5 = s5(su, a_re.astype(F32), a_im.astype(F32), log_dt.astype(F32), b_re.astype(F32), b_im.astype(F32),
              c_re.astype(F32), c_im.astype(F32), d_skip.astype(F32), w_glu.astype(F32))
    y_s5 = y_s5 * jax.nn.silu(sz)
    mix = jnp.concatenate([y_nsa, y_s5], -1).astype(u.dtype)
    return mix @ w_out


def setup_inputs(seed: int = 0) -> dict:
    key = jax.random.key(seed)
    keys = list(jax.random.split(key, 40))
    n_even = (DEPTH + 1) // 2
    n_odd = DEPTH // 2

    def nrm(shape, scale):
        return jax.random.normal(keys.pop(), shape, F32) * scale

    def gain(shape):
        return 1.0 + nrm(shape, 0.02)

    d = D_MODEL
    inp = {}
    inp['x'] = nrm((BATCH, SEQ, d), 1.0)
    inp['p'] = nrm((DEPTH, BATCH, SEQ, PE_DIM), 1.0)
    inp['norm_pre'] = gain((DEPTH, d))
    inp['norm_post'] = gain((DEPTH, d))
    inp['pe_proj'] = nrm((DEPTH, PE_DIM, d), PE_DIM ** -0.5)
    inp['pe_gate'] = nrm((DEPTH, d, d), d ** -0.5)
    inp['ab_w_in'] = nrm((n_even, d, AB_IN), d ** -0.5)
    inp['ret_norm'] = gain((n_even, RET_HEADS * RET_DV))
    inp['ml_conv_w'] = nrm((n_even, ML_CONV, 2 * ML_HEADS * ML_DK), ML_CONV ** -0.5)
    inp['ml_conv_b'] = nrm((n_even, 2 * ML_HEADS * ML_DK), 0.02)
    f_bias = jnp.linspace(ML_F_BIAS_LO, ML_F_BIAS_HI, ML_HEADS, dtype=F32)
    inp['ml_gate_b'] = jnp.concatenate([nrm((n_even, ML_HEADS), 0.1),
                                        f_bias[None] + nrm((n_even, ML_HEADS), 0.1)], -1)
    inp['ml_norm'] = gain((n_even, ML_HEADS * ML_DV))
    inp['ab_w_out'] = nrm((n_even, AB_MIX, d), AB_MIX ** -0.5)
    inp['cd_w_in'] = nrm((n_odd, d, CD_IN), d ** -0.5)
    inp['cmp_pos_k'] = nrm((n_odd, CMP_LEN, NSA_DH), 0.02)
    inp['cmp_pos_v'] = nrm((n_odd, CMP_LEN, NSA_DH), 0.02)
    inp['cmp_w1_k'] = nrm((n_odd, CMP_LEN * NSA_DH, CMP_HIDDEN), (CMP_LEN * NSA_DH) ** -0.5)
    inp['cmp_w2_k'] = nrm((n_odd, CMP_HIDDEN, NSA_DH), CMP_HIDDEN ** -0.5)
    inp['cmp_w1_v'] = nrm((n_odd, CMP_LEN * NSA_DH, CMP_HIDDEN), (CMP_LEN * NSA_DH) ** -0.5)
    inp['cmp_w2_v'] = nrm((n_odd, CMP_HIDDEN, NSA_DH), CMP_HIDDEN ** -0.5)
    inp['s5_a_re'] = -0.5 + nrm((n_odd, S5_GROUPS, S5_STATE), 0.01)
    inp['s5_a_im'] = math.pi * jnp.arange(S5_STATE, dtype=F32) + nrm((n_odd, S5_GROUPS, S5_STATE), 0.01)
    inp['s5_log_dt'] = jax.random.uniform(keys.pop(), (n_odd, S5_GROUPS), F32, math.log(1e-3), math.log(1e-1))
    inp['s5_b_re'] = nrm((n_odd, S5_GROUPS, S5_STATE, S5_GROUP_CH), (2 * S5_GROUP_CH) ** -0.5)
    inp['s5_b_im'] = nrm((n_odd, S5_GROUPS, S5_STATE, S5_GROUP_CH), (2 * S5_GROUP_CH) ** -0.5)
    inp['s5_c_re'] = nrm((n_odd, S5_GROUPS, S5_GROUP_CH, S5_STATE), S5_STATE ** -0.5)
    inp['s5_c_im'] = nrm((n_odd, S5_GROUPS, S5_GROUP_CH, S5_STATE), S5_STATE ** -0.5)
    inp['s5_d'] = nrm((n_odd, S5_WIDTH), 1.0)
    inp['s5_w_glu'] = nrm((n_odd, S5_WIDTH, 2 * S5_WIDTH), S5_WIDTH ** -0.5)
    inp['cd_w_out'] = nrm((n_odd, CD_MIX, d), CD_MIX ** -0.5)
    return inp


def reference(x, p, norm_pre, norm_post, pe_proj, pe_gate,
              ab_w_in, ret_norm, ml_conv_w, ml_conv_b, ml_gate_b, ml_norm, ab_w_out,
              cd_w_in, cmp_pos_k, cmp_pos_v, cmp_w1_k, cmp_w2_k, cmp_w1_v, cmp_w2_v,
              s5_a_re, s5_a_im, s5_log_dt, s5_b_re, s5_b_im, s5_c_re, s5_c_im, s5_d, s5_w_glu, cd_w_out):
    h = x
    for i in range(DEPTH):
        u = rmsnorm(h, norm_pre[i])
        j = i // 2
        if i % 2 == 0:
            y = ab_mixer(u, ab_w_in[j], ret_norm[j], ml_conv_w[j], ml_conv_b[j], ml_gate_b[j], ml_norm[j], ab_w_out[j])
        else:
            y = cd_mixer(u, cd_w_in[j], cmp_pos_k[j], cmp_pos_v[j], cmp_w1_k[j], cmp_w2_k[j], cmp_w1_v[j], cmp_w2_v[j],
                         s5_a_re[j], s5_a_im[j], s5_log_dt[j], s5_b_re[j], s5_b_im[j], s5_c_re[j], s5_c_im[j],
                         s5_d[j], s5_w_glu[j], cd_w_out[j])
        h = h + rmsnorm(y.astype(h.dtype), norm_post[i])
        h = h + jax.nn.sigmoid(h @ pe_gate[i]) * (p[i] @ pe_proj[i])
    return h
```

```python
import functools
import math

import numpy as np
import jax
import jax.numpy as jnp
from jax import lax
from jax.experimental import pallas as pl
from jax.experimental.pallas import tpu as pltpu

F32 = jnp.float32
BF16 = jnp.bfloat16
EPS = 1e-6
NEG = -1e30
BIG = 1e30

LANES = 128
CHUNK = 128
RET_HEADS, RET_DK, RET_DV = 4, 128, 256
ML_HEADS, ML_DK, ML_DV = 4, 128, 256
ML_CONV = 4
NSA_HEADS, NSA_GROUPS, NSA_DH = 16, 4, 64
NSA_HPG = NSA_HEADS // NSA_GROUPS
CMP_STRIDE, CMP_LEN, CMP_HIDDEN = 16, 32, 128
SEL_BLOCK, SEL_TOPN, WINDOW = 64, 4, 512
S5_GROUPS, S5_GROUP_CH, S5_STATE = 32, 16, 64
S5_WIDTH = S5_GROUPS * S5_GROUP_CH
S5_NSTATE = S5_GROUPS * S5_STATE

TQ = 128
KCH = 512
SEL_LANE0 = 64
ALI_LANE0 = 96
VMEM_LIMIT = 56 * 1024 * 1024


def _cparams(sem):
    return pltpu.CompilerParams(dimension_semantics=sem, vmem_limit_bytes=VMEM_LIMIT)


def _dot(a, b):
    return jnp.dot(a, b, preferred_element_type=F32)


def _dot_nt(a, b):
    return lax.dot_general(a, b, (((1,), (1,)), ((), ())), preferred_element_type=F32)


def _split3(x):
    hi = x.astype(BF16)
    r1 = x - hi.astype(F32)
    mid = r1.astype(BF16)
    lo = (r1 - mid.astype(F32)).astype(BF16)
    return hi, mid, lo


def _dot_f32_lhs(x, m):
    hi, mid, lo = _split3(x)
    return _dot(hi, m) + _dot(mid, m) + _dot(lo, m)


def _sigmoid(x):
    return 1.0 / (1.0 + jnp.exp(-x))


def _silu(x):
    return x * _sigmoid(x)


def _gelu(x):
    return 0.5 * x * (1.0 + jnp.tanh(math.sqrt(2.0 / math.pi) * (x + 0.044715 * (x * x * x))))


def _rms(x, g):
    return x * lax.rsqrt(jnp.mean(x * x, -1, keepdims=True) + EPS) * g


def _norm_proj_kernel(h_ref, g_ref, wg_ref, wm_ref, om_ref, og_ref, u_ref):
    @pl.when(pl.program_id(1) == 0)
    def _():
        ub = _rms(h_ref[...], g_ref[...]).astype(BF16)
        u_ref[...] = ub
        og_ref[...] = _dot(ub, wg_ref[...])

    om_ref[...] = _dot(u_ref[...], wm_ref[...]).astype(om_ref.dtype)


def _norm_proj(h2, g, w_gate, w_main, tm=1024, tn=512):
    n, d = h2.shape
    nm = w_main.shape[1]
    return pl.pallas_call(
        _norm_proj_kernel,
        grid=(n // tm, nm // tn),
        in_specs=[pl.BlockSpec((tm, d), lambda i, j: (i, 0)),
                  pl.BlockSpec((1, d), lambda i, j: (0, 0)),
                  pl.BlockSpec((d, LANES), lambda i, j: (0, 0)),
                  pl.BlockSpec((d, tn), lambda i, j: (0, j))],
        out_specs=[pl.BlockSpec((tm, tn), lambda i, j: (i, j)),
                   pl.BlockSpec((tm, LANES), lambda i, j: (i, 0))],
        out_shape=[jax.ShapeDtypeStruct((n, nm), BF16), jax.ShapeDtypeStruct((n, LANES), F32)],
        scratch_shapes=[pltpu.VMEM((tm, d), BF16)],
        compiler_params=_cparams(("parallel", "arbitrary")),
        name="norm_proj",
    )(h2, g, w_gate, w_main)


def _norm_proj_t_kernel(h_ref, g_ref, w_ref, o_ref):
    ub = _rms(h_ref[...], g_ref[...]).astype(BF16)
    o_ref[...] = _dot(ub, w_ref[...]).astype(o_ref.dtype)


def _norm_proj_t(h2, g, w, bsz, seq, tm=1024):
    n, d = h2.shape
    wd = w.shape[1]
    tm = min(tm, seq)
    nt = seq // tm
    return pl.pallas_call(
        _norm_proj_t_kernel,
        grid=(bsz, nt),
        in_specs=[pl.BlockSpec((tm, d), lambda b, s: (b * nt + s, 0)),
                  pl.BlockSpec((1, d), lambda b, s: (0, 0)),
                  pl.BlockSpec((d, wd), lambda b, s: (0, 0))],
        out_specs=pl.BlockSpec((tm, wd), lambda b, s: (s, b)),
        out_shape=jax.ShapeDtypeStruct((seq, bsz * wd), BF16),
        compiler_params=_cparams(("parallel", "arbitrary")),
        name="norm_proj_t",
    )(h2, g, w)


def _out_pe_kernel(gated, *refs):
    if gated:
        ma_ref, mb_ref, sz_ref, h_ref, p_ref, wa_ref, wb_ref, gp_ref, wg_ref, wp_ref, o_ref = refs
        sz = sz_ref[...].astype(F32)
        mb = (mb_ref[...].astype(F32) * _silu(sz)).astype(BF16)
    else:
        ma_ref, mb_ref, h_ref, p_ref, wa_ref, wb_ref, gp_ref, wg_ref, wp_ref, o_ref = refs
        mb = mb_ref[...]
    y = _dot(ma_ref[...], wa_ref[...]) + _dot(mb, wb_ref[...])
    h1 = h_ref[...] + _rms(y, gp_ref[...])
    gate = _sigmoid(_dot(h1.astype(BF16), wg_ref[...]))
    pp = _dot(p_ref[...].astype(BF16), wp_ref[...])
    o_ref[...] = h1 + gate * pp


def _out_pe(mix_a, a_map, ka, mix_b, b_map, kb, sz, sz_map, h2, p2, w_a, w_b, g_post, w_gate, w_pe,
            bsz, seq, tm=512):
    n, d = h2.shape
    pe = p2.shape[1]
    nt = seq // tm
    row = lambda b, s: (b * nt + s, 0)
    const = lambda b, s: (0, 0)
    gated = sz is not None
    in_specs = [pl.BlockSpec((tm, ka), a_map), pl.BlockSpec((tm, kb), b_map)]
    args = [mix_a, mix_b]
    if gated:
        in_specs.append(pl.BlockSpec((tm, kb), sz_map))
        args.append(sz)
    in_specs += [pl.BlockSpec((tm, d), row), pl.BlockSpec((tm, pe), row),
                 pl.BlockSpec((ka, d), const), pl.BlockSpec((kb, d), const), pl.BlockSpec((1, d), const),
                 pl.BlockSpec((d, d), const), pl.BlockSpec((pe, d), const)]
    args += [h2, p2, w_a, w_b, g_post, w_gate, w_pe]
    return pl.pallas_call(
        functools.partial(_out_pe_kernel, gated),
        grid=(bsz, nt),
        in_specs=in_specs,
        out_specs=pl.BlockSpec((tm, d), row),
        out_shape=jax.ShapeDtypeStruct((n, d), F32),
        compiler_params=_cparams(("parallel", "arbitrary")),
        name="out_pe_gated" if gated else "out_pe",
    )(*args)


_RQ, _RK, _RV, _RG = 0, 512, 1024, 2048
_MQ, _MK, _MV, _MO, _MZ = 3072, 3584, 4096, 5120, 6144
AB_MAIN = 7168


def _head_norm(y, g):
    yc = y - jnp.mean(y, -1, keepdims=True)
    return yc * lax.rsqrt(jnp.mean(yc * yc, -1, keepdims=True) + EPS) * g


def _ab_kernel(pm_ref, pg_ref, cw_ref, cb_ref, gb_ref, rn_ref, mn_ref, mix_ref,
               ext_ref, r_ref, c_ref, n_ref, m_ref):
    L = CHUNK
    c_idx = pl.program_id(1)

    @pl.when(c_idx == 0)
    def _():
        ext_ref[0:8, :] = jnp.zeros((8, ext_ref.shape[1]), F32)
        r_ref[...] = jnp.zeros(r_ref.shape, F32)
        c_ref[...] = jnp.zeros(c_ref.shape, F32)
        n_ref[...] = jnp.zeros(n_ref.shape, F32)
        m_ref[...] = jnp.full(m_ref.shape, NEG, F32)

    row = lax.broadcasted_iota(jnp.int32, (L, L), 0)
    col = lax.broadcasted_iota(jnp.int32, (L, L), 1)
    causal = row >= col
    diff = (row - col).astype(F32)
    tcol = lax.broadcasted_iota(jnp.int32, (L, 1), 0).astype(F32)

    for h in range(RET_HEADS):
        lg = math.log1p(-(2.0 ** (-5.0 - h)))
        scale = RET_DK ** -0.5
        decay = jnp.where(causal, jnp.exp(jnp.maximum(diff, 0.0) * lg), 0.0) * scale
        q = pm_ref[:, _RQ + h * RET_DK:_RQ + (h + 1) * RET_DK]
        k = pm_ref[:, _RK + h * RET_DK:_RK + (h + 1) * RET_DK]
        v = pm_ref[:, _RV + h * RET_DV:_RV + (h + 1) * RET_DV]
        sc = _dot_nt(q, k) * decay
        xi = jnp.exp((tcol + 1.0) * lg) * scale
        qx = (q.astype(F32) * xi).astype(BF16)
        r_prev = r_ref[h]
        y = _dot(sc.astype(BF16), v) + _dot(qx, r_prev.astype(BF16))
        zeta = jnp.exp((L - 1.0 - tcol) * lg)
        kz_t = (k.astype(F32) * zeta).T.astype(BF16)
        r_ref[h] = r_prev * math.exp(L * lg) + _dot(kz_t, v)
        g = pm_ref[:, _RG + h * RET_DV:_RG + (h + 1) * RET_DV].astype(F32)
        out = _head_norm(y, rn_ref[:, h * RET_DV:(h + 1) * RET_DV]) * _silu(g)
        mix_ref[:, h * RET_DV:(h + 1) * RET_DV] = out.astype(mix_ref.dtype)

    nqk = 2 * ML_HEADS * ML_DK
    x = pm_ref[:, _MQ:_MQ + nqk].astype(F32)
    ext_ref[8:8 + L, :] = x
    acc = cb_ref[...] + cw_ref[ML_CONV - 1:ML_CONV, :] * x
    for j in range(ML_CONV - 1):
        acc = acc + cw_ref[j:j + 1, :] * ext_ref[pl.ds(8 - (ML_CONV - 1) + j, L), :]
    ext_ref[0:8, :] = x[L - 8:L, :]
    qk = _silu(acc)

    gt = (pg_ref[...] + gb_ref[...]).T
    g8 = gt[0:8, :]
    logf8 = jnp.minimum(g8, 0.0) - jnp.log(1.0 + jnp.exp(-jnp.abs(g8)))
    triu = jnp.where(row <= col, 1.0, 0.0).astype(BF16)
    bcum8 = _dot_f32_lhs(logf8, triu)
    b_t = pltpu.roll(bcum8, 4, axis=0)
    b_last = b_t[:, L - 1:L]
    a_t = b_last - b_t + g8
    mu = jnp.max(a_t, axis=1, keepdims=True)
    w_t = jnp.exp(a_t - mu)
    r_t = g8 - b_t
    zpad = jnp.zeros((L - 8, L), F32)
    bcol = jnp.concatenate([b_t, zpad], axis=0).T
    wcol = jnp.concatenate([w_t, zpad], axis=0).T
    m_prev = m_ref[...][:, 0:1]
    m_new = jnp.maximum(b_last + m_prev, mu)
    sp = jnp.exp(b_last + m_prev - m_new)
    scn = jnp.exp(mu - m_new)
    m_ref[...] = jnp.broadcast_to(m_new, m_ref.shape)

    for h in range(ML_HEADS):
        bc = bcol[:, h:h + 1]
        log_d = jnp.where(causal, bc + r_t[h:h + 1, :], NEG)
        inter = bc + m_prev[h:h + 1, :]
        m_t = jnp.maximum(inter, jnp.max(log_d, axis=1, keepdims=True))
        dmat = jnp.exp(log_d - m_t)
        w_int = jnp.exp(inter - m_t)
        qh = qk[:, h * ML_DK:(h + 1) * ML_DK]
        kh = qk[:, ML_HEADS * ML_DK + h * ML_DK:ML_HEADS * ML_DK + (h + 1) * ML_DK] * (ML_DK ** -0.5)
        qb = qh.astype(BF16)
        v = pm_ref[:, _MV + h * ML_DV:_MV + (h + 1) * ML_DV]
        s = _dot_nt(qb, kh.astype(BF16)) * dmat
        c_prev = c_ref[h]
        n_prev = n_ref[h:h + 1, :]
        num = _dot(s.astype(BF16), v) + w_int * _dot(qb, c_prev.astype(BF16))
        den = jnp.sum(s, axis=1, keepdims=True) + w_int * jnp.sum(qh * n_prev, axis=1, keepdims=True)
        den = jnp.maximum(jnp.abs(den), jnp.exp(-m_t))
        hcell = num * (1.0 / den)
        o = pm_ref[:, _MO + h * ML_DV:_MO + (h + 1) * ML_DV].astype(F32)
        y = _sigmoid(o) * hcell
        z = pm_ref[:, _MZ + h * ML_DV:_MZ + (h + 1) * ML_DV].astype(F32)
        out = _head_norm(y, mn_ref[:, h * ML_DV:(h + 1) * ML_DV]) * _silu(z)
        mix_ref[:, RET_HEADS * RET_DV + h * ML_DV:RET_HEADS * RET_DV + (h + 1) * ML_DV] = out.astype(mix_ref.dtype)
        kw = kh * wcol[:, h:h + 1]
        kv = _dot(kw.T.astype(BF16), v)
        ksum = jnp.sum(kw, axis=0, keepdims=True)
        sp_h = sp[h:h + 1, :]
        sc_h = scn[h:h + 1, :]
        c_ref[h] = sp_h * c_prev + sc_h * kv
        n_ref[h:h + 1, :] = sp_h * n_prev + sc_h * ksum


def _ab_mixer(pm, pg, conv_w, conv_b, gate_b, ret_norm, ml_norm, bsz, seq):
    n = pm.shape[0]
    nc = seq // CHUNK
    nqk = 2 * ML_HEADS * ML_DK
    row = lambda b, c: (b * nc + c, 0)
    const = lambda b, c: (0, 0)
    nmix = RET_HEADS * RET_DV + ML_HEADS * ML_DV
    return pl.pallas_call(
        _ab_kernel,
        grid=(bsz, nc),
        in_specs=[pl.BlockSpec((CHUNK, AB_MAIN), row), pl.BlockSpec((CHUNK, LANES), row),
                  pl.BlockSpec((ML_CONV, nqk), const), pl.BlockSpec((1, nqk), const),
                  pl.BlockSpec((1, LANES), const),
                  pl.BlockSpec((1, RET_HEADS * RET_DV), const), pl.BlockSpec((1, ML_HEADS * ML_DV), const)],
        out_specs=pl.BlockSpec((CHUNK, nmix), row),
        out_shape=jax.ShapeDtypeStruct((n, nmix), BF16),
        scratch_shapes=[pltpu.VMEM((CHUNK + 8, nqk), F32),
                        pltpu.VMEM((RET_HEADS, RET_DK, RET_DV), F32),
                        pltpu.VMEM((ML_HEADS, ML_DK, ML_DV), F32),
                        pltpu.VMEM((8, ML_DK), F32),
                        pltpu.VMEM((8, LANES), F32)],
        compiler_params=_cparams(("parallel", "arbitrary")),
        name="ab_mixer",
    )(pm, pg, conv_w, conv_b, gate_b, ret_norm, ml_norm)


_Q0 = 0
_KC0, _VC0, _KS0, _VS0, _KW0, _VW0 = 16, 20, 24, 28, 32, 36
_NZ0 = 40 * LANES
_SZ0 = _NZ0 + NSA_HEADS * NSA_DH
CD_MAIN = _SZ0 + S5_WIDTH


def _compress_kernel(kr_ref, vr_ref, pk_ref, pv_ref, w1k_ref, w1v_ref, w2k_ref, w2v_ref, ko_ref, vo_ref):
    nb = kr_ref.shape[0]
    slotw = NSA_GROUPS * LANES
    for x_ref, p_ref, w1_ref, w2_ref, o_ref in ((kr_ref, pk_ref, w1k_ref, w2k_ref, ko_ref),
                                                 (vr_ref, pv_ref, w1v_ref, w2v_ref, vo_ref)):
        for g in range(NSA_GROUPS):
            a = jnp.zeros((nb, CMP_HIDDEN), F32)
            b = jnp.zeros((nb, CMP_HIDDEN), F32)
            for r in range(CMP_STRIDE):
                xr = x_ref[:, r * slotw + g * LANES:r * slotw + (g + 1) * LANES].astype(F32)
                a = a + _dot((xr + p_ref[r:r + 1, :]).astype(BF16), w1_ref[r])
                b = b + _dot((xr + p_ref[CMP_STRIDE + r:CMP_STRIDE + r + 1, :]).astype(BF16),
                             w1_ref[CMP_STRIDE + r])
            h1 = a + pltpu.roll(b, nb - 1, axis=0)
            o_ref[g] = _dot(_gelu(h1).astype(BF16), w2_ref[...]).astype(o_ref.dtype)


def _compress(kr, vr, pos_k, pos_v, w1k, w1v, w2k, w2v):
    bsz, nb, wd = kr.shape
    c3 = lambda b: (0, 0, 0)
    c2 = lambda b: (0, 0)
    return pl.pallas_call(
        _compress_kernel,
        grid=(bsz,),
        in_specs=[pl.BlockSpec((None, nb, wd), lambda b: (b, 0, 0)),
                  pl.BlockSpec((None, nb, wd), lambda b: (b, 0, 0)),
                  pl.BlockSpec((CMP_LEN, LANES), c2), pl.BlockSpec((CMP_LEN, LANES), c2),
                  pl.BlockSpec((CMP_LEN, LANES, CMP_HIDDEN), c3), pl.BlockSpec((CMP_LEN, LANES, CMP_HIDDEN), c3),
                  pl.BlockSpec((CMP_HIDDEN, LANES), c2), pl.BlockSpec((CMP_HIDDEN, LANES), c2)],
        out_specs=[pl.BlockSpec((None, NSA_GROUPS, nb, LANES), lambda b: (b, 0, 0, 0)),
                   pl.BlockSpec((None, NSA_GROUPS, nb, LANES), lambda b: (b, 0, 0, 0))],
        out_shape=[jax.ShapeDtypeStruct((bsz, NSA_GROUPS, nb, LANES), BF16)] * 2,
        compiler_params=_cparams(("parallel",)),
        name="nsa_compress",
    )(kr, vr, pos_k, pos_v, w1k, w1v, w2k, w2v)


def _nsa_kernel(q_ref, kc_ref, vc_ref, ks_ref, vs_ref, kw_ref, vw_ref, nz_ref, pg_ref,
                ek_ref, ev_ref, al_ref, sl_ref, ov_ref, y_ref,
                ksa_ref, vsa_ref, kwa_ref, vwa_ref):
    seq = ks_ref.shape[0]
    g_idx = pl.program_id(1)
    qi = pl.program_id(2)
    t0 = qi * TQ
    hp = NSA_HPG

    @pl.when(qi == 0)
    def _():
        ek = ek_ref[...]
        ev = ev_ref[...]
        ksa_ref[...] = ks_ref[...] + ek
        vsa_ref[...] = vs_ref[...] + ev
        zeros = jnp.zeros((WINDOW, LANES), BF16)
        lane_k = lax.broadcasted_iota(jnp.int32, (seq, LANES), 1)
        ek_w = jnp.where(lane_k >= ALI_LANE0, ek, jnp.zeros_like(ek))
        kwa_ref[0:WINDOW, :] = zeros
        vwa_ref[0:WINDOW, :] = zeros
        kwa_ref[WINDOW:WINDOW + seq, :] = kw_ref[...] + ek_w
        vwa_ref[WINDOW:WINDOW + seq, :] = vw_ref[...] + ev

    lane = lax.broadcasted_iota(jnp.int32, (TQ, LANES), 1)
    rowt = t0 + lax.broadcasted_iota(jnp.int32, (TQ, LANES), 0)

    dist_c = (rowt - (lane * CMP_STRIDE + (CMP_LEN - 1))).astype(F32)
    kc = kc_ref[...]
    vc = vc_ref[...]
    psum = jnp.zeros((TQ, LANES), F32)
    o_c = []
    for hh in range(hp):
        qh = q_ref[:, hh * LANES:(hh + 1) * LANES]
        slope = sl_ref[pl.ds(g_idx * hp + hh, 1), :]
        s = _dot_nt(qh, kc) - slope * dist_c
        s = jnp.where(dist_c >= 0.0, s, NEG)
        e = jnp.exp(s - jnp.max(s, axis=1, keepdims=True))
        p = e * (1.0 / jnp.sum(e, axis=1, keepdims=True))
        p = jnp.where(rowt >= CMP_LEN - 1, p, 0.0)
        o_c.append(_dot(p.astype(BF16), vc))
        psum = psum + p
    p_hi = psum.astype(BF16)
    p_lo = (psum - p_hi.astype(F32)).astype(BF16)
    imp = _dot(p_hi, ov_ref[...]) + _dot(p_lo, ov_ref[...])

    nblk = lane - SEL_LANE0
    in_sel = (lane >= SEL_LANE0) & (lane < SEL_LANE0 + seq // SEL_BLOCK)
    own = rowt >> int(math.log2(SEL_BLOCK))
    valid = nblk <= own
    forced = (nblk == 0) | (nblk == own)
    score = jnp.where(in_sel, jnp.where(forced, BIG, jnp.where(valid, imp, -1.0)), -2.0)
    chosen = jnp.zeros((TQ, LANES), F32)
    lane_f = lane.astype(F32)
    for _ in range(SEL_TOPN):
        mx = jnp.max(score, axis=1, keepdims=True)
        first = jnp.min(jnp.where(score == mx, lane_f, 4.0 * LANES), axis=1, keepdims=True)
        hit = lane_f == first
        chosen = jnp.where(hit, 1.0, chosen)
        score = jnp.where(hit, -3.0, score)
    keep = (chosen > 0.0) & valid
    maskcols = jnp.where(in_sel, jnp.where(keep, 0.0, NEG), 0.0)

    qa = []
    for hh in range(hp):
        qh = q_ref[:, hh * LANES:(hh + 1) * LANES].astype(F32)
        al = al_ref[pl.ds(g_idx * hp + hh, 1), :]
        qa.append((qh + maskcols + al).astype(BF16))
    qa = jnp.concatenate(qa, axis=0)
    rows = hp * TQ
    rr = lax.broadcasted_iota(jnp.int32, (rows, 1), 0) & (TQ - 1)

    dsel = lax.broadcasted_iota(jnp.int32, (rows, KCH), 1) - rr

    def sel_body(kc_i, carry):
        m, acc = carry
        k0 = pl.multiple_of(kc_i * KCH, KCH)
        s = _dot_nt(qa, ksa_ref[pl.ds(k0, KCH), :])
        s = jnp.where(dsel <= t0 - k0, s, NEG)
        m_new = jnp.maximum(m, jnp.max(s, axis=1, keepdims=True))
        alpha = jnp.exp(m - m_new)
        p = jnp.exp(s - m_new)
        acc = alpha * acc + _dot(p.astype(BF16), vsa_ref[pl.ds(k0, KCH), :])
        return m_new, acc

    nk = (t0 + TQ + KCH - 1) // KCH
    _, acc_s = lax.fori_loop(0, nk, sel_body,
                             (jnp.full((rows, 1), NEG, F32), jnp.zeros((rows, LANES), F32)))

    span = WINDOW + TQ
    tw = pl.multiple_of(t0, TQ)
    dwin = lax.broadcasted_iota(jnp.int32, (rows, span), 1) - rr
    colw = lax.broadcasted_iota(jnp.int32, (rows, span), 1)
    s = _dot_nt(qa, kwa_ref[pl.ds(tw, span), :])
    ok = (dwin > 0) & (dwin <= WINDOW) & (colw >= WINDOW - t0)
    s = jnp.where(ok, s, NEG)
    p = jnp.exp(s - jnp.max(s, axis=1, keepdims=True))
    acc_w = _dot(p.astype(BF16), vwa_ref[pl.ds(tw, span), :])

    gt = _sigmoid(pg_ref[...])
    outs = []
    for hh in range(hp):
        head = g_idx * hp + hh
        gsel = [jnp.sum(jnp.where(lane == j * NSA_HEADS + head, gt, 0.0), axis=1, keepdims=True)
                for j in range(3)]
        a_s = acc_s[hh * TQ:(hh + 1) * TQ, :]
        a_w = acc_w[hh * TQ:(hh + 1) * TQ, :]
        o = (gsel[0] * o_c[hh]
             + (gsel[1] * (1.0 / a_s[:, NSA_DH:NSA_DH + 1])) * a_s
             + (gsel[2] * (1.0 / a_w[:, NSA_DH:NSA_DH + 1])) * a_w)
        outs.append(o)
    for pr in range(hp // 2):
        packed = jnp.where(lane < NSA_DH, outs[2 * pr], pltpu.roll(outs[2 * pr + 1], NSA_DH, axis=1))
        z = nz_ref[:, pr * LANES:(pr + 1) * LANES].astype(F32)
        y_ref[:, pr * LANES:(pr + 1) * LANES] = (packed * _silu(z)).astype(y_ref.dtype)


def _nsa(pm3, pg3, kcc, vcc, ek, ev, al, sl, ov):
    bsz, seq, _ = pm3.shape
    nq = seq // TQ
    gw = NSA_HPG * NSA_DH
    kv = lambda slot0: pl.BlockSpec((None, seq, LANES), lambda b, g, q: (b, 0, slot0 + g))
    c2 = lambda b, g, q: (0, 0)
    return pl.pallas_call(
        _nsa_kernel,
        grid=(bsz, NSA_GROUPS, nq),
        in_specs=[pl.BlockSpec((None, TQ, NSA_HPG * LANES), lambda b, g, q: (b, q, g)),
                  pl.BlockSpec((None, None, LANES, LANES), lambda b, g, q: (b, g, 0, 0)),
                  pl.BlockSpec((None, None, LANES, LANES), lambda b, g, q: (b, g, 0, 0)),
                  kv(_KS0), kv(_VS0), kv(_KW0), kv(_VW0),
                  pl.BlockSpec((None, TQ, gw), lambda b, g, q: (b, q, _NZ0 // gw + g)),
                  pl.BlockSpec((None, TQ, LANES), lambda b, g, q: (b, q, 0)),
                  pl.BlockSpec((seq, LANES), c2), pl.BlockSpec((1, LANES), c2),
                  pl.BlockSpec((NSA_HEADS, LANES), c2), pl.BlockSpec((NSA_HEADS, LANES), c2),
                  pl.BlockSpec((LANES, LANES), c2)],
        out_specs=pl.BlockSpec((None, TQ, gw), lambda b, g, q: (b, q, g)),
        out_shape=jax.ShapeDtypeStruct((bsz, seq, NSA_HEADS * NSA_DH), BF16),
        scratch_shapes=[pltpu.VMEM((seq, LANES), BF16), pltpu.VMEM((seq, LANES), BF16),
                        pltpu.VMEM((seq + WINDOW, LANES), BF16), pltpu.VMEM((seq + WINDOW, LANES), BF16)],
        compiler_params=_cparams(("parallel", "arbitrary", "arbitrary")),
        name="nsa_attn",
    )(pm3, kcc, vcc, pm3, pm3, pm3, pm3, pm3, pg3, ek, ev, al, sl, ov)


def _nsa_tables(seq):
    t = np.arange(seq)
    ek = np.zeros((seq, LANES), np.float32)
    ek[t, SEL_LANE0 + t // SEL_BLOCK] = 1.0
    for c in range(3):
        ek[:, ALI_LANE0 + c] = t // SEL_BLOCK
        ek[:, ALI_LANE0 + 3 + c] = t % SEL_BLOCK
    ev = np.zeros((1, LANES), np.float32)
    ev[0, NSA_DH] = 1.0
    slopes = jnp.exp2(-8.0 * jnp.arange(1, NSA_HEADS + 1, dtype=F32) / NSA_HEADS)
    s_hi, s_mid, s_lo = _split3(slopes)
    parts = [p.astype(F32) for p in (s_hi, s_mid, s_lo)]
    al = jnp.zeros((NSA_HEADS, LANES), F32)
    for c in range(3):
        al = al.at[:, ALI_LANE0 + c].set(parts[c] * SEL_BLOCK)
        al = al.at[:, ALI_LANE0 + 3 + c].set(parts[c])
    sl = jnp.broadcast_to(slopes[:, None], (NSA_HEADS, LANES))
    nb = seq // CMP_STRIDE
    cstart = np.arange(nb) * CMP_STRIDE
    sel = np.arange(seq // SEL_BLOCK)
    ov = np.zeros((max(nb, LANES), LANES), np.float32)
    ovl = (cstart[:, None] < (sel[None, :] + 1) * SEL_BLOCK) & (cstart[:, None] + CMP_LEN > sel[None, :] * SEL_BLOCK)
    ovl[nb - 1, :] = False
    ov[:nb, SEL_LANE0:SEL_LANE0 + len(sel)] = ovl
    return jnp.asarray(ek, BF16), jnp.asarray(ev, BF16), al, sl, jnp.asarray(ov, BF16)


def _s5_disc_kernel(are_ref, aim_ref, ldt_ref, bre_ref, bim_ref, abr_ref, abi_ref, bbr_ref, bbi_ref):
    a_re = are_ref[...]
    a_im = aim_ref[...]
    dt = jnp.exp(ldt_ref[...])
    er = jnp.exp(a_re * dt)
    abar_re = er * jnp.cos(a_im * dt)
    abar_im = er * jnp.sin(a_im * dt)
    lam2 = a_re * a_re + a_im * a_im
    cr = ((abar_re - 1.0) * a_re + abar_im * a_im) / lam2
    ci = (abar_im * a_re - (abar_re - 1.0) * a_im) / lam2
    b_re = bre_ref[...]
    b_im = bim_ref[...]
    abr_ref[...] = abar_re
    abi_ref[...] = abar_im
    bbr_ref[...] = cr * b_re - ci * b_im
    bbi_ref[...] = cr * b_im + ci * b_re


def _s5_disc(a_re_rep, a_im_rep, ldt_rep, b_re_t, b_im_t):
    shp = jax.ShapeDtypeStruct(a_re_rep.shape, F32)
    return pl.pallas_call(_s5_disc_kernel, out_shape=[shp] * 4, name="s5_disc")(
        a_re_rep, a_im_rep, ldt_rep, b_re_t, b_im_t)


S5_COLCH = 512


def _s5_kernel(u_ref, bre_ref, bim_ref, ar_ref, ai_ref, cre_ref, cim_ref, d_ref, wg_ref, o_ref,
               xr_ref, xi_ref, br_ref, bi_ref):
    tt = u_ref.shape[0] // 8

    @pl.when(pl.program_id(0) == 0)
    def _():
        xr_ref[...] = jnp.zeros(xr_ref.shape, F32)
        xi_ref[...] = jnp.zeros(xi_ref.shape, F32)

    u = u_ref[...]
    br_ref[...] = _dot(u, bre_ref[...])
    bi_ref[...] = _dot(u, bim_ref[...])

    for cc in range(S5_NSTATE // S5_COLCH):
        cols = slice(cc * S5_COLCH, (cc + 1) * S5_COLCH)
        ar = ar_ref[:, cols]
        ai = ai_ref[:, cols]

        def body(t, carry, cols=cols, ar=ar, ai=ai):
            xr, xi = carry
            rows = pl.ds(pl.multiple_of(t * 8, 8), 8)
            nxr = ar * xr - ai * xi + br_ref[rows, cols]
            nxi = ar * xi + ai * xr + bi_ref[rows, cols]
            br_ref[rows, cols] = nxr
            bi_ref[rows, cols] = nxi
            return nxr, nxi

        xr, xi = lax.fori_loop(0, tt, body, (xr_ref[:, cols], xi_ref[:, cols]), unroll=4)
        xr_ref[:, cols] = xr
        xi_ref[:, cols] = xi

    y = _dot(br_ref[...].astype(BF16), cre_ref[...]) - _dot(bi_ref[...].astype(BF16), cim_ref[...])
    y = _gelu(y + d_ref[...] * u.astype(F32))
    z = _dot(y.astype(BF16), wg_ref[...])
    o_ref[...] = (z[:, :S5_WIDTH] * _sigmoid(z[:, S5_WIDTH:])).astype(o_ref.dtype)


def _s5(u_t, bbd_re, bbd_im, ar8, ai8, cbd_re, cbd_im, d_row, w_glu, tt=64):
    rows_total = u_t.shape[0]
    nsteps = rows_total // (8 * tt)
    c2 = lambda i: (0, 0)
    blk = 8 * tt
    return pl.pallas_call(
        _s5_kernel,
        grid=(nsteps,),
        in_specs=[pl.BlockSpec((blk, S5_WIDTH), lambda i: (i, 0)),
                  pl.BlockSpec((S5_WIDTH, S5_NSTATE), c2), pl.BlockSpec((S5_WIDTH, S5_NSTATE), c2),
                  pl.BlockSpec((8, S5_NSTATE), c2), pl.BlockSpec((8, S5_NSTATE), c2),
                  pl.BlockSpec((S5_NSTATE, S5_WIDTH), c2), pl.BlockSpec((S5_NSTATE, S5_WIDTH), c2),
                  pl.BlockSpec((1, S5_WIDTH), c2), pl.BlockSpec((S5_WIDTH, 2 * S5_WIDTH), c2)],
        out_specs=pl.BlockSpec((blk, S5_WIDTH), lambda i: (i, 0)),
        out_shape=jax.ShapeDtypeStruct((rows_total, S5_WIDTH), BF16),
        scratch_shapes=[pltpu.VMEM((8, S5_NSTATE), F32), pltpu.VMEM((8, S5_NSTATE), F32),
                        pltpu.VMEM((blk, S5_NSTATE), F32), pltpu.VMEM((blk, S5_NSTATE), F32)],
        compiler_params=_cparams(("arbitrary",)),
        name="s5_scan",
    )(u_t, bbd_re, bbd_im, ar8, ai8, cbd_re, cbd_im, d_row, w_glu)


def _pad_cols(w, n):
    return jnp.pad(w, ((0, 0), (0, n - w.shape[1])))


def _slot_cols(w, width):
    d, n = w.shape
    k = n // width
    return jnp.pad(w.reshape(d, k, width), ((0, 0), (0, 0), (0, LANES - width))).reshape(d, k * LANES)


def _layer0(h2, p2, bsz, seq, norm_pre, norm_post, pe_proj, pe_gate,
            w_in, ret_norm, conv_w, conv_b, gate_b, ml_norm, w_out):
    d = h2.shape[1]
    g0 = 5120
    w_main = jnp.concatenate([w_in[:, :g0], w_in[:, g0 + 2 * ML_HEADS:]], axis=1).astype(BF16)
    w_gate = _pad_cols(w_in[:, g0:g0 + 2 * ML_HEADS], LANES).astype(BF16)
    pm, pg = _norm_proj(h2, norm_pre.reshape(1, d), w_gate, w_main)
    mix = _ab_mixer(pm, pg, conv_w, conv_b.reshape(1, -1), _pad_cols(gate_b.reshape(1, -1), LANES),
                    ret_norm.reshape(1, -1), ml_norm.reshape(1, -1), bsz, seq)
    ka = RET_HEADS * RET_DV
    nt = seq // 512
    return _out_pe(mix, lambda b, s: (b * nt + s, 0), ka, mix, lambda b, s: (b * nt + s, 1), ML_HEADS * ML_DV,
                   None, None, h2, p2, w_out[:ka].astype(BF16), w_out[ka:].astype(BF16),
                   norm_post.reshape(1, d), pe_gate.astype(BF16), pe_proj.astype(BF16), bsz, seq)


def _layer1(h2, p2, bsz, seq, norm_pre, norm_post, pe_proj, pe_gate,
            w_in, pos_k, pos_v, w1_k, w2_k, w1_v, w2_v,
            a_re, a_im, log_dt, b_re, b_im, c_re, c_im, d_skip, w_glu, w_out):
    d = h2.shape[1]
    nw, nkv = NSA_HEADS * NSA_DH, NSA_GROUPS * NSA_DH
    offs = np.cumsum((0, nw, nkv, nkv, nkv, nkv, nkv, nkv, 3 * NSA_HEADS, nw, S5_WIDTH, S5_WIDTH))
    seg = lambda i: w_in[:, offs[i]:offs[i + 1]]
    w_main = jnp.concatenate(
        [_slot_cols(seg(0) * (NSA_DH ** -0.5), NSA_DH)] + [_slot_cols(seg(i), NSA_DH) for i in range(1, 7)]
        + [seg(8), seg(10)], axis=1).astype(BF16)
    w_gate = seg(7).reshape(d, NSA_HEADS, 3).transpose(0, 2, 1).reshape(d, 3 * NSA_HEADS)
    w_gate = _pad_cols(w_gate, LANES).astype(BF16)
    g_pre = norm_pre.reshape(1, d)
    pm, pg = _norm_proj(h2, g_pre, w_gate, w_main)
    u_t = _norm_proj_t(h2, g_pre, seg(9).astype(BF16), bsz, seq)

    nb = seq // CMP_STRIDE
    pm3 = pm.reshape(bsz, seq, CD_MAIN)
    slotw = NSA_GROUPS * LANES
    kr = pm3[:, :, _KC0 * LANES:_KC0 * LANES + slotw].reshape(bsz, nb, CMP_STRIDE * slotw)
    vr = pm3[:, :, _VC0 * LANES:_VC0 * LANES + slotw].reshape(bsz, nb, CMP_STRIDE * slotw)
    padl = lambda a: jnp.pad(a, ((0, 0),) * (a.ndim - 1) + ((0, LANES - a.shape[-1]),))
    w1 = lambda w: jnp.pad(w.reshape(CMP_LEN, NSA_DH, CMP_HIDDEN), ((0, 0), (0, LANES - NSA_DH), (0, 0))).astype(BF16)
    kcc, vcc = _compress(kr, vr, padl(pos_k), padl(pos_v), w1(w1_k), w1(w1_v),
                         padl(w2_k).astype(BF16), padl(w2_v).astype(BF16))
    if nb < LANES:
        rpad = ((0, 0), (0, 0), (0, LANES - nb), (0, 0))
        kcc, vcc = jnp.pad(kcc, rpad), jnp.pad(vcc, rpad)
    ek, ev, al, sl, ov = _nsa_tables(seq)
    y_nsa = _nsa(pm3, pg.reshape(bsz, seq, LANES), kcc, vcc, ek, ev, al, sl, ov)

    rep = lambda a: jnp.repeat(a, S5_GROUP_CH, axis=0)
    ldt = jnp.broadcast_to(rep(log_dt[:, None]), (S5_WIDTH, S5_STATE))
    bt = lambda b: b.transpose(0, 2, 1).reshape(S5_WIDTH, S5_STATE)
    abr, abi, bbr, bbi = _s5_disc(rep(a_re), rep(a_im), ldt, bt(b_re), bt(b_im))
    eye = jnp.eye(S5_GROUPS, dtype=F32)
    bd_in = lambda bb: (bb.reshape(S5_GROUPS, S5_GROUP_CH, 1, S5_STATE)
                        * eye[:, None, :, None]).reshape(S5_WIDTH, S5_NSTATE).astype(BF16)
    bd_out = lambda c: (c.transpose(0, 2, 1).reshape(S5_GROUPS, S5_STATE, 1, S5_GROUP_CH)
                        * eye[:, None, :, None]).reshape(S5_NSTATE, S5_WIDTH).astype(BF16)
    row8 = lambda a: jnp.broadcast_to(a[::S5_GROUP_CH].reshape(1, S5_NSTATE), (8, S5_NSTATE))
    y_s5 = _s5(u_t.reshape(seq * bsz, S5_WIDTH), bd_in(bbr), bd_in(bbi), row8(abr), row8(abi),
               bd_out(c_re), bd_out(c_im), d_skip.reshape(1, -1), w_glu.astype(BF16))
    y_s5 = y_s5.reshape(seq, bsz * S5_WIDTH)

    nt = seq // 512
    szb = _SZ0 // S5_WIDTH
    return _out_pe(y_nsa.reshape(bsz * seq, nw), lambda b, s: (b * nt + s, 0), nw,
                   y_s5, lambda b, s: (s, b), S5_WIDTH,
                   pm, lambda b, s: (b * nt + s, szb),
                   h2, p2, w_out[:nw].astype(BF16), w_out[nw:].astype(BF16),
                   norm_post.reshape(1, d), pe_gate.astype(BF16), pe_proj.astype(BF16), bsz, seq)


def kernel(x, p, norm_pre, norm_post, pe_proj, pe_gate, ab_w_in, ret_norm, ml_conv_w, ml_conv_b, ml_gate_b,
           ml_norm, ab_w_out, cd_w_in, cmp_pos_k, cmp_pos_v, cmp_w1_k, cmp_w2_k, cmp_w1_v, cmp_w2_v,
           s5_a_re, s5_a_im, s5_log_dt, s5_b_re, s5_b_im, s5_c_re, s5_c_im, s5_d, s5_w_glu, cd_w_out):
    bsz, seq, d = x.shape
    assert bsz == 8, "the S5 scan maps the batch onto the 8 sublanes of a vreg"
    assert seq % max(KCH, 1024) == 0
    depth = p.shape[0]
    h2 = x.reshape(bsz * seq, d)
    for i in range(depth):
        j = i // 2
        p2 = p[i].reshape(bsz * seq, -1)
        if i % 2 == 0:
            h2 = _layer0(h2, p2, bsz, seq, norm_pre[i], norm_post[i], pe_proj[i], pe_gate[i],
                         ab_w_in[j], ret_norm[j], ml_conv_w[j], ml_conv_b[j], ml_gate_b[j], ml_norm[j], ab_w_out[j])
        else:
            h2 = _layer1(h2, p2, bsz, seq, norm_pre[i], norm_post[i], pe_proj[i], pe_gate[i],
                         cd_w_in[j], cmp_pos_k[j], cmp_pos_v[j], cmp_w1_k[j], cmp_w2_k[j], cmp_w1_v[j], cmp_w2_v[j],
                         s5_a_re[j], s5_a_im[j], s5_log_dt[j], s5_b_re[j], s5_b_im[j], s5_c_re[j], s5_c_im[j],
                         s5_d[j], s5_w_glu[j], cd_w_out[j])
    return h2.reshape(bsz, seq, d)
```

```python
import functools
import math

import numpy as np
import jax
import jax.numpy as jnp
from jax import lax
from jax.experimental import pallas as pl
from jax.experimental.pallas import tpu as pltpu

F32 = jnp.float32
BF16 = jnp.bfloat16
EPS = 1e-6
NEG = -1e30
BIG = 1e30

LANES = 128
CHUNK = 128
RET_HEADS, RET_DK, RET_DV = 4, 128, 256
ML_HEADS, ML_DK, ML_DV = 4, 128, 256
ML_CONV = 4
NSA_HEADS, NSA_GROUPS, NSA_DH = 16, 4, 64
NSA_HPG = NSA_HEADS // NSA_GROUPS
CMP_STRIDE, CMP_LEN, CMP_HIDDEN = 16, 32, 128
SEL_BLOCK, SEL_TOPN, WINDOW = 64, 4, 512
S5_GROUPS, S5_GROUP_CH, S5_STATE = 32, 16, 64
S5_WIDTH = S5_GROUPS * S5_GROUP_CH
S5_NSTATE = S5_GROUPS * S5_STATE

TQ = 128
KCH = 512
SEL_LANE0 = 64
ALI_LANE0 = 96
VMEM_LIMIT = 56 * 1024 * 1024


def _cparams(sem):
    return pltpu.CompilerParams(dimension_semantics=sem, vmem_limit_bytes=VMEM_LIMIT)


def _dot(a, b):
    return jnp.dot(a, b, preferred_element_type=F32)


def _dot_nt(a, b):
    return lax.dot_general(a, b, (((1,), (1,)), ((), ())), preferred_element_type=F32)


def _split3(x):
    hi = x.astype(BF16)
    r1 = x - hi.astype(F32)
    mid = r1.astype(BF16)
    lo = (r1 - mid.astype(F32)).astype(BF16)
    return hi, mid, lo


def _dot_f32_lhs(x, m):
    hi, mid, lo = _split3(x)
    return _dot(hi, m) + _dot(mid, m) + _dot(lo, m)


def _sigmoid(x):
    return 1.0 / (1.0 + jnp.exp(-x))


def _silu(x):
    return x * _sigmoid(x)


def _gelu(x):
    return 0.5 * x * (1.0 + jnp.tanh(math.sqrt(2.0 / math.pi) * (x + 0.044715 * (x * x * x))))


def _rms(x, g):
    return x * lax.rsqrt(jnp.mean(x * x, -1, keepdims=True) + EPS) * g


def _norm_proj_kernel(h_ref, g_ref, wg_ref, wm_ref, om_ref, og_ref, u_ref):
    @pl.when(pl.program_id(1) == 0)
    def _():
        ub = _rms(h_ref[...], g_ref[...]).astype(BF16)
        u_ref[...] = ub
        og_ref[...] = _dot(ub, wg_ref[...])

    om_ref[...] = _dot(u_ref[...], wm_ref[...]).astype(om_ref.dtype)


def _norm_proj(h2, g, w_gate, w_main, tm=1024, tn=512):
    n, d = h2.shape
    nm = w_main.shape[1]
    ng = w_gate.shape[1]
    return pl.pallas_call(
        _norm_proj_kernel,
        grid=(n // tm, nm // tn),
        in_specs=[pl.BlockSpec((tm, d), lambda i, j: (i, 0)),
                  pl.BlockSpec((1, d), lambda i, j: (0, 0)),
                  pl.BlockSpec((d, ng), lambda i, j: (0, 0)),
                  pl.BlockSpec((d, tn), lambda i, j: (0, j))],
        out_specs=[pl.BlockSpec((tm, tn), lambda i, j: (i, j)),
                   pl.BlockSpec((tm, ng), lambda i, j: (i, 0))],
        out_shape=[jax.ShapeDtypeStruct((n, nm), BF16), jax.ShapeDtypeStruct((n, ng), F32)],
        scratch_shapes=[pltpu.VMEM((tm, d), BF16)],
        compiler_params=_cparams(("parallel", "arbitrary")),
        name="norm_proj",
    )(h2, g, w_gate, w_main)


def _out_pe_kernel(gated, *refs):
    if gated:
        ma_ref, mb_ref, sz_ref, h_ref, p_ref, wa_ref, wb_ref, gp_ref, wg_ref, wp_ref, o_ref = refs
        sz = sz_ref[...].astype(F32)
        mb = (mb_ref[...].astype(F32) * _silu(sz)).astype(BF16)
    else:
        ma_ref, mb_ref, h_ref, p_ref, wa_ref, wb_ref, gp_ref, wg_ref, wp_ref, o_ref = refs
        mb = mb_ref[...]
    y = _dot(ma_ref[...], wa_ref[...]) + _dot(mb, wb_ref[...])
    h1 = h_ref[...] + _rms(y, gp_ref[...])
    gate = _sigmoid(_dot(h1.astype(BF16), wg_ref[...]))
    pp = _dot(p_ref[...].astype(BF16), wp_ref[...])
    o_ref[...] = h1 + gate * pp


def _out_pe(mix_a, a_map, ka, mix_b, b_map, kb, sz, sz_map, h2, p4, layer, w_a, w_b, g_post, w_gate, w_pe,
            bsz, seq, tm=512):
    n, d = h2.shape
    pe = p4.shape[-1]
    nt = seq // tm
    row = lambda b, s: (b * nt + s, 0)
    const = lambda b, s: (0, 0)
    gated = sz is not None
    in_specs = [pl.BlockSpec((tm, ka), a_map), pl.BlockSpec((tm, kb), b_map)]
    args = [mix_a, mix_b]
    if gated:
        in_specs.append(pl.BlockSpec((tm, kb), sz_map))
        args.append(sz)
    in_specs += [pl.BlockSpec((tm, d), row), pl.BlockSpec((None, None, tm, pe), lambda b, s: (layer, b, s, 0)),
                 pl.BlockSpec((ka, d), const), pl.BlockSpec((kb, d), const), pl.BlockSpec((1, d), const),
                 pl.BlockSpec((d, d), const), pl.BlockSpec((pe, d), const)]
    args += [h2, p4, w_a, w_b, g_post, w_gate, w_pe]
    return pl.pallas_call(
        functools.partial(_out_pe_kernel, gated),
        grid=(bsz, nt),
        in_specs=in_specs,
        out_specs=pl.BlockSpec((tm, d), row),
        out_shape=jax.ShapeDtypeStruct((n, d), F32),
        compiler_params=_cparams(("parallel", "arbitrary")),
        name="out_pe_gated" if gated else "out_pe",
    )(*args)


_RQ, _RK, _RV, _RG = 0, 512, 1024, 2048
_MQ, _MK, _MV, _MO, _MZ = 3072, 3584, 4096, 5120, 6144
AB_MAIN = 7168


def _head_norm(y, g):
    yc = y - jnp.mean(y, -1, keepdims=True)
    return yc * lax.rsqrt(jnp.mean(yc * yc, -1, keepdims=True) + EPS) * g


def _ab_kernel(pm_ref, pg_ref, cw_ref, cb_ref, gb_ref, rn_ref, mn_ref, mix_ref,
               ext_ref, r_ref, c_ref, n_ref, m_ref):
    L = CHUNK
    c_idx = pl.program_id(1)

    @pl.when(c_idx == 0)
    def _():
        ext_ref[0:8, :] = jnp.zeros((8, ext_ref.shape[1]), F32)
        r_ref[...] = jnp.zeros(r_ref.shape, F32)
        c_ref[...] = jnp.zeros(c_ref.shape, F32)
        n_ref[...] = jnp.zeros(n_ref.shape, F32)
        m_ref[...] = jnp.full(m_ref.shape, NEG, F32)

    row = lax.broadcasted_iota(jnp.int32, (L, L), 0)
    col = lax.broadcasted_iota(jnp.int32, (L, L), 1)
    causal = row >= col
    diff = (row - col).astype(F32)
    tcol = lax.broadcasted_iota(jnp.int32, (L, 1), 0).astype(F32)

    for h in range(RET_HEADS):
        lg = math.log1p(-(2.0 ** (-5.0 - h)))
        scale = RET_DK ** -0.5
        decay = jnp.where(causal, jnp.exp(jnp.maximum(diff, 0.0) * lg), 0.0) * scale
        q = pm_ref[:, _RQ + h * RET_DK:_RQ + (h + 1) * RET_DK]
        k = pm_ref[:, _RK + h * RET_DK:_RK + (h + 1) * RET_DK]
        v = pm_ref[:, _RV + h * RET_DV:_RV + (h + 1) * RET_DV]
        sc = _dot_nt(q, k) * decay
        xi = jnp.exp((tcol + 1.0) * lg) * scale
        qx = (q.astype(F32) * xi).astype(BF16)
        r_prev = r_ref[h]
        y = _dot(sc.astype(BF16), v) + _dot(qx, r_prev.astype(BF16))
        zeta = jnp.exp((L - 1.0 - tcol) * lg)
        kz_t = (k.astype(F32) * zeta).T.astype(BF16)
        r_ref[h] = r_prev * math.exp(L * lg) + _dot(kz_t, v)
        g = pm_ref[:, _RG + h * RET_DV:_RG + (h + 1) * RET_DV].astype(F32)
        out = _head_norm(y, rn_ref[:, h * RET_DV:(h + 1) * RET_DV]) * _silu(g)
        mix_ref[:, h * RET_DV:(h + 1) * RET_DV] = out.astype(mix_ref.dtype)

    nqk = 2 * ML_HEADS * ML_DK
    x = pm_ref[:, _MQ:_MQ + nqk].astype(F32)
    ext_ref[8:8 + L, :] = x
    acc = cb_ref[...] + cw_ref[ML_CONV - 1:ML_CONV, :] * x
    for j in range(ML_CONV - 1):
        acc = acc + cw_ref[j:j + 1, :] * ext_ref[pl.ds(8 - (ML_CONV - 1) + j, L), :]
    ext_ref[0:8, :] = x[L - 8:L, :]
    qk = _silu(acc)

    gt = (pg_ref[...] + gb_ref[...]).T
    g8 = gt[0:8, :]
    logf8 = jnp.minimum(g8, 0.0) - jnp.log(1.0 + jnp.exp(-jnp.abs(g8)))
    triu = jnp.where(row <= col, 1.0, 0.0).astype(BF16)
    bcum8 = _dot_f32_lhs(logf8, triu)
    b_t = pltpu.roll(bcum8, 4, axis=0)
    b_last = b_t[:, L - 1:L]
    a_t = b_last - b_t + g8
    mu = jnp.max(a_t, axis=1, keepdims=True)
    w_t = jnp.exp(a_t - mu)
    r_t = g8 - b_t
    zpad = jnp.zeros((L - 8, L), F32)
    bcol = jnp.concatenate([b_t, zpad], axis=0).T
    wcol = jnp.concatenate([w_t, zpad], axis=0).T
    m_prev = m_ref[...][:, 0:1]
    m_new = jnp.maximum(b_last + m_prev, mu)
    sp = jnp.exp(b_last + m_prev - m_new)
    scn = jnp.exp(mu - m_new)
    m_ref[...] = jnp.broadcast_to(m_new, m_ref.shape)

    for h in range(ML_HEADS):
        bc = bcol[:, h:h + 1]
        log_d = jnp.where(causal, bc + r_t[h:h + 1, :], NEG)
        inter = bc + m_prev[h:h + 1, :]
        m_t = jnp.maximum(inter, jnp.max(log_d, axis=1, keepdims=True))
        dmat = jnp.exp(log_d - m_t)
        w_int = jnp.exp(inter - m_t)
        qh = qk[:, h * ML_DK:(h + 1) * ML_DK]
        kh = qk[:, ML_HEADS * ML_DK + h * ML_DK:ML_HEADS * ML_DK + (h + 1) * ML_DK] * (ML_DK ** -0.5)
        qb = qh.astype(BF16)
        v = pm_ref[:, _MV + h * ML_DV:_MV + (h + 1) * ML_DV]
        s = _dot_nt(qb, kh.astype(BF16)) * dmat
        c_prev = c_ref[h]
        n_prev = n_ref[h:h + 1, :]
        num = _dot(s.astype(BF16), v) + w_int * _dot(qb, c_prev.astype(BF16))
        den = jnp.sum(s, axis=1, keepdims=True) + w_int * jnp.sum(qh * n_prev, axis=1, keepdims=True)
        den = jnp.maximum(jnp.abs(den), jnp.exp(-m_t))
        hcell = num * (1.0 / den)
        o = pm_ref[:, _MO + h * ML_DV:_MO + (h + 1) * ML_DV].astype(F32)
        y = _sigmoid(o) * hcell
        z = pm_ref[:, _MZ + h * ML_DV:_MZ + (h + 1) * ML_DV].astype(F32)
        out = _head_norm(y, mn_ref[:, h * ML_DV:(h + 1) * ML_DV]) * _silu(z)
        mix_ref[:, RET_HEADS * RET_DV + h * ML_DV:RET_HEADS * RET_DV + (h + 1) * ML_DV] = out.astype(mix_ref.dtype)
        kw = kh * wcol[:, h:h + 1]
        kv = _dot(kw.T.astype(BF16), v)
        ksum = jnp.sum(kw, axis=0, keepdims=True)
        sp_h = sp[h:h + 1, :]
        sc_h = scn[h:h + 1, :]
        c_ref[h] = sp_h * c_prev + sc_h * kv
        n_ref[h:h + 1, :] = sp_h * n_prev + sc_h * ksum


def _ab_mixer(pm, pg, conv_w, conv_b, gate_b, ret_norm, ml_norm, bsz, seq):
    n = pm.shape[0]
    nc = seq // CHUNK
    nqk = 2 * ML_HEADS * ML_DK
    row = lambda b, c: (b * nc + c, 0)
    const = lambda b, c: (0, 0)
    nmix = RET_HEADS * RET_DV + ML_HEADS * ML_DV
    return pl.pallas_call(
        _ab_kernel,
        grid=(bsz, nc),
        in_specs=[pl.BlockSpec((CHUNK, AB_MAIN), row), pl.BlockSpec((CHUNK, LANES), row),
                  pl.BlockSpec((ML_CONV, nqk), const), pl.BlockSpec((1, nqk), const),
                  pl.BlockSpec((1, LANES), const),
                  pl.BlockSpec((1, RET_HEADS * RET_DV), const), pl.BlockSpec((1, ML_HEADS * ML_DV), const)],
        out_specs=pl.BlockSpec((CHUNK, nmix), row),
        out_shape=jax.ShapeDtypeStruct((n, nmix), BF16),
        scratch_shapes=[pltpu.VMEM((CHUNK + 8, nqk), F32),
                        pltpu.VMEM((RET_HEADS, RET_DK, RET_DV), F32),
                        pltpu.VMEM((ML_HEADS, ML_DK, ML_DV), F32),
                        pltpu.VMEM((8, ML_DK), F32),
                        pltpu.VMEM((8, LANES), F32)],
        compiler_params=_cparams(("parallel", "arbitrary")),
        name="ab_mixer",
    )(pm, pg, conv_w, conv_b, gate_b, ret_norm, ml_norm)


_Q0 = 0
_KC0, _VC0, _KS0, _VS0, _KW0, _VW0 = 16, 20, 24, 28, 32, 36
_NZ0 = 40 * LANES
_SU0 = _NZ0 + NSA_HEADS * NSA_DH
_SZ0 = _SU0 + S5_WIDTH
CD_MAIN = _SZ0 + S5_WIDTH
SLOTW = NSA_GROUPS * LANES
LOG2E = math.log2(math.e)


def _compress_kernel(kx_ref, vx_ref, pk_ref, pv_ref, w1k_ref, w1v_ref, w2k_ref, w2vt_ref, ko_ref, vot_ref, xs_ref):
    seq = kx_ref.shape[0]
    nb = seq // CMP_STRIDE
    for kind, (x_ref, p_ref, w1_ref) in enumerate(((kx_ref, pk_ref, w1k_ref), (vx_ref, pv_ref, w1v_ref))):
        for g in range(NSA_GROUPS):
            xs_ref[...] = x_ref[:, g * LANES:(g + 1) * LANES].astype(F32)
            a = jnp.zeros((nb, CMP_HIDDEN), F32)
            b = jnp.zeros((nb, CMP_HIDDEN), F32)
            for r in range(CMP_STRIDE):
                xr = xs_ref[pl.ds(r, nb, stride=CMP_STRIDE), :]
                a = a + _dot((xr + p_ref[r:r + 1, :]).astype(BF16), w1_ref[r])
                b = b + _dot((xr + p_ref[CMP_STRIDE + r:CMP_STRIDE + r + 1, :]).astype(BF16),
                             w1_ref[CMP_STRIDE + r])
            h1 = a + pltpu.roll(b, nb - 1, axis=0)
            hb = _gelu(h1).astype(BF16)
            if kind == 0:
                ko_ref[g] = _dot(hb, w2k_ref[...]).astype(ko_ref.dtype)
            else:
                vot_ref[g] = _dot_nt(w2vt_ref[...], hb).astype(vot_ref.dtype)


def _compress(pm3, pos_k, pos_v, w1k, w1v, w2k, w2vt):
    bsz, seq, _ = pm3.shape
    nb = seq // CMP_STRIDE
    c3 = lambda b: (0, 0, 0)
    c2 = lambda b: (0, 0)
    return pl.pallas_call(
        _compress_kernel,
        grid=(bsz,),
        in_specs=[pl.BlockSpec((None, seq, SLOTW), lambda b: (b, 0, _KC0 * LANES // SLOTW)),
                  pl.BlockSpec((None, seq, SLOTW), lambda b: (b, 0, _VC0 * LANES // SLOTW)),
                  pl.BlockSpec((CMP_LEN, LANES), c2), pl.BlockSpec((CMP_LEN, LANES), c2),
                  pl.BlockSpec((CMP_LEN, LANES, CMP_HIDDEN), c3), pl.BlockSpec((CMP_LEN, LANES, CMP_HIDDEN), c3),
                  pl.BlockSpec((CMP_HIDDEN, LANES), c2), pl.BlockSpec((LANES, CMP_HIDDEN), c2)],
        out_specs=[pl.BlockSpec((None, NSA_GROUPS, nb, LANES), lambda b: (b, 0, 0, 0)),
                   pl.BlockSpec((None, NSA_GROUPS, LANES, nb), lambda b: (b, 0, 0, 0))],
        out_shape=[jax.ShapeDtypeStruct((bsz, NSA_GROUPS, nb, LANES), BF16),
                   jax.ShapeDtypeStruct((bsz, NSA_GROUPS, LANES, nb), BF16)],
        scratch_shapes=[pltpu.VMEM((seq, LANES), F32)],
        compiler_params=_cparams(("parallel",)),
        name="nsa_compress",
    )(pm3, pm3, pos_k, pos_v, w1k, w1v, w2k, w2vt)


def _nsa_kernel(q_ref, kc_ref, vct_ref, ks_ref, vs_ref, kw_ref, vw_ref, nz_ref, pg_ref,
                ek_ref, ev_ref, al_ref, sl_ref, ovt_ref, y_ref,
                ksa_ref, vsa_ref, kwa_ref, vwa_ref):
    seq = ks_ref.shape[0]
    g_idx = pl.program_id(1)
    qi = pl.program_id(2)
    t0 = qi * TQ

    @pl.when(qi == 0)
    def _():
        ek = ek_ref[...]
        ev = ev_ref[...]
        ksa_ref[...] = ks_ref[...] + ek
        vsa_ref[...] = vs_ref[...] + ev
        zeros = jnp.zeros((WINDOW, LANES), BF16)
        lane_k = lax.broadcasted_iota(jnp.int32, (seq, LANES), 1)
        ek_w = jnp.where(lane_k >= ALI_LANE0, ek, jnp.zeros_like(ek))
        kwa_ref[0:WINDOW, :] = zeros
        vwa_ref[0:WINDOW, :] = zeros
        kwa_ref[WINDOW:WINDOW + seq, :] = kw_ref[...] + ek_w
        vwa_ref[WINDOW:WINDOW + seq, :] = vw_ref[...] + ev

    nsel = seq // SEL_BLOCK
    nk_dyn = (t0 + TQ + KCH - 1) // KCH
    for nk in range(1, seq // KCH + 1):
        pl.when(nk_dyn == nk)(functools.partial(
            _nsa_tile, nk, nsel, t0, g_idx, q_ref, kc_ref, vct_ref, nz_ref, pg_ref, al_ref, sl_ref, ovt_ref,
            y_ref, ksa_ref, vsa_ref, kwa_ref, vwa_ref))


def _nsa_tile(nk, nsel, t0, g_idx, q_ref, kc_ref, vct_ref, nz_ref, pg_ref, al_ref, sl_ref, ovt_ref,
              y_ref, ksa_ref, vsa_ref, kwa_ref, vwa_ref):
    hp = NSA_HPG
    rows = hp * TQ
    span = WINDOW + TQ
    tw = pl.multiple_of(t0, TQ)
    rr = lax.broadcasted_iota(jnp.int32, (rows, 1), 0) & (TQ - 1)
    jrow = lax.broadcasted_iota(jnp.int32, (LANES, rows), 0)
    tlane = t0 + (lax.broadcasted_iota(jnp.int32, (LANES, rows), 1) & (TQ - 1))

    qs = jnp.concatenate([q_ref[:, hh * LANES:(hh + 1) * LANES] for hh in range(hp)], axis=0)
    al = jnp.concatenate([jnp.broadcast_to(al_ref[pl.ds(g_idx * hp + hh, 1), :], (TQ, LANES))
                          for hh in range(hp)], axis=0)
    qal = qs.astype(F32) + al
    s_c = _dot_nt(kc_ref[...], qs)
    s_w = _dot_nt(qal.astype(BF16), kwa_ref[pl.ds(tw, span), :])

    dist_c = (tlane - (jrow * CMP_STRIDE + (CMP_LEN - 1))).astype(F32)
    slope = jnp.concatenate([sl_ref[pl.ds(g_idx * hp + hh, 1), :] for hh in range(hp)], axis=1)
    s = jnp.where(dist_c >= 0.0, s_c - slope * dist_c, NEG)
    e = jnp.exp2(s - jnp.max(s, axis=0, keepdims=True))
    p = e * (1.0 / jnp.sum(e, axis=0, keepdims=True))
    p = jnp.where(tlane >= CMP_LEN - 1, p, 0.0)
    oc_t = _dot(vct_ref[...], p.astype(BF16))
    psum_t = p[:, 0:TQ]
    for hh in range(1, hp):
        psum_t = psum_t + p[:, hh * TQ:(hh + 1) * TQ]
    p_hi = psum_t.astype(BF16)
    p_lo = (psum_t - p_hi.astype(F32)).astype(BF16)
    imp_t = _dot(ovt_ref[...], p_hi) + _dot(ovt_ref[...], p_lo)

    colw = lax.broadcasted_iota(jnp.int32, (rows, span), 1)
    dwin = colw - rr
    s = jnp.where(dwin > 0, jnp.where(dwin <= WINDOW, jnp.where(colw >= WINDOW - t0, s_w, NEG), NEG), NEG)
    p = jnp.exp2(s - jnp.max(s, axis=1, keepdims=True))
    acc_w = _dot(p.astype(BF16), vwa_ref[pl.ds(tw, span), :])

    score = imp_t[SEL_LANE0:SEL_LANE0 + nsel, :]
    nrow = lax.broadcasted_iota(jnp.int32, (nsel, TQ), 0)
    own = (t0 + lax.broadcasted_iota(jnp.int32, (nsel, TQ), 1)) >> int(math.log2(SEL_BLOCK))
    valid = nrow <= own
    forced = (nrow == 0) | (nrow == own)
    score = jnp.where(forced, BIG, jnp.where(valid, score, -1.0))
    chosen = jnp.zeros((nsel, TQ), F32)
    nrow_f = nrow.astype(F32)
    for _ in range(SEL_TOPN):
        mx = jnp.max(score, axis=0, keepdims=True)
        first = jnp.min(jnp.where(score == mx, nrow_f, 4.0 * LANES), axis=0, keepdims=True)
        hit = nrow_f == first
        chosen = jnp.where(hit, 1.0, chosen)
        score = jnp.where(hit, -3.0, score)
    mask_t = jnp.where(chosen > 0.0, jnp.where(valid, 0.0, NEG), NEG)
    maskcols = jnp.concatenate([jnp.zeros((SEL_LANE0, TQ), F32), mask_t,
                                jnp.zeros((LANES - SEL_LANE0 - nsel, TQ), F32)], axis=0).T

    qa = (qal + jnp.concatenate([maskcols] * hp, axis=0)).astype(BF16)

    def scores(c):
        s = _dot_nt(qa, ksa_ref[c * KCH:(c + 1) * KCH, :])
        if c == nk - 1:
            dsel = lax.broadcasted_iota(jnp.int32, (rows, KCH), 1) - rr
            s = jnp.where(dsel <= t0 - c * KCH, s, NEG)
        return s

    def weighted(c, s):
        mc = jnp.max(s, axis=1, keepdims=True)
        return mc, _dot(jnp.exp2(s - mc).astype(BF16), vsa_ref[c * KCH:(c + 1) * KCH, :])

    parts = []
    s_next = scores(0)
    for c in range(nk):
        s_cur = s_next
        if c + 1 < nk:
            s_next = scores(c + 1)
        parts.append(weighted(c, s_cur))
    m = parts[0][0]
    for mc, _ in parts[1:]:
        m = jnp.maximum(m, mc)
    acc_s = None
    for mc, d in parts:
        d = d if nk == 1 else jnp.exp2(mc - m) * d
        acc_s = d if acc_s is None else acc_s + d

    gt_t = _sigmoid(pg_ref[...]).T
    outs = []
    for hh in range(hp):
        g_c, g_s, g_w = [gt_t[j * hp + hh:j * hp + hh + 1, :] for j in range(3)]
        as_t = acc_s[hh * TQ:(hh + 1) * TQ, :].T
        aw_t = acc_w[hh * TQ:(hh + 1) * TQ, :].T
        o = (g_c * oc_t[:, hh * TQ:(hh + 1) * TQ]
             + (g_s * (1.0 / as_t[NSA_DH:NSA_DH + 1, :])) * as_t
             + (g_w * (1.0 / aw_t[NSA_DH:NSA_DH + 1, :])) * aw_t)
        outs.append(o[0:NSA_DH, :])
    for pr in range(hp // 2):
        packed = jnp.concatenate([outs[2 * pr], outs[2 * pr + 1]], axis=0).T
        z = nz_ref[:, pr * LANES:(pr + 1) * LANES].astype(F32)
        y_ref[:, pr * LANES:(pr + 1) * LANES] = (packed * _silu(z)).astype(y_ref.dtype)


def _nsa(pm3, pg3, kcc, vct, ek, ev, al, sl, ovt):
    bsz, seq, _ = pm3.shape
    nq = seq // TQ
    gw = NSA_HPG * NSA_DH
    kv = lambda slot0: pl.BlockSpec((None, seq, LANES), lambda b, g, q: (b, 0, slot0 + g))
    c2 = lambda b, g, q: (0, 0)
    return pl.pallas_call(
        _nsa_kernel,
        grid=(bsz, NSA_GROUPS, nq),
        in_specs=[pl.BlockSpec((None, TQ, NSA_HPG * LANES), lambda b, g, q: (b, q, g)),
                  pl.BlockSpec((None, None, LANES, LANES), lambda b, g, q: (b, g, 0, 0)),
                  pl.BlockSpec((None, None, LANES, LANES), lambda b, g, q: (b, g, 0, 0)),
                  kv(_KS0), kv(_VS0), kv(_KW0), kv(_VW0),
                  pl.BlockSpec((None, TQ, gw), lambda b, g, q: (b, q, _NZ0 // gw + g)),
                  pl.BlockSpec((None, TQ, LANES), lambda b, g, q: (b, q, g)),
                  pl.BlockSpec((seq, LANES), c2), pl.BlockSpec((1, LANES), c2),
                  pl.BlockSpec((NSA_HEADS, LANES), c2), pl.BlockSpec((NSA_HEADS, LANES), c2),
                  pl.BlockSpec((LANES, LANES), c2)],
        out_specs=pl.BlockSpec((None, TQ, gw), lambda b, g, q: (b, q, g)),
        out_shape=jax.ShapeDtypeStruct((bsz, seq, NSA_HEADS * NSA_DH), BF16),
        scratch_shapes=[pltpu.VMEM((seq, LANES), BF16), pltpu.VMEM((seq, LANES), BF16),
                        pltpu.VMEM((seq + WINDOW, LANES), BF16), pltpu.VMEM((seq + WINDOW, LANES), BF16)],
        compiler_params=_cparams(("parallel", "arbitrary", "arbitrary")),
        name="nsa_attn",
    )(pm3, kcc, vct, pm3, pm3, pm3, pm3, pm3, pg3, ek, ev, al, sl, ovt)


def _nsa_tables(seq):
    t = np.arange(seq)
    ek = np.zeros((seq, LANES), np.float32)
    ek[t, SEL_LANE0 + t // SEL_BLOCK] = 1.0
    for c in range(3):
        ek[:, ALI_LANE0 + c] = t // SEL_BLOCK
        ek[:, ALI_LANE0 + 3 + c] = t % SEL_BLOCK
    ev = np.zeros((1, LANES), np.float32)
    ev[0, NSA_DH] = 1.0
    slopes = jnp.exp2(-8.0 * jnp.arange(1, NSA_HEADS + 1, dtype=F32) / NSA_HEADS) * LOG2E
    s_hi, s_mid, s_lo = _split3(slopes)
    parts = [p.astype(F32) for p in (s_hi, s_mid, s_lo)]
    al = jnp.zeros((NSA_HEADS, LANES), F32)
    for c in range(3):
        al = al.at[:, ALI_LANE0 + c].set(parts[c] * SEL_BLOCK)
        al = al.at[:, ALI_LANE0 + 3 + c].set(parts[c])
    sl = jnp.broadcast_to(slopes[:, None], (NSA_HEADS, LANES))
    nb = seq // CMP_STRIDE
    cstart = np.arange(nb) * CMP_STRIDE
    sel = np.arange(seq // SEL_BLOCK)
    ovt = np.zeros((LANES, max(nb, LANES)), np.float32)
    ovl = (cstart[:, None] < (sel[None, :] + 1) * SEL_BLOCK) & (cstart[:, None] + CMP_LEN > sel[None, :] * SEL_BLOCK)
    ovl[nb - 1, :] = False
    ovt[SEL_LANE0:SEL_LANE0 + len(sel), :nb] = ovl.T
    return jnp.asarray(ek, BF16), jnp.asarray(ev, BF16), al, sl, jnp.asarray(ovt, BF16)


def _s5_disc_kernel(are_ref, aim_ref, ldt_ref, bre_ref, bim_ref, abr_ref, abi_ref, bbr_ref, bbi_ref):
    a_re = are_ref[...]
    a_im = aim_ref[...]
    dt = jnp.exp(ldt_ref[...])
    er = jnp.exp(a_re * dt)
    abar_re = er * jnp.cos(a_im * dt)
    abar_im = er * jnp.sin(a_im * dt)
    lam2 = a_re * a_re + a_im * a_im
    cr = ((abar_re - 1.0) * a_re + abar_im * a_im) / lam2
    ci = (abar_im * a_re - (abar_re - 1.0) * a_im) / lam2
    b_re = bre_ref[...]
    b_im = bim_ref[...]
    abr_ref[...] = abar_re
    abi_ref[...] = abar_im
    bbr_ref[...] = cr * b_re - ci * b_im
    bbi_ref[...] = cr * b_im + ci * b_re


def _s5_disc(a_re_rep, a_im_rep, ldt_rep, b_re_t, b_im_t):
    shp = jax.ShapeDtypeStruct(a_re_rep.shape, F32)
    return pl.pallas_call(_s5_disc_kernel, out_shape=[shp] * 4, name="s5_disc")(
        a_re_rep, a_im_rep, ldt_rep, b_re_t, b_im_t)


S5_SLABG = 4
S5_TT = 64
S5_PITCH = S5_TT + 4


def _s5_kernel(u_ref, bre_ref, bim_ref, ar_ref, ai_ref, cre_ref, cim_ref, d_ref, wg_ref, o_ref,
               xr_ref, xi_ref, sr_ref, si_ref):
    nb, tt, wd = u_ref.shape
    pitch = sr_ref.shape[1] // nb
    gcols = S5_SLABG * LANES
    ngroups = S5_NSTATE // gcols

    @pl.when(pl.program_id(0) == 0)
    def _():
        xr_ref[...] = jnp.zeros(xr_ref.shape, F32)
        xi_ref[...] = jnp.zeros(xi_ref.shape, F32)

    u = u_ref[...].reshape(nb * tt, wd)

    for kg in range(ngroups):
        cols = slice(kg * gcols, (kg + 1) * gcols)
        br = _dot(u, bre_ref[:, cols])
        bi = _dot(u, bim_ref[:, cols])
        for j in range(S5_SLABG):
            for b in range(nb):
                sr_ref[kg * S5_SLABG + j, b * pitch:b * pitch + tt, :] = br[b * tt:(b + 1) * tt, j * LANES:(j + 1) * LANES]
                si_ref[kg * S5_SLABG + j, b * pitch:b * pitch + tt, :] = bi[b * tt:(b + 1) * tt, j * LANES:(j + 1) * LANES]

    for kg in range(ngroups):
        cols = slice(kg * gcols, (kg + 1) * gcols)
        slabs = range(kg * S5_SLABG, (kg + 1) * S5_SLABG)
        ar = ar_ref[:, cols]
        ai = ai_ref[:, cols]

        def body(t, carry, slabs=slabs, ar=ar, ai=ai):
            xr, xi = carry
            rows = pl.ds(t, nb, stride=pitch)
            nxr = ar * xr - ai * xi + jnp.concatenate([sr_ref[k, rows, :] for k in slabs], axis=1)
            nxi = ar * xi + ai * xr + jnp.concatenate([si_ref[k, rows, :] for k in slabs], axis=1)
            for j, k in enumerate(slabs):
                sr_ref[k, rows, :] = nxr[:, j * LANES:(j + 1) * LANES]
                si_ref[k, rows, :] = nxi[:, j * LANES:(j + 1) * LANES]
            return nxr, nxi

        xr, xi = lax.fori_loop(0, tt, body, (xr_ref[:, cols], xi_ref[:, cols]), unroll=4)
        xr_ref[:, cols] = xr
        xi_ref[:, cols] = xi

    y = jnp.zeros((nb * tt, wd), F32)
    for kg in range(ngroups):
        cols = slice(kg * gcols, (kg + 1) * gcols)
        gather = lambda ref: jnp.concatenate(
            [jnp.concatenate([ref[kg * S5_SLABG + j, b * pitch:b * pitch + tt, :] for j in range(S5_SLABG)], axis=1)
             for b in range(nb)], axis=0).astype(BF16)
        y = y + _dot(gather(sr_ref), cre_ref[cols, :]) - _dot(gather(si_ref), cim_ref[cols, :])
    y = _gelu(y + d_ref[...] * u.astype(F32))
    z = _dot(y.astype(BF16), wg_ref[...])
    o = z[:, :S5_WIDTH] * _sigmoid(z[:, S5_WIDTH:])
    o_ref[...] = o.reshape(nb, tt, wd).astype(o_ref.dtype)


def _s5(pm3, bbd_re, bbd_im, ar8, ai8, cbd_re, cbd_im, d_row, w_glu):
    bsz, seq, _ = pm3.shape
    tt = S5_TT
    c2 = lambda i: (0, 0)
    nslab = S5_NSTATE // LANES
    return pl.pallas_call(
        _s5_kernel,
        grid=(seq // tt,),
        in_specs=[pl.BlockSpec((bsz, tt, S5_WIDTH), lambda i: (0, i, _SU0 // S5_WIDTH)),
                  pl.BlockSpec((S5_WIDTH, S5_NSTATE), c2), pl.BlockSpec((S5_WIDTH, S5_NSTATE), c2),
                  pl.BlockSpec((8, S5_NSTATE), c2), pl.BlockSpec((8, S5_NSTATE), c2),
                  pl.BlockSpec((S5_NSTATE, S5_WIDTH), c2), pl.BlockSpec((S5_NSTATE, S5_WIDTH), c2),
                  pl.BlockSpec((1, S5_WIDTH), c2), pl.BlockSpec((S5_WIDTH, 2 * S5_WIDTH), c2)],
        out_specs=pl.BlockSpec((bsz, tt, S5_WIDTH), lambda i: (0, i, 0)),
        out_shape=jax.ShapeDtypeStruct((bsz, seq, S5_WIDTH), BF16),
        scratch_shapes=[pltpu.VMEM((8, S5_NSTATE), F32), pltpu.VMEM((8, S5_NSTATE), F32),
                        pltpu.VMEM((nslab, bsz * S5_PITCH, LANES), F32),
                        pltpu.VMEM((nslab, bsz * S5_PITCH, LANES), F32)],
        compiler_params=_cparams(("arbitrary",)),
        name="s5_scan",
    )(pm3, bbd_re, bbd_im, ar8, ai8, cbd_re, cbd_im, d_row, w_glu)


def _pad_cols(w, n):
    return jnp.pad(w, ((0, 0), (0, n - w.shape[1])))


def _slot_cols(w, width):
    d, n = w.shape
    k = n // width
    return jnp.pad(w.reshape(d, k, width), ((0, 0), (0, 0), (0, LANES - width))).reshape(d, k * LANES)


def _layer0(h2, p4, layer, bsz, seq, norm_pre, norm_post, pe_proj, pe_gate,
            w_in, ret_norm, conv_w, conv_b, gate_b, ml_norm, w_out):
    d = h2.shape[1]
    g0 = 5120
    w_main = jnp.concatenate([w_in[:, :g0], w_in[:, g0 + 2 * ML_HEADS:]], axis=1).astype(BF16)
    w_gate = _pad_cols(w_in[:, g0:g0 + 2 * ML_HEADS], LANES).astype(BF16)
    pm, pg = _norm_proj(h2, norm_pre.reshape(1, d), w_gate, w_main)
    mix = _ab_mixer(pm, pg, conv_w, conv_b.reshape(1, -1), _pad_cols(gate_b.reshape(1, -1), LANES),
                    ret_norm.reshape(1, -1), ml_norm.reshape(1, -1), bsz, seq)
    ka = RET_HEADS * RET_DV
    nt = seq // 512
    return _out_pe(mix, lambda b, s: (b * nt + s, 0), ka, mix, lambda b, s: (b * nt + s, 1), ML_HEADS * ML_DV,
                   None, None, h2, p4, layer, w_out[:ka].astype(BF16), w_out[ka:].astype(BF16),
                   norm_post.reshape(1, d), pe_gate.astype(BF16), pe_proj.astype(BF16), bsz, seq)


def _layer1(h2, p4, layer, bsz, seq, norm_pre, norm_post, pe_proj, pe_gate,
            w_in, pos_k, pos_v, w1_k, w2_k, w1_v, w2_v,
            a_re, a_im, log_dt, b_re, b_im, c_re, c_im, d_skip, w_glu, w_out):
    d = h2.shape[1]
    nw, nkv = NSA_HEADS * NSA_DH, NSA_GROUPS * NSA_DH
    offs = np.cumsum((0, nw, nkv, nkv, nkv, nkv, nkv, nkv, 3 * NSA_HEADS, nw, S5_WIDTH, S5_WIDTH))
    seg = lambda i: w_in[:, offs[i]:offs[i + 1]]
    w_main = jnp.concatenate(
        [_slot_cols(seg(0) * (NSA_DH ** -0.5 * LOG2E), NSA_DH)] + [_slot_cols(seg(i), NSA_DH) for i in range(1, 7)]
        + [seg(8), seg(9), seg(10)], axis=1).astype(BF16)
    w_gate = seg(7).reshape(d, NSA_GROUPS, NSA_HPG, 3).transpose(0, 1, 3, 2).reshape(d, NSA_GROUPS, 3 * NSA_HPG)
    w_gate = jnp.pad(w_gate, ((0, 0), (0, 0), (0, LANES - 3 * NSA_HPG))).reshape(d, NSA_GROUPS * LANES).astype(BF16)
    pm, pg = _norm_proj(h2, norm_pre.reshape(1, d), w_gate, w_main)

    nb = seq // CMP_STRIDE
    pm3 = pm.reshape(bsz, seq, CD_MAIN)
    padl = lambda a: jnp.pad(a, ((0, 0),) * (a.ndim - 1) + ((0, LANES - a.shape[-1]),))
    w1 = lambda w: jnp.pad(w.reshape(CMP_LEN, NSA_DH, CMP_HIDDEN), ((0, 0), (0, LANES - NSA_DH), (0, 0))).astype(BF16)
    kcc, vct = _compress(pm3, padl(pos_k), padl(pos_v), w1(w1_k), w1(w1_v),
                         padl(w2_k).astype(BF16), padl(w2_v).T.astype(BF16))
    if nb < LANES:
        kcc = jnp.pad(kcc, ((0, 0), (0, 0), (0, LANES - nb), (0, 0)))
        vct = jnp.pad(vct, ((0, 0), (0, 0), (0, 0), (0, LANES - nb)))
    ek, ev, al, sl, ovt = _nsa_tables(seq)
    y_nsa = _nsa(pm3, pg.reshape(bsz, seq, NSA_GROUPS * LANES), kcc, vct, ek, ev, al, sl, ovt)

    rep = lambda a: jnp.repeat(a, S5_GROUP_CH, axis=0)
    ldt = jnp.broadcast_to(rep(log_dt[:, None]), (S5_WIDTH, S5_STATE))
    bt = lambda b: b.transpose(0, 2, 1).reshape(S5_WIDTH, S5_STATE)
    abr, abi, bbr, bbi = _s5_disc(rep(a_re), rep(a_im), ldt, bt(b_re), bt(b_im))
    eye = jnp.eye(S5_GROUPS, dtype=F32)
    bd_in = lambda bb: (bb.reshape(S5_GROUPS, S5_GROUP_CH, 1, S5_STATE)
                        * eye[:, None, :, None]).reshape(S5_WIDTH, S5_NSTATE).astype(BF16)
    bd_out = lambda c: (c.transpose(0, 2, 1).reshape(S5_GROUPS, S5_STATE, 1, S5_GROUP_CH)
                        * eye[:, None, :, None]).reshape(S5_NSTATE, S5_WIDTH).astype(BF16)
    row8 = lambda a: jnp.broadcast_to(a[::S5_GROUP_CH].reshape(1, S5_NSTATE), (8, S5_NSTATE))
    y_s5 = _s5(pm3, bd_in(bbr), bd_in(bbi), row8(abr), row8(abi),
               bd_out(c_re), bd_out(c_im), d_skip.reshape(1, -1), w_glu.astype(BF16))

    nt = seq // 512
    szb = _SZ0 // S5_WIDTH
    return _out_pe(y_nsa.reshape(bsz * seq, nw), lambda b, s: (b * nt + s, 0), nw,
                   y_s5.reshape(bsz * seq, S5_WIDTH), lambda b, s: (b * nt + s, 0), S5_WIDTH,
                   pm, lambda b, s: (b * nt + s, szb),
                   h2, p4, layer, w_out[:nw].astype(BF16), w_out[nw:].astype(BF16),
                   norm_post.reshape(1, d), pe_gate.astype(BF16), pe_proj.astype(BF16), bsz, seq)


def kernel(x, p, norm_pre, norm_post, pe_proj, pe_gate, ab_w_in, ret_norm, ml_conv_w, ml_conv_b, ml_gate_b,
           ml_norm, ab_w_out, cd_w_in, cmp_pos_k, cmp_pos_v, cmp_w1_k, cmp_w2_k, cmp_w1_v, cmp_w2_v,
           s5_a_re, s5_a_im, s5_log_dt, s5_b_re, s5_b_im, s5_c_re, s5_c_im, s5_d, s5_w_glu, cd_w_out):
    bsz, seq, d = x.shape
    assert bsz == 8, "the S5 scan maps the batch onto the 8 sublanes of a vreg"
    assert seq % max(KCH, 1024) == 0
    depth = p.shape[0]
    h2 = x.reshape(bsz * seq, d)
    for i in range(depth):
        j = i // 2
        if i % 2 == 0:
            h2 = _layer0(h2, p, i, bsz, seq, norm_pre[i], norm_post[i], pe_proj[i], pe_gate[i],
                         ab_w_in[j], ret_norm[j], ml_conv_w[j], ml_conv_b[j], ml_gate_b[j], ml_norm[j], ab_w_out[j])
        else:
            h2 = _layer1(h2, p, i, bsz, seq, norm_pre[i], norm_post[i], pe_proj[i], pe_gate[i],
                         cd_w_in[j], cmp_pos_k[j], cmp_pos_v[j], cmp_w1_k[j], cmp_w2_k[j], cmp_w1_v[j], cmp_w2_v[j],
                         s5_a_re[j], s5_a_im[j], s5_log_dt[j], s5_b_re[j], s5_b_im[j], s5_c_re[j], s5_c_im[j],
                         s5_d[j], s5_w_glu[j], cd_w_out[j])
    return h2.reshape(bsz, seq, d)
```

```python
import functools
import math

import numpy as np
import jax
import jax.numpy as jnp
from jax import lax
from jax.experimental import pallas as pl
from jax.experimental.pallas import tpu as pltpu

F32 = jnp.float32
BF16 = jnp.bfloat16
EPS = 1e-6
NEG = -1e30
BIG = 1e30

LANES = 128
CHUNK = 128
RET_HEADS, RET_DK, RET_DV = 4, 128, 256
ML_HEADS, ML_DK, ML_DV = 4, 128, 256
ML_CONV = 4
NSA_HEADS, NSA_GROUPS, NSA_DH = 16, 4, 64
NSA_HPG = NSA_HEADS // NSA_GROUPS
CMP_STRIDE, CMP_LEN, CMP_HIDDEN = 16, 32, 128
SEL_BLOCK, SEL_TOPN, WINDOW = 64, 4, 512
S5_GROUPS, S5_GROUP_CH, S5_STATE = 32, 16, 64
S5_WIDTH = S5_GROUPS * S5_GROUP_CH
S5_NSTATE = S5_GROUPS * S5_STATE

TQ = 256
KCH = 512
SEL_LANE0 = 64
ALI_LANE0 = 96
VMEM_LIMIT = 56 * 1024 * 1024


def _cparams(sem):
    return pltpu.CompilerParams(dimension_semantics=sem, vmem_limit_bytes=VMEM_LIMIT)


def _dot(a, b):
    return jnp.dot(a, b, preferred_element_type=F32)


def _dot_nt(a, b):
    return lax.dot_general(a, b, (((1,), (1,)), ((), ())), preferred_element_type=F32)


def _split3(x):
    hi = x.astype(BF16)
    r1 = x - hi.astype(F32)
    mid = r1.astype(BF16)
    lo = (r1 - mid.astype(F32)).astype(BF16)
    return hi, mid, lo


def _dot_f32_lhs(x, m):
    hi, mid, lo = _split3(x)
    return _dot(hi, m) + _dot(mid, m) + _dot(lo, m)


def _sigmoid(x):
    return 1.0 / (1.0 + jnp.exp(-x))


def _silu(x):
    return x * _sigmoid(x)


def _gelu(x):
    return 0.5 * x * (1.0 + jnp.tanh(math.sqrt(2.0 / math.pi) * (x + 0.044715 * (x * x * x))))


def _rms(x, g):
    return x * lax.rsqrt(jnp.mean(x * x, -1, keepdims=True) + EPS) * g


PROJ_TN = 1024


def _norm_proj_kernel(h_ref, g_ref, wg_ref, wm_ref, om_ref, og_ref):
    ub = _rms(h_ref[...], g_ref[...]).astype(BF16)
    for c0 in range(0, om_ref.shape[1], PROJ_TN):
        om_ref[:, c0:c0 + PROJ_TN] = _dot(ub, wm_ref[:, c0:c0 + PROJ_TN]).astype(om_ref.dtype)
    og_ref[...] = _dot(ub, wg_ref[...])


def _norm_proj(h2, g, w_gate, w_main, tm=512):
    n, d = h2.shape
    nm = w_main.shape[1]
    ng = w_gate.shape[1]
    once = pl.Buffered(1)
    return pl.pallas_call(
        _norm_proj_kernel,
        grid=(n // tm,),
        in_specs=[pl.BlockSpec((tm, d), lambda i: (i, 0)),
                  pl.BlockSpec((1, d), lambda i: (0, 0)),
                  pl.BlockSpec((d, ng), lambda i: (0, 0), pipeline_mode=once),
                  pl.BlockSpec((d, nm), lambda i: (0, 0), pipeline_mode=once)],
        out_specs=[pl.BlockSpec((tm, nm), lambda i: (i, 0)),
                   pl.BlockSpec((tm, ng), lambda i: (i, 0))],
        out_shape=[jax.ShapeDtypeStruct((n, nm), BF16), jax.ShapeDtypeStruct((n, ng), F32)],
        compiler_params=_cparams(("parallel",)),
        name="norm_proj",
    )(h2, g, w_gate, w_main)


def _out_pe_kernel(gated, *refs):
    if gated:
        ma_ref, mb_ref, sz_ref, h_ref, p_ref, wa_ref, wb_ref, gp_ref, wg_ref, wp_ref, o_ref = refs
        sz = sz_ref[...].astype(F32)
        mb = (mb_ref[...].astype(F32) * _silu(sz)).astype(BF16)
    else:
        ma_ref, mb_ref, h_ref, p_ref, wa_ref, wb_ref, gp_ref, wg_ref, wp_ref, o_ref = refs
        mb = mb_ref[...]
    y = _dot(ma_ref[...], wa_ref[...]) + _dot(mb, wb_ref[...])
    h1 = h_ref[...] + _rms(y, gp_ref[...])
    gate = _sigmoid(_dot(h1.astype(BF16), wg_ref[...]))
    pp = _dot(p_ref[...].astype(BF16), wp_ref[...])
    o_ref[...] = h1 + gate * pp


def _out_pe(mix_a, a_map, ka, mix_b, b_map, kb, sz, sz_map, h2, p4, layer, w_a, w_b, g_post, w_gate, w_pe,
            bsz, seq, tm=512):
    n, d = h2.shape
    pe = p4.shape[-1]
    nt = seq // tm
    row = lambda b, s: (b * nt + s, 0)
    const = lambda b, s: (0, 0)
    gated = sz is not None
    in_specs = [pl.BlockSpec((tm, ka), a_map), pl.BlockSpec((tm, kb), b_map)]
    args = [mix_a, mix_b]
    if gated:
        in_specs.append(pl.BlockSpec((tm, kb), sz_map))
        args.append(sz)
    in_specs += [pl.BlockSpec((tm, d), row), pl.BlockSpec((None, None, tm, pe), lambda b, s: (layer, b, s, 0)),
                 pl.BlockSpec((ka, d), const), pl.BlockSpec((kb, d), const), pl.BlockSpec((1, d), const),
                 pl.BlockSpec((d, d), const), pl.BlockSpec((pe, d), const)]
    args += [h2, p4, w_a, w_b, g_post, w_gate, w_pe]
    return pl.pallas_call(
        functools.partial(_out_pe_kernel, gated),
        grid=(bsz, nt),
        in_specs=in_specs,
        out_specs=pl.BlockSpec((tm, d), row),
        out_shape=jax.ShapeDtypeStruct((n, d), F32),
        compiler_params=_cparams(("parallel", "arbitrary")),
        name="out_pe_gated" if gated else "out_pe",
    )(*args)


_RQ, _RK, _RV, _RG = 0, 512, 1024, 2048
_MQ, _MK, _MV, _MO, _MZ = 3072, 3584, 4096, 5120, 6144
AB_MAIN = 7168


def _head_norm(y, g):
    yc = y - jnp.mean(y, -1, keepdims=True)
    return yc * lax.rsqrt(jnp.mean(yc * yc, -1, keepdims=True) + EPS) * g


def _ab_kernel(pm_ref, pg_ref, cw_ref, cb_ref, gb_ref, rn_ref, mn_ref, mix_ref,
               ext_ref, r_ref, c_ref, n_ref, m_ref):
    L = CHUNK
    c_idx = pl.program_id(1)

    @pl.when(c_idx == 0)
    def _():
        ext_ref[0:8, :] = jnp.zeros((8, ext_ref.shape[1]), F32)
        r_ref[...] = jnp.zeros(r_ref.shape, F32)
        c_ref[...] = jnp.zeros(c_ref.shape, F32)
        n_ref[...] = jnp.zeros(n_ref.shape, F32)
        m_ref[...] = jnp.full(m_ref.shape, NEG, F32)

    row = lax.broadcasted_iota(jnp.int32, (L, L), 0)
    col = lax.broadcasted_iota(jnp.int32, (L, L), 1)
    causal = row >= col
    diff = (row - col).astype(F32)
    tcol = lax.broadcasted_iota(jnp.int32, (L, 1), 0).astype(F32)

    for h in range(RET_HEADS):
        lg = math.log1p(-(2.0 ** (-5.0 - h)))
        scale = RET_DK ** -0.5
        decay = jnp.where(causal, jnp.exp(jnp.maximum(diff, 0.0) * lg), 0.0) * scale
        q = pm_ref[:, _RQ + h * RET_DK:_RQ + (h + 1) * RET_DK]
        k = pm_ref[:, _RK + h * RET_DK:_RK + (h + 1) * RET_DK]
        v = pm_ref[:, _RV + h * RET_DV:_RV + (h + 1) * RET_DV]
        sc = _dot_nt(q, k) * decay
        xi = jnp.exp((tcol + 1.0) * lg) * scale
        qx = (q.astype(F32) * xi).astype(BF16)
        r_prev = r_ref[h]
        y = _dot(sc.astype(BF16), v) + _dot(qx, r_prev.astype(BF16))
        zeta = jnp.exp((L - 1.0 - tcol) * lg)
        kz_t = (k.astype(F32) * zeta).T.astype(BF16)
        r_ref[h] = r_prev * math.exp(L * lg) + _dot(kz_t, v)
        g = pm_ref[:, _RG + h * RET_DV:_RG + (h + 1) * RET_DV].astype(F32)
        out = _head_norm(y, rn_ref[:, h * RET_DV:(h + 1) * RET_DV]) * _silu(g)
        mix_ref[:, h * RET_DV:(h + 1) * RET_DV] = out.astype(mix_ref.dtype)

    nqk = 2 * ML_HEADS * ML_DK
    x = pm_ref[:, _MQ:_MQ + nqk].astype(F32)
    ext_ref[8:8 + L, :] = x
    acc = cb_ref[...] + cw_ref[ML_CONV - 1:ML_CONV, :] * x
    for j in range(ML_CONV - 1):
        acc = acc + cw_ref[j:j + 1, :] * ext_ref[pl.ds(8 - (ML_CONV - 1) + j, L), :]
    ext_ref[0:8, :] = x[L - 8:L, :]
    qk = _silu(acc)

    gt = (pg_ref[...] + gb_ref[...]).T
    g8 = gt[0:8, :]
    logf8 = jnp.minimum(g8, 0.0) - jnp.log(1.0 + jnp.exp(-jnp.abs(g8)))
    triu = jnp.where(row <= col, 1.0, 0.0).astype(BF16)
    bcum8 = _dot_f32_lhs(logf8, triu)
    b_t = pltpu.roll(bcum8, 4, axis=0)
    b_last = b_t[:, L - 1:L]
    a_t = b_last - b_t + g8
    mu = jnp.max(a_t, axis=1, keepdims=True)
    w_t = jnp.exp(a_t - mu)
    r_t = g8 - b_t
    zpad = jnp.zeros((L - 8, L), F32)
    bcol = jnp.concatenate([b_t, zpad], axis=0).T
    wcol = jnp.concatenate([w_t, zpad], axis=0).T
    m_prev = m_ref[...][:, 0:1]
    m_new = jnp.maximum(b_last + m_prev, mu)
    sp = jnp.exp(b_last + m_prev - m_new)
    scn = jnp.exp(mu - m_new)
    m_ref[...] = jnp.broadcast_to(m_new, m_ref.shape)

    for h in range(ML_HEADS):
        bc = bcol[:, h:h + 1]
        log_d = jnp.where(causal, bc + r_t[h:h + 1, :], NEG)
        inter = bc + m_prev[h:h + 1, :]
        m_t = jnp.maximum(inter, jnp.max(log_d, axis=1, keepdims=True))
        dmat = jnp.exp(log_d - m_t)
        w_int = jnp.exp(inter - m_t)
        qh = qk[:, h * ML_DK:(h + 1) * ML_DK]
        kh = qk[:, ML_HEADS * ML_DK + h * ML_DK:ML_HEADS * ML_DK + (h + 1) * ML_DK] * (ML_DK ** -0.5)
        qb = qh.astype(BF16)
        v = pm_ref[:, _MV + h * ML_DV:_MV + (h + 1) * ML_DV]
        s = _dot_nt(qb, kh.astype(BF16)) * dmat
        c_prev = c_ref[h]
        n_prev = n_ref[h:h + 1, :]
        num = _dot(s.astype(BF16), v) + w_int * _dot(qb, c_prev.astype(BF16))
        den = jnp.sum(s, axis=1, keepdims=True) + w_int * jnp.sum(qh * n_prev, axis=1, keepdims=True)
        den = jnp.maximum(jnp.abs(den), jnp.exp(-m_t))
        hcell = num * (1.0 / den)
        o = pm_ref[:, _MO + h * ML_DV:_MO + (h + 1) * ML_DV].astype(F32)
        y = _sigmoid(o) * hcell
        z = pm_ref[:, _MZ + h * ML_DV:_MZ + (h + 1) * ML_DV].astype(F32)
        out = _head_norm(y, mn_ref[:, h * ML_DV:(h + 1) * ML_DV]) * _silu(z)
        mix_ref[:, RET_HEADS * RET_DV + h * ML_DV:RET_HEADS * RET_DV + (h + 1) * ML_DV] = out.astype(mix_ref.dtype)
        kw = kh * wcol[:, h:h + 1]
        kv = _dot(kw.T.astype(BF16), v)
        ksum = jnp.sum(kw, axis=0, keepdims=True)
        sp_h = sp[h:h + 1, :]
        sc_h = scn[h:h + 1, :]
        c_ref[h] = sp_h * c_prev + sc_h * kv
        n_ref[h:h + 1, :] = sp_h * n_prev + sc_h * ksum


def _ab_mixer(pm, pg, conv_w, conv_b, gate_b, ret_norm, ml_norm, bsz, seq):
    n = pm.shape[0]
    nc = seq // CHUNK
    nqk = 2 * ML_HEADS * ML_DK
    row = lambda b, c: (b * nc + c, 0)
    const = lambda b, c: (0, 0)
    nmix = RET_HEADS * RET_DV + ML_HEADS * ML_DV
    return pl.pallas_call(
        _ab_kernel,
        grid=(bsz, nc),
        in_specs=[pl.BlockSpec((CHUNK, AB_MAIN), row), pl.BlockSpec((CHUNK, LANES), row),
                  pl.BlockSpec((ML_CONV, nqk), const), pl.BlockSpec((1, nqk), const),
                  pl.BlockSpec((1, LANES), const),
                  pl.BlockSpec((1, RET_HEADS * RET_DV), const), pl.BlockSpec((1, ML_HEADS * ML_DV), const)],
        out_specs=pl.BlockSpec((CHUNK, nmix), row),
        out_shape=jax.ShapeDtypeStruct((n, nmix), BF16),
        scratch_shapes=[pltpu.VMEM((CHUNK + 8, nqk), F32),
                        pltpu.VMEM((RET_HEADS, RET_DK, RET_DV), F32),
                        pltpu.VMEM((ML_HEADS, ML_DK, ML_DV), F32),
                        pltpu.VMEM((8, ML_DK), F32),
                        pltpu.VMEM((8, LANES), F32)],
        compiler_params=_cparams(("parallel", "arbitrary")),
        name="ab_mixer",
    )(pm, pg, conv_w, conv_b, gate_b, ret_norm, ml_norm)


_Q0 = 0
_KC0, _VC0, _KS0, _VS0, _KW0, _VW0 = 16, 20, 24, 28, 32, 36
_NZ0 = 40 * LANES
_SU0 = _NZ0 + NSA_HEADS * NSA_DH
_SZ0 = _SU0 + S5_WIDTH
CD_MAIN = _SZ0 + S5_WIDTH
SLOTW = NSA_GROUPS * LANES
LOG2E = math.log2(math.e)


def _compress_kernel(kx_ref, vx_ref, pk_ref, pv_ref, w1k_ref, w1v_ref, w2k_ref, w2vt_ref, ekc_ref,
                     ko_ref, vot_ref, xs_ref):
    seq = kx_ref.shape[0]
    nb = seq // CMP_STRIDE
    for kind, (x_ref, p_ref, w1_ref) in enumerate(((kx_ref, pk_ref, w1k_ref), (vx_ref, pv_ref, w1v_ref))):
        for g in range(NSA_GROUPS):
            xs_ref[...] = x_ref[:, g * LANES:(g + 1) * LANES].astype(F32)
            a = jnp.zeros((nb, CMP_HIDDEN), F32)
            b = jnp.zeros((nb, CMP_HIDDEN), F32)
            for r in range(CMP_STRIDE):
                xr = xs_ref[pl.ds(r, nb, stride=CMP_STRIDE), :]
                a = a + _dot((xr + p_ref[r:r + 1, :]).astype(BF16), w1_ref[r])
                b = b + _dot((xr + p_ref[CMP_STRIDE + r:CMP_STRIDE + r + 1, :]).astype(BF16),
                             w1_ref[CMP_STRIDE + r])
            h1 = a + pltpu.roll(b, nb - 1, axis=0)
            hb = _gelu(h1).astype(BF16)
            if kind == 0:
                ko_ref[g] = (_dot(hb, w2k_ref[...]) + ekc_ref[...]).astype(ko_ref.dtype)
            else:
                vot_ref[g] = _dot_nt(w2vt_ref[...], hb).astype(vot_ref.dtype)


def _compress(pm3, pos_k, pos_v, w1k, w1v, w2k, w2vt, ekc):
    bsz, seq, _ = pm3.shape
    nb = seq // CMP_STRIDE
    c3 = lambda b: (0, 0, 0)
    c2 = lambda b: (0, 0)
    return pl.pallas_call(
        _compress_kernel,
        grid=(bsz,),
        in_specs=[pl.BlockSpec((None, seq, SLOTW), lambda b: (b, 0, _KC0 * LANES // SLOTW)),
                  pl.BlockSpec((None, seq, SLOTW), lambda b: (b, 0, _VC0 * LANES // SLOTW)),
                  pl.BlockSpec((CMP_LEN, LANES), c2), pl.BlockSpec((CMP_LEN, LANES), c2),
                  pl.BlockSpec((CMP_LEN, LANES, CMP_HIDDEN), c3), pl.BlockSpec((CMP_LEN, LANES, CMP_HIDDEN), c3),
                  pl.BlockSpec((CMP_HIDDEN, LANES), c2), pl.BlockSpec((LANES, CMP_HIDDEN), c2),
                  pl.BlockSpec((nb, LANES), c2)],
        out_specs=[pl.BlockSpec((None, NSA_GROUPS, nb, LANES), lambda b: (b, 0, 0, 0)),
                   pl.BlockSpec((None, NSA_GROUPS, LANES, nb), lambda b: (b, 0, 0, 0))],
        out_shape=[jax.ShapeDtypeStruct((bsz, NSA_GROUPS, nb, LANES), BF16),
                   jax.ShapeDtypeStruct((bsz, NSA_GROUPS, LANES, nb), BF16)],
        scratch_shapes=[pltpu.VMEM((seq, LANES), F32)],
        compiler_params=_cparams(("parallel",)),
        name="nsa_compress",
    )(pm3, pm3, pos_k, pos_v, w1k, w1v, w2k, w2vt, ekc)


def _nsa_kernel(q_ref, kc_ref, vct_ref, ks_ref, vs_ref, kw_ref, vw_ref, nz_ref, pg_ref,
                ek_ref, ev_ref, al_ref, ovt_ref, y_ref,
                ksa_ref, vsat_ref, kwa_ref, vwat_ref):
    seq = ks_ref.shape[0]
    g_idx = pl.program_id(1)
    qi = pl.program_id(2)
    t0 = qi * TQ
    wtiles = WINDOW // LANES

    @pl.when(qi == 0)
    def _():
        ek = ek_ref[...]
        ev = ev_ref[...]
        ksa_ref[...] = ks_ref[...] + ek
        lane_k = lax.broadcasted_iota(jnp.int32, (seq, LANES), 1)
        ek_w = jnp.where(lane_k >= ALI_LANE0, ek, jnp.zeros_like(ek))
        kwa_ref[0:WINDOW, :] = jnp.zeros((WINDOW, LANES), BF16)
        kwa_ref[WINDOW:WINDOW + seq, :] = kw_ref[...] + ek_w
        for i in range(wtiles):
            vwat_ref[i] = jnp.zeros((LANES, LANES), BF16)
        for i in range(seq // LANES):
            rws = slice(i * LANES, (i + 1) * LANES)
            vsat_ref[i] = (vs_ref[rws, :] + ev).astype(F32).T.astype(BF16)
            vwat_ref[wtiles + i] = (vw_ref[rws, :] + ev).astype(F32).T.astype(BF16)

    nsel = seq // SEL_BLOCK
    nk_dyn = (t0 + TQ + KCH - 1) // KCH
    for nk in range(1, seq // KCH + 1):
        pl.when(nk_dyn == nk)(functools.partial(
            _nsa_tile, nk, nsel, qi, g_idx, q_ref, kc_ref, vct_ref, nz_ref, pg_ref, al_ref, ovt_ref,
            y_ref, ksa_ref, vsat_ref, kwa_ref, vwat_ref))


def _nsa_tile(nk, nsel, qi, g_idx, q_ref, kc_ref, vct_ref, nz_ref, pg_ref, al_ref, ovt_ref,
              y_ref, ksa_ref, vsat_ref, kwa_ref, vwat_ref):
    hp = NSA_HPG
    rows = hp * TQ
    span = WINDOW + TQ
    t0 = qi * TQ
    tw = pl.multiple_of(t0, TQ)
    jrow = lax.broadcasted_iota(jnp.int32, (LANES, rows), 0)
    tq = lax.broadcasted_iota(jnp.int32, (LANES, rows), 1) & (TQ - 1)
    kmt = jrow - tq

    qs = jnp.concatenate([q_ref[:, hh * LANES:(hh + 1) * LANES] for hh in range(hp)], axis=0)
    al = jnp.concatenate([jnp.broadcast_to(al_ref[pl.ds(g_idx * hp + hh, 1), :], (TQ, LANES))
                          for hh in range(hp)], axis=0)
    qal = qs.astype(F32) + al
    qalb = qal.astype(BF16)
    s_c = _dot_nt(kc_ref[...], qalb)
    s_w = _dot_nt(kwa_ref[pl.ds(tw, span), :], qalb)

    s = jnp.where(tq - jrow * CMP_STRIDE >= (CMP_LEN - 1) - t0, s_c, NEG)
    e = jnp.exp2(s - jnp.max(s, axis=0, keepdims=True))
    inv = 1.0 / jnp.sum(e, axis=0, keepdims=True)
    if nk == 1:
        inv = jnp.where(t0 + tq[0:1, :] >= CMP_LEN - 1, inv, 0.0)
    p = e * inv
    oc_t = _dot(vct_ref[...], p.astype(BF16))
    psum_t = p[:, 0:TQ]
    for hh in range(1, hp):
        psum_t = psum_t + p[:, hh * TQ:(hh + 1) * TQ]
    p_hi = psum_t.astype(BF16)
    p_lo = (psum_t - p_hi.astype(F32)).astype(BF16)
    imp_t = _dot(ovt_ref[...], p_hi) + _dot(ovt_ref[...], p_lo)

    blocks = [s_w[i * LANES:(i + 1) * LANES, :] for i in range(span // LANES)]
    for i in range(span // LANES):
        if i * LANES < TQ:
            blocks[i] = jnp.where(kmt > -i * LANES, blocks[i], NEG)
        if (i + 1) * LANES > WINDOW:
            blocks[i] = jnp.where(kmt <= WINDOW - i * LANES, blocks[i], NEG)
        if nk == 1:
            blocks[i] = jnp.where(jrow >= WINDOW - t0 - i * LANES, blocks[i], NEG)
    s = jnp.concatenate(blocks, axis=0)
    p = jnp.exp2(s - jnp.max(s, axis=0, keepdims=True)).astype(BF16)
    vw_t = jnp.concatenate([vwat_ref[qi * (TQ // LANES) + i] for i in range(span // LANES)], axis=1)
    accw_t = _dot(vw_t, p)

    score = imp_t[SEL_LANE0:SEL_LANE0 + nsel, :]
    nrow = lax.broadcasted_iota(jnp.int32, (nsel, TQ), 0)
    own = (t0 + lax.broadcasted_iota(jnp.int32, (nsel, TQ), 1)) >> int(math.log2(SEL_BLOCK))
    valid = nrow <= own
    forced = (nrow == 0) | (nrow == own)
    score = jnp.where(forced, BIG, jnp.where(valid, score, -1.0))
    chosen = jnp.zeros((nsel, TQ), F32)
    nrow_f = nrow.astype(F32)
    for _ in range(SEL_TOPN):
        mx = jnp.max(score, axis=0, keepdims=True)
        first = jnp.min(jnp.where(score == mx, nrow_f, 4.0 * LANES), axis=0, keepdims=True)
        hit = nrow_f == first
        chosen = jnp.where(hit, 1.0, chosen)
        score = jnp.where(hit, -3.0, score)
    mask_t = jnp.where(chosen > 0.0, jnp.where(valid, 0.0, NEG), NEG)
    maskcols = jnp.concatenate([jnp.zeros((SEL_LANE0, TQ), F32), mask_t,
                                jnp.zeros((LANES - SEL_LANE0 - nsel, TQ), F32)], axis=0).T

    qa = (qal + jnp.concatenate([maskcols] * hp, axis=0)).astype(BF16)

    def scores(c):
        s = _dot_nt(ksa_ref[c * KCH:(c + 1) * KCH, :], qa)
        if c == nk - 1:
            s = jnp.concatenate(
                [jnp.where(kmt <= t0 - c * KCH - i * LANES, s[i * LANES:(i + 1) * LANES, :], NEG)
                 for i in range(KCH // LANES)], axis=0)
        return s

    def weighted(c, s):
        mc = jnp.max(s, axis=0, keepdims=True)
        v_t = jnp.concatenate([vsat_ref[c * (KCH // LANES) + i] for i in range(KCH // LANES)], axis=1)
        return mc, _dot(v_t, jnp.exp2(s - mc).astype(BF16))

    parts = []
    s_next = scores(0)
    for c in range(nk):
        s_cur = s_next
        if c + 1 < nk:
            s_next = scores(c + 1)
        parts.append(weighted(c, s_cur))
    m = parts[0][0]
    for mc, _ in parts[1:]:
        m = jnp.maximum(m, mc)
    accs_t = None
    for mc, d in parts:
        d = d if nk == 1 else jnp.exp2(mc - m) * d
        accs_t = d if accs_t is None else accs_t + d

    gt_t = _sigmoid(pg_ref[...]).T
    outs = []
    for hh in range(hp):
        g_c, g_s, g_w = [gt_t[j * hp + hh:j * hp + hh + 1, :] for j in range(3)]
        hcols = slice(hh * TQ, (hh + 1) * TQ)
        as_t = accs_t[:, hcols]
        aw_t = accw_t[:, hcols]
        o = (g_c * oc_t[:, hcols]
             + (g_s * (1.0 / as_t[NSA_DH:NSA_DH + 1, :])) * as_t
             + (g_w * (1.0 / aw_t[NSA_DH:NSA_DH + 1, :])) * aw_t)
        outs.append(o[0:NSA_DH, :])
    for pr in range(hp // 2):
        packed = jnp.concatenate([outs[2 * pr], outs[2 * pr + 1]], axis=0).T
        z = nz_ref[:, pr * LANES:(pr + 1) * LANES].astype(F32)
        y_ref[:, pr * LANES:(pr + 1) * LANES] = (packed * _silu(z)).astype(y_ref.dtype)


def _nsa(pm3, pg3, kcc, vct, ek, ev, al, ovt):
    bsz, seq, _ = pm3.shape
    nq = seq // TQ
    gw = NSA_HPG * NSA_DH
    kv = lambda slot0: pl.BlockSpec((None, seq, LANES), lambda b, g, q: (b, 0, slot0 + g))
    c2 = lambda b, g, q: (0, 0)
    return pl.pallas_call(
        _nsa_kernel,
        grid=(bsz, NSA_GROUPS, nq),
        in_specs=[pl.BlockSpec((None, TQ, NSA_HPG * LANES), lambda b, g, q: (b, q, g)),
                  pl.BlockSpec((None, None, LANES, LANES), lambda b, g, q: (b, g, 0, 0)),
                  pl.BlockSpec((None, None, LANES, LANES), lambda b, g, q: (b, g, 0, 0)),
                  kv(_KS0), kv(_VS0), kv(_KW0), kv(_VW0),
                  pl.BlockSpec((None, TQ, gw), lambda b, g, q: (b, q, _NZ0 // gw + g)),
                  pl.BlockSpec((None, TQ, LANES), lambda b, g, q: (b, q, g)),
                  pl.BlockSpec((seq, LANES), c2), pl.BlockSpec((1, LANES), c2),
                  pl.BlockSpec((NSA_HEADS, LANES), c2),
                  pl.BlockSpec((LANES, LANES), c2)],
        out_specs=pl.BlockSpec((None, TQ, gw), lambda b, g, q: (b, q, g)),
        out_shape=jax.ShapeDtypeStruct((bsz, seq, NSA_HEADS * NSA_DH), BF16),
        scratch_shapes=[pltpu.VMEM((seq, LANES), BF16), pltpu.VMEM((seq // LANES, LANES, LANES), BF16),
                        pltpu.VMEM((seq + WINDOW, LANES), BF16),
                        pltpu.VMEM(((seq + WINDOW) // LANES, LANES, LANES), BF16)],
        compiler_params=_cparams(("parallel", "arbitrary", "arbitrary")),
        name="nsa_attn",
    )(pm3, kcc, vct, pm3, pm3, pm3, pm3, pm3, pg3, ek, ev, al, ovt)


def _nsa_tables(seq):
    t = np.arange(seq)
    ek = np.zeros((seq, LANES), np.float32)
    ek[t, SEL_LANE0 + t // SEL_BLOCK] = 1.0
    for c in range(3):
        ek[:, ALI_LANE0 + c] = t // SEL_BLOCK
        ek[:, ALI_LANE0 + 3 + c] = t % SEL_BLOCK
    ev = np.zeros((1, LANES), np.float32)
    ev[0, NSA_DH] = 1.0
    slopes = jnp.exp2(-8.0 * jnp.arange(1, NSA_HEADS + 1, dtype=F32) / NSA_HEADS) * LOG2E
    s_hi, s_mid, s_lo = _split3(slopes)
    parts = [p.astype(F32) for p in (s_hi, s_mid, s_lo)]
    al = jnp.zeros((NSA_HEADS, LANES), F32)
    for c in range(3):
        al = al.at[:, ALI_LANE0 + c].set(parts[c] * SEL_BLOCK)
        al = al.at[:, ALI_LANE0 + 3 + c].set(parts[c])
    nb = seq // CMP_STRIDE
    cstart = np.arange(nb) * CMP_STRIDE
    cend = cstart + CMP_LEN - 1
    ekc = np.zeros((nb, LANES), np.float32)
    for c in range(3):
        ekc[:, ALI_LANE0 + c] = cend // SEL_BLOCK
        ekc[:, ALI_LANE0 + 3 + c] = cend % SEL_BLOCK
    sel = np.arange(seq // SEL_BLOCK)
    ovt = np.zeros((LANES, max(nb, LANES)), np.float32)
    ovl = (cstart[:, None] < (sel[None, :] + 1) * SEL_BLOCK) & (cstart[:, None] + CMP_LEN > sel[None, :] * SEL_BLOCK)
    ovl[nb - 1, :] = False
    ovt[SEL_LANE0:SEL_LANE0 + len(sel), :nb] = ovl.T
    return jnp.asarray(ek, BF16), jnp.asarray(ev, BF16), al, jnp.asarray(ekc), jnp.asarray(ovt, BF16)


def _s5_disc_kernel(are_ref, aim_ref, ldt_ref, bre_ref, bim_ref, abr_ref, abi_ref, bbr_ref, bbi_ref):
    a_re = are_ref[...]
    a_im = aim_ref[...]
    dt = jnp.exp(ldt_ref[...])
    er = jnp.exp(a_re * dt)
    abar_re = er * jnp.cos(a_im * dt)
    abar_im = er * jnp.sin(a_im * dt)
    lam2 = a_re * a_re + a_im * a_im
    cr = ((abar_re - 1.0) * a_re + abar_im * a_im) / lam2
    ci = (abar_im * a_re - (abar_re - 1.0) * a_im) / lam2
    b_re = bre_ref[...]
    b_im = bim_ref[...]
    abr_ref[...] = abar_re
    abi_ref[...] = abar_im
    bbr_ref[...] = cr * b_re - ci * b_im
    bbi_ref[...] = cr * b_im + ci * b_re


def _s5_disc(a_re_rep, a_im_rep, ldt_rep, b_re_t, b_im_t):
    shp = jax.ShapeDtypeStruct(a_re_rep.shape, F32)
    return pl.pallas_call(_s5_disc_kernel, out_shape=[shp] * 4, name="s5_disc")(
        a_re_rep, a_im_rep, ldt_rep, b_re_t, b_im_t)


S5_SLABG = 4
S5_TT = 64
S5_PITCH = S5_TT + 4


def _s5_kernel(u_ref, bre_ref, bim_ref, ar_ref, ai_ref, cre_ref, cim_ref, d_ref, wg_ref, o_ref,
               xr_ref, xi_ref, sr_ref, si_ref):
    nb, tt, wd = u_ref.shape
    pitch = sr_ref.shape[1] // nb
    gcols = S5_SLABG * LANES
    ngroups = S5_NSTATE // gcols

    @pl.when(pl.program_id(0) == 0)
    def _():
        xr_ref[...] = jnp.zeros(xr_ref.shape, F32)
        xi_ref[...] = jnp.zeros(xi_ref.shape, F32)

    u = u_ref[...].reshape(nb * tt, wd)

    for kg in range(ngroups):
        cols = slice(kg * gcols, (kg + 1) * gcols)
        br = _dot(u, bre_ref[:, cols])
        bi = _dot(u, bim_ref[:, cols])
        for j in range(S5_SLABG):
            for b in range(nb):
                sr_ref[kg * S5_SLABG + j, b * pitch:b * pitch + tt, :] = br[b * tt:(b + 1) * tt, j * LANES:(j + 1) * LANES]
                si_ref[kg * S5_SLABG + j, b * pitch:b * pitch + tt, :] = bi[b * tt:(b + 1) * tt, j * LANES:(j + 1) * LANES]

    for kg in range(ngroups):
        cols = slice(kg * gcols, (kg + 1) * gcols)
        slabs = range(kg * S5_SLABG, (kg + 1) * S5_SLABG)
        ar = ar_ref[:, cols]
        ai = ai_ref[:, cols]

        def body(t, carry, slabs=slabs, ar=ar, ai=ai):
            xr, xi = carry
            rows = pl.ds(t, nb, stride=pitch)
            nxr = ar * xr - ai * xi + jnp.concatenate([sr_ref[k, rows, :] for k in slabs], axis=1)
            nxi = ar * xi + ai * xr + jnp.concatenate([si_ref[k, rows, :] for k in slabs], axis=1)
            for j, k in enumerate(slabs):
                sr_ref[k, rows, :] = nxr[:, j * LANES:(j + 1) * LANES]
                si_ref[k, rows, :] = nxi[:, j * LANES:(j + 1) * LANES]
            return nxr, nxi

        xr, xi = lax.fori_loop(0, tt, body, (xr_ref[:, cols], xi_ref[:, cols]), unroll=4)
        xr_ref[:, cols] = xr
        xi_ref[:, cols] = xi

    y = jnp.zeros((nb * tt, wd), F32)
    for kg in range(ngroups):
        cols = slice(kg * gcols, (kg + 1) * gcols)
        gather = lambda ref: jnp.concatenate(
            [jnp.concatenate([ref[kg * S5_SLABG + j, b * pitch:b * pitch + tt, :] for j in range(S5_SLABG)], axis=1)
             for b in range(nb)], axis=0).astype(BF16)
        y = y + _dot(gather(sr_ref), cre_ref[cols, :]) - _dot(gather(si_ref), cim_ref[cols, :])
    y = _gelu(y + d_ref[...] * u.astype(F32))
    z = _dot(y.astype(BF16), wg_ref[...])
    o = z[:, :S5_WIDTH] * _sigmoid(z[:, S5_WIDTH:])
    o_ref[...] = o.reshape(nb, tt, wd).astype(o_ref.dtype)


def _s5(pm3, bbd_re, bbd_im, ar8, ai8, cbd_re, cbd_im, d_row, w_glu):
    bsz, seq, _ = pm3.shape
    tt = S5_TT
    c2 = lambda i: (0, 0)
    nslab = S5_NSTATE // LANES
    return pl.pallas_call(
        _s5_kernel,
        grid=(seq // tt,),
        in_specs=[pl.BlockSpec((bsz, tt, S5_WIDTH), lambda i: (0, i, _SU0 // S5_WIDTH)),
                  pl.BlockSpec((S5_WIDTH, S5_NSTATE), c2), pl.BlockSpec((S5_WIDTH, S5_NSTATE), c2),
                  pl.BlockSpec((8, S5_NSTATE), c2), pl.BlockSpec((8, S5_NSTATE), c2),
                  pl.BlockSpec((S5_NSTATE, S5_WIDTH), c2), pl.BlockSpec((S5_NSTATE, S5_WIDTH), c2),
                  pl.BlockSpec((1, S5_WIDTH), c2), pl.BlockSpec((S5_WIDTH, 2 * S5_WIDTH), c2)],
        out_specs=pl.BlockSpec((bsz, tt, S5_WIDTH), lambda i: (0, i, 0)),
        out_shape=jax.ShapeDtypeStruct((bsz, seq, S5_WIDTH), BF16),
        scratch_shapes=[pltpu.VMEM((8, S5_NSTATE), F32), pltpu.VMEM((8, S5_NSTATE), F32),
                        pltpu.VMEM((nslab, bsz * S5_PITCH, LANES), F32),
                        pltpu.VMEM((nslab, bsz * S5_PITCH, LANES), F32)],
        compiler_params=_cparams(("arbitrary",)),
        name="s5_scan",
    )(pm3, bbd_re, bbd_im, ar8, ai8, cbd_re, cbd_im, d_row, w_glu)


def _pad_cols(w, n):
    return jnp.pad(w, ((0, 0), (0, n - w.shape[1])))


def _slot_cols(w, width):
    d, n = w.shape
    k = n // width
    return jnp.pad(w.reshape(d, k, width), ((0, 0), (0, 0), (0, LANES - width))).reshape(d, k * LANES)


def _layer0(h2, p4, layer, bsz, seq, norm_pre, norm_post, pe_proj, pe_gate,
            w_in, ret_norm, conv_w, conv_b, gate_b, ml_norm, w_out):
    d = h2.shape[1]
    g0 = 5120
    w_main = jnp.concatenate([w_in[:, :g0], w_in[:, g0 + 2 * ML_HEADS:]], axis=1).astype(BF16)
    w_gate = _pad_cols(w_in[:, g0:g0 + 2 * ML_HEADS], LANES).astype(BF16)
    pm, pg = _norm_proj(h2, norm_pre.reshape(1, d), w_gate, w_main)
    mix = _ab_mixer(pm, pg, conv_w, conv_b.reshape(1, -1), _pad_cols(gate_b.reshape(1, -1), LANES),
                    ret_norm.reshape(1, -1), ml_norm.reshape(1, -1), bsz, seq)
    ka = RET_HEADS * RET_DV
    nt = seq // 512
    return _out_pe(mix, lambda b, s: (b * nt + s, 0), ka, mix, lambda b, s: (b * nt + s, 1), ML_HEADS * ML_DV,
                   None, None, h2, p4, layer, w_out[:ka].astype(BF16), w_out[ka:].astype(BF16),
                   norm_post.reshape(1, d), pe_gate.astype(BF16), pe_proj.astype(BF16), bsz, seq)


def _layer1(h2, p4, layer, bsz, seq, norm_pre, norm_post, pe_proj, pe_gate,
            w_in, pos_k, pos_v, w1_k, w2_k, w1_v, w2_v,
            a_re, a_im, log_dt, b_re, b_im, c_re, c_im, d_skip, w_glu, w_out):
    d = h2.shape[1]
    nw, nkv = NSA_HEADS * NSA_DH, NSA_GROUPS * NSA_DH
    offs = np.cumsum((0, nw, nkv, nkv, nkv, nkv, nkv, nkv, 3 * NSA_HEADS, nw, S5_WIDTH, S5_WIDTH))
    seg = lambda i: w_in[:, offs[i]:offs[i + 1]]
    w_main = jnp.concatenate(
        [_slot_cols(seg(0) * (NSA_DH ** -0.5 * LOG2E), NSA_DH)] + [_slot_cols(seg(i), NSA_DH) for i in range(1, 7)]
        + [seg(8), seg(9), seg(10)], axis=1).astype(BF16)
    w_gate = seg(7).reshape(d, NSA_GROUPS, NSA_HPG, 3).transpose(0, 1, 3, 2).reshape(d, NSA_GROUPS, 3 * NSA_HPG)
    w_gate = jnp.pad(w_gate, ((0, 0), (0, 0), (0, LANES - 3 * NSA_HPG))).reshape(d, NSA_GROUPS * LANES).astype(BF16)
    pm, pg = _norm_proj(h2, norm_pre.reshape(1, d), w_gate, w_main)

    nb = seq // CMP_STRIDE
    pm3 = pm.reshape(bsz, seq, CD_MAIN)
    padl = lambda a: jnp.pad(a, ((0, 0),) * (a.ndim - 1) + ((0, LANES - a.shape[-1]),))
    w1 = lambda w: jnp.pad(w.reshape(CMP_LEN, NSA_DH, CMP_HIDDEN), ((0, 0), (0, LANES - NSA_DH), (0, 0))).astype(BF16)
    ek, ev, al, ekc, ovt = _nsa_tables(seq)
    kcc, vct = _compress(pm3, padl(pos_k), padl(pos_v), w1(w1_k), w1(w1_v),
                         padl(w2_k).astype(BF16), padl(w2_v).T.astype(BF16), ekc)
    if nb < LANES:
        kcc = jnp.pad(kcc, ((0, 0), (0, 0), (0, LANES - nb), (0, 0)))
        vct = jnp.pad(vct, ((0, 0), (0, 0), (0, 0), (0, LANES - nb)))
    y_nsa = _nsa(pm3, pg.reshape(bsz, seq, NSA_GROUPS * LANES), kcc, vct, ek, ev, al, ovt)

    rep = lambda a: jnp.repeat(a, S5_GROUP_CH, axis=0)
    ldt = jnp.broadcast_to(rep(log_dt[:, None]), (S5_WIDTH, S5_STATE))
    bt = lambda b: b.transpose(0, 2, 1).reshape(S5_WIDTH, S5_STATE)
    abr, abi, bbr, bbi = _s5_disc(rep(a_re), rep(a_im), ldt, bt(b_re), bt(b_im))
    eye = jnp.eye(S5_GROUPS, dtype=F32)
    bd_in = lambda bb: (bb.reshape(S5_GROUPS, S5_GROUP_CH, 1, S5_STATE)
                        * eye[:, None, :, None]).reshape(S5_WIDTH, S5_NSTATE).astype(BF16)
    bd_out = lambda c: (c.transpose(0, 2, 1).reshape(S5_GROUPS, S5_STATE, 1, S5_GROUP_CH)
                        * eye[:, None, :, None]).reshape(S5_NSTATE, S5_WIDTH).astype(BF16)
    row8 = lambda a: jnp.broadcast_to(a[::S5_GROUP_CH].reshape(1, S5_NSTATE), (8, S5_NSTATE))
    y_s5 = _s5(pm3, bd_in(bbr), bd_in(bbi), row8(abr), row8(abi),
               bd_out(c_re), bd_out(c_im), d_skip.reshape(1, -1), w_glu.astype(BF16))

    nt = seq // 512
    szb = _SZ0 // S5_WIDTH
    return _out_pe(y_nsa.reshape(bsz * seq, nw), lambda b, s: (b * nt + s, 0), nw,
                   y_s5.reshape(bsz * seq, S5_WIDTH), lambda b, s: (b * nt + s, 0), S5_WIDTH,
                   pm, lambda b, s: (b * nt + s, szb),
                   h2, p4, layer, w_out[:nw].astype(BF16), w_out[nw:].astype(BF16),
                   norm_post.reshape(1, d), pe_gate.astype(BF16), pe_proj.astype(BF16), bsz, seq)


def kernel(x, p, norm_pre, norm_post, pe_proj, pe_gate, ab_w_in, ret_norm, ml_conv_w, ml_conv_b, ml_gate_b,
           ml_norm, ab_w_out, cd_w_in, cmp_pos_k, cmp_pos_v, cmp_w1_k, cmp_w2_k, cmp_w1_v, cmp_w2_v,
           s5_a_re, s5_a_im, s5_log_dt, s5_b_re, s5_b_im, s5_c_re, s5_c_im, s5_d, s5_w_glu, cd_w_out):
    bsz, seq, d = x.shape
    assert bsz == 8, "the S5 scan maps the batch onto the 8 sublanes of a vreg"
    assert seq % max(KCH, 1024) == 0
    depth = p.shape[0]
    h2 = x.reshape(bsz * seq, d)
    for i in range(depth):
        j = i // 2
        if i % 2 == 0:
            h2 = _layer0(h2, p, i, bsz, seq, norm_pre[i], norm_post[i], pe_proj[i], pe_gate[i],
                         ab_w_in[j], ret_norm[j], ml_conv_w[j], ml_conv_b[j], ml_gate_b[j], ml_norm[j], ab_w_out[j])
        else:
            h2 = _layer1(h2, p, i, bsz, seq, norm_pre[i], norm_post[i], pe_proj[i], pe_gate[i],
                         cd_w_in[j], cmp_pos_k[j], cmp_pos_v[j], cmp_w1_k[j], cmp_w2_k[j], cmp_w1_v[j], cmp_w2_v[j],
                         s5_a_re[j], s5_a_im[j], s5_log_dt[j], s5_b_re[j], s5_b_im[j], s5_c_re[j], s5_c_im[j],
                         s5_d[j], s5_w_glu[j], cd_w_out[j])
    return h2.reshape(bsz, seq, d)
```

```python
import functools
import math

import numpy as np
import jax
import jax.numpy as jnp
from jax import lax
from jax.experimental import pallas as pl
from jax.experimental.pallas import tpu as pltpu

F32 = jnp.float32
BF16 = jnp.bfloat16
EPS = 1e-6
NEG = -1e30
BIG = 1e30

LANES = 128
CHUNK = 128
RET_HEADS, RET_DK, RET_DV = 4, 128, 256
ML_HEADS, ML_DK, ML_DV = 4, 128, 256
ML_CONV = 4
NSA_HEADS, NSA_GROUPS, NSA_DH = 16, 4, 64
NSA_HPG = NSA_HEADS // NSA_GROUPS
CMP_STRIDE, CMP_LEN, CMP_HIDDEN = 16, 32, 128
SEL_BLOCK, SEL_TOPN, WINDOW = 64, 4, 512
S5_GROUPS, S5_GROUP_CH, S5_STATE = 32, 16, 64
S5_WIDTH = S5_GROUPS * S5_GROUP_CH
S5_NSTATE = S5_GROUPS * S5_STATE

TQ = 256
KCH = 512
SEL_LANE0 = 64
ALI_LANE0 = 96
VMEM_LIMIT = 56 * 1024 * 1024


def _cparams(sem):
    return pltpu.CompilerParams(dimension_semantics=sem, vmem_limit_bytes=VMEM_LIMIT)


def _dot(a, b):
    return jnp.dot(a, b, preferred_element_type=F32)


def _dot_nt(a, b):
    return lax.dot_general(a, b, (((1,), (1,)), ((), ())), preferred_element_type=F32)


def _split3(x):
    hi = x.astype(BF16)
    r1 = x - hi.astype(F32)
    mid = r1.astype(BF16)
    lo = (r1 - mid.astype(F32)).astype(BF16)
    return hi, mid, lo


def _dot_f32_lhs(x, m):
    hi, mid, lo = _split3(x)
    return _dot(hi, m) + _dot(mid, m) + _dot(lo, m)


def _sigmoid(x):
    return 1.0 / (1.0 + jnp.exp(-x))


def _silu(x):
    return x * _sigmoid(x)


def _gelu(x):
    return 0.5 * x * (1.0 + jnp.tanh(math.sqrt(2.0 / math.pi) * (x + 0.044715 * (x * x * x))))


def _rms(x, g):
    return x * lax.rsqrt(jnp.mean(x * x, -1, keepdims=True) + EPS) * g


PROJ_TN = 1024


def _norm_proj_kernel(h_ref, g_ref, wg_ref, wm_ref, om_ref, og_ref):
    ub = _rms(h_ref[...], g_ref[...]).astype(BF16)
    nm = om_ref.shape[1]
    for c0 in range(0, nm, PROJ_TN):
        c1 = min(c0 + PROJ_TN, nm)
        om_ref[:, c0:c1] = _dot(ub, wm_ref[:, c0:c1]).astype(om_ref.dtype)
    og_ref[...] = _dot(ub, wg_ref[...])


def _norm_proj(h2, g, w_gate, w_main, tm=512):
    n, d = h2.shape
    nm = w_main.shape[1]
    ng = w_gate.shape[1]
    once = pl.Buffered(1)
    return pl.pallas_call(
        _norm_proj_kernel,
        grid=(n // tm,),
        in_specs=[pl.BlockSpec((tm, d), lambda i: (i, 0)),
                  pl.BlockSpec((1, d), lambda i: (0, 0)),
                  pl.BlockSpec((d, ng), lambda i: (0, 0), pipeline_mode=once),
                  pl.BlockSpec((d, nm), lambda i: (0, 0), pipeline_mode=once)],
        out_specs=[pl.BlockSpec((tm, nm), lambda i: (i, 0)),
                   pl.BlockSpec((tm, ng), lambda i: (i, 0))],
        out_shape=[jax.ShapeDtypeStruct((n, nm), BF16), jax.ShapeDtypeStruct((n, ng), F32)],
        compiler_params=_cparams(("parallel",)),
        name="norm_proj",
    )(h2, g, w_gate, w_main)


def _out_pe_kernel(gated, *refs):
    if gated:
        ma_ref, mb_ref, sz_ref, h_ref, p_ref, wa_ref, wb_ref, gp_ref, wg_ref, wp_ref, o_ref = refs
        sz = sz_ref[...].astype(F32)
        mb = (mb_ref[...].astype(F32) * _silu(sz)).astype(BF16)
    else:
        ma_ref, mb_ref, h_ref, p_ref, wa_ref, wb_ref, gp_ref, wg_ref, wp_ref, o_ref = refs
        mb = mb_ref[...]
    y = _dot(ma_ref[...], wa_ref[...]) + _dot(mb, wb_ref[...])
    h1 = h_ref[...] + _rms(y, gp_ref[...])
    gate = _sigmoid(_dot(h1.astype(BF16), wg_ref[...]))
    pp = _dot(p_ref[...].astype(BF16), wp_ref[...])
    o_ref[...] = h1 + gate * pp


def _out_pe(mix_a, a_map, ka, mix_b, b_map, kb, sz, sz_map, h2, p4, layer, w_a, w_b, g_post, w_gate, w_pe,
            bsz, seq, tm=512):
    n, d = h2.shape
    pe = p4.shape[-1]
    nt = seq // tm
    row = lambda b, s: (b * nt + s, 0)
    const = lambda b, s: (0, 0)
    gated = sz is not None
    in_specs = [pl.BlockSpec((tm, ka), a_map), pl.BlockSpec((tm, kb), b_map)]
    args = [mix_a, mix_b]
    if gated:
        in_specs.append(pl.BlockSpec((tm, kb), sz_map))
        args.append(sz)
    in_specs += [pl.BlockSpec((tm, d), row), pl.BlockSpec((None, None, tm, pe), lambda b, s: (layer, b, s, 0)),
                 pl.BlockSpec((ka, d), const), pl.BlockSpec((kb, d), const), pl.BlockSpec((1, d), const),
                 pl.BlockSpec((d, d), const), pl.BlockSpec((pe, d), const)]
    args += [h2, p4, w_a, w_b, g_post, w_gate, w_pe]
    return pl.pallas_call(
        functools.partial(_out_pe_kernel, gated),
        grid=(bsz, nt),
        in_specs=in_specs,
        out_specs=pl.BlockSpec((tm, d), row),
        out_shape=jax.ShapeDtypeStruct((n, d), F32),
        compiler_params=_cparams(("parallel", "arbitrary")),
        name="out_pe_gated" if gated else "out_pe",
    )(*args)


_RQ, _RK, _RV, _RG = 0, 512, 1024, 2048
_MQ, _MK, _MV, _MO, _MZ = 3072, 3584, 4096, 5120, 6144
AB_MAIN = 7168


def _head_norm(y, g):
    yc = y - jnp.mean(y, -1, keepdims=True)
    return yc * lax.rsqrt(jnp.mean(yc * yc, -1, keepdims=True) + EPS) * g


def _ab_kernel(pm_ref, pg_ref, cw_ref, cb_ref, gb_ref, rn_ref, mn_ref, mix_ref,
               ext_ref, r_ref, c_ref, n_ref, m_ref):
    L = CHUNK
    c_idx = pl.program_id(1)

    @pl.when(c_idx == 0)
    def _():
        ext_ref[0:8, :] = jnp.zeros((8, ext_ref.shape[1]), F32)
        r_ref[...] = jnp.zeros(r_ref.shape, F32)
        c_ref[...] = jnp.zeros(c_ref.shape, F32)
        n_ref[...] = jnp.zeros(n_ref.shape, F32)
        m_ref[...] = jnp.full(m_ref.shape, NEG, F32)

    row = lax.broadcasted_iota(jnp.int32, (L, L), 0)
    col = lax.broadcasted_iota(jnp.int32, (L, L), 1)
    causal = row >= col
    diff = (row - col).astype(F32)
    tcol = lax.broadcasted_iota(jnp.int32, (L, 1), 0).astype(F32)

    for h in range(RET_HEADS):
        lg = math.log1p(-(2.0 ** (-5.0 - h)))
        scale = RET_DK ** -0.5
        decay = jnp.where(causal, jnp.exp(jnp.maximum(diff, 0.0) * lg), 0.0) * scale
        q = pm_ref[:, _RQ + h * RET_DK:_RQ + (h + 1) * RET_DK]
        k = pm_ref[:, _RK + h * RET_DK:_RK + (h + 1) * RET_DK]
        v = pm_ref[:, _RV + h * RET_DV:_RV + (h + 1) * RET_DV]
        sc = _dot_nt(q, k) * decay
        xi = jnp.exp((tcol + 1.0) * lg) * scale
        qx = (q.astype(F32) * xi).astype(BF16)
        r_prev = r_ref[h]
        y = _dot(sc.astype(BF16), v) + _dot(qx, r_prev.astype(BF16))
        zeta = jnp.exp((L - 1.0 - tcol) * lg)
        kz_t = (k.astype(F32) * zeta).T.astype(BF16)
        r_ref[h] = r_prev * math.exp(L * lg) + _dot(kz_t, v)
        g = pm_ref[:, _RG + h * RET_DV:_RG + (h + 1) * RET_DV].astype(F32)
        out = _head_norm(y, rn_ref[:, h * RET_DV:(h + 1) * RET_DV]) * _silu(g)
        mix_ref[:, h * RET_DV:(h + 1) * RET_DV] = out.astype(mix_ref.dtype)

    nqk = 2 * ML_HEADS * ML_DK
    x = pm_ref[:, _MQ:_MQ + nqk].astype(F32)
    ext_ref[8:8 + L, :] = x
    acc = cb_ref[...] + cw_ref[ML_CONV - 1:ML_CONV, :] * x
    for j in range(ML_CONV - 1):
        acc = acc + cw_ref[j:j + 1, :] * ext_ref[pl.ds(8 - (ML_CONV - 1) + j, L), :]
    ext_ref[0:8, :] = x[L - 8:L, :]
    qk = _silu(acc)

    gt = (pg_ref[...] + gb_ref[...]).T
    g8 = gt[0:8, :]
    logf8 = jnp.minimum(g8, 0.0) - jnp.log(1.0 + jnp.exp(-jnp.abs(g8)))
    triu = jnp.where(row <= col, 1.0, 0.0).astype(BF16)
    bcum8 = _dot_f32_lhs(logf8, triu)
    b_t = pltpu.roll(bcum8, 4, axis=0)
    b_last = b_t[:, L - 1:L]
    a_t = b_last - b_t + g8
    mu = jnp.max(a_t, axis=1, keepdims=True)
    w_t = jnp.exp(a_t - mu)
    r_t = g8 - b_t
    zpad = jnp.zeros((L - 8, L), F32)
    bcol = jnp.concatenate([b_t, zpad], axis=0).T
    wcol = jnp.concatenate([w_t, zpad], axis=0).T
    m_prev = m_ref[...][:, 0:1]
    m_new = jnp.maximum(b_last + m_prev, mu)
    sp = jnp.exp(b_last + m_prev - m_new)
    scn = jnp.exp(mu - m_new)
    m_ref[...] = jnp.broadcast_to(m_new, m_ref.shape)

    for h in range(ML_HEADS):
        bc = bcol[:, h:h + 1]
        log_d = jnp.where(causal, bc + r_t[h:h + 1, :], NEG)
        inter = bc + m_prev[h:h + 1, :]
        m_t = jnp.maximum(inter, jnp.max(log_d, axis=1, keepdims=True))
        dmat = jnp.exp(log_d - m_t)
        w_int = jnp.exp(inter - m_t)
        qh = qk[:, h * ML_DK:(h + 1) * ML_DK]
        kh = qk[:, ML_HEADS * ML_DK + h * ML_DK:ML_HEADS * ML_DK + (h + 1) * ML_DK] * (ML_DK ** -0.5)
        qb = qh.astype(BF16)
        v = pm_ref[:, _MV + h * ML_DV:_MV + (h + 1) * ML_DV]
        s = _dot_nt(qb, kh.astype(BF16)) * dmat
        c_prev = c_ref[h]
        n_prev = n_ref[h:h + 1, :]
        num = _dot(s.astype(BF16), v) + w_int * _dot(qb, c_prev.astype(BF16))
        den = jnp.sum(s, axis=1, keepdims=True) + w_int * jnp.sum(qh * n_prev, axis=1, keepdims=True)
        den = jnp.maximum(jnp.abs(den), jnp.exp(-m_t))
        hcell = num * (1.0 / den)
        o = pm_ref[:, _MO + h * ML_DV:_MO + (h + 1) * ML_DV].astype(F32)
        y = _sigmoid(o) * hcell
        z = pm_ref[:, _MZ + h * ML_DV:_MZ + (h + 1) * ML_DV].astype(F32)
        out = _head_norm(y, mn_ref[:, h * ML_DV:(h + 1) * ML_DV]) * _silu(z)
        mix_ref[:, RET_HEADS * RET_DV + h * ML_DV:RET_HEADS * RET_DV + (h + 1) * ML_DV] = out.astype(mix_ref.dtype)
        kw = kh * wcol[:, h:h + 1]
        kv = _dot(kw.T.astype(BF16), v)
        ksum = jnp.sum(kw, axis=0, keepdims=True)
        sp_h = sp[h:h + 1, :]
        sc_h = scn[h:h + 1, :]
        c_ref[h] = sp_h * c_prev + sc_h * kv
        n_ref[h:h + 1, :] = sp_h * n_prev + sc_h * ksum


def _ab_mixer(pm, pg, conv_w, conv_b, gate_b, ret_norm, ml_norm, bsz, seq):
    n = pm.shape[0]
    nc = seq // CHUNK
    nqk = 2 * ML_HEADS * ML_DK
    row = lambda b, c: (b * nc + c, 0)
    const = lambda b, c: (0, 0)
    nmix = RET_HEADS * RET_DV + ML_HEADS * ML_DV
    return pl.pallas_call(
        _ab_kernel,
        grid=(bsz, nc),
        in_specs=[pl.BlockSpec((CHUNK, AB_MAIN), row), pl.BlockSpec((CHUNK, LANES), row),
                  pl.BlockSpec((ML_CONV, nqk), const), pl.BlockSpec((1, nqk), const),
                  pl.BlockSpec((1, LANES), const),
                  pl.BlockSpec((1, RET_HEADS * RET_DV), const), pl.BlockSpec((1, ML_HEADS * ML_DV), const)],
        out_specs=pl.BlockSpec((CHUNK, nmix), row),
        out_shape=jax.ShapeDtypeStruct((n, nmix), BF16),
        scratch_shapes=[pltpu.VMEM((CHUNK + 8, nqk), F32),
                        pltpu.VMEM((RET_HEADS, RET_DK, RET_DV), F32),
                        pltpu.VMEM((ML_HEADS, ML_DK, ML_DV), F32),
                        pltpu.VMEM((8, ML_DK), F32),
                        pltpu.VMEM((8, LANES), F32)],
        compiler_params=_cparams(("parallel", "arbitrary")),
        name="ab_mixer",
    )(pm, pg, conv_w, conv_b, gate_b, ret_norm, ml_norm)


_QNZ0 = 0
_KVC0 = NSA_HEADS
_KVS0 = _KVC0 + NSA_GROUPS
_KVW0 = _KVS0 + NSA_GROUPS
_SU0 = (_KVW0 + NSA_GROUPS) * LANES
_SZ0 = _SU0 + S5_WIDTH
CD_MAIN = _SZ0 + S5_WIDTH
SLOTW = NSA_GROUPS * LANES
VT_ROWS = NSA_DH + 16
LOG2E = math.log2(math.e)


def _compress_kernel(x_ref, pk_ref, pv_ref, w1k_ref, w1v_ref, w2k_ref, w2vt_ref, ekc_ref,
                     ko_ref, vot_ref, xs_ref):
    seq = x_ref.shape[0]
    nb = seq // CMP_STRIDE
    for g in range(NSA_GROUPS):
        xs_ref[...] = x_ref[:, g * LANES:(g + 1) * LANES].astype(F32)
        acc = [jnp.zeros((nb, CMP_HIDDEN), F32) for _ in range(4)]
        for r in range(CMP_STRIDE):
            xr = xs_ref[pl.ds(r, nb, stride=CMP_STRIDE), :]
            for kind, (p_ref, w1_ref) in enumerate(((pk_ref, w1k_ref), (pv_ref, w1v_ref))):
                acc[2 * kind] = acc[2 * kind] + _dot((xr + p_ref[r:r + 1, :]).astype(BF16), w1_ref[r])
                acc[2 * kind + 1] = acc[2 * kind + 1] + _dot(
                    (xr + p_ref[CMP_STRIDE + r:CMP_STRIDE + r + 1, :]).astype(BF16), w1_ref[CMP_STRIDE + r])
        hk = _gelu(acc[0] + pltpu.roll(acc[1], nb - 1, axis=0)).astype(BF16)
        hv = _gelu(acc[2] + pltpu.roll(acc[3], nb - 1, axis=0)).astype(BF16)
        ko_ref[g] = (_dot(hk, w2k_ref[...]) + ekc_ref[...]).astype(ko_ref.dtype)
        vot_ref[g] = _dot_nt(w2vt_ref[...], hv).astype(vot_ref.dtype)


def _compress(pm3, pos_k, pos_v, w1k, w1v, w2k, w2vt, ekc):
    bsz, seq, _ = pm3.shape
    nb = seq // CMP_STRIDE
    c3 = lambda b: (0, 0, 0)
    c2 = lambda b: (0, 0)
    return pl.pallas_call(
        _compress_kernel,
        grid=(bsz,),
        in_specs=[pl.BlockSpec((None, seq, SLOTW), lambda b: (b, 0, _KVC0 * LANES // SLOTW)),
                  pl.BlockSpec((CMP_LEN, LANES), c2), pl.BlockSpec((CMP_LEN, LANES), c2),
                  pl.BlockSpec((CMP_LEN, LANES, CMP_HIDDEN), c3), pl.BlockSpec((CMP_LEN, LANES, CMP_HIDDEN), c3),
                  pl.BlockSpec((CMP_HIDDEN, LANES), c2), pl.BlockSpec((LANES, CMP_HIDDEN), c2),
                  pl.BlockSpec((nb, LANES), c2)],
        out_specs=[pl.BlockSpec((None, NSA_GROUPS, nb, LANES), lambda b: (b, 0, 0, 0)),
                   pl.BlockSpec((None, NSA_GROUPS, LANES, nb), lambda b: (b, 0, 0, 0))],
        out_shape=[jax.ShapeDtypeStruct((bsz, NSA_GROUPS, nb, LANES), BF16),
                   jax.ShapeDtypeStruct((bsz, NSA_GROUPS, LANES, nb), BF16)],
        scratch_shapes=[pltpu.VMEM((seq, LANES), F32)],
        compiler_params=_cparams(("parallel",)),
        name="nsa_compress",
    )(pm3, pos_k, pos_v, w1k, w1v, w2k, w2vt, ekc)


def _nsa_kernel(q_ref, kc_ref, vct_ref, kvs_ref, kvw_ref, pg_ref,
                ek_ref, al_ref, ovt_ref, y_ref,
                ksa_ref, vsat_ref, kwa_ref, vwat_ref):
    seq = kvs_ref.shape[0]
    g_idx = pl.program_id(1)
    qi = pl.program_id(2)
    t0 = qi * TQ
    wtiles = WINDOW // LANES

    @pl.when(qi == 0)
    def _():
        ek = ek_ref[...]
        lane_k = lax.broadcasted_iota(jnp.int32, (seq, LANES), 1)
        zero = jnp.zeros((seq, LANES), BF16)
        ksa_ref[...] = jnp.where(lane_k < NSA_DH, kvs_ref[...], zero) + ek
        ek_w = jnp.where(lane_k >= ALI_LANE0, ek, zero)
        kwa_ref[0:WINDOW, :] = jnp.zeros((WINDOW, LANES), BF16)
        kwa_ref[WINDOW:WINDOW + seq, :] = jnp.where(lane_k < NSA_DH, kvw_ref[...], zero) + ek_w
        ones_rows = jnp.where(lax.broadcasted_iota(jnp.int32, (VT_ROWS - NSA_DH, LANES), 0) == 0, 1.0, 0.0)
        for i in range(wtiles):
            vwat_ref[i] = jnp.zeros((VT_ROWS, LANES), BF16)
        for i in range(seq // LANES):
            rws = slice(i * LANES, (i + 1) * LANES)
            for src_ref, dst_ref, j in ((kvs_ref, vsat_ref, i), (kvw_ref, vwat_ref, wtiles + i)):
                v_t = src_ref[rws, :].astype(F32).T[NSA_DH:2 * NSA_DH, :]
                dst_ref[j] = jnp.concatenate([v_t, ones_rows], axis=0).astype(BF16)

    nsel = seq // SEL_BLOCK
    nk_dyn = (t0 + TQ + KCH - 1) // KCH
    for nk in range(1, seq // KCH + 1):
        pl.when(nk_dyn == nk)(functools.partial(
            _nsa_tile, nk, nsel, qi, g_idx, q_ref, kc_ref, vct_ref, pg_ref, al_ref, ovt_ref,
            y_ref, ksa_ref, vsat_ref, kwa_ref, vwat_ref))


def _nsa_tile(nk, nsel, qi, g_idx, q_ref, kc_ref, vct_ref, pg_ref, al_ref, ovt_ref,
              y_ref, ksa_ref, vsat_ref, kwa_ref, vwat_ref):
    hp = NSA_HPG
    rows = hp * TQ
    span = WINDOW + TQ
    t0 = qi * TQ
    tw = pl.multiple_of(t0, TQ)
    jrow = lax.broadcasted_iota(jnp.int32, (LANES, rows), 0)
    tq = lax.broadcasted_iota(jnp.int32, (LANES, rows), 1) & (TQ - 1)
    kmt = jrow - tq

    qnz = [q_ref[:, hh * LANES:(hh + 1) * LANES].astype(F32) for hh in range(hp)]
    lane_q = lax.broadcasted_iota(jnp.int32, (TQ, LANES), 1)
    qal = jnp.concatenate([jnp.where(lane_q < NSA_DH, qnz[hh], al_ref[pl.ds(g_idx * hp + hh, 1), :])
                           for hh in range(hp)], axis=0)
    qalb = qal.astype(BF16)
    s_c = _dot_nt(kc_ref[...], qalb)
    s_w = _dot_nt(kwa_ref[pl.ds(tw, span), :], qalb)

    s = jnp.where(tq - jrow * CMP_STRIDE >= (CMP_LEN - 1) - t0, s_c, NEG)
    e = jnp.exp2(s - jnp.max(s, axis=0, keepdims=True))
    inv = 1.0 / jnp.sum(e, axis=0, keepdims=True)
    if nk == 1:
        inv = jnp.where(t0 + tq[0:1, :] >= CMP_LEN - 1, inv, 0.0)
    p = e * inv
    oc_t = _dot(vct_ref[...], p.astype(BF16))
    psum_t = p[:, 0:TQ]
    for hh in range(1, hp):
        psum_t = psum_t + p[:, hh * TQ:(hh + 1) * TQ]
    p_hi = psum_t.astype(BF16)
    p_lo = (psum_t - p_hi.astype(F32)).astype(BF16)
    imp_t = _dot(ovt_ref[...], p_hi) + _dot(ovt_ref[...], p_lo)

    blocks = [s_w[i * LANES:(i + 1) * LANES, :] for i in range(span // LANES)]
    for i in range(span // LANES):
        if i * LANES < TQ:
            blocks[i] = jnp.where(kmt > -i * LANES, blocks[i], NEG)
        if (i + 1) * LANES > WINDOW:
            blocks[i] = jnp.where(kmt <= WINDOW - i * LANES, blocks[i], NEG)
        if nk == 1:
            blocks[i] = jnp.where(jrow >= WINDOW - t0 - i * LANES, blocks[i], NEG)
    s = jnp.concatenate(blocks, axis=0)
    p = jnp.exp2(s - jnp.max(s, axis=0, keepdims=True)).astype(BF16)
    vw_t = jnp.concatenate([vwat_ref[qi * (TQ // LANES) + i] for i in range(span // LANES)], axis=1)
    accw_t = _dot(vw_t, p)

    score = imp_t[SEL_LANE0:SEL_LANE0 + nsel, :]
    nrow = lax.broadcasted_iota(jnp.int32, (nsel, TQ), 0)
    own = (t0 + lax.broadcasted_iota(jnp.int32, (nsel, TQ), 1)) >> int(math.log2(SEL_BLOCK))
    valid = nrow <= own
    forced = (nrow == 0) | (nrow == own)
    score = jnp.where(forced, BIG, jnp.where(valid, score, -1.0))
    chosen = jnp.zeros((nsel, TQ), F32)
    nrow_f = nrow.astype(F32)
    for _ in range(SEL_TOPN):
        mx = jnp.max(score, axis=0, keepdims=True)
        first = jnp.min(jnp.where(score == mx, nrow_f, 4.0 * LANES), axis=0, keepdims=True)
        hit = nrow_f == first
        chosen = jnp.where(hit, 1.0, chosen)
        score = jnp.where(hit, -3.0, score)
    mask_t = jnp.where(chosen > 0.0, jnp.where(valid, 0.0, NEG), NEG)
    maskcols = jnp.concatenate([jnp.zeros((SEL_LANE0, TQ), F32), mask_t,
                                jnp.zeros((LANES - SEL_LANE0 - nsel, TQ), F32)], axis=0).T

    qa = (qal + jnp.concatenate([maskcols] * hp, axis=0)).astype(BF16)

    def scores(c):
        s = _dot_nt(ksa_ref[c * KCH:(c + 1) * KCH, :], qa)
        if c == nk - 1:
            s = jnp.concatenate(
                [jnp.where(kmt <= t0 - c * KCH - i * LANES, s[i * LANES:(i + 1) * LANES, :], NEG)
                 for i in range(KCH // LANES)], axis=0)
        return s

    def weighted(c, s):
        mc = jnp.max(s, axis=0, keepdims=True)
        v_t = jnp.concatenate([vsat_ref[c * (KCH // LANES) + i] for i in range(KCH // LANES)], axis=1)
        return mc, _dot(v_t, jnp.exp2(s - mc).astype(BF16))

    parts = []
    s_next = scores(0)
    for c in range(nk):
        s_cur = s_next
        if c + 1 < nk:
            s_next = scores(c + 1)
        parts.append(weighted(c, s_cur))
    m = parts[0][0]
    for mc, _ in parts[1:]:
        m = jnp.maximum(m, mc)
    accs_t = None
    for mc, d in parts:
        d = d if nk == 1 else jnp.exp2(mc - m) * d
        accs_t = d if accs_t is None else accs_t + d

    gt_t = _sigmoid(pg_ref[...]).T
    outs = []
    for hh in range(hp):
        g_c, g_s, g_w = [gt_t[j * hp + hh:j * hp + hh + 1, :] for j in range(3)]
        hcols = slice(hh * TQ, (hh + 1) * TQ)
        as_t = accs_t[:, hcols]
        aw_t = accw_t[:, hcols]
        o = (g_c * oc_t[0:NSA_DH, hcols]
             + (g_s * (1.0 / as_t[NSA_DH:NSA_DH + 1, :])) * as_t[0:NSA_DH, :]
             + (g_w * (1.0 / aw_t[NSA_DH:NSA_DH + 1, :])) * aw_t[0:NSA_DH, :])
        z_t = qnz[hh].T[NSA_DH:2 * NSA_DH, :]
        outs.append(o * _silu(z_t))
    for pr in range(hp // 2):
        packed = jnp.concatenate([outs[2 * pr], outs[2 * pr + 1]], axis=0).T
        y_ref[:, pr * LANES:(pr + 1) * LANES] = packed.astype(y_ref.dtype)


def _nsa(pm3, pg3, kcc, vct, ek, al, ovt):
    bsz, seq, _ = pm3.shape
    nq = seq // TQ
    gw = NSA_HPG * NSA_DH
    kv = lambda tile0: pl.BlockSpec((None, seq, LANES), lambda b, g, q: (b, 0, tile0 + g))
    c2 = lambda b, g, q: (0, 0)
    return pl.pallas_call(
        _nsa_kernel,
        grid=(bsz, NSA_GROUPS, nq),
        in_specs=[pl.BlockSpec((None, TQ, NSA_HPG * LANES), lambda b, g, q: (b, q, g)),
                  pl.BlockSpec((None, None, LANES, LANES), lambda b, g, q: (b, g, 0, 0)),
                  pl.BlockSpec((None, None, LANES, LANES), lambda b, g, q: (b, g, 0, 0)),
                  kv(_KVS0), kv(_KVW0),
                  pl.BlockSpec((None, TQ, LANES), lambda b, g, q: (b, q, g)),
                  pl.BlockSpec((seq, LANES), c2),
                  pl.BlockSpec((NSA_HEADS, LANES), c2),
                  pl.BlockSpec((LANES, LANES), c2)],
        out_specs=pl.BlockSpec((None, TQ, gw), lambda b, g, q: (b, q, g)),
        out_shape=jax.ShapeDtypeStruct((bsz, seq, NSA_HEADS * NSA_DH), BF16),
        scratch_shapes=[pltpu.VMEM((seq, LANES), BF16), pltpu.VMEM((seq // LANES, VT_ROWS, LANES), BF16),
                        pltpu.VMEM((seq + WINDOW, LANES), BF16),
                        pltpu.VMEM(((seq + WINDOW) // LANES, VT_ROWS, LANES), BF16)],
        compiler_params=_cparams(("parallel", "arbitrary", "arbitrary")),
        name="nsa_attn",
    )(pm3, kcc, vct, pm3, pm3, pg3, ek, al, ovt)


def _nsa_tables(seq):
    t = np.arange(seq)
    ek = np.zeros((seq, LANES), np.float32)
    ek[t, SEL_LANE0 + t // SEL_BLOCK] = 1.0
    for c in range(3):
        ek[:, ALI_LANE0 + c] = t // SEL_BLOCK
        ek[:, ALI_LANE0 + 3 + c] = t % SEL_BLOCK
    slopes = jnp.exp2(-8.0 * jnp.arange(1, NSA_HEADS + 1, dtype=F32) / NSA_HEADS) * LOG2E
    s_hi, s_mid, s_lo = _split3(slopes)
    parts = [p.astype(F32) for p in (s_hi, s_mid, s_lo)]
    al = jnp.zeros((NSA_HEADS, LANES), F32)
    for c in range(3):
        al = al.at[:, ALI_LANE0 + c].set(parts[c] * SEL_BLOCK)
        al = al.at[:, ALI_LANE0 + 3 + c].set(parts[c])
    nb = seq // CMP_STRIDE
    cstart = np.arange(nb) * CMP_STRIDE
    cend = cstart + CMP_LEN - 1
    ekc = np.zeros((nb, LANES), np.float32)
    for c in range(3):
        ekc[:, ALI_LANE0 + c] = cend // SEL_BLOCK
        ekc[:, ALI_LANE0 + 3 + c] = cend % SEL_BLOCK
    sel = np.arange(seq // SEL_BLOCK)
    ovt = np.zeros((LANES, max(nb, LANES)), np.float32)
    ovl = (cstart[:, None] < (sel[None, :] + 1) * SEL_BLOCK) & (cstart[:, None] + CMP_LEN > sel[None, :] * SEL_BLOCK)
    ovl[nb - 1, :] = False
    ovt[SEL_LANE0:SEL_LANE0 + len(sel), :nb] = ovl.T
    return jnp.asarray(ek, BF16), al, jnp.asarray(ekc), jnp.asarray(ovt, BF16)


def _s5_disc_kernel(are_ref, aim_ref, ldt_ref, bre_ref, bim_ref, abr_ref, abi_ref, bbr_ref, bbi_ref):
    a_re = are_ref[...]
    a_im = aim_ref[...]
    dt = jnp.exp(ldt_ref[...])
    er = jnp.exp(a_re * dt)
    abar_re = er * jnp.cos(a_im * dt)
    abar_im = er * jnp.sin(a_im * dt)
    lam2 = a_re * a_re + a_im * a_im
    cr = ((abar_re - 1.0) * a_re + abar_im * a_im) / lam2
    ci = (abar_im * a_re - (abar_re - 1.0) * a_im) / lam2
    b_re = bre_ref[...]
    b_im = bim_ref[...]
    abr_ref[...] = abar_re
    abi_ref[...] = abar_im
    bbr_ref[...] = cr * b_re - ci * b_im
    bbi_ref[...] = cr * b_im + ci * b_re


def _s5_disc(a_re_rep, a_im_rep, ldt_rep, b_re_t, b_im_t):
    shp = jax.ShapeDtypeStruct(a_re_rep.shape, F32)
    return pl.pallas_call(_s5_disc_kernel, out_shape=[shp] * 4, name="s5_disc")(
        a_re_rep, a_im_rep, ldt_rep, b_re_t, b_im_t)


S5_SLABG = 4
S5_TT = 64
S5_PITCH = S5_TT + 4
S5_DIAG = 256
S5_NSTATE_PER_DIAG = S5_DIAG // S5_GROUP_CH * S5_STATE


def _s5_kernel(u_ref, bre_ref, bim_ref, ar_ref, ai_ref, cre_ref, cim_ref, d_ref, wg_ref, o_ref,
               xr_ref, xi_ref, sr_ref, si_ref):
    nb, tt, wd = u_ref.shape
    pitch = sr_ref.shape[1] // nb
    gcols = S5_SLABG * LANES
    ngroups = S5_NSTATE // gcols

    @pl.when(pl.program_id(0) == 0)
    def _():
        xr_ref[...] = jnp.zeros(xr_ref.shape, F32)
        xi_ref[...] = jnp.zeros(xi_ref.shape, F32)

    u = u_ref[...].reshape(nb * tt, wd)

    chan = lambda kg: slice(kg * gcols // S5_NSTATE_PER_DIAG * S5_DIAG,
                            (kg * gcols // S5_NSTATE_PER_DIAG + 1) * S5_DIAG)
    for kg in range(ngroups):
        cols = slice(kg * gcols, (kg + 1) * gcols)
        br = _dot(u[:, chan(kg)], bre_ref[chan(kg), cols])
        bi = _dot(u[:, chan(kg)], bim_ref[chan(kg), cols])
        for j in range(S5_SLABG):
            for b in range(nb):
                sr_ref[kg * S5_SLABG + j, b * pitch:b * pitch + tt, :] = br[b * tt:(b + 1) * tt, j * LANES:(j + 1) * LANES]
                si_ref[kg * S5_SLABG + j, b * pitch:b * pitch + tt, :] = bi[b * tt:(b + 1) * tt, j * LANES:(j + 1) * LANES]

    for kg in range(ngroups):
        cols = slice(kg * gcols, (kg + 1) * gcols)
        slabs = range(kg * S5_SLABG, (kg + 1) * S5_SLABG)
        ar = ar_ref[:, cols]
        ai = ai_ref[:, cols]

        def body(t, carry, slabs=slabs, ar=ar, ai=ai):
            xr, xi = carry
            rows = pl.ds(t, nb, stride=pitch)
            nxr = ar * xr - ai * xi + jnp.concatenate([sr_ref[k, rows, :] for k in slabs], axis=1)
            nxi = ar * xi + ai * xr + jnp.concatenate([si_ref[k, rows, :] for k in slabs], axis=1)
            for j, k in enumerate(slabs):
                sr_ref[k, rows, :] = nxr[:, j * LANES:(j + 1) * LANES]
                si_ref[k, rows, :] = nxi[:, j * LANES:(j + 1) * LANES]
            return nxr, nxi

        xr, xi = lax.fori_loop(0, tt, body, (xr_ref[:, cols], xi_ref[:, cols]), unroll=4)
        xr_ref[:, cols] = xr
        xi_ref[:, cols] = xi

    y_blocks = [jnp.zeros((nb * tt, S5_DIAG), F32) for _ in range(wd // S5_DIAG)]
    for kg in range(ngroups):
        cols = slice(kg * gcols, (kg + 1) * gcols)
        gather = lambda ref: jnp.concatenate(
            [jnp.concatenate([ref[kg * S5_SLABG + j, b * pitch:b * pitch + tt, :] for j in range(S5_SLABG)], axis=1)
             for b in range(nb)], axis=0).astype(BF16)
        blk = chan(kg).start // S5_DIAG
        y_blocks[blk] = (y_blocks[blk] + _dot(gather(sr_ref), cre_ref[cols, chan(kg)])
                         - _dot(gather(si_ref), cim_ref[cols, chan(kg)]))
    y = jnp.concatenate(y_blocks, axis=1)
    y = _gelu(y + d_ref[...] * u.astype(F32))
    z = _dot(y.astype(BF16), wg_ref[...])
    o = z[:, :S5_WIDTH] * _sigmoid(z[:, S5_WIDTH:])
    o_ref[...] = o.reshape(nb, tt, wd).astype(o_ref.dtype)


def _s5(pm3, bbd_re, bbd_im, ar8, ai8, cbd_re, cbd_im, d_row, w_glu):
    bsz, seq, _ = pm3.shape
    tt = S5_TT
    c2 = lambda i: (0, 0)
    nslab = S5_NSTATE // LANES
    return pl.pallas_call(
        _s5_kernel,
        grid=(seq // tt,),
        in_specs=[pl.BlockSpec((bsz, tt, S5_WIDTH), lambda i: (0, i, _SU0 // S5_WIDTH)),
                  pl.BlockSpec((S5_WIDTH, S5_NSTATE), c2), pl.BlockSpec((S5_WIDTH, S5_NSTATE), c2),
                  pl.BlockSpec((8, S5_NSTATE), c2), pl.BlockSpec((8, S5_NSTATE), c2),
                  pl.BlockSpec((S5_NSTATE, S5_WIDTH), c2), pl.BlockSpec((S5_NSTATE, S5_WIDTH), c2),
                  pl.BlockSpec((1, S5_WIDTH), c2), pl.BlockSpec((S5_WIDTH, 2 * S5_WIDTH), c2)],
        out_specs=pl.BlockSpec((bsz, tt, S5_WIDTH), lambda i: (0, i, 0)),
        out_shape=jax.ShapeDtypeStruct((bsz, seq, S5_WIDTH), BF16),
        scratch_shapes=[pltpu.VMEM((8, S5_NSTATE), F32), pltpu.VMEM((8, S5_NSTATE), F32),
                        pltpu.VMEM((nslab, bsz * S5_PITCH, LANES), F32),
                        pltpu.VMEM((nslab, bsz * S5_PITCH, LANES), F32)],
        compiler_params=_cparams(("arbitrary",)),
        name="s5_scan",
    )(pm3, bbd_re, bbd_im, ar8, ai8, cbd_re, cbd_im, d_row, w_glu)


def _pad_cols(w, n):
    return jnp.pad(w, ((0, 0), (0, n - w.shape[1])))


def _layer0(h2, p4, layer, bsz, seq, norm_pre, norm_post, pe_proj, pe_gate,
            w_in, ret_norm, conv_w, conv_b, gate_b, ml_norm, w_out):
    d = h2.shape[1]
    g0 = 5120
    w_main = jnp.concatenate([w_in[:, :g0], w_in[:, g0 + 2 * ML_HEADS:]], axis=1).astype(BF16)
    w_gate = _pad_cols(w_in[:, g0:g0 + 2 * ML_HEADS], LANES).astype(BF16)
    pm, pg = _norm_proj(h2, norm_pre.reshape(1, d), w_gate, w_main)
    mix = _ab_mixer(pm, pg, conv_w, conv_b.reshape(1, -1), _pad_cols(gate_b.reshape(1, -1), LANES),
                    ret_norm.reshape(1, -1), ml_norm.reshape(1, -1), bsz, seq)
    ka = RET_HEADS * RET_DV
    nt = seq // 512
    return _out_pe(mix, lambda b, s: (b * nt + s, 0), ka, mix, lambda b, s: (b * nt + s, 1), ML_HEADS * ML_DV,
                   None, None, h2, p4, layer, w_out[:ka].astype(BF16), w_out[ka:].astype(BF16),
                   norm_post.reshape(1, d), pe_gate.astype(BF16), pe_proj.astype(BF16), bsz, seq)


def _layer1(h2, p4, layer, bsz, seq, norm_pre, norm_post, pe_proj, pe_gate,
            w_in, pos_k, pos_v, w1_k, w2_k, w1_v, w2_v,
            a_re, a_im, log_dt, b_re, b_im, c_re, c_im, d_skip, w_glu, w_out):
    d = h2.shape[1]
    nw, nkv = NSA_HEADS * NSA_DH, NSA_GROUPS * NSA_DH
    offs = np.cumsum((0, nw, nkv, nkv, nkv, nkv, nkv, nkv, 3 * NSA_HEADS, nw, S5_WIDTH, S5_WIDTH))
    seg = lambda i: w_in[:, offs[i]:offs[i + 1]]
    pair = lambda a, b: jnp.stack([a.reshape(d, -1, NSA_DH), b.reshape(d, -1, NSA_DH)], axis=2).reshape(d, -1)
    w_main = jnp.concatenate(
        [pair(seg(0) * (NSA_DH ** -0.5 * LOG2E), seg(8)), pair(seg(1), seg(2)), pair(seg(3), seg(4)),
         pair(seg(5), seg(6)), seg(9), seg(10)], axis=1).astype(BF16)
    w_gate = seg(7).reshape(d, NSA_GROUPS, NSA_HPG, 3).transpose(0, 1, 3, 2).reshape(d, NSA_GROUPS, 3 * NSA_HPG)
    w_gate = jnp.pad(w_gate, ((0, 0), (0, 0), (0, LANES - 3 * NSA_HPG))).reshape(d, NSA_GROUPS * LANES).astype(BF16)
    pm, pg = _norm_proj(h2, norm_pre.reshape(1, d), w_gate, w_main)

    nb = seq // CMP_STRIDE
    pm3 = pm.reshape(bsz, seq, CD_MAIN)
    padl = lambda a, lo=True: jnp.pad(a, ((0, 0),) * (a.ndim - 1)
                                      + ((0, LANES - a.shape[-1]) if lo else (LANES - a.shape[-1], 0),))
    w1 = lambda w, lo: jnp.pad(w.reshape(CMP_LEN, NSA_DH, CMP_HIDDEN),
                               ((0, 0), (0, LANES - NSA_DH) if lo else (LANES - NSA_DH, 0), (0, 0))).astype(BF16)
    ek, al, ekc, ovt = _nsa_tables(seq)
    kcc, vct = _compress(pm3, padl(pos_k), padl(pos_v, False), w1(w1_k, True), w1(w1_v, False),
                         padl(w2_k).astype(BF16), padl(w2_v).T.astype(BF16), ekc)
    if nb < LANES:
        kcc = jnp.pad(kcc, ((0, 0), (0, 0), (0, LANES - nb), (0, 0)))
        vct = jnp.pad(vct, ((0, 0), (0, 0), (0, 0), (0, LANES - nb)))
    y_nsa = _nsa(pm3, pg.reshape(bsz, seq, NSA_GROUPS * LANES), kcc, vct, ek, al, ovt)

    rep = lambda a: jnp.repeat(a, S5_GROUP_CH, axis=0)
    ldt = jnp.broadcast_to(rep(log_dt[:, None]), (S5_WIDTH, S5_STATE))
    bt = lambda b: b.transpose(0, 2, 1).reshape(S5_WIDTH, S5_STATE)
    abr, abi, bbr, bbi = _s5_disc(rep(a_re), rep(a_im), ldt, bt(b_re), bt(b_im))
    eye = jnp.eye(S5_GROUPS, dtype=F32)
    bd_in = lambda bb: (bb.reshape(S5_GROUPS, S5_GROUP_CH, 1, S5_STATE)
                        * eye[:, None, :, None]).reshape(S5_WIDTH, S5_NSTATE).astype(BF16)
    bd_out = lambda c: (c.transpose(0, 2, 1).reshape(S5_GROUPS, S5_STATE, 1, S5_GROUP_CH)
                        * eye[:, None, :, None]).reshape(S5_NSTATE, S5_WIDTH).astype(BF16)
    row8 = lambda a: jnp.broadcast_to(a[::S5_GROUP_CH].reshape(1, S5_NSTATE), (8, S5_NSTATE))
    y_s5 = _s5(pm3, bd_in(bbr), bd_in(bbi), row8(abr), row8(abi),
               bd_out(c_re), bd_out(c_im), d_skip.reshape(1, -1), w_glu.astype(BF16))

    nt = seq // 512
    szb = _SZ0 // S5_WIDTH
    return _out_pe(y_nsa.reshape(bsz * seq, nw), lambda b, s: (b * nt + s, 0), nw,
                   y_s5.reshape(bsz * seq, S5_WIDTH), lambda b, s: (b * nt + s, 0), S5_WIDTH,
                   pm, lambda b, s: (b * nt + s, szb),
                   h2, p4, layer, w_out[:nw].astype(BF16), w_out[nw:].astype(BF16),
                   norm_post.reshape(1, d), pe_gate.astype(BF16), pe_proj.astype(BF16), bsz, seq)


def kernel(x, p, norm_pre, norm_post, pe_proj, pe_gate, ab_w_in, ret_norm, ml_conv_w, ml_conv_b, ml_gate_b,
           ml_norm, ab_w_out, cd_w_in, cmp_pos_k, cmp_pos_v, cmp_w1_k, cmp_w2_k, cmp_w1_v, cmp_w2_v,
           s5_a_re, s5_a_im, s5_log_dt, s5_b_re, s5_b_im, s5_c_re, s5_c_im, s5_d, s5_w_glu, cd_w_out):
    bsz, seq, d = x.shape
    assert bsz == 8, "the S5 scan maps the batch onto the 8 sublanes of a vreg"
    assert seq % max(KCH, 1024) == 0
    depth = p.shape[0]
    h2 = x.reshape(bsz * seq, d)
    for i in range(depth):
        j = i // 2
        if i % 2 == 0:
            h2 = _layer0(h2, p, i, bsz, seq, norm_pre[i], norm_post[i], pe_proj[i], pe_gate[i],
                         ab_w_in[j], ret_norm[j], ml_conv_w[j], ml_conv_b[j], ml_gate_b[j], ml_norm[j], ab_w_out[j])
        else:
            h2 = _layer1(h2, p, i, bsz, seq, norm_pre[i], norm_post[i], pe_proj[i], pe_gate[i],
                         cd_w_in[j], cmp_pos_k[j], cmp_pos_v[j], cmp_w1_k[j], cmp_w2_k[j], cmp_w1_v[j], cmp_w2_v[j],
                         s5_a_re[j], s5_a_im[j], s5_log_dt[j], s5_b_re[j], s5_b_im[j], s5_c_re[j], s5_c_im[j],
                         s5_d[j], s5_w_glu[j], cd_w_out[j])
    return h2.reshape(bsz, seq, d)
```

```python
import functools
import math

import numpy as np
import jax
import jax.numpy as jnp
from jax import lax
from jax.experimental import pallas as pl
from jax.experimental.pallas import tpu as pltpu

F32 = jnp.float32
BF16 = jnp.bfloat16
EPS = 1e-6
NEG = -1e30
BIG = 1e30

LANES = 128
CHUNK = 128
RET_HEADS, RET_DK, RET_DV = 4, 128, 256
ML_HEADS, ML_DK, ML_DV = 4, 128, 256
ML_CONV = 4
NSA_HEADS, NSA_GROUPS, NSA_DH = 16, 4, 64
NSA_HPG = NSA_HEADS // NSA_GROUPS
CMP_STRIDE, CMP_LEN, CMP_HIDDEN = 16, 32, 128
SEL_BLOCK, SEL_TOPN, WINDOW = 64, 4, 512
S5_GROUPS, S5_GROUP_CH, S5_STATE = 32, 16, 64
S5_WIDTH = S5_GROUPS * S5_GROUP_CH
S5_NSTATE = S5_GROUPS * S5_STATE

TQ = 256
KCH = 512
SEL_LANE0 = 64
ALI_LANE0 = 96
VMEM_LIMIT = 56 * 1024 * 1024


def _cparams(sem):
    return pltpu.CompilerParams(dimension_semantics=sem, vmem_limit_bytes=VMEM_LIMIT)


def _dot(a, b):
    return jnp.dot(a, b, preferred_element_type=F32)


def _dot_nt(a, b):
    return lax.dot_general(a, b, (((1,), (1,)), ((), ())), preferred_element_type=F32)


def _split3(x):
    hi = x.astype(BF16)
    r1 = x - hi.astype(F32)
    mid = r1.astype(BF16)
    lo = (r1 - mid.astype(F32)).astype(BF16)
    return hi, mid, lo


def _dot_f32_lhs(x, m):
    hi, mid, lo = _split3(x)
    return _dot(hi, m) + _dot(mid, m) + _dot(lo, m)


def _sigmoid(x):
    return 1.0 / (1.0 + jnp.exp(-x))


def _silu(x):
    return x * _sigmoid(x)


def _gelu(x):
    return 0.5 * x * (1.0 + jnp.tanh(math.sqrt(2.0 / math.pi) * (x + 0.044715 * (x * x * x))))


def _rms(x, g):
    return x * lax.rsqrt(jnp.mean(x * x, -1, keepdims=True) + EPS) * g


PROJ_TN = 1024


def _norm_proj_kernel(h_ref, g_ref, wg_ref, wm_ref, om_ref, og_ref):
    ub = _rms(h_ref[...], g_ref[...]).astype(BF16)
    nm = om_ref.shape[1]
    for c0 in range(0, nm, PROJ_TN):
        c1 = min(c0 + PROJ_TN, nm)
        om_ref[:, c0:c1] = _dot(ub, wm_ref[:, c0:c1]).astype(om_ref.dtype)
    og_ref[...] = _dot(ub, wg_ref[...])


def _norm_proj(h2, g, w_gate, w_main, tm=512):
    n, d = h2.shape
    nm = w_main.shape[1]
    ng = w_gate.shape[1]
    once = pl.Buffered(1)
    return pl.pallas_call(
        _norm_proj_kernel,
        grid=(n // tm,),
        in_specs=[pl.BlockSpec((tm, d), lambda i: (i, 0)),
                  pl.BlockSpec((1, d), lambda i: (0, 0)),
                  pl.BlockSpec((d, ng), lambda i: (0, 0), pipeline_mode=once),
                  pl.BlockSpec((d, nm), lambda i: (0, 0), pipeline_mode=once)],
        out_specs=[pl.BlockSpec((tm, nm), lambda i: (i, 0)),
                   pl.BlockSpec((tm, ng), lambda i: (i, 0))],
        out_shape=[jax.ShapeDtypeStruct((n, nm), BF16), jax.ShapeDtypeStruct((n, ng), F32)],
        compiler_params=_cparams(("parallel",)),
        name="norm_proj",
    )(h2, g, w_gate, w_main)


def _out_pe_kernel(gated, *refs):
    if gated:
        ma_ref, mb_ref, sz_ref, h_ref, p_ref, wa_ref, wb_ref, gp_ref, wg_ref, wp_ref, o_ref = refs
        sz = sz_ref[...].astype(F32)
        mb = (mb_ref[...].astype(F32) * _silu(sz)).astype(BF16)
    else:
        ma_ref, mb_ref, h_ref, p_ref, wa_ref, wb_ref, gp_ref, wg_ref, wp_ref, o_ref = refs
        mb = mb_ref[...]
    y = _dot(ma_ref[...], wa_ref[...]) + _dot(mb, wb_ref[...])
    h1 = h_ref[...] + _rms(y, gp_ref[...])
    gate = _sigmoid(_dot(h1.astype(BF16), wg_ref[...]))
    pp = _dot(p_ref[...].astype(BF16), wp_ref[...])
    o_ref[...] = h1 + gate * pp


def _out_pe(mix_a, a_map, ka, mix_b, b_map, kb, sz, sz_map, h2, p4, layer, w_a, w_b, g_post, w_gate, w_pe,
            bsz, seq, tm=512):
    n, d = h2.shape
    pe = p4.shape[-1]
    nt = seq // tm
    row = lambda b, s: (b * nt + s, 0)
    const = lambda b, s: (0, 0)
    gated = sz is not None
    in_specs = [pl.BlockSpec((tm, ka), a_map), pl.BlockSpec((tm, kb), b_map)]
    args = [mix_a, mix_b]
    if gated:
        in_specs.append(pl.BlockSpec((tm, kb), sz_map))
        args.append(sz)
    in_specs += [pl.BlockSpec((tm, d), row), pl.BlockSpec((None, None, tm, pe), lambda b, s: (layer, b, s, 0)),
                 pl.BlockSpec((ka, d), const), pl.BlockSpec((kb, d), const), pl.BlockSpec((1, d), const),
                 pl.BlockSpec((d, d), const), pl.BlockSpec((pe, d), const)]
    args += [h2, p4, w_a, w_b, g_post, w_gate, w_pe]
    return pl.pallas_call(
        functools.partial(_out_pe_kernel, gated),
        grid=(bsz, nt),
        in_specs=in_specs,
        out_specs=pl.BlockSpec((tm, d), row),
        out_shape=jax.ShapeDtypeStruct((n, d), F32),
        compiler_params=_cparams(("parallel", "arbitrary")),
        name="out_pe_gated" if gated else "out_pe",
    )(*args)


_RQ, _RK, _RV, _RG = 0, 512, 1024, 2048
_MQ, _MK, _MV, _MO, _MZ = 3072, 3584, 4096, 5120, 6144
AB_MAIN = 7168


def _head_norm(y, g):
    yc = y - jnp.mean(y, -1, keepdims=True)
    return yc * lax.rsqrt(jnp.mean(yc * yc, -1, keepdims=True) + EPS) * g


def _ab_kernel(pm_ref, pg_ref, cw_ref, cb_ref, gb_ref, rn_ref, mn_ref, mix_ref,
               ext_ref, r_ref, c_ref, n_ref, m_ref):
    L = CHUNK
    c_idx = pl.program_id(1)

    @pl.when(c_idx == 0)
    def _():
        ext_ref[0:8, :] = jnp.zeros((8, ext_ref.shape[1]), F32)
        r_ref[...] = jnp.zeros(r_ref.shape, F32)
        c_ref[...] = jnp.zeros(c_ref.shape, F32)
        n_ref[...] = jnp.zeros(n_ref.shape, F32)
        m_ref[...] = jnp.full(m_ref.shape, NEG, F32)

    row = lax.broadcasted_iota(jnp.int32, (L, L), 0)
    col = lax.broadcasted_iota(jnp.int32, (L, L), 1)
    causal = row >= col
    diff = (row - col).astype(F32)
    tcol = lax.broadcasted_iota(jnp.int32, (L, 1), 0).astype(F32)

    for h in range(RET_HEADS):
        lg = math.log1p(-(2.0 ** (-5.0 - h)))
        scale = RET_DK ** -0.5
        decay = jnp.where(causal, jnp.exp(jnp.maximum(diff, 0.0) * lg), 0.0) * scale
        q = pm_ref[:, _RQ + h * RET_DK:_RQ + (h + 1) * RET_DK]
        k = pm_ref[:, _RK + h * RET_DK:_RK + (h + 1) * RET_DK]
        v = pm_ref[:, _RV + h * RET_DV:_RV + (h + 1) * RET_DV]
        sc = _dot_nt(q, k) * decay
        xi = jnp.exp((tcol + 1.0) * lg) * scale
        qx = (q.astype(F32) * xi).astype(BF16)
        r_prev = r_ref[h]
        y = _dot(sc.astype(BF16), v) + _dot(qx, r_prev.astype(BF16))
        zeta = jnp.exp((L - 1.0 - tcol) * lg)
        kz_t = (k.astype(F32) * zeta).T.astype(BF16)
        r_ref[h] = r_prev * math.exp(L * lg) + _dot(kz_t, v)
        g = pm_ref[:, _RG + h * RET_DV:_RG + (h + 1) * RET_DV].astype(F32)
        out = _head_norm(y, rn_ref[:, h * RET_DV:(h + 1) * RET_DV]) * _silu(g)
        mix_ref[:, h * RET_DV:(h + 1) * RET_DV] = out.astype(mix_ref.dtype)

    nqk = 2 * ML_HEADS * ML_DK
    x = pm_ref[:, _MQ:_MQ + nqk].astype(F32)
    ext_ref[8:8 + L, :] = x
    acc = cb_ref[...] + cw_ref[ML_CONV - 1:ML_CONV, :] * x
    for j in range(ML_CONV - 1):
        acc = acc + cw_ref[j:j + 1, :] * ext_ref[pl.ds(8 - (ML_CONV - 1) + j, L), :]
    ext_ref[0:8, :] = x[L - 8:L, :]
    qk = _silu(acc)

    gt = (pg_ref[...] + gb_ref[...]).T
    g8 = gt[0:8, :]
    logf8 = jnp.minimum(g8, 0.0) - jnp.log(1.0 + jnp.exp(-jnp.abs(g8)))
    triu = jnp.where(row <= col, 1.0, 0.0).astype(BF16)
    bcum8 = _dot_f32_lhs(logf8, triu)
    b_t = pltpu.roll(bcum8, 4, axis=0)
    b_last = b_t[:, L - 1:L]
    a_t = b_last - b_t + g8
    mu = jnp.max(a_t, axis=1, keepdims=True)
    w_t = jnp.exp(a_t - mu)
    r_t = g8 - b_t
    zpad = jnp.zeros((L - 8, L), F32)
    bcol = jnp.concatenate([b_t, zpad], axis=0).T
    wcol = jnp.concatenate([w_t, zpad], axis=0).T
    m_prev = m_ref[...][:, 0:1]
    m_new = jnp.maximum(b_last + m_prev, mu)
    sp = jnp.exp(b_last + m_prev - m_new)
    scn = jnp.exp(mu - m_new)
    m_ref[...] = jnp.broadcast_to(m_new, m_ref.shape)

    for h in range(ML_HEADS):
        bc = bcol[:, h:h + 1]
        log_d = jnp.where(causal, bc + r_t[h:h + 1, :], NEG)
        inter = bc + m_prev[h:h + 1, :]
        m_t = jnp.maximum(inter, jnp.max(log_d, axis=1, keepdims=True))
        dmat = jnp.exp(log_d - m_t)
        w_int = jnp.exp(inter - m_t)
        qh = qk[:, h * ML_DK:(h + 1) * ML_DK]
        kh = qk[:, ML_HEADS * ML_DK + h * ML_DK:ML_HEADS * ML_DK + (h + 1) * ML_DK] * (ML_DK ** -0.5)
        qb = qh.astype(BF16)
        v = pm_ref[:, _MV + h * ML_DV:_MV + (h + 1) * ML_DV]
        s = _dot_nt(qb, kh.astype(BF16)) * dmat
        c_prev = c_ref[h]
        n_prev = n_ref[h:h + 1, :]
        num = _dot(s.astype(BF16), v) + w_int * _dot(qb, c_prev.astype(BF16))
        den = jnp.sum(s, axis=1, keepdims=True) + w_int * jnp.sum(qh * n_prev, axis=1, keepdims=True)
        den = jnp.maximum(jnp.abs(den), jnp.exp(-m_t))
        hcell = num * (1.0 / den)
        o = pm_ref[:, _MO + h * ML_DV:_MO + (h + 1) * ML_DV].astype(F32)
        y = _sigmoid(o) * hcell
        z = pm_ref[:, _MZ + h * ML_DV:_MZ + (h + 1) * ML_DV].astype(F32)
        out = _head_norm(y, mn_ref[:, h * ML_DV:(h + 1) * ML_DV]) * _silu(z)
        mix_ref[:, RET_HEADS * RET_DV + h * ML_DV:RET_HEADS * RET_DV + (h + 1) * ML_DV] = out.astype(mix_ref.dtype)
        kw = kh * wcol[:, h:h + 1]
        kv = _dot(kw.T.astype(BF16), v)
        ksum = jnp.sum(kw, axis=0, keepdims=True)
        sp_h = sp[h:h + 1, :]
        sc_h = scn[h:h + 1, :]
        c_ref[h] = sp_h * c_prev + sc_h * kv
        n_ref[h:h + 1, :] = sp_h * n_prev + sc_h * ksum


def _ab_mixer(pm, pg, conv_w, conv_b, gate_b, ret_norm, ml_norm, bsz, seq):
    n = pm.shape[0]
    nc = seq // CHUNK
    nqk = 2 * ML_HEADS * ML_DK
    row = lambda b, c: (b * nc + c, 0)
    const = lambda b, c: (0, 0)
    nmix = RET_HEADS * RET_DV + ML_HEADS * ML_DV
    return pl.pallas_call(
        _ab_kernel,
        grid=(bsz, nc),
        in_specs=[pl.BlockSpec((CHUNK, AB_MAIN), row), pl.BlockSpec((CHUNK, LANES), row),
                  pl.BlockSpec((ML_CONV, nqk), const), pl.BlockSpec((1, nqk), const),
                  pl.BlockSpec((1, LANES), const),
                  pl.BlockSpec((1, RET_HEADS * RET_DV), const), pl.BlockSpec((1, ML_HEADS * ML_DV), const)],
        out_specs=pl.BlockSpec((CHUNK, nmix), row),
        out_shape=jax.ShapeDtypeStruct((n, nmix), BF16),
        scratch_shapes=[pltpu.VMEM((CHUNK + 8, nqk), F32),
                        pltpu.VMEM((RET_HEADS, RET_DK, RET_DV), F32),
                        pltpu.VMEM((ML_HEADS, ML_DK, ML_DV), F32),
                        pltpu.VMEM((8, ML_DK), F32),
                        pltpu.VMEM((8, LANES), F32)],
        compiler_params=_cparams(("parallel", "arbitrary")),
        name="ab_mixer",
    )(pm, pg, conv_w, conv_b, gate_b, ret_norm, ml_norm)


_QNZ0 = 0
_KVC0 = NSA_HEADS
_KVS0 = _KVC0 + NSA_GROUPS
_KVW0 = _KVS0 + NSA_GROUPS
_SU0 = (_KVW0 + NSA_GROUPS) * LANES
_SZ0 = _SU0 + S5_WIDTH
CD_MAIN = _SZ0 + S5_WIDTH
SLOTW = NSA_GROUPS * LANES
VT_ROWS = NSA_DH + 16
LOG2E = math.log2(math.e)


def _compress_kernel(x_ref, pk_ref, pv_ref, w1k_ref, w1v_ref, w2k_ref, w2vt_ref, ekc_ref,
                     ko_ref, vot_ref, xs_ref):
    seq = x_ref.shape[0]
    nb = seq // CMP_STRIDE
    for g in range(NSA_GROUPS):
        xs_ref[...] = x_ref[:, g * LANES:(g + 1) * LANES].astype(F32)
        acc = [jnp.zeros((nb, CMP_HIDDEN), F32) for _ in range(4)]
        for r in range(CMP_STRIDE):
            xr = xs_ref[pl.ds(r, nb, stride=CMP_STRIDE), :]
            for kind, (p_ref, w1_ref) in enumerate(((pk_ref, w1k_ref), (pv_ref, w1v_ref))):
                acc[2 * kind] = acc[2 * kind] + _dot((xr + p_ref[r:r + 1, :]).astype(BF16), w1_ref[r])
                acc[2 * kind + 1] = acc[2 * kind + 1] + _dot(
                    (xr + p_ref[CMP_STRIDE + r:CMP_STRIDE + r + 1, :]).astype(BF16), w1_ref[CMP_STRIDE + r])
        hk = _gelu(acc[0] + pltpu.roll(acc[1], nb - 1, axis=0)).astype(BF16)
        hv = _gelu(acc[2] + pltpu.roll(acc[3], nb - 1, axis=0)).astype(BF16)
        ko_ref[g] = (_dot(hk, w2k_ref[...]) + ekc_ref[...]).astype(ko_ref.dtype)
        vot_ref[g] = _dot_nt(w2vt_ref[...], hv).astype(vot_ref.dtype)


def _compress(pm3, pos_k, pos_v, w1k, w1v, w2k, w2vt, ekc):
    bsz, seq, _ = pm3.shape
    nb = seq // CMP_STRIDE
    c3 = lambda b: (0, 0, 0)
    c2 = lambda b: (0, 0)
    return pl.pallas_call(
        _compress_kernel,
        grid=(bsz,),
        in_specs=[pl.BlockSpec((None, seq, SLOTW), lambda b: (b, 0, _KVC0 * LANES // SLOTW)),
                  pl.BlockSpec((CMP_LEN, LANES), c2), pl.BlockSpec((CMP_LEN, LANES), c2),
                  pl.BlockSpec((CMP_LEN, LANES, CMP_HIDDEN), c3), pl.BlockSpec((CMP_LEN, LANES, CMP_HIDDEN), c3),
                  pl.BlockSpec((CMP_HIDDEN, LANES), c2), pl.BlockSpec((LANES, CMP_HIDDEN), c2),
                  pl.BlockSpec((nb, LANES), c2)],
        out_specs=[pl.BlockSpec((None, NSA_GROUPS, nb, LANES), lambda b: (b, 0, 0, 0)),
                   pl.BlockSpec((None, NSA_GROUPS, LANES, nb), lambda b: (b, 0, 0, 0))],
        out_shape=[jax.ShapeDtypeStruct((bsz, NSA_GROUPS, nb, LANES), BF16),
                   jax.ShapeDtypeStruct((bsz, NSA_GROUPS, LANES, nb), BF16)],
        scratch_shapes=[pltpu.VMEM((seq, LANES), F32)],
        compiler_params=_cparams(("parallel",)),
        name="nsa_compress",
    )(pm3, pos_k, pos_v, w1k, w1v, w2k, w2vt, ekc)


def _nsa_kernel(q_ref, kc_ref, vct_ref, kvs_ref, kvw_ref, pg_ref,
                ek_ref, al_ref, ovt_ref, y_ref,
                ksa_ref, vsat_ref, kwa_ref, vwat_ref, qa_scr, part_scr, gzs_scr):
    seq = kvs_ref.shape[0]
    g_idx = pl.program_id(1)
    qi = pl.program_id(2)
    wtiles = WINDOW // LANES

    @pl.when(qi == 0)
    def _():
        ek = ek_ref[...]
        lane_k = lax.broadcasted_iota(jnp.int32, (seq, LANES), 1)
        zero = jnp.zeros((seq, LANES), BF16)
        ksa_ref[...] = jnp.where(lane_k < NSA_DH, kvs_ref[...], zero) + ek
        ek_w = jnp.where(lane_k >= ALI_LANE0, ek, zero)
        kwa_ref[0:WINDOW, :] = jnp.zeros((WINDOW, LANES), BF16)
        kwa_ref[WINDOW:WINDOW + seq, :] = jnp.where(lane_k < NSA_DH, kvw_ref[...], zero) + ek_w
        ones_rows = jnp.where(lax.broadcasted_iota(jnp.int32, (VT_ROWS - NSA_DH, LANES), 0) == 0, 1.0, 0.0)
        for i in range(wtiles):
            vwat_ref[i] = jnp.zeros((VT_ROWS, LANES), BF16)
        for i in range(seq // LANES):
            rws = slice(i * LANES, (i + 1) * LANES)
            for src_ref, dst_ref, j in ((kvs_ref, vsat_ref, i), (kvw_ref, vwat_ref, wtiles + i)):
                v_t = src_ref[rws, :].astype(F32).T[NSA_DH:2 * NSA_DH, :]
                dst_ref[j] = jnp.concatenate([v_t, ones_rows], axis=0).astype(BF16)

    nq = seq // TQ
    refs = (q_ref, kc_ref, vct_ref, pg_ref, al_ref, ovt_ref, y_ref, ksa_ref, vsat_ref, kwa_ref, vwat_ref,
            qa_scr, part_scr, gzs_scr)
    for step in range(nq + 1):
        pl.when(qi == step)(functools.partial(_nsa_step, step, nq, seq // SEL_BLOCK, g_idx, refs))


def _nsa_step(step, nq, nsel, g_idx, refs):
    stages = []
    if step < nq:
        stages.append(_nsa_front(nsel, step, g_idx, refs))
    if step > 0:
        stages.append(_nsa_back(step - 1, refs))
    while stages:
        for st in list(stages):
            if next(st, "done") == "done":
                stages.remove(st)


def _tile_iotas():
    rows = NSA_HPG * TQ
    jrow = lax.broadcasted_iota(jnp.int32, (LANES, rows), 0)
    tq = lax.broadcasted_iota(jnp.int32, (LANES, rows), 1) & (TQ - 1)
    return jrow, tq, jrow - tq


def _nsa_front(nsel, qi, g_idx, refs):
    (q_ref, kc_ref, vct_ref, pg_ref, al_ref, ovt_ref, _, _, _, kwa_ref, vwat_ref,
     qa_scr, part_scr, gzs_scr) = refs
    hp = NSA_HPG
    span = WINDOW + TQ
    t0 = qi * TQ
    tw = t0
    early = t0 < WINDOW
    par = qi & 1
    jrow, tq, kmt = _tile_iotas()

    qnz = [q_ref[:, hh * LANES:(hh + 1) * LANES].astype(F32) for hh in range(hp)]
    lane_q = lax.broadcasted_iota(jnp.int32, (TQ, LANES), 1)
    qal = jnp.concatenate([jnp.where(lane_q < NSA_DH, qnz[hh], al_ref[pl.ds(g_idx * hp + hh, 1), :])
                           for hh in range(hp)], axis=0)
    qalb = qal.astype(BF16)
    s_c = _dot_nt(kc_ref[...], qalb)
    s_w = _dot_nt(kwa_ref[pl.ds(tw, span), :], qalb)
    yield

    s = jnp.where(tq - jrow * CMP_STRIDE >= (CMP_LEN - 1) - t0, s_c, NEG)
    e = jnp.exp2(s - jnp.max(s, axis=0, keepdims=True))
    inv = 1.0 / jnp.sum(e, axis=0, keepdims=True)
    if early:
        inv = jnp.where(t0 + tq[0:1, :] >= CMP_LEN - 1, inv, 0.0)
    p = e * inv
    oc_t = _dot(vct_ref[...], p.astype(BF16))
    psum_t = p[:, 0:TQ]
    for hh in range(1, hp):
        psum_t = psum_t + p[:, hh * TQ:(hh + 1) * TQ]
    p_hi = psum_t.astype(BF16)
    p_lo = (psum_t - p_hi.astype(F32)).astype(BF16)
    imp_t = _dot(ovt_ref[...], p_hi) + _dot(ovt_ref[...], p_lo)
    yield

    blocks = [s_w[i * LANES:(i + 1) * LANES, :] for i in range(span // LANES)]
    for i in range(span // LANES):
        if i * LANES < TQ:
            blocks[i] = jnp.where(kmt > -i * LANES, blocks[i], NEG)
        if (i + 1) * LANES > WINDOW:
            blocks[i] = jnp.where(kmt <= WINDOW - i * LANES, blocks[i], NEG)
        if early:
            blocks[i] = jnp.where(jrow >= WINDOW - t0 - i * LANES, blocks[i], NEG)
    s = jnp.concatenate(blocks, axis=0)
    p = jnp.exp2(s - jnp.max(s, axis=0, keepdims=True)).astype(BF16)
    vw_t = jnp.concatenate([vwat_ref[qi * (TQ // LANES) + i] for i in range(span // LANES)], axis=1)
    accw_t = _dot(vw_t, p)
    yield

    score = imp_t[SEL_LANE0:SEL_LANE0 + nsel, :]
    nrow = lax.broadcasted_iota(jnp.int32, (nsel, TQ), 0)
    own = (t0 + lax.broadcasted_iota(jnp.int32, (nsel, TQ), 1)) >> int(math.log2(SEL_BLOCK))
    valid = nrow <= own
    forced = (nrow == 0) | (nrow == own)
    score = jnp.where(forced, BIG, jnp.where(valid, score, -1.0))
    chosen = jnp.zeros((nsel, TQ), F32)
    nrow_f = nrow.astype(F32)
    for _ in range(SEL_TOPN):
        mx = jnp.max(score, axis=0, keepdims=True)
        first = jnp.min(jnp.where(score == mx, nrow_f, 4.0 * LANES), axis=0, keepdims=True)
        hit = nrow_f == first
        chosen = jnp.where(hit, 1.0, chosen)
        score = jnp.where(hit, -3.0, score)
    mask_t = jnp.where(chosen > 0.0, jnp.where(valid, 0.0, NEG), NEG)
    maskcols = jnp.concatenate([jnp.zeros((SEL_LANE0, TQ), F32), mask_t,
                                jnp.zeros((LANES - SEL_LANE0 - nsel, TQ), F32)], axis=0).T

    qa_scr[par] = (qal + jnp.concatenate([maskcols] * hp, axis=0)).astype(BF16)

    gt_t = _sigmoid(pg_ref[...]).T
    for hh in range(hp):
        g_c, g_s, g_w = [gt_t[j * hp + hh:j * hp + hh + 1, :] for j in range(3)]
        hcols = slice(hh * TQ, (hh + 1) * TQ)
        aw_t = accw_t[:, hcols]
        gz = _silu(qnz[hh].T[NSA_DH:2 * NSA_DH, :])
        o = g_c * oc_t[0:NSA_DH, hcols] + (g_w * (1.0 / aw_t[NSA_DH:NSA_DH + 1, :])) * aw_t[0:NSA_DH, :]
        part_scr[par, hh * NSA_DH:(hh + 1) * NSA_DH, :] = o * gz
        gzs_scr[par, hh * NSA_DH:(hh + 1) * NSA_DH, :] = g_s * gz


def _nsa_back(qi, refs):
    _, _, _, _, _, _, y_ref, ksa_ref, vsat_ref, _, _, qa_scr, part_scr, gzs_scr = refs
    hp = NSA_HPG
    t0 = qi * TQ
    par = qi & 1
    _, _, kmt = _tile_iotas()
    qa = qa_scr[par]
    bounds = [(k0, min(k0 + KCH, t0 + TQ)) for k0 in range(0, t0 + TQ, KCH)]
    nk = len(bounds)

    def scores(c):
        k0, k1 = bounds[c]
        s = _dot_nt(ksa_ref[k0:k1, :], qa)
        if k1 > t0:
            s = jnp.concatenate(
                [jnp.where(kmt <= t0 - kb, s[kb - k0:kb - k0 + LANES, :], NEG) if kb + LANES > t0
                 else s[kb - k0:kb - k0 + LANES, :] for kb in range(k0, k1, LANES)], axis=0)
        return s

    def weighted(c, s):
        k0, k1 = bounds[c]
        mc = jnp.max(s, axis=0, keepdims=True)
        v_t = jnp.concatenate([vsat_ref[i] for i in range(k0 // LANES, k1 // LANES)], axis=1)
        return mc, _dot(v_t, jnp.exp2(s - mc).astype(BF16))

    parts = []
    s_next = scores(0)
    for c in range(nk):
        s_cur = s_next
        if c + 1 < nk:
            s_next = scores(c + 1)
        yield
        parts.append(weighted(c, s_cur))
    yield
    m = parts[0][0]
    for mc, _ in parts[1:]:
        m = jnp.maximum(m, mc)
    accs_t = None
    for mc, d in parts:
        d = d if nk == 1 else jnp.exp2(mc - m) * d
        accs_t = d if accs_t is None else accs_t + d

    outs = []
    for hh in range(hp):
        as_t = accs_t[:, hh * TQ:(hh + 1) * TQ]
        rws = slice(hh * NSA_DH, (hh + 1) * NSA_DH)
        outs.append(part_scr[par, rws, :] + (gzs_scr[par, rws, :] * (1.0 / as_t[NSA_DH:NSA_DH + 1, :])) * as_t[0:NSA_DH, :])
    for pr in range(hp // 2):
        packed = jnp.concatenate([outs[2 * pr], outs[2 * pr + 1]], axis=0).T
        y_ref[:, pr * LANES:(pr + 1) * LANES] = packed.astype(y_ref.dtype)


def _nsa(pm3, pg3, kcc, vct, ek, al, ovt):
    bsz, seq, _ = pm3.shape
    nq = seq // TQ
    gw = NSA_HPG * NSA_DH
    kv = lambda tile0: pl.BlockSpec((None, seq, LANES), lambda b, g, q: (b, 0, tile0 + g))
    c2 = lambda b, g, q: (0, 0)
    front = lambda b, g, q: (b, jnp.minimum(q, nq - 1), g)
    back = lambda b, g, q: (b, jnp.maximum(q - 1, 0), g)
    return pl.pallas_call(
        _nsa_kernel,
        grid=(bsz, NSA_GROUPS, nq + 1),
        in_specs=[pl.BlockSpec((None, TQ, NSA_HPG * LANES), front),
                  pl.BlockSpec((None, None, LANES, LANES), lambda b, g, q: (b, g, 0, 0)),
                  pl.BlockSpec((None, None, LANES, LANES), lambda b, g, q: (b, g, 0, 0)),
                  kv(_KVS0), kv(_KVW0),
                  pl.BlockSpec((None, TQ, LANES), front),
                  pl.BlockSpec((seq, LANES), c2),
                  pl.BlockSpec((NSA_HEADS, LANES), c2),
                  pl.BlockSpec((LANES, LANES), c2)],
        out_specs=pl.BlockSpec((None, TQ, gw), back),
        out_shape=jax.ShapeDtypeStruct((bsz, seq, NSA_HEADS * NSA_DH), BF16),
        scratch_shapes=[pltpu.VMEM((seq, LANES), BF16), pltpu.VMEM((seq // LANES, VT_ROWS, LANES), BF16),
                        pltpu.VMEM((seq + WINDOW, LANES), BF16),
                        pltpu.VMEM(((seq + WINDOW) // LANES, VT_ROWS, LANES), BF16),
                        pltpu.VMEM((2, NSA_HPG * TQ, LANES), BF16),
                        pltpu.VMEM((2, gw, TQ), F32), pltpu.VMEM((2, gw, TQ), F32)],
        compiler_params=_cparams(("parallel", "arbitrary", "arbitrary")),
        name="nsa_attn",
    )(pm3, kcc, vct, pm3, pm3, pg3, ek, al, ovt)


def _nsa_tables(seq):
    t = np.arange(seq)
    ek = np.zeros((seq, LANES), np.float32)
    ek[t, SEL_LANE0 + t // SEL_BLOCK] = 1.0
    for c in range(3):
        ek[:, ALI_LANE0 + c] = t // SEL_BLOCK
        ek[:, ALI_LANE0 + 3 + c] = t % SEL_BLOCK
    slopes = jnp.exp2(-8.0 * jnp.arange(1, NSA_HEADS + 1, dtype=F32) / NSA_HEADS) * LOG2E
    s_hi, s_mid, s_lo = _split3(slopes)
    parts = [p.astype(F32) for p in (s_hi, s_mid, s_lo)]
    al = jnp.zeros((NSA_HEADS, LANES), F32)
    for c in range(3):
        al = al.at[:, ALI_LANE0 + c].set(parts[c] * SEL_BLOCK)
        al = al.at[:, ALI_LANE0 + 3 + c].set(parts[c])
    nb = seq // CMP_STRIDE
    cstart = np.arange(nb) * CMP_STRIDE
    cend = cstart + CMP_LEN - 1
    ekc = np.zeros((nb, LANES), np.float32)
    for c in range(3):
        ekc[:, ALI_LANE0 + c] = cend // SEL_BLOCK
        ekc[:, ALI_LANE0 + 3 + c] = cend % SEL_BLOCK
    sel = np.arange(seq // SEL_BLOCK)
    ovt = np.zeros((LANES, max(nb, LANES)), np.float32)
    ovl = (cstart[:, None] < (sel[None, :] + 1) * SEL_BLOCK) & (cstart[:, None] + CMP_LEN > sel[None, :] * SEL_BLOCK)
    ovl[nb - 1, :] = False
    ovt[SEL_LANE0:SEL_LANE0 + len(sel), :nb] = ovl.T
    return jnp.asarray(ek, BF16), al, jnp.asarray(ekc), jnp.asarray(ovt, BF16)


def _s5_disc_kernel(are_ref, aim_ref, ldt_ref, bre_ref, bim_ref, abr_ref, abi_ref, bbr_ref, bbi_ref):
    a_re = are_ref[...]
    a_im = aim_ref[...]
    dt = jnp.exp(ldt_ref[...])
    er = jnp.exp(a_re * dt)
    abar_re = er * jnp.cos(a_im * dt)
    abar_im = er * jnp.sin(a_im * dt)
    lam2 = a_re * a_re + a_im * a_im
    cr = ((abar_re - 1.0) * a_re + abar_im * a_im) / lam2
    ci = (abar_im * a_re - (abar_re - 1.0) * a_im) / lam2
    b_re = bre_ref[...]
    b_im = bim_ref[...]
    abr_ref[...] = abar_re
    abi_ref[...] = abar_im
    bbr_ref[...] = cr * b_re - ci * b_im
    bbi_ref[...] = cr * b_im + ci * b_re


def _s5_disc(a_re_rep, a_im_rep, ldt_rep, b_re_t, b_im_t):
    shp = jax.ShapeDtypeStruct(a_re_rep.shape, F32)
    return pl.pallas_call(_s5_disc_kernel, out_shape=[shp] * 4, name="s5_disc")(
        a_re_rep, a_im_rep, ldt_rep, b_re_t, b_im_t)


S5_SLABG = 4
S5_TT = 64
S5_PITCH = S5_TT + 4
S5_DIAG = 256
S5_NSTATE_PER_DIAG = S5_DIAG // S5_GROUP_CH * S5_STATE


def _s5_kernel(u_ref, bre_ref, bim_ref, ar_ref, ai_ref, cre_ref, cim_ref, d_ref, wg_ref, o_ref,
               xr_ref, xi_ref, sr_ref, si_ref):
    nb, tt, wd = u_ref.shape
    pitch = sr_ref.shape[1] // nb
    gcols = S5_SLABG * LANES
    ngroups = S5_NSTATE // gcols

    @pl.when(pl.program_id(0) == 0)
    def _():
        xr_ref[...] = jnp.zeros(xr_ref.shape, F32)
        xi_ref[...] = jnp.zeros(xi_ref.shape, F32)

    u = u_ref[...].reshape(nb * tt, wd)

    chan = lambda kg: slice(kg * gcols // S5_NSTATE_PER_DIAG * S5_DIAG,
                            (kg * gcols // S5_NSTATE_PER_DIAG + 1) * S5_DIAG)
    for kg in range(ngroups):
        cols = slice(kg * gcols, (kg + 1) * gcols)
        br = _dot(u[:, chan(kg)], bre_ref[chan(kg), cols])
        bi = _dot(u[:, chan(kg)], bim_ref[chan(kg), cols])
        for j in range(S5_SLABG):
            for b in range(nb):
                sr_ref[kg * S5_SLABG + j, b * pitch:b * pitch + tt, :] = br[b * tt:(b + 1) * tt, j * LANES:(j + 1) * LANES]
                si_ref[kg * S5_SLABG + j, b * pitch:b * pitch + tt, :] = bi[b * tt:(b + 1) * tt, j * LANES:(j + 1) * LANES]

    for kg in range(ngroups):
        cols = slice(kg * gcols, (kg + 1) * gcols)
        slabs = range(kg * S5_SLABG, (kg + 1) * S5_SLABG)
        ar = ar_ref[:, cols]
        ai = ai_ref[:, cols]

        def body(t, carry, slabs=slabs, ar=ar, ai=ai):
            xr, xi = carry
            rows = pl.ds(t, nb, stride=pitch)
            nxr = ar * xr - ai * xi + jnp.concatenate([sr_ref[k, rows, :] for k in slabs], axis=1)
            nxi = ar * xi + ai * xr + jnp.concatenate([si_ref[k, rows, :] for k in slabs], axis=1)
            for j, k in enumerate(slabs):
                sr_ref[k, rows, :] = nxr[:, j * LANES:(j + 1) * LANES]
                si_ref[k, rows, :] = nxi[:, j * LANES:(j + 1) * LANES]
            return nxr, nxi

        xr, xi = lax.fori_loop(0, tt, body, (xr_ref[:, cols], xi_ref[:, cols]), unroll=4)
        xr_ref[:, cols] = xr
        xi_ref[:, cols] = xi

    y_blocks = [jnp.zeros((nb * tt, S5_DIAG), F32) for _ in range(wd // S5_DIAG)]
    for kg in range(ngroups):
        cols = slice(kg * gcols, (kg + 1) * gcols)
        gather = lambda ref: jnp.concatenate(
            [jnp.concatenate([ref[kg * S5_SLABG + j, b * pitch:b * pitch + tt, :] for j in range(S5_SLABG)], axis=1)
             for b in range(nb)], axis=0).astype(BF16)
        blk = chan(kg).start // S5_DIAG
        y_blocks[blk] = (y_blocks[blk] + _dot(gather(sr_ref), cre_ref[cols, chan(kg)])
                         - _dot(gather(si_ref), cim_ref[cols, chan(kg)]))
    y = jnp.concatenate(y_blocks, axis=1)
    y = _gelu(y + d_ref[...] * u.astype(F32))
    z = _dot(y.astype(BF16), wg_ref[...])
    o = z[:, :S5_WIDTH] * _sigmoid(z[:, S5_WIDTH:])
    o_ref[...] = o.reshape(nb, tt, wd).astype(o_ref.dtype)


def _s5(pm3, bbd_re, bbd_im, ar8, ai8, cbd_re, cbd_im, d_row, w_glu):
    bsz, seq, _ = pm3.shape
    tt = S5_TT
    c2 = lambda i: (0, 0)
    nslab = S5_NSTATE // LANES
    return pl.pallas_call(
        _s5_kernel,
        grid=(seq // tt,),
        in_specs=[pl.BlockSpec((bsz, tt, S5_WIDTH), lambda i: (0, i, _SU0 // S5_WIDTH)),
                  pl.BlockSpec((S5_WIDTH, S5_NSTATE), c2), pl.BlockSpec((S5_WIDTH, S5_NSTATE), c2),
                  pl.BlockSpec((8, S5_NSTATE), c2), pl.BlockSpec((8, S5_NSTATE), c2),
                  pl.BlockSpec((S5_NSTATE, S5_WIDTH), c2), pl.BlockSpec((S5_NSTATE, S5_WIDTH), c2),
                  pl.BlockSpec((1, S5_WIDTH), c2), pl.BlockSpec((S5_WIDTH, 2 * S5_WIDTH), c2)],
        out_specs=pl.BlockSpec((bsz, tt, S5_WIDTH), lambda i: (0, i, 0)),
        out_shape=jax.ShapeDtypeStruct((bsz, seq, S5_WIDTH), BF16),
        scratch_shapes=[pltpu.VMEM((8, S5_NSTATE), F32), pltpu.VMEM((8, S5_NSTATE), F32),
                        pltpu.VMEM((nslab, bsz * S5_PITCH, LANES), F32),
                        pltpu.VMEM((nslab, bsz * S5_PITCH, LANES), F32)],
        compiler_params=_cparams(("arbitrary",)),
        name="s5_scan",
    )(pm3, bbd_re, bbd_im, ar8, ai8, cbd_re, cbd_im, d_row, w_glu)


def _pad_cols(w, n):
    return jnp.pad(w, ((0, 0), (0, n - w.shape[1])))


def _cd_proj_weights(w_in):
    d = w_in.shape[0]
    nw, nkv = NSA_HEADS * NSA_DH, NSA_GROUPS * NSA_DH
    offs = np.cumsum((0, nw, nkv, nkv, nkv, nkv, nkv, nkv, 3 * NSA_HEADS, nw, S5_WIDTH, S5_WIDTH))
    seg = lambda i: w_in[:, offs[i]:offs[i + 1]]
    pair = lambda a, b: jnp.stack([a.reshape(d, -1, NSA_DH), b.reshape(d, -1, NSA_DH)], axis=2).reshape(d, -1)
    w_main = jnp.concatenate(
        [pair(seg(0) * (NSA_DH ** -0.5 * LOG2E), seg(8)), pair(seg(1), seg(2)), pair(seg(3), seg(4)),
         pair(seg(5), seg(6)), seg(9), seg(10)], axis=1).astype(BF16)
    w_gate = seg(7).reshape(d, NSA_GROUPS, NSA_HPG, 3).transpose(0, 1, 3, 2).reshape(d, NSA_GROUPS, 3 * NSA_HPG)
    w_gate = jnp.pad(w_gate, ((0, 0), (0, 0), (0, LANES - 3 * NSA_HPG))).reshape(d, NSA_GROUPS * LANES).astype(BF16)
    return w_gate, w_main


def _layer0(h2, p4, layer, bsz, seq, norm_pre, norm_post, pe_proj, pe_gate,
            w_in, ret_norm, conv_w, conv_b, gate_b, ml_norm, w_out):
    d = h2.shape[1]
    g0 = 5120
    w_main = jnp.concatenate([w_in[:, :g0], w_in[:, g0 + 2 * ML_HEADS:]], axis=1).astype(BF16)
    w_gate = _pad_cols(w_in[:, g0:g0 + 2 * ML_HEADS], LANES).astype(BF16)
    pm, pg = _norm_proj(h2, norm_pre.reshape(1, d), w_gate, w_main)
    mix = _ab_mixer(pm, pg, conv_w, conv_b.reshape(1, -1), _pad_cols(gate_b.reshape(1, -1), LANES),
                    ret_norm.reshape(1, -1), ml_norm.reshape(1, -1), bsz, seq)
    ka = RET_HEADS * RET_DV
    nt = seq // 512
    return _out_pe(mix, lambda b, s: (b * nt + s, 0), ka, mix, lambda b, s: (b * nt + s, 1), ML_HEADS * ML_DV,
                   None, None, h2, p4, layer, w_out[:ka].astype(BF16), w_out[ka:].astype(BF16),
                   norm_post.reshape(1, d), pe_gate.astype(BF16), pe_proj.astype(BF16), bsz, seq)


def _layer1(h2, p4, layer, bsz, seq, norm_pre, norm_post, pe_proj, pe_gate,
            w_in, pos_k, pos_v, w1_k, w2_k, w1_v, w2_v,
            a_re, a_im, log_dt, b_re, b_im, c_re, c_im, d_skip, w_glu, w_out):
    d = h2.shape[1]
    nw = NSA_HEADS * NSA_DH
    pm, pg = _norm_proj(h2, norm_pre.reshape(1, d), *_cd_proj_weights(w_in))

    nb = seq // CMP_STRIDE
    pm3 = pm.reshape(bsz, seq, CD_MAIN)
    padl = lambda a, lo=True: jnp.pad(a, ((0, 0),) * (a.ndim - 1)
                                      + ((0, LANES - a.shape[-1]) if lo else (LANES - a.shape[-1], 0),))
    w1 = lambda w, lo: jnp.pad(w.reshape(CMP_LEN, NSA_DH, CMP_HIDDEN),
                               ((0, 0), (0, LANES - NSA_DH) if lo else (LANES - NSA_DH, 0), (0, 0))).astype(BF16)
    ek, al, ekc, ovt = _nsa_tables(seq)
    kcc, vct = _compress(pm3, padl(pos_k), padl(pos_v, False), w1(w1_k, True), w1(w1_v, False),
                         padl(w2_k).astype(BF16), padl(w2_v).T.astype(BF16), ekc)
    if nb < LANES:
        kcc = jnp.pad(kcc, ((0, 0), (0, 0), (0, LANES - nb), (0, 0)))
        vct = jnp.pad(vct, ((0, 0), (0, 0), (0, 0), (0, LANES - nb)))
    y_nsa = _nsa(pm3, pg.reshape(bsz, seq, NSA_GROUPS * LANES), kcc, vct, ek, al, ovt)

    rep = lambda a: jnp.repeat(a, S5_GROUP_CH, axis=0)
    ldt = jnp.broadcast_to(rep(log_dt[:, None]), (S5_WIDTH, S5_STATE))
    bt = lambda b: b.transpose(0, 2, 1).reshape(S5_WIDTH, S5_STATE)
    abr, abi, bbr, bbi = _s5_disc(rep(a_re), rep(a_im), ldt, bt(b_re), bt(b_im))
    eye = jnp.eye(S5_GROUPS, dtype=F32)
    bd_in = lambda bb: (bb.reshape(S5_GROUPS, S5_GROUP_CH, 1, S5_STATE)
                        * eye[:, None, :, None]).reshape(S5_WIDTH, S5_NSTATE).astype(BF16)
    bd_out = lambda c: (c.transpose(0, 2, 1).reshape(S5_GROUPS, S5_STATE, 1, S5_GROUP_CH)
                        * eye[:, None, :, None]).reshape(S5_NSTATE, S5_WIDTH).astype(BF16)
    row8 = lambda a: jnp.broadcast_to(a[::S5_GROUP_CH].reshape(1, S5_NSTATE), (8, S5_NSTATE))
    y_s5 = _s5(pm3, bd_in(bbr), bd_in(bbi), row8(abr), row8(abi),
               bd_out(c_re), bd_out(c_im), d_skip.reshape(1, -1), w_glu.astype(BF16))

    nt = seq // 512
    szb = _SZ0 // S5_WIDTH
    return _out_pe(y_nsa.reshape(bsz * seq, nw), lambda b, s: (b * nt + s, 0), nw,
                   y_s5.reshape(bsz * seq, S5_WIDTH), lambda b, s: (b * nt + s, 0), S5_WIDTH,
                   pm, lambda b, s: (b * nt + s, szb),
                   h2, p4, layer, w_out[:nw].astype(BF16), w_out[nw:].astype(BF16),
                   norm_post.reshape(1, d), pe_gate.astype(BF16), pe_proj.astype(BF16), bsz, seq)


def kernel(x, p, norm_pre, norm_post, pe_proj, pe_gate, ab_w_in, ret_norm, ml_conv_w, ml_conv_b, ml_gate_b,
           ml_norm, ab_w_out, cd_w_in, cmp_pos_k, cmp_pos_v, cmp_w1_k, cmp_w2_k, cmp_w1_v, cmp_w2_v,
           s5_a_re, s5_a_im, s5_log_dt, s5_b_re, s5_b_im, s5_c_re, s5_c_im, s5_d, s5_w_glu, cd_w_out):
    bsz, seq, d = x.shape
    assert bsz == 8, "the S5 scan maps the batch onto the 8 sublanes of a vreg"
    assert seq % max(KCH, 1024) == 0
    depth = p.shape[0]
    h2 = x.reshape(bsz * seq, d)
    for i in range(depth):
        j = i // 2
        if i % 2 == 0:
            h2 = _layer0(h2, p, i, bsz, seq, norm_pre[i], norm_post[i], pe_proj[i], pe_gate[i],
                         ab_w_in[j], ret_norm[j], ml_conv_w[j], ml_conv_b[j], ml_gate_b[j], ml_norm[j], ab_w_out[j])
        else:
            h2 = _layer1(h2, p, i, bsz, seq, norm_pre[i], norm_post[i], pe_proj[i], pe_gate[i],
                         cd_w_in[j], cmp_pos_k[j], cmp_pos_v[j], cmp_w1_k[j], cmp_w2_k[j], cmp_w1_v[j], cmp_w2_v[j],
                         s5_a_re[j], s5_a_im[j], s5_log_dt[j], s5_b_re[j], s5_b_im[j], s5_c_re[j], s5_c_im[j],
                         s5_d[j], s5_w_glu[j], cd_w_out[j])
    return h2.reshape(bsz, seq, d)
```

```python
import functools
import math

import numpy as np
import jax
import jax.numpy as jnp
from jax import lax
from jax.experimental import pallas as pl
from jax.experimental.pallas import tpu as pltpu

F32 = jnp.float32
BF16 = jnp.bfloat16
EPS = 1e-6
NEG = -1e30
BIG = 1e30

LANES = 128
CHUNK = 128
RET_HEADS, RET_DK, RET_DV = 4, 128, 256
ML_HEADS, ML_DK, ML_DV = 4, 128, 256
ML_CONV = 4
NSA_HEADS, NSA_GROUPS, NSA_DH = 16, 4, 64
NSA_HPG = NSA_HEADS // NSA_GROUPS
CMP_STRIDE, CMP_LEN, CMP_HIDDEN = 16, 32, 128
SEL_BLOCK, SEL_TOPN, WINDOW = 64, 4, 512
S5_GROUPS, S5_GROUP_CH, S5_STATE = 32, 16, 64
S5_WIDTH = S5_GROUPS * S5_GROUP_CH
S5_NSTATE = S5_GROUPS * S5_STATE

TQ = 256
KCH = 512
SEL_LANE0 = 64
ALI_LANE0 = 96
VMEM_LIMIT = 56 * 1024 * 1024


def _cparams(sem):
    return pltpu.CompilerParams(dimension_semantics=sem, vmem_limit_bytes=VMEM_LIMIT)


def _dot(a, b):
    return jnp.dot(a, b, preferred_element_type=F32)


def _dot_nt(a, b):
    return lax.dot_general(a, b, (((1,), (1,)), ((), ())), preferred_element_type=F32)


def _split3(x):
    hi = x.astype(BF16)
    r1 = x - hi.astype(F32)
    mid = r1.astype(BF16)
    lo = (r1 - mid.astype(F32)).astype(BF16)
    return hi, mid, lo


def _dot_f32_lhs(x, m):
    hi, mid, lo = _split3(x)
    return _dot(hi, m) + _dot(mid, m) + _dot(lo, m)


def _sigmoid(x):
    return 1.0 / (1.0 + jnp.exp(-x))


def _silu(x):
    return x * _sigmoid(x)


def _gelu(x):
    return 0.5 * x * (1.0 + jnp.tanh(math.sqrt(2.0 / math.pi) * (x + 0.044715 * (x * x * x))))


def _rms(x, g):
    return x * lax.rsqrt(jnp.mean(x * x, -1, keepdims=True) + EPS) * g


PROJ_TN = 1024


def _norm_proj_kernel(h_ref, g_ref, wg_ref, wm_ref, om_ref, og_ref):
    ub = _rms(h_ref[...], g_ref[...]).astype(BF16)
    nm = om_ref.shape[1]
    for c0 in range(0, nm, PROJ_TN):
        c1 = min(c0 + PROJ_TN, nm)
        om_ref[:, c0:c1] = _dot(ub, wm_ref[:, c0:c1]).astype(om_ref.dtype)
    og_ref[...] = _dot(ub, wg_ref[...])


def _norm_proj(h2, g, w_gate, w_main, tm=512):
    n, d = h2.shape
    nm = w_main.shape[1]
    ng = w_gate.shape[1]
    once = pl.Buffered(1)
    return pl.pallas_call(
        _norm_proj_kernel,
        grid=(n // tm,),
        in_specs=[pl.BlockSpec((tm, d), lambda i: (i, 0)),
                  pl.BlockSpec((1, d), lambda i: (0, 0)),
                  pl.BlockSpec((d, ng), lambda i: (0, 0), pipeline_mode=once),
                  pl.BlockSpec((d, nm), lambda i: (0, 0), pipeline_mode=once)],
        out_specs=[pl.BlockSpec((tm, nm), lambda i: (i, 0)),
                   pl.BlockSpec((tm, ng), lambda i: (i, 0))],
        out_shape=[jax.ShapeDtypeStruct((n, nm), BF16), jax.ShapeDtypeStruct((n, ng), F32)],
        compiler_params=_cparams(("parallel",)),
        name="norm_proj",
    )(h2, g, w_gate, w_main)


def _out_pe_kernel(gated, *refs):
    if gated:
        ma_ref, mb_ref, sz_ref, h_ref, p_ref, wa_ref, wb_ref, gp_ref, wg_ref, wp_ref, o_ref = refs
        sz = sz_ref[...].astype(F32)
        mb = (mb_ref[...].astype(F32) * _silu(sz)).astype(BF16)
    else:
        ma_ref, mb_ref, h_ref, p_ref, wa_ref, wb_ref, gp_ref, wg_ref, wp_ref, o_ref = refs
        mb = mb_ref[...]
    y = _dot(ma_ref[...], wa_ref[...]) + _dot(mb, wb_ref[...])
    h1 = h_ref[...] + _rms(y, gp_ref[...])
    gate = _sigmoid(_dot(h1.astype(BF16), wg_ref[...]))
    pp = _dot(p_ref[...].astype(BF16), wp_ref[...])
    o_ref[...] = h1 + gate * pp


def _out_pe(mix_a, a_map, ka, mix_b, b_map, kb, sz, sz_map, h2, p4, layer, w_a, w_b, g_post, w_gate, w_pe,
            bsz, seq, tm=512):
    n, d = h2.shape
    pe = p4.shape[-1]
    nt = seq // tm
    row = lambda b, s: (b * nt + s, 0)
    const = lambda b, s: (0, 0)
    gated = sz is not None
    in_specs = [pl.BlockSpec((tm, ka), a_map), pl.BlockSpec((tm, kb), b_map)]
    args = [mix_a, mix_b]
    if gated:
        in_specs.append(pl.BlockSpec((tm, kb), sz_map))
        args.append(sz)
    in_specs += [pl.BlockSpec((tm, d), row), pl.BlockSpec((None, None, tm, pe), lambda b, s: (layer, b, s, 0)),
                 pl.BlockSpec((ka, d), const), pl.BlockSpec((kb, d), const), pl.BlockSpec((1, d), const),
                 pl.BlockSpec((d, d), const), pl.BlockSpec((pe, d), const)]
    args += [h2, p4, w_a, w_b, g_post, w_gate, w_pe]
    return pl.pallas_call(
        functools.partial(_out_pe_kernel, gated),
        grid=(bsz, nt),
        in_specs=in_specs,
        out_specs=pl.BlockSpec((tm, d), row),
        out_shape=jax.ShapeDtypeStruct((n, d), F32),
        compiler_params=_cparams(("parallel", "arbitrary")),
        name="out_pe_gated" if gated else "out_pe",
    )(*args)


_RQ, _RK, _RV, _RG = 0, 512, 1024, 2048
_MQ, _MK, _MV, _MO, _MZ = 3072, 3584, 4096, 5120, 6144
AB_MAIN = 7168


def _head_norm(y, g):
    yc = y - jnp.mean(y, -1, keepdims=True)
    return yc * lax.rsqrt(jnp.mean(yc * yc, -1, keepdims=True) + EPS) * g


def _ab_kernel(pm_ref, pg_ref, cw_ref, cb_ref, gb_ref, rn_ref, mn_ref, mix_ref,
               ext_ref, r_ref, c_ref, n_ref, m_ref):
    L = CHUNK
    c_idx = pl.program_id(1)

    @pl.when(c_idx == 0)
    def _():
        ext_ref[0:8, :] = jnp.zeros((8, ext_ref.shape[1]), F32)
        r_ref[...] = jnp.zeros(r_ref.shape, F32)
        c_ref[...] = jnp.zeros(c_ref.shape, F32)
        n_ref[...] = jnp.zeros(n_ref.shape, F32)
        m_ref[...] = jnp.full(m_ref.shape, NEG, F32)

    row = lax.broadcasted_iota(jnp.int32, (L, L), 0)
    col = lax.broadcasted_iota(jnp.int32, (L, L), 1)
    causal = row >= col
    diff = (row - col).astype(F32)
    tcol = lax.broadcasted_iota(jnp.int32, (L, 1), 0).astype(F32)

    for h in range(RET_HEADS):
        lg = math.log1p(-(2.0 ** (-5.0 - h)))
        scale = RET_DK ** -0.5
        decay = jnp.where(causal, jnp.exp(jnp.maximum(diff, 0.0) * lg), 0.0) * scale
        q = pm_ref[:, _RQ + h * RET_DK:_RQ + (h + 1) * RET_DK]
        k = pm_ref[:, _RK + h * RET_DK:_RK + (h + 1) * RET_DK]
        v = pm_ref[:, _RV + h * RET_DV:_RV + (h + 1) * RET_DV]
        sc = _dot_nt(q, k) * decay
        xi = jnp.exp((tcol + 1.0) * lg) * scale
        qx = (q.astype(F32) * xi).astype(BF16)
        r_prev = r_ref[h]
        y = _dot(sc.astype(BF16), v) + _dot(qx, r_prev.astype(BF16))
        zeta = jnp.exp((L - 1.0 - tcol) * lg)
        kz_t = (k.astype(F32) * zeta).T.astype(BF16)
        r_ref[h] = r_prev * math.exp(L * lg) + _dot(kz_t, v)
        g = pm_ref[:, _RG + h * RET_DV:_RG + (h + 1) * RET_DV].astype(F32)
        out = _head_norm(y, rn_ref[:, h * RET_DV:(h + 1) * RET_DV]) * _silu(g)
        mix_ref[:, h * RET_DV:(h + 1) * RET_DV] = out.astype(mix_ref.dtype)

    nqk = 2 * ML_HEADS * ML_DK
    x = pm_ref[:, _MQ:_MQ + nqk].astype(F32)
    ext_ref[8:8 + L, :] = x
    acc = cb_ref[...] + cw_ref[ML_CONV - 1:ML_CONV, :] * x
    for j in range(ML_CONV - 1):
        acc = acc + cw_ref[j:j + 1, :] * ext_ref[pl.ds(8 - (ML_CONV - 1) + j, L), :]
    ext_ref[0:8, :] = x[L - 8:L, :]
    qk = _silu(acc)

    gt = (pg_ref[...] + gb_ref[...]).T
    g8 = gt[0:8, :]
    logf8 = jnp.minimum(g8, 0.0) - jnp.log(1.0 + jnp.exp(-jnp.abs(g8)))
    triu = jnp.where(row <= col, 1.0, 0.0).astype(BF16)
    bcum8 = _dot_f32_lhs(logf8, triu)
    b_t = pltpu.roll(bcum8, 4, axis=0)
    b_last = b_t[:, L - 1:L]
    a_t = b_last - b_t + g8
    mu = jnp.max(a_t, axis=1, keepdims=True)
    w_t = jnp.exp(a_t - mu)
    r_t = g8 - b_t
    zpad = jnp.zeros((L - 8, L), F32)
    bcol = jnp.concatenate([b_t, zpad], axis=0).T
    wcol = jnp.concatenate([w_t, zpad], axis=0).T
    m_prev = m_ref[...][:, 0:1]
    m_new = jnp.maximum(b_last + m_prev, mu)
    sp = jnp.exp(b_last + m_prev - m_new)
    scn = jnp.exp(mu - m_new)
    m_ref[...] = jnp.broadcast_to(m_new, m_ref.shape)

    for h in range(ML_HEADS):
        bc = bcol[:, h:h + 1]
        log_d = jnp.where(causal, bc + r_t[h:h + 1, :], NEG)
        inter = bc + m_prev[h:h + 1, :]
        m_t = jnp.maximum(inter, jnp.max(log_d, axis=1, keepdims=True))
        dmat = jnp.exp(log_d - m_t)
        w_int = jnp.exp(inter - m_t)
        qh = qk[:, h * ML_DK:(h + 1) * ML_DK]
        kh = qk[:, ML_HEADS * ML_DK + h * ML_DK:ML_HEADS * ML_DK + (h + 1) * ML_DK] * (ML_DK ** -0.5)
        qb = qh.astype(BF16)
        v = pm_ref[:, _MV + h * ML_DV:_MV + (h + 1) * ML_DV]
        s = _dot_nt(qb, kh.astype(BF16)) * dmat
        c_prev = c_ref[h]
        n_prev = n_ref[h:h + 1, :]
        num = _dot(s.astype(BF16), v) + w_int * _dot(qb, c_prev.astype(BF16))
        den = jnp.sum(s, axis=1, keepdims=True) + w_int * jnp.sum(qh * n_prev, axis=1, keepdims=True)
        den = jnp.maximum(jnp.abs(den), jnp.exp(-m_t))
        hcell = num * (1.0 / den)
        o = pm_ref[:, _MO + h * ML_DV:_MO + (h + 1) * ML_DV].astype(F32)
        y = _sigmoid(o) * hcell
        z = pm_ref[:, _MZ + h * ML_DV:_MZ + (h + 1) * ML_DV].astype(F32)
        out = _head_norm(y, mn_ref[:, h * ML_DV:(h + 1) * ML_DV]) * _silu(z)
        mix_ref[:, RET_HEADS * RET_DV + h * ML_DV:RET_HEADS * RET_DV + (h + 1) * ML_DV] = out.astype(mix_ref.dtype)
        kw = kh * wcol[:, h:h + 1]
        kv = _dot(kw.T.astype(BF16), v)
        ksum = jnp.sum(kw, axis=0, keepdims=True)
        sp_h = sp[h:h + 1, :]
        sc_h = scn[h:h + 1, :]
        c_ref[h] = sp_h * c_prev + sc_h * kv
        n_ref[h:h + 1, :] = sp_h * n_prev + sc_h * ksum


def _ab_mixer(pm, pg, conv_w, conv_b, gate_b, ret_norm, ml_norm, bsz, seq):
    n = pm.shape[0]
    nc = seq // CHUNK
    nqk = 2 * ML_HEADS * ML_DK
    row = lambda b, c: (b * nc + c, 0)
    const = lambda b, c: (0, 0)
    nmix = RET_HEADS * RET_DV + ML_HEADS * ML_DV
    return pl.pallas_call(
        _ab_kernel,
        grid=(bsz, nc),
        in_specs=[pl.BlockSpec((CHUNK, AB_MAIN), row), pl.BlockSpec((CHUNK, LANES), row),
                  pl.BlockSpec((ML_CONV, nqk), const), pl.BlockSpec((1, nqk), const),
                  pl.BlockSpec((1, LANES), const),
                  pl.BlockSpec((1, RET_HEADS * RET_DV), const), pl.BlockSpec((1, ML_HEADS * ML_DV), const)],
        out_specs=pl.BlockSpec((CHUNK, nmix), row),
        out_shape=jax.ShapeDtypeStruct((n, nmix), BF16),
        scratch_shapes=[pltpu.VMEM((CHUNK + 8, nqk), F32),
                        pltpu.VMEM((RET_HEADS, RET_DK, RET_DV), F32),
                        pltpu.VMEM((ML_HEADS, ML_DK, ML_DV), F32),
                        pltpu.VMEM((8, ML_DK), F32),
                        pltpu.VMEM((8, LANES), F32)],
        compiler_params=_cparams(("parallel", "arbitrary")),
        name="ab_mixer",
    )(pm, pg, conv_w, conv_b, gate_b, ret_norm, ml_norm)


_QNZ0 = 0
_KVC0 = NSA_HEADS
_KVS0 = _KVC0 + NSA_GROUPS
_KVW0 = _KVS0 + NSA_GROUPS
_SU0 = (_KVW0 + NSA_GROUPS) * LANES
_SZ0 = _SU0 + S5_WIDTH
CD_MAIN = _SZ0 + S5_WIDTH
SLOTW = NSA_GROUPS * LANES
VT_ROWS = NSA_DH + 16
LOG2E = math.log2(math.e)


def _compress_kernel(x_ref, pk_ref, pv_ref, w1k_ref, w1v_ref, w2k_ref, w2vt_ref, ekc_ref,
                     ko_ref, vot_ref, xs_ref):
    seq = x_ref.shape[0]
    nb = seq // CMP_STRIDE
    for g in range(NSA_GROUPS):
        xs_ref[...] = x_ref[:, g * LANES:(g + 1) * LANES].astype(F32)
        acc = [jnp.zeros((nb, CMP_HIDDEN), F32) for _ in range(4)]
        for r in range(CMP_STRIDE):
            xr = xs_ref[pl.ds(r, nb, stride=CMP_STRIDE), :]
            for kind, (p_ref, w1_ref) in enumerate(((pk_ref, w1k_ref), (pv_ref, w1v_ref))):
                acc[2 * kind] = acc[2 * kind] + _dot((xr + p_ref[r:r + 1, :]).astype(BF16), w1_ref[r])
                acc[2 * kind + 1] = acc[2 * kind + 1] + _dot(
                    (xr + p_ref[CMP_STRIDE + r:CMP_STRIDE + r + 1, :]).astype(BF16), w1_ref[CMP_STRIDE + r])
        hk = _gelu(acc[0] + pltpu.roll(acc[1], nb - 1, axis=0)).astype(BF16)
        hv = _gelu(acc[2] + pltpu.roll(acc[3], nb - 1, axis=0)).astype(BF16)
        ko_ref[g] = (_dot(hk, w2k_ref[...]) + ekc_ref[...]).astype(ko_ref.dtype)
        vot_ref[g] = _dot_nt(w2vt_ref[...], hv).astype(vot_ref.dtype)


def _compress(pm3, pos_k, pos_v, w1k, w1v, w2k, w2vt, ekc):
    bsz, seq, _ = pm3.shape
    nb = seq // CMP_STRIDE
    c3 = lambda b: (0, 0, 0)
    c2 = lambda b: (0, 0)
    return pl.pallas_call(
        _compress_kernel,
        grid=(bsz,),
        in_specs=[pl.BlockSpec((None, seq, SLOTW), lambda b: (b, 0, _KVC0 * LANES // SLOTW)),
                  pl.BlockSpec((CMP_LEN, LANES), c2), pl.BlockSpec((CMP_LEN, LANES), c2),
                  pl.BlockSpec((CMP_LEN, LANES, CMP_HIDDEN), c3), pl.BlockSpec((CMP_LEN, LANES, CMP_HIDDEN), c3),
                  pl.BlockSpec((CMP_HIDDEN, LANES), c2), pl.BlockSpec((LANES, CMP_HIDDEN), c2),
                  pl.BlockSpec((nb, LANES), c2)],
        out_specs=[pl.BlockSpec((None, NSA_GROUPS, nb, LANES), lambda b: (b, 0, 0, 0)),
                   pl.BlockSpec((None, NSA_GROUPS, LANES, nb), lambda b: (b, 0, 0, 0))],
        out_shape=[jax.ShapeDtypeStruct((bsz, NSA_GROUPS, nb, LANES), BF16),
                   jax.ShapeDtypeStruct((bsz, NSA_GROUPS, LANES, nb), BF16)],
        scratch_shapes=[pltpu.VMEM((seq, LANES), F32)],
        compiler_params=_cparams(("parallel",)),
        name="nsa_compress",
    )(pm3, pos_k, pos_v, w1k, w1v, w2k, w2vt, ekc)


def _nsa_kernel(q_ref, kc_ref, vct_ref, kvs_ref, kvw_ref, pg_ref,
                ek_ref, al_ref, ovt_ref, y_ref,
                ksa_ref, vsat_ref, kwa_ref, vwat_ref, qa_scr, part_scr, gzs_scr, sel_smem):
    seq = kvs_ref.shape[0]
    g_idx = pl.program_id(1)
    qi = pl.program_id(2)
    wtiles = WINDOW // LANES

    @pl.when(qi == 0)
    def _():
        ek = ek_ref[...]
        lane_k = lax.broadcasted_iota(jnp.int32, (seq, LANES), 1)
        zero = jnp.zeros((seq, LANES), BF16)
        ksa_ref[0:seq, :] = jnp.where(lane_k < NSA_DH, kvs_ref[...], zero) + ek
        lane_d = lax.broadcasted_iota(jnp.int32, (LANES, LANES), 1)
        ksa_ref[seq:seq + LANES, :] = jnp.where((lane_d >= SEL_LANE0) & (lane_d < SEL_LANE0 + seq // SEL_BLOCK),
                                                1.0, 0.0).astype(BF16)
        vsat_ref[seq // LANES] = jnp.zeros((VT_ROWS, LANES), BF16)
        ek_w = jnp.where(lane_k >= ALI_LANE0, ek, zero)
        kwa_ref[0:WINDOW, :] = jnp.zeros((WINDOW, LANES), BF16)
        kwa_ref[WINDOW:WINDOW + seq, :] = jnp.where(lane_k < NSA_DH, kvw_ref[...], zero) + ek_w
        ones_rows = jnp.where(lax.broadcasted_iota(jnp.int32, (VT_ROWS - NSA_DH, LANES), 0) == 0, 1.0, 0.0)
        for i in range(wtiles):
            vwat_ref[i] = jnp.zeros((VT_ROWS, LANES), BF16)
        for i in range(seq // LANES):
            rws = slice(i * LANES, (i + 1) * LANES)
            for src_ref, dst_ref, j in ((kvs_ref, vsat_ref, i), (kvw_ref, vwat_ref, wtiles + i)):
                v_t = src_ref[rws, :].astype(F32).T[NSA_DH:2 * NSA_DH, :]
                dst_ref[j] = jnp.concatenate([v_t, ones_rows], axis=0).astype(BF16)

    nq = seq // TQ
    refs = (q_ref, kc_ref, vct_ref, pg_ref, al_ref, ovt_ref, y_ref, ksa_ref, vsat_ref, kwa_ref, vwat_ref,
            qa_scr, part_scr, gzs_scr, sel_smem)
    for step in range(nq + 1):
        pl.when(qi == step)(functools.partial(_nsa_step, step, nq, seq // SEL_BLOCK, g_idx, refs))


SEL_DYN = 3
SEL_CNT = 7


def _sel_compactable(qi):
    return (qi + 1) * TQ // LANES > 1 + SEL_DYN + TQ // LANES


def _nsa_step(step, nq, nsel, g_idx, refs):
    def run(compact):
        stages = []
        if step < nq:
            stages.append(_nsa_front(nsel, step, g_idx, refs))
        if step > 0:
            stages.append(_nsa_back(step - 1, compact, refs))
        while stages:
            for st in list(stages):
                if next(st, "done") == "done":
                    stages.remove(st)

    if step > 0 and _sel_compactable(step - 1):
        fits = refs[-1][(step - 1) & 1, SEL_CNT] <= SEL_DYN
        pl.when(fits)(functools.partial(run, True))
        pl.when(jnp.logical_not(fits))(functools.partial(run, False))
    else:
        run(False)


def _tile_iotas():
    rows = NSA_HPG * TQ
    jrow = lax.broadcasted_iota(jnp.int32, (LANES, rows), 0)
    tq = lax.broadcasted_iota(jnp.int32, (LANES, rows), 1) & (TQ - 1)
    return jrow, tq, jrow - tq


def _nsa_front(nsel, qi, g_idx, refs):
    (q_ref, kc_ref, vct_ref, pg_ref, al_ref, ovt_ref, _, _, _, kwa_ref, vwat_ref,
     qa_scr, part_scr, gzs_scr, sel_smem) = refs
    hp = NSA_HPG
    span = WINDOW + TQ
    t0 = qi * TQ
    tw = t0
    early = t0 < WINDOW
    par = qi & 1
    jrow, tq, kmt = _tile_iotas()

    qnz = [q_ref[:, hh * LANES:(hh + 1) * LANES].astype(F32) for hh in range(hp)]
    lane_q = lax.broadcasted_iota(jnp.int32, (TQ, LANES), 1)
    qal = jnp.concatenate([jnp.where(lane_q < NSA_DH, qnz[hh], al_ref[pl.ds(g_idx * hp + hh, 1), :])
                           for hh in range(hp)], axis=0)
    qalb = qal.astype(BF16)
    s_c = _dot_nt(kc_ref[...], qalb)
    s_w = _dot_nt(kwa_ref[pl.ds(tw, span), :], qalb)
    yield

    s = jnp.where(tq - jrow * CMP_STRIDE >= (CMP_LEN - 1) - t0, s_c, NEG)
    e = jnp.exp2(s - jnp.max(s, axis=0, keepdims=True))
    inv = 1.0 / jnp.sum(e, axis=0, keepdims=True)
    if early:
        inv = jnp.where(t0 + tq[0:1, :] >= CMP_LEN - 1, inv, 0.0)
    p = e * inv
    oc_t = _dot(vct_ref[...], p.astype(BF16))
    psum_t = p[:, 0:TQ]
    for hh in range(1, hp):
        psum_t = psum_t + p[:, hh * TQ:(hh + 1) * TQ]
    p_hi = psum_t.astype(BF16)
    p_lo = (psum_t - p_hi.astype(F32)).astype(BF16)
    imp_t = _dot(ovt_ref[...], p_hi) + _dot(ovt_ref[...], p_lo)
    yield

    blocks = [s_w[i * LANES:(i + 1) * LANES, :] for i in range(span // LANES)]
    for i in range(span // LANES):
        if i * LANES < TQ:
            blocks[i] = jnp.where(kmt > -i * LANES, blocks[i], NEG)
        if (i + 1) * LANES > WINDOW:
            blocks[i] = jnp.where(kmt <= WINDOW - i * LANES, blocks[i], NEG)
        if early:
            blocks[i] = jnp.where(jrow >= WINDOW - t0 - i * LANES, blocks[i], NEG)
    s = jnp.concatenate(blocks, axis=0)
    p = jnp.exp2(s - jnp.max(s, axis=0, keepdims=True)).astype(BF16)
    vw_t = jnp.concatenate([vwat_ref[qi * (TQ // LANES) + i] for i in range(span // LANES)], axis=1)
    accw_t = _dot(vw_t, p)
    yield

    score = imp_t[SEL_LANE0:SEL_LANE0 + nsel, :]
    nrow = lax.broadcasted_iota(jnp.int32, (nsel, TQ), 0)
    own = (t0 + lax.broadcasted_iota(jnp.int32, (nsel, TQ), 1)) >> int(math.log2(SEL_BLOCK))
    valid = nrow <= own
    forced = (nrow == 0) | (nrow == own)
    score = jnp.where(forced, BIG, jnp.where(valid, score, -1.0))
    chosen = jnp.zeros((nsel, TQ), F32)
    nrow_f = nrow.astype(F32)
    for _ in range(SEL_TOPN):
        mx = jnp.max(score, axis=0, keepdims=True)
        first = jnp.min(jnp.where(score == mx, nrow_f, 4.0 * LANES), axis=0, keepdims=True)
        hit = nrow_f == first
        chosen = jnp.where(hit, 1.0, chosen)
        score = jnp.where(hit, -3.0, score)
    mask_t = jnp.where(chosen > 0.0, jnp.where(valid, 0.0, NEG), NEG)
    if _sel_compactable(qi):
        used_blk = jnp.max(jnp.where(mask_t == 0.0, 1.0, 0.0), axis=1, keepdims=True)
        blk_per_tile = LANES // SEL_BLOCK
        for slot in range(SEL_DYN):
            sel_smem[par, slot] = jnp.int32(nsel // blk_per_tile)
        cnt = jnp.int32(0)
        for i in range(1, t0 // LANES):
            used = jnp.max(used_blk[i * blk_per_tile:(i + 1) * blk_per_tile, :]) > 0.0

            @pl.when(used & (cnt < SEL_DYN))
            def _(i=i, cnt=cnt):
                sel_smem[par, cnt] = jnp.int32(i)

            cnt = cnt + used.astype(jnp.int32)
        sel_smem[par, SEL_CNT] = cnt
    maskcols = jnp.concatenate([jnp.zeros((SEL_LANE0, TQ), F32), mask_t,
                                jnp.zeros((LANES - SEL_LANE0 - nsel, TQ), F32)], axis=0).T

    qa_scr[par] = (qal + jnp.concatenate([maskcols] * hp, axis=0)).astype(BF16)

    gt_t = _sigmoid(pg_ref[...]).T
    for hh in range(hp):
        g_c, g_s, g_w = [gt_t[j * hp + hh:j * hp + hh + 1, :] for j in range(3)]
        hcols = slice(hh * TQ, (hh + 1) * TQ)
        aw_t = accw_t[:, hcols]
        gz = _silu(qnz[hh].T[NSA_DH:2 * NSA_DH, :])
        o = g_c * oc_t[0:NSA_DH, hcols] + (g_w * (1.0 / aw_t[NSA_DH:NSA_DH + 1, :])) * aw_t[0:NSA_DH, :]
        part_scr[par, hh * NSA_DH:(hh + 1) * NSA_DH, :] = o * gz
        gzs_scr[par, hh * NSA_DH:(hh + 1) * NSA_DH, :] = g_s * gz


def _nsa_back(qi, compact, refs):
    _, _, _, _, _, _, y_ref, ksa_ref, vsat_ref, _, _, qa_scr, part_scr, gzs_scr, sel_smem = refs
    hp = NSA_HPG
    t0 = qi * TQ
    par = qi & 1
    _, _, kmt = _tile_iotas()
    qa = qa_scr[par]
    ntiles = (t0 + TQ) // LANES
    if compact:
        tiles = [0] + [sel_smem[par, j] for j in range(SEL_DYN)] + list(range(t0 // LANES, ntiles))
    else:
        tiles = list(range(ntiles))
    per = KCH // LANES
    chunks = [tiles[i:i + per] for i in range(0, len(tiles), per)]
    nk = len(chunks)

    def k_tile(i):
        if isinstance(i, int):
            return ksa_ref[i * LANES:(i + 1) * LANES, :]
        return ksa_ref[pl.ds(pl.multiple_of(i * LANES, LANES), LANES), :]

    def scores(c):
        s = _dot_nt(jnp.concatenate([k_tile(i) for i in chunks[c]], axis=0), qa)
        if any(isinstance(i, int) and (i + 1) * LANES > t0 for i in chunks[c]):
            s = jnp.concatenate(
                [jnp.where(kmt <= t0 - i * LANES, s[n * LANES:(n + 1) * LANES, :], NEG)
                 if isinstance(i, int) and (i + 1) * LANES > t0 else s[n * LANES:(n + 1) * LANES, :]
                 for n, i in enumerate(chunks[c])], axis=0)
        return s

    def weighted(c, s):
        mc = jnp.max(s, axis=0, keepdims=True)
        v_t = jnp.concatenate([vsat_ref[i] for i in chunks[c]], axis=1)
        return mc, _dot(v_t, jnp.exp2(s - mc).astype(BF16))

    parts = []
    s_next = scores(0)
    for c in range(nk):
        s_cur = s_next
        if c + 1 < nk:
            s_next = scores(c + 1)
        yield
        parts.append(weighted(c, s_cur))
    yield
    m = parts[0][0]
    for mc, _ in parts[1:]:
        m = jnp.maximum(m, mc)
    accs_t = None
    for mc, d in parts:
        d = d if nk == 1 else jnp.exp2(mc - m) * d
        accs_t = d if accs_t is None else accs_t + d

    outs = []
    for hh in range(hp):
        as_t = accs_t[:, hh * TQ:(hh + 1) * TQ]
        rws = slice(hh * NSA_DH, (hh + 1) * NSA_DH)
        outs.append(part_scr[par, rws, :] + (gzs_scr[par, rws, :] * (1.0 / as_t[NSA_DH:NSA_DH + 1, :])) * as_t[0:NSA_DH, :])
    for pr in range(hp // 2):
        packed = jnp.concatenate([outs[2 * pr], outs[2 * pr + 1]], axis=0).T
        y_ref[:, pr * LANES:(pr + 1) * LANES] = packed.astype(y_ref.dtype)


def _nsa(pm3, pg3, kcc, vct, ek, al, ovt):
    bsz, seq, _ = pm3.shape
    nq = seq // TQ
    gw = NSA_HPG * NSA_DH
    kv = lambda tile0: pl.BlockSpec((None, seq, LANES), lambda b, g, q: (b, 0, tile0 + g))
    c2 = lambda b, g, q: (0, 0)
    front = lambda b, g, q: (b, jnp.minimum(q, nq - 1), g)
    back = lambda b, g, q: (b, jnp.maximum(q - 1, 0), g)
    return pl.pallas_call(
        _nsa_kernel,
        grid=(bsz, NSA_GROUPS, nq + 1),
        in_specs=[pl.BlockSpec((None, TQ, NSA_HPG * LANES), front),
                  pl.BlockSpec((None, None, LANES, LANES), lambda b, g, q: (b, g, 0, 0)),
                  pl.BlockSpec((None, None, LANES, LANES), lambda b, g, q: (b, g, 0, 0)),
                  kv(_KVS0), kv(_KVW0),
                  pl.BlockSpec((None, TQ, LANES), front),
                  pl.BlockSpec((seq, LANES), c2),
                  pl.BlockSpec((NSA_HEADS, LANES), c2),
                  pl.BlockSpec((LANES, LANES), c2)],
        out_specs=pl.BlockSpec((None, TQ, gw), back),
        out_shape=jax.ShapeDtypeStruct((bsz, seq, NSA_HEADS * NSA_DH), BF16),
        scratch_shapes=[pltpu.VMEM((seq + LANES, LANES), BF16),
                        pltpu.VMEM((seq // LANES + 1, VT_ROWS, LANES), BF16),
                        pltpu.VMEM((seq + WINDOW, LANES), BF16),
                        pltpu.VMEM(((seq + WINDOW) // LANES, VT_ROWS, LANES), BF16),
                        pltpu.VMEM((2, NSA_HPG * TQ, LANES), BF16),
                        pltpu.VMEM((2, gw, TQ), F32), pltpu.VMEM((2, gw, TQ), F32),
                        pltpu.SMEM((2, SEL_CNT + 1), jnp.int32)],
        compiler_params=_cparams(("parallel", "arbitrary", "arbitrary")),
        name="nsa_attn",
    )(pm3, kcc, vct, pm3, pm3, pg3, ek, al, ovt)


def _nsa_tables(seq):
    t = np.arange(seq)
    ek = np.zeros((seq, LANES), np.float32)
    ek[t, SEL_LANE0 + t // SEL_BLOCK] = 1.0
    for c in range(3):
        ek[:, ALI_LANE0 + c] = t // SEL_BLOCK
        ek[:, ALI_LANE0 + 3 + c] = t % SEL_BLOCK
    slopes = jnp.exp2(-8.0 * jnp.arange(1, NSA_HEADS + 1, dtype=F32) / NSA_HEADS) * LOG2E
    s_hi, s_mid, s_lo = _split3(slopes)
    parts = [p.astype(F32) for p in (s_hi, s_mid, s_lo)]
    al = jnp.zeros((NSA_HEADS, LANES), F32)
    for c in range(3):
        al = al.at[:, ALI_LANE0 + c].set(parts[c] * SEL_BLOCK)
        al = al.at[:, ALI_LANE0 + 3 + c].set(parts[c])
    nb = seq // CMP_STRIDE
    cstart = np.arange(nb) * CMP_STRIDE
    cend = cstart + CMP_LEN - 1
    ekc = np.zeros((nb, LANES), np.float32)
    for c in range(3):
        ekc[:, ALI_LANE0 + c] = cend // SEL_BLOCK
        ekc[:, ALI_LANE0 + 3 + c] = cend % SEL_BLOCK
    sel = np.arange(seq // SEL_BLOCK)
    ovt = np.zeros((LANES, max(nb, LANES)), np.float32)
    ovl = (cstart[:, None] < (sel[None, :] + 1) * SEL_BLOCK) & (cstart[:, None] + CMP_LEN > sel[None, :] * SEL_BLOCK)
    ovl[nb - 1, :] = False
    ovt[SEL_LANE0:SEL_LANE0 + len(sel), :nb] = ovl.T
    return jnp.asarray(ek, BF16), al, jnp.asarray(ekc), jnp.asarray(ovt, BF16)


def _s5_disc_kernel(are_ref, aim_ref, ldt_ref, bre_ref, bim_ref, abr_ref, abi_ref, bbr_ref, bbi_ref):
    a_re = are_ref[...]
    a_im = aim_ref[...]
    dt = jnp.exp(ldt_ref[...])
    er = jnp.exp(a_re * dt)
    abar_re = er * jnp.cos(a_im * dt)
    abar_im = er * jnp.sin(a_im * dt)
    lam2 = a_re * a_re + a_im * a_im
    cr = ((abar_re - 1.0) * a_re + abar_im * a_im) / lam2
    ci = (abar_im * a_re - (abar_re - 1.0) * a_im) / lam2
    b_re = bre_ref[...]
    b_im = bim_ref[...]
    abr_ref[...] = abar_re
    abi_ref[...] = abar_im
    bbr_ref[...] = cr * b_re - ci * b_im
    bbi_ref[...] = cr * b_im + ci * b_re


def _s5_disc(a_re_rep, a_im_rep, ldt_rep, b_re_t, b_im_t):
    shp = jax.ShapeDtypeStruct(a_re_rep.shape, F32)
    return pl.pallas_call(_s5_disc_kernel, out_shape=[shp] * 4, name="s5_disc")(
        a_re_rep, a_im_rep, ldt_rep, b_re_t, b_im_t)


S5_SLABG = 4
S5_TT = 64
S5_PITCH = S5_TT + 4
S5_DIAG = 256
S5_NSTATE_PER_DIAG = S5_DIAG // S5_GROUP_CH * S5_STATE


def _s5_kernel(u_ref, bre_ref, bim_ref, ar_ref, ai_ref, cre_ref, cim_ref, d_ref, wg_ref, o_ref,
               xr_ref, xi_ref, sr_ref, si_ref):
    nb, tt, wd = u_ref.shape
    pitch = sr_ref.shape[1] // nb
    gcols = S5_SLABG * LANES
    ngroups = S5_NSTATE // gcols

    @pl.when(pl.program_id(0) == 0)
    def _():
        xr_ref[...] = jnp.zeros(xr_ref.shape, F32)
        xi_ref[...] = jnp.zeros(xi_ref.shape, F32)

    u = u_ref[...].reshape(nb * tt, wd)

    chan = lambda kg: slice(kg * gcols // S5_NSTATE_PER_DIAG * S5_DIAG,
                            (kg * gcols // S5_NSTATE_PER_DIAG + 1) * S5_DIAG)
    for kg in range(ngroups):
        cols = slice(kg * gcols, (kg + 1) * gcols)
        br = _dot(u[:, chan(kg)], bre_ref[chan(kg), cols])
        bi = _dot(u[:, chan(kg)], bim_ref[chan(kg), cols])
        for j in range(S5_SLABG):
            for b in range(nb):
                sr_ref[kg * S5_SLABG + j, b * pitch:b * pitch + tt, :] = br[b * tt:(b + 1) * tt, j * LANES:(j + 1) * LANES]
                si_ref[kg * S5_SLABG + j, b * pitch:b * pitch + tt, :] = bi[b * tt:(b + 1) * tt, j * LANES:(j + 1) * LANES]

    for kg in range(ngroups):
        cols = slice(kg * gcols, (kg + 1) * gcols)
        slabs = range(kg * S5_SLABG, (kg + 1) * S5_SLABG)
        ar = ar_ref[:, cols]
        ai = ai_ref[:, cols]

        def body(t, carry, slabs=slabs, ar=ar, ai=ai):
            xr, xi = carry
            rows = pl.ds(t, nb, stride=pitch)
            nxr = ar * xr - ai * xi + jnp.concatenate([sr_ref[k, rows, :] for k in slabs], axis=1)
            nxi = ar * xi + ai * xr + jnp.concatenate([si_ref[k, rows, :] for k in slabs], axis=1)
            for j, k in enumerate(slabs):
                sr_ref[k, rows, :] = nxr[:, j * LANES:(j + 1) * LANES]
                si_ref[k, rows, :] = nxi[:, j * LANES:(j + 1) * LANES]
            return nxr, nxi

        xr, xi = lax.fori_loop(0, tt, body, (xr_ref[:, cols], xi_ref[:, cols]), unroll=4)
        xr_ref[:, cols] = xr
        xi_ref[:, cols] = xi

    y_blocks = [jnp.zeros((nb * tt, S5_DIAG), F32) for _ in range(wd // S5_DIAG)]
    for kg in range(ngroups):
        cols = slice(kg * gcols, (kg + 1) * gcols)
        gather = lambda ref: jnp.concatenate(
            [jnp.concatenate([ref[kg * S5_SLABG + j, b * pitch:b * pitch + tt, :] for j in range(S5_SLABG)], axis=1)
             for b in range(nb)], axis=0).astype(BF16)
        blk = chan(kg).start // S5_DIAG
        y_blocks[blk] = (y_blocks[blk] + _dot(gather(sr_ref), cre_ref[cols, chan(kg)])
                         - _dot(gather(si_ref), cim_ref[cols, chan(kg)]))
    y = jnp.concatenate(y_blocks, axis=1)
    y = _gelu(y + d_ref[...] * u.astype(F32))
    z = _dot(y.astype(BF16), wg_ref[...])
    o = z[:, :S5_WIDTH] * _sigmoid(z[:, S5_WIDTH:])
    o_ref[...] = o.reshape(nb, tt, wd).astype(o_ref.dtype)


def _s5(pm3, bbd_re, bbd_im, ar8, ai8, cbd_re, cbd_im, d_row, w_glu):
    bsz, seq, _ = pm3.shape
    tt = S5_TT
    c2 = lambda i: (0, 0)
    nslab = S5_NSTATE // LANES
    return pl.pallas_call(
        _s5_kernel,
        grid=(seq // tt,),
        in_specs=[pl.BlockSpec((bsz, tt, S5_WIDTH), lambda i: (0, i, _SU0 // S5_WIDTH)),
                  pl.BlockSpec((S5_WIDTH, S5_NSTATE), c2), pl.BlockSpec((S5_WIDTH, S5_NSTATE), c2),
                  pl.BlockSpec((8, S5_NSTATE), c2), pl.BlockSpec((8, S5_NSTATE), c2),
                  pl.BlockSpec((S5_NSTATE, S5_WIDTH), c2), pl.BlockSpec((S5_NSTATE, S5_WIDTH), c2),
                  pl.BlockSpec((1, S5_WIDTH), c2), pl.BlockSpec((S5_WIDTH, 2 * S5_WIDTH), c2)],
        out_specs=pl.BlockSpec((bsz, tt, S5_WIDTH), lambda i: (0, i, 0)),
        out_shape=jax.ShapeDtypeStruct((bsz, seq, S5_WIDTH), BF16),
        scratch_shapes=[pltpu.VMEM((8, S5_NSTATE), F32), pltpu.VMEM((8, S5_NSTATE), F32),
                        pltpu.VMEM((nslab, bsz * S5_PITCH, LANES), F32),
                        pltpu.VMEM((nslab, bsz * S5_PITCH, LANES), F32)],
        compiler_params=_cparams(("arbitrary",)),
        name="s5_scan",
    )(pm3, bbd_re, bbd_im, ar8, ai8, cbd_re, cbd_im, d_row, w_glu)


def _pad_cols(w, n):
    return jnp.pad(w, ((0, 0), (0, n - w.shape[1])))


def _cd_proj_weights(w_in):
    d = w_in.shape[0]
    nw, nkv = NSA_HEADS * NSA_DH, NSA_GROUPS * NSA_DH
    offs = np.cumsum((0, nw, nkv, nkv, nkv, nkv, nkv, nkv, 3 * NSA_HEADS, nw, S5_WIDTH, S5_WIDTH))
    seg = lambda i: w_in[:, offs[i]:offs[i + 1]]
    pair = lambda a, b: jnp.stack([a.reshape(d, -1, NSA_DH), b.reshape(d, -1, NSA_DH)], axis=2).reshape(d, -1)
    w_main = jnp.concatenate(
        [pair(seg(0) * (NSA_DH ** -0.5 * LOG2E), seg(8)), pair(seg(1), seg(2)), pair(seg(3), seg(4)),
         pair(seg(5), seg(6)), seg(9), seg(10)], axis=1).astype(BF16)
    w_gate = seg(7).reshape(d, NSA_GROUPS, NSA_HPG, 3).transpose(0, 1, 3, 2).reshape(d, NSA_GROUPS, 3 * NSA_HPG)
    w_gate = jnp.pad(w_gate, ((0, 0), (0, 0), (0, LANES - 3 * NSA_HPG))).reshape(d, NSA_GROUPS * LANES).astype(BF16)
    return w_gate, w_main


def _layer0(h2, p4, layer, bsz, seq, norm_pre, norm_post, pe_proj, pe_gate,
            w_in, ret_norm, conv_w, conv_b, gate_b, ml_norm, w_out):
    d = h2.shape[1]
    g0 = 5120
    w_main = jnp.concatenate([w_in[:, :g0], w_in[:, g0 + 2 * ML_HEADS:]], axis=1).astype(BF16)
    w_gate = _pad_cols(w_in[:, g0:g0 + 2 * ML_HEADS], LANES).astype(BF16)
    pm, pg = _norm_proj(h2, norm_pre.reshape(1, d), w_gate, w_main)
    mix = _ab_mixer(pm, pg, conv_w, conv_b.reshape(1, -1), _pad_cols(gate_b.reshape(1, -1), LANES),
                    ret_norm.reshape(1, -1), ml_norm.reshape(1, -1), bsz, seq)
    ka = RET_HEADS * RET_DV
    nt = seq // 512
    return _out_pe(mix, lambda b, s: (b * nt + s, 0), ka, mix, lambda b, s: (b * nt + s, 1), ML_HEADS * ML_DV,
                   None, None, h2, p4, layer, w_out[:ka].astype(BF16), w_out[ka:].astype(BF16),
                   norm_post.reshape(1, d), pe_gate.astype(BF16), pe_proj.astype(BF16), bsz, seq)


def _layer1(h2, p4, layer, bsz, seq, norm_pre, norm_post, pe_proj, pe_gate,
            w_in, pos_k, pos_v, w1_k, w2_k, w1_v, w2_v,
            a_re, a_im, log_dt, b_re, b_im, c_re, c_im, d_skip, w_glu, w_out):
    d = h2.shape[1]
    nw = NSA_HEADS * NSA_DH
    pm, pg = _norm_proj(h2, norm_pre.reshape(1, d), *_cd_proj_weights(w_in))

    nb = seq // CMP_STRIDE
    pm3 = pm.reshape(bsz, seq, CD_MAIN)
    padl = lambda a, lo=True: jnp.pad(a, ((0, 0),) * (a.ndim - 1)
                                      + ((0, LANES - a.shape[-1]) if lo else (LANES - a.shape[-1], 0),))
    w1 = lambda w, lo: jnp.pad(w.reshape(CMP_LEN, NSA_DH, CMP_HIDDEN),
                               ((0, 0), (0, LANES - NSA_DH) if lo else (LANES - NSA_DH, 0), (0, 0))).astype(BF16)
    ek, al, ekc, ovt = _nsa_tables(seq)
    kcc, vct = _compress(pm3, padl(pos_k), padl(pos_v, False), w1(w1_k, True), w1(w1_v, False),
                         padl(w2_k).astype(BF16), padl(w2_v).T.astype(BF16), ekc)
    if nb < LANES:
        kcc = jnp.pad(kcc, ((0, 0), (0, 0), (0, LANES - nb), (0, 0)))
        vct = jnp.pad(vct, ((0, 0), (0, 0), (0, 0), (0, LANES - nb)))
    y_nsa = _nsa(pm3, pg.reshape(bsz, seq, NSA_GROUPS * LANES), kcc, vct, ek, al, ovt)

    rep = lambda a: jnp.repeat(a, S5_GROUP_CH, axis=0)
    ldt = jnp.broadcast_to(rep(log_dt[:, None]), (S5_WIDTH, S5_STATE))
    bt = lambda b: b.transpose(0, 2, 1).reshape(S5_WIDTH, S5_STATE)
    abr, abi, bbr, bbi = _s5_disc(rep(a_re), rep(a_im), ldt, bt(b_re), bt(b_im))
    eye = jnp.eye(S5_GROUPS, dtype=F32)
    bd_in = lambda bb: (bb.reshape(S5_GROUPS, S5_GROUP_CH, 1, S5_STATE)
                        * eye[:, None, :, None]).reshape(S5_WIDTH, S5_NSTATE).astype(BF16)
    bd_out = lambda c: (c.transpose(0, 2, 1).reshape(S5_GROUPS, S5_STATE, 1, S5_GROUP_CH)
                        * eye[:, None, :, None]).reshape(S5_NSTATE, S5_WIDTH).astype(BF16)
    row8 = lambda a: jnp.broadcast_to(a[::S5_GROUP_CH].reshape(1, S5_NSTATE), (8, S5_NSTATE))
    y_s5 = _s5(pm3, bd_in(bbr), bd_in(bbi), row8(abr), row8(abi),
               bd_out(c_re), bd_out(c_im), d_skip.reshape(1, -1), w_glu.astype(BF16))

    nt = seq // 512
    szb = _SZ0 // S5_WIDTH
    return _out_pe(y_nsa.reshape(bsz * seq, nw), lambda b, s: (b * nt + s, 0), nw,
                   y_s5.reshape(bsz * seq, S5_WIDTH), lambda b, s: (b * nt + s, 0), S5_WIDTH,
                   pm, lambda b, s: (b * nt + s, szb),
                   h2, p4, layer, w_out[:nw].astype(BF16), w_out[nw:].astype(BF16),
                   norm_post.reshape(1, d), pe_gate.astype(BF16), pe_proj.astype(BF16), bsz, seq)


def kernel(x, p, norm_pre, norm_post, pe_proj, pe_gate, ab_w_in, ret_norm, ml_conv_w, ml_conv_b, ml_gate_b,
           ml_norm, ab_w_out, cd_w_in, cmp_pos_k, cmp_pos_v, cmp_w1_k, cmp_w2_k, cmp_w1_v, cmp_w2_v,
           s5_a_re, s5_a_im, s5_log_dt, s5_b_re, s5_b_im, s5_c_re, s5_c_im, s5_d, s5_w_glu, cd_w_out):
    bsz, seq, d = x.shape
    assert bsz == 8, "the S5 scan maps the batch onto the 8 sublanes of a vreg"
    assert seq % max(KCH, 1024) == 0
    depth = p.shape[0]
    h2 = x.reshape(bsz * seq, d)
    for i in range(depth):
        j = i // 2
        if i % 2 == 0:
            h2 = _layer0(h2, p, i, bsz, seq, norm_pre[i], norm_post[i], pe_proj[i], pe_gate[i],
                         ab_w_in[j], ret_norm[j], ml_conv_w[j], ml_conv_b[j], ml_gate_b[j], ml_norm[j], ab_w_out[j])
        else:
            h2 = _layer1(h2, p, i, bsz, seq, norm_pre[i], norm_post[i], pe_proj[i], pe_gate[i],
                         cd_w_in[j], cmp_pos_k[j], cmp_pos_v[j], cmp_w1_k[j], cmp_w2_k[j], cmp_w1_v[j], cmp_w2_v[j],
                         s5_a_re[j], s5_a_im[j], s5_log_dt[j], s5_b_re[j], s5_b_im[j], s5_c_re[j], s5_c_im[j],
                         s5_d[j], s5_w_glu[j], cd_w_out[j])
    return h2.reshape(bsz, seq, d)
```

```python
import functools
import math

import numpy as np
import jax
import jax.numpy as jnp
from jax import lax
from jax.experimental import pallas as pl
from jax.experimental.pallas import tpu as pltpu

F32 = jnp.float32
BF16 = jnp.bfloat16
EPS = 1e-6
NEG = -1e30
BIG = 1e30

LANES = 128
CHUNK = 128
RET_HEADS, RET_DK, RET_DV = 4, 128, 256
ML_HEADS, ML_DK, ML_DV = 4, 128, 256
ML_CONV = 4
NSA_HEADS, NSA_GROUPS, NSA_DH = 16, 4, 64
NSA_HPG = NSA_HEADS // NSA_GROUPS
CMP_STRIDE, CMP_LEN, CMP_HIDDEN = 16, 32, 128
SEL_BLOCK, SEL_TOPN, WINDOW = 64, 4, 512
S5_GROUPS, S5_GROUP_CH, S5_STATE = 32, 16, 64
S5_WIDTH = S5_GROUPS * S5_GROUP_CH
S5_NSTATE = S5_GROUPS * S5_STATE

TQ = 256
KCH = 512
SEL_LANE0 = 64
ALI_LANE0 = 96
VMEM_LIMIT = 56 * 1024 * 1024


def _cparams(sem):
    return pltpu.CompilerParams(dimension_semantics=sem, vmem_limit_bytes=VMEM_LIMIT)


def _dot(a, b):
    return jnp.dot(a, b, preferred_element_type=F32)


def _dot_nt(a, b):
    return lax.dot_general(a, b, (((1,), (1,)), ((), ())), preferred_element_type=F32)


def _split3(x):
    hi = x.astype(BF16)
    r1 = x - hi.astype(F32)
    mid = r1.astype(BF16)
    lo = (r1 - mid.astype(F32)).astype(BF16)
    return hi, mid, lo


def _dot_f32_lhs(x, m):
    hi, mid, lo = _split3(x)
    return _dot(hi, m) + _dot(mid, m) + _dot(lo, m)


def _sigmoid(x):
    return 1.0 / (1.0 + jnp.exp(-x))


def _silu(x):
    return x * _sigmoid(x)


def _gelu(x):
    return 0.5 * x * (1.0 + jnp.tanh(math.sqrt(2.0 / math.pi) * (x + 0.044715 * (x * x * x))))


def _rms(x, g):
    return x * lax.rsqrt(jnp.mean(x * x, -1, keepdims=True) + EPS) * g


PROJ_TN = 1024


def _norm_proj_kernel(h_ref, g_ref, wg_ref, wm_ref, om_ref, og_ref):
    ub = _rms(h_ref[...], g_ref[...]).astype(BF16)
    nm = om_ref.shape[1]
    for c0 in range(0, nm, PROJ_TN):
        c1 = min(c0 + PROJ_TN, nm)
        om_ref[:, c0:c1] = _dot(ub, wm_ref[:, c0:c1]).astype(om_ref.dtype)
    og_ref[...] = _dot(ub, wg_ref[...])


def _norm_proj(h2, g, w_gate, w_main, tm=512):
    n, d = h2.shape
    nm = w_main.shape[1]
    ng = w_gate.shape[1]
    once = pl.Buffered(1)
    return pl.pallas_call(
        _norm_proj_kernel,
        grid=(n // tm,),
        in_specs=[pl.BlockSpec((tm, d), lambda i: (i, 0)),
                  pl.BlockSpec((1, d), lambda i: (0, 0)),
                  pl.BlockSpec((d, ng), lambda i: (0, 0), pipeline_mode=once),
                  pl.BlockSpec((d, nm), lambda i: (0, 0), pipeline_mode=once)],
        out_specs=[pl.BlockSpec((tm, nm), lambda i: (i, 0)),
                   pl.BlockSpec((tm, ng), lambda i: (i, 0))],
        out_shape=[jax.ShapeDtypeStruct((n, nm), BF16), jax.ShapeDtypeStruct((n, ng), F32)],
        compiler_params=_cparams(("parallel",)),
        name="norm_proj",
    )(h2, g, w_gate, w_main)


def _out_pe_kernel(gated, *refs):
    if gated:
        ma_ref, mb_ref, sz_ref, h_ref, p_ref, wa_ref, wb_ref, gp_ref, wg_ref, wp_ref, o_ref = refs
        sz = sz_ref[...].astype(F32)
        mb = (mb_ref[...].astype(F32) * _silu(sz)).astype(BF16)
    else:
        ma_ref, mb_ref, h_ref, p_ref, wa_ref, wb_ref, gp_ref, wg_ref, wp_ref, o_ref = refs
        mb = mb_ref[...]
    y = _dot(ma_ref[...], wa_ref[...]) + _dot(mb, wb_ref[...])
    h1 = h_ref[...] + _rms(y, gp_ref[...])
    gate = _sigmoid(_dot(h1.astype(BF16), wg_ref[...]))
    pp = _dot(p_ref[...].astype(BF16), wp_ref[...])
    o_ref[...] = h1 + gate * pp


def _out_pe(mix_a, a_map, ka, mix_b, b_map, kb, sz, sz_map, h2, p4, layer, w_a, w_b, g_post, w_gate, w_pe,
            bsz, seq, tm=512):
    n, d = h2.shape
    pe = p4.shape[-1]
    nt = seq // tm
    row = lambda b, s: (b * nt + s, 0)
    const = lambda b, s: (0, 0)
    gated = sz is not None
    in_specs = [pl.BlockSpec((tm, ka), a_map), pl.BlockSpec((tm, kb), b_map)]
    args = [mix_a, mix_b]
    if gated:
        in_specs.append(pl.BlockSpec((tm, kb), sz_map))
        args.append(sz)
    in_specs += [pl.BlockSpec((tm, d), row), pl.BlockSpec((None, None, tm, pe), lambda b, s: (layer, b, s, 0)),
                 pl.BlockSpec((ka, d), const), pl.BlockSpec((kb, d), const), pl.BlockSpec((1, d), const),
                 pl.BlockSpec((d, d), const), pl.BlockSpec((pe, d), const)]
    args += [h2, p4, w_a, w_b, g_post, w_gate, w_pe]
    return pl.pallas_call(
        functools.partial(_out_pe_kernel, gated),
        grid=(bsz, nt),
        in_specs=in_specs,
        out_specs=pl.BlockSpec((tm, d), row),
        out_shape=jax.ShapeDtypeStruct((n, d), F32),
        compiler_params=_cparams(("parallel", "arbitrary")),
        name="out_pe_gated" if gated else "out_pe",
    )(*args)


_RQ, _RK, _RV, _RG = 0, 512, 1024, 2048
_MQ, _MK, _MV, _MO, _MZ = 3072, 3584, 4096, 5120, 6144
AB_MAIN = 7168


def _head_norm(y, g):
    yc = y - jnp.mean(y, -1, keepdims=True)
    return yc * lax.rsqrt(jnp.mean(yc * yc, -1, keepdims=True) + EPS) * g


def _ab_kernel(pm_ref, pg_ref, cw_ref, cb_ref, gb_ref, rn_ref, mn_ref, mix_ref,
               ext_ref, r_ref, c_ref, n_ref, m_ref):
    L = CHUNK
    c_idx = pl.program_id(1)

    @pl.when(c_idx == 0)
    def _():
        ext_ref[0:8, :] = jnp.zeros((8, ext_ref.shape[1]), F32)
        r_ref[...] = jnp.zeros(r_ref.shape, F32)
        c_ref[...] = jnp.zeros(c_ref.shape, F32)
        n_ref[...] = jnp.zeros(n_ref.shape, F32)
        m_ref[...] = jnp.full(m_ref.shape, NEG, F32)

    row = lax.broadcasted_iota(jnp.int32, (L, L), 0)
    col = lax.broadcasted_iota(jnp.int32, (L, L), 1)
    causal = row >= col
    diff = (row - col).astype(F32)
    tcol = lax.broadcasted_iota(jnp.int32, (L, 1), 0).astype(F32)

    for h in range(RET_HEADS):
        lg = math.log1p(-(2.0 ** (-5.0 - h)))
        scale = RET_DK ** -0.5
        decay = jnp.where(causal, jnp.exp(jnp.maximum(diff, 0.0) * lg), 0.0) * scale
        q = pm_ref[:, _RQ + h * RET_DK:_RQ + (h + 1) * RET_DK]
        k = pm_ref[:, _RK + h * RET_DK:_RK + (h + 1) * RET_DK]
        v = pm_ref[:, _RV + h * RET_DV:_RV + (h + 1) * RET_DV]
        sc = _dot_nt(q, k) * decay
        xi = jnp.exp((tcol + 1.0) * lg) * scale
        qx = (q.astype(F32) * xi).astype(BF16)
        r_prev = r_ref[h]
        y = _dot(sc.astype(BF16), v) + _dot(qx, r_prev.astype(BF16))
        zeta = jnp.exp((L - 1.0 - tcol) * lg)
        kz_t = (k.astype(F32) * zeta).T.astype(BF16)
        r_ref[h] = r_prev * math.exp(L * lg) + _dot(kz_t, v)
        g = pm_ref[:, _RG + h * RET_DV:_RG + (h + 1) * RET_DV].astype(F32)
        out = _head_norm(y, rn_ref[:, h * RET_DV:(h + 1) * RET_DV]) * _silu(g)
        mix_ref[:, h * RET_DV:(h + 1) * RET_DV] = out.astype(mix_ref.dtype)

    nqk = 2 * ML_HEADS * ML_DK
    x = pm_ref[:, _MQ:_MQ + nqk].astype(F32)
    ext_ref[8:8 + L, :] = x
    acc = cb_ref[...] + cw_ref[ML_CONV - 1:ML_CONV, :] * x
    for j in range(ML_CONV - 1):
        acc = acc + cw_ref[j:j + 1, :] * ext_ref[pl.ds(8 - (ML_CONV - 1) + j, L), :]
    ext_ref[0:8, :] = x[L - 8:L, :]
    qk = _silu(acc)

    gt = (pg_ref[...] + gb_ref[...]).T
    g8 = gt[0:8, :]
    logf8 = jnp.minimum(g8, 0.0) - jnp.log(1.0 + jnp.exp(-jnp.abs(g8)))
    triu = jnp.where(row <= col, 1.0, 0.0).astype(BF16)
    bcum8 = _dot_f32_lhs(logf8, triu)
    b_t = pltpu.roll(bcum8, 4, axis=0)
    b_last = b_t[:, L - 1:L]
    a_t = b_last - b_t + g8
    mu = jnp.max(a_t, axis=1, keepdims=True)
    w_t = jnp.exp(a_t - mu)
    r_t = g8 - b_t
    zpad = jnp.zeros((L - 8, L), F32)
    bcol = jnp.concatenate([b_t, zpad], axis=0).T
    wcol = jnp.concatenate([w_t, zpad], axis=0).T
    m_prev = m_ref[...][:, 0:1]
    m_new = jnp.maximum(b_last + m_prev, mu)
    sp = jnp.exp(b_last + m_prev - m_new)
    scn = jnp.exp(mu - m_new)
    m_ref[...] = jnp.broadcast_to(m_new, m_ref.shape)

    for h in range(ML_HEADS):
        bc = bcol[:, h:h + 1]
        log_d = jnp.where(causal, bc + r_t[h:h + 1, :], NEG)
        inter = bc + m_prev[h:h + 1, :]
        m_t = jnp.maximum(inter, jnp.max(log_d, axis=1, keepdims=True))
        dmat = jnp.exp(log_d - m_t)
        w_int = jnp.exp(inter - m_t)
        qh = qk[:, h * ML_DK:(h + 1) * ML_DK]
        kh = qk[:, ML_HEADS * ML_DK + h * ML_DK:ML_HEADS * ML_DK + (h + 1) * ML_DK] * (ML_DK ** -0.5)
        qb = qh.astype(BF16)
        v = pm_ref[:, _MV + h * ML_DV:_MV + (h + 1) * ML_DV]
        s = _dot_nt(qb, kh.astype(BF16)) * dmat
        c_prev = c_ref[h]
        n_prev = n_ref[h:h + 1, :]
        num = _dot(s.astype(BF16), v) + w_int * _dot(qb, c_prev.astype(BF16))
        den = jnp.sum(s, axis=1, keepdims=True) + w_int * jnp.sum(qh * n_prev, axis=1, keepdims=True)
        den = jnp.maximum(jnp.abs(den), jnp.exp(-m_t))
        hcell = num * (1.0 / den)
        o = pm_ref[:, _MO + h * ML_DV:_MO + (h + 1) * ML_DV].astype(F32)
        y = _sigmoid(o) * hcell
        z = pm_ref[:, _MZ + h * ML_DV:_MZ + (h + 1) * ML_DV].astype(F32)
        out = _head_norm(y, mn_ref[:, h * ML_DV:(h + 1) * ML_DV]) * _silu(z)
        mix_ref[:, RET_HEADS * RET_DV + h * ML_DV:RET_HEADS * RET_DV + (h + 1) * ML_DV] = out.astype(mix_ref.dtype)
        kw = kh * wcol[:, h:h + 1]
        kv = _dot(kw.T.astype(BF16), v)
        ksum = jnp.sum(kw, axis=0, keepdims=True)
        sp_h = sp[h:h + 1, :]
        sc_h = scn[h:h + 1, :]
        c_ref[h] = sp_h * c_prev + sc_h * kv
        n_ref[h:h + 1, :] = sp_h * n_prev + sc_h * ksum


def _ab_mixer(pm, pg, conv_w, conv_b, gate_b, ret_norm, ml_norm, bsz, seq):
    n = pm.shape[0]
    nc = seq // CHUNK
    nqk = 2 * ML_HEADS * ML_DK
    row = lambda b, c: (b * nc + c, 0)
    const = lambda b, c: (0, 0)
    nmix = RET_HEADS * RET_DV + ML_HEADS * ML_DV
    return pl.pallas_call(
        _ab_kernel,
        grid=(bsz, nc),
        in_specs=[pl.BlockSpec((CHUNK, AB_MAIN), row), pl.BlockSpec((CHUNK, LANES), row),
                  pl.BlockSpec((ML_CONV, nqk), const), pl.BlockSpec((1, nqk), const),
                  pl.BlockSpec((1, LANES), const),
                  pl.BlockSpec((1, RET_HEADS * RET_DV), const), pl.BlockSpec((1, ML_HEADS * ML_DV), const)],
        out_specs=pl.BlockSpec((CHUNK, nmix), row),
        out_shape=jax.ShapeDtypeStruct((n, nmix), BF16),
        scratch_shapes=[pltpu.VMEM((CHUNK + 8, nqk), F32),
                        pltpu.VMEM((RET_HEADS, RET_DK, RET_DV), F32),
                        pltpu.VMEM((ML_HEADS, ML_DK, ML_DV), F32),
                        pltpu.VMEM((8, ML_DK), F32),
                        pltpu.VMEM((8, LANES), F32)],
        compiler_params=_cparams(("parallel", "arbitrary")),
        name="ab_mixer",
    )(pm, pg, conv_w, conv_b, gate_b, ret_norm, ml_norm)


_QNZ0 = 0
_KVC0 = NSA_HEADS
_KVS0 = _KVC0 + NSA_GROUPS
_KVW0 = _KVS0 + NSA_GROUPS
_SU0 = (_KVW0 + NSA_GROUPS) * LANES
_SZ0 = _SU0 + S5_WIDTH
CD_MAIN = _SZ0 + S5_WIDTH
SLOTW = NSA_GROUPS * LANES
VT_ROWS = NSA_DH + 16
LOG2E = math.log2(math.e)


def _compress_kernel(x_ref, pk_ref, pv_ref, w1k_ref, w1v_ref, w2k_ref, w2vt_ref, ekc_ref,
                     ko_ref, vot_ref, xs_ref):
    seq = x_ref.shape[0]
    nb = seq // CMP_STRIDE
    for g in range(NSA_GROUPS):
        xs_ref[...] = x_ref[:, g * LANES:(g + 1) * LANES].astype(F32)
        acc = [jnp.zeros((nb, CMP_HIDDEN), F32) for _ in range(4)]
        for r in range(CMP_STRIDE):
            xr = xs_ref[pl.ds(r, nb, stride=CMP_STRIDE), :]
            for kind, (p_ref, w1_ref) in enumerate(((pk_ref, w1k_ref), (pv_ref, w1v_ref))):
                acc[2 * kind] = acc[2 * kind] + _dot((xr + p_ref[r:r + 1, :]).astype(BF16), w1_ref[r])
                acc[2 * kind + 1] = acc[2 * kind + 1] + _dot(
                    (xr + p_ref[CMP_STRIDE + r:CMP_STRIDE + r + 1, :]).astype(BF16), w1_ref[CMP_STRIDE + r])
        hk = _gelu(acc[0] + pltpu.roll(acc[1], nb - 1, axis=0)).astype(BF16)
        hv = _gelu(acc[2] + pltpu.roll(acc[3], nb - 1, axis=0)).astype(BF16)
        ko_ref[g] = (_dot(hk, w2k_ref[...]) + ekc_ref[...]).astype(ko_ref.dtype)
        vot_ref[g] = _dot_nt(w2vt_ref[...], hv).astype(vot_ref.dtype)


def _compress(pm3, pos_k, pos_v, w1k, w1v, w2k, w2vt, ekc):
    bsz, seq, _ = pm3.shape
    nb = seq // CMP_STRIDE
    c3 = lambda b: (0, 0, 0)
    c2 = lambda b: (0, 0)
    return pl.pallas_call(
        _compress_kernel,
        grid=(bsz,),
        in_specs=[pl.BlockSpec((None, seq, SLOTW), lambda b: (b, 0, _KVC0 * LANES // SLOTW)),
                  pl.BlockSpec((CMP_LEN, LANES), c2), pl.BlockSpec((CMP_LEN, LANES), c2),
                  pl.BlockSpec((CMP_LEN, LANES, CMP_HIDDEN), c3), pl.BlockSpec((CMP_LEN, LANES, CMP_HIDDEN), c3),
                  pl.BlockSpec((CMP_HIDDEN, LANES), c2), pl.BlockSpec((LANES, CMP_HIDDEN), c2),
                  pl.BlockSpec((nb, LANES), c2)],
        out_specs=[pl.BlockSpec((None, NSA_GROUPS, nb, LANES), lambda b: (b, 0, 0, 0)),
                   pl.BlockSpec((None, NSA_GROUPS, LANES, nb), lambda b: (b, 0, 0, 0))],
        out_shape=[jax.ShapeDtypeStruct((bsz, NSA_GROUPS, nb, LANES), BF16),
                   jax.ShapeDtypeStruct((bsz, NSA_GROUPS, LANES, nb), BF16)],
        scratch_shapes=[pltpu.VMEM((seq, LANES), F32)],
        compiler_params=_cparams(("parallel",)),
        name="nsa_compress",
    )(pm3, pos_k, pos_v, w1k, w1v, w2k, w2vt, ekc)


def _nsa_kernel(q_ref, kc_ref, vct_ref, kvs_ref, kvw_ref, pg_ref,
                ek_ref, al_ref, ovt_ref, y_ref,
                ksa_ref, vsat_ref, kwa_ref, vwat_ref, qa_scr, part_scr, gzs_scr, sel_smem):
    seq = kvs_ref.shape[0]
    g_idx = pl.program_id(1)
    qi = pl.program_id(2)
    wtiles = WINDOW // LANES

    @pl.when(qi == 0)
    def _():
        ek = ek_ref[...]
        lane_k = lax.broadcasted_iota(jnp.int32, (seq, LANES), 1)
        zero = jnp.zeros((seq, LANES), BF16)
        ksa_ref[0:seq, :] = jnp.where(lane_k < NSA_DH, kvs_ref[...], zero) + ek
        lane_d = lax.broadcasted_iota(jnp.int32, (LANES, LANES), 1)
        ksa_ref[seq:seq + LANES, :] = jnp.where((lane_d >= SEL_LANE0) & (lane_d < SEL_LANE0 + seq // SEL_BLOCK),
                                                1.0, 0.0).astype(BF16)
        vsat_ref[seq // LANES] = jnp.zeros((VT_ROWS, LANES), BF16)
        ek_w = jnp.where(lane_k >= ALI_LANE0, ek, zero)
        kwa_ref[0:WINDOW, :] = jnp.zeros((WINDOW, LANES), BF16)
        kwa_ref[WINDOW:WINDOW + seq, :] = jnp.where(lane_k < NSA_DH, kvw_ref[...], zero) + ek_w
        ones_rows = jnp.where(lax.broadcasted_iota(jnp.int32, (VT_ROWS - NSA_DH, LANES), 0) == 0, 1.0, 0.0)
        for i in range(wtiles):
            vwat_ref[i] = jnp.zeros((VT_ROWS, LANES), BF16)
        for i in range(seq // LANES):
            rws = slice(i * LANES, (i + 1) * LANES)
            for src_ref, dst_ref, j in ((kvs_ref, vsat_ref, i), (kvw_ref, vwat_ref, wtiles + i)):
                v_t = src_ref[rws, :].astype(F32).T[NSA_DH:2 * NSA_DH, :]
                dst_ref[j] = jnp.concatenate([v_t, ones_rows], axis=0).astype(BF16)

    nq = seq // TQ
    refs = (q_ref, kc_ref, vct_ref, pg_ref, al_ref, ovt_ref, y_ref, ksa_ref, vsat_ref, kwa_ref, vwat_ref,
            qa_scr, part_scr, gzs_scr, sel_smem)
    for step in range(nq + 1):
        pl.when(qi == step)(functools.partial(_nsa_step, step, nq, seq // SEL_BLOCK, g_idx, refs))


SEL_DYN = 2
SEL_CNT = 7


def _sel_compactable(qi):
    return (qi + 1) * TQ // LANES > 1 + SEL_DYN + TQ // LANES


def _nsa_step(step, nq, nsel, g_idx, refs):
    def run(compact):
        stages = []
        if step < nq:
            stages.append(_nsa_front(nsel, step, g_idx, refs))
        if step > 0:
            stages.append(_nsa_back(step - 1, compact, refs))
        while stages:
            for st in list(stages):
                if next(st, "done") == "done":
                    stages.remove(st)

    if step > 0 and _sel_compactable(step - 1):
        fits = refs[-1][(step - 1) & 1, SEL_CNT] <= SEL_DYN
        pl.when(fits)(functools.partial(run, True))
        pl.when(jnp.logical_not(fits))(functools.partial(run, False))
    else:
        run(False)


def _tile_iotas():
    rows = NSA_HPG * TQ
    jrow = lax.broadcasted_iota(jnp.int32, (LANES, rows), 0)
    tq = lax.broadcasted_iota(jnp.int32, (LANES, rows), 1) & (TQ - 1)
    return jrow, tq, jrow - tq


def _nsa_front(nsel, qi, g_idx, refs):
    (q_ref, kc_ref, vct_ref, pg_ref, al_ref, ovt_ref, _, _, _, kwa_ref, vwat_ref,
     qa_scr, part_scr, gzs_scr, sel_smem) = refs
    hp = NSA_HPG
    span = WINDOW + TQ
    t0 = qi * TQ
    tw = t0
    early = t0 < WINDOW
    par = qi & 1
    jrow, tq, kmt = _tile_iotas()

    qnz = [q_ref[:, hh * LANES:(hh + 1) * LANES].astype(F32) for hh in range(hp)]
    lane_q = lax.broadcasted_iota(jnp.int32, (TQ, LANES), 1)
    qal = jnp.concatenate([jnp.where(lane_q < NSA_DH, qnz[hh], al_ref[pl.ds(g_idx * hp + hh, 1), :])
                           for hh in range(hp)], axis=0)
    qalb = qal.astype(BF16)
    s_c = _dot_nt(kc_ref[...], qalb)
    s_w = _dot_nt(kwa_ref[pl.ds(tw, span), :], qalb)
    yield

    s = jnp.where(tq - jrow * CMP_STRIDE >= (CMP_LEN - 1) - t0, s_c, NEG)
    e = jnp.exp2(s - jnp.max(s, axis=0, keepdims=True))
    inv = 1.0 / jnp.sum(e, axis=0, keepdims=True)
    if early:
        inv = jnp.where(t0 + tq[0:1, :] >= CMP_LEN - 1, inv, 0.0)
    p = e * inv
    oc_t = _dot(vct_ref[...], p.astype(BF16))
    psum_t = p[:, 0:TQ]
    for hh in range(1, hp):
        psum_t = psum_t + p[:, hh * TQ:(hh + 1) * TQ]
    p_hi = psum_t.astype(BF16)
    p_lo = (psum_t - p_hi.astype(F32)).astype(BF16)
    imp_t = _dot(ovt_ref[...], p_hi) + _dot(ovt_ref[...], p_lo)
    yield

    blocks = [s_w[i * LANES:(i + 1) * LANES, :] for i in range(span // LANES)]
    for i in range(span // LANES):
        if i * LANES < TQ:
            blocks[i] = jnp.where(kmt > -i * LANES, blocks[i], NEG)
        if (i + 1) * LANES > WINDOW:
            blocks[i] = jnp.where(kmt <= WINDOW - i * LANES, blocks[i], NEG)
        if early:
            blocks[i] = jnp.where(jrow >= WINDOW - t0 - i * LANES, blocks[i], NEG)
    s = jnp.concatenate(blocks, axis=0)
    p = jnp.exp2(s - jnp.max(s, axis=0, keepdims=True)).astype(BF16)
    vw_t = jnp.concatenate([vwat_ref[qi * (TQ // LANES) + i] for i in range(span // LANES)], axis=1)
    accw_t = _dot(vw_t, p)
    yield

    score = imp_t[SEL_LANE0:SEL_LANE0 + nsel, :]
    nrow = lax.broadcasted_iota(jnp.int32, (nsel, TQ), 0)
    own = (t0 + lax.broadcasted_iota(jnp.int32, (nsel, TQ), 1)) >> int(math.log2(SEL_BLOCK))
    valid = nrow <= own
    forced = (nrow == 0) | (nrow == own)
    score = jnp.where(forced, BIG, jnp.where(valid, score, -1.0))
    chosen = jnp.zeros((nsel, TQ), F32)
    nrow_f = nrow.astype(F32)
    for _ in range(SEL_TOPN):
        mx = jnp.max(score, axis=0, keepdims=True)
        first = jnp.min(jnp.where(score == mx, nrow_f, 4.0 * LANES), axis=0, keepdims=True)
        hit = nrow_f == first
        chosen = jnp.where(hit, 1.0, chosen)
        score = jnp.where(hit, -3.0, score)
    mask_t = jnp.where(chosen > 0.0, jnp.where(valid, 0.0, NEG), NEG)
    if _sel_compactable(qi):
        used_blk = jnp.max(jnp.where(mask_t == 0.0, 1.0, 0.0), axis=1, keepdims=True)
        blk_per_tile = LANES // SEL_BLOCK
        for slot in range(SEL_DYN):
            sel_smem[par, slot] = jnp.int32(nsel // blk_per_tile)
        cnt = jnp.int32(0)
        for i in range(1, t0 // LANES):
            used = jnp.max(used_blk[i * blk_per_tile:(i + 1) * blk_per_tile, :]) > 0.0

            @pl.when(used & (cnt < SEL_DYN))
            def _(i=i, cnt=cnt):
                sel_smem[par, cnt] = jnp.int32(i)

            cnt = cnt + used.astype(jnp.int32)
        sel_smem[par, SEL_CNT] = cnt
    maskcols = jnp.concatenate([jnp.zeros((SEL_LANE0, TQ), F32), mask_t,
                                jnp.zeros((LANES - SEL_LANE0 - nsel, TQ), F32)], axis=0).T

    qa_scr[par] = (qal + jnp.concatenate([maskcols] * hp, axis=0)).astype(BF16)

    gt_t = _sigmoid(pg_ref[...]).T
    for hh in range(hp):
        g_c, g_s, g_w = [gt_t[j * hp + hh:j * hp + hh + 1, :] for j in range(3)]
        hcols = slice(hh * TQ, (hh + 1) * TQ)
        aw_t = accw_t[:, hcols]
        gz = _silu(qnz[hh].T[NSA_DH:2 * NSA_DH, :])
        o = g_c * oc_t[0:NSA_DH, hcols] + (g_w * (1.0 / aw_t[NSA_DH:NSA_DH + 1, :])) * aw_t[0:NSA_DH, :]
        part_scr[par, hh * NSA_DH:(hh + 1) * NSA_DH, :] = o * gz
        gzs_scr[par, hh * NSA_DH:(hh + 1) * NSA_DH, :] = g_s * gz


def _nsa_back(qi, compact, refs):
    _, _, _, _, _, _, y_ref, ksa_ref, vsat_ref, _, _, qa_scr, part_scr, gzs_scr, sel_smem = refs
    hp = NSA_HPG
    t0 = qi * TQ
    par = qi & 1
    _, _, kmt = _tile_iotas()
    qa = qa_scr[par]
    ntiles = (t0 + TQ) // LANES
    if compact:
        tiles = [0] + [sel_smem[par, j] for j in range(SEL_DYN)] + list(range(t0 // LANES, ntiles))
    else:
        tiles = list(range(ntiles))
    per = KCH // LANES
    chunks = [tiles[i:i + per] for i in range(0, len(tiles), per)]
    nk = len(chunks)

    def k_tile(i):
        if isinstance(i, int):
            return ksa_ref[i * LANES:(i + 1) * LANES, :]
        return ksa_ref[pl.ds(pl.multiple_of(i * LANES, LANES), LANES), :]

    def scores(c):
        s = _dot_nt(jnp.concatenate([k_tile(i) for i in chunks[c]], axis=0), qa)
        if any(isinstance(i, int) and (i + 1) * LANES > t0 for i in chunks[c]):
            s = jnp.concatenate(
                [jnp.where(kmt <= t0 - i * LANES, s[n * LANES:(n + 1) * LANES, :], NEG)
                 if isinstance(i, int) and (i + 1) * LANES > t0 else s[n * LANES:(n + 1) * LANES, :]
                 for n, i in enumerate(chunks[c])], axis=0)
        return s

    def weighted(c, s):
        mc = jnp.max(s, axis=0, keepdims=True)
        v_t = jnp.concatenate([vsat_ref[i] for i in chunks[c]], axis=1)
        return mc, _dot(v_t, jnp.exp2(s - mc).astype(BF16))

    parts = []
    s_next = scores(0)
    for c in range(nk):
        s_cur = s_next
        if c + 1 < nk:
            s_next = scores(c + 1)
        yield
        parts.append(weighted(c, s_cur))
    yield
    m = parts[0][0]
    for mc, _ in parts[1:]:
        m = jnp.maximum(m, mc)
    accs_t = None
    for mc, d in parts:
        d = d if nk == 1 else jnp.exp2(mc - m) * d
        accs_t = d if accs_t is None else accs_t + d

    outs = []
    for hh in range(hp):
        as_t = accs_t[:, hh * TQ:(hh + 1) * TQ]
        rws = slice(hh * NSA_DH, (hh + 1) * NSA_DH)
        outs.append(part_scr[par, rws, :] + (gzs_scr[par, rws, :] * (1.0 / as_t[NSA_DH:NSA_DH + 1, :])) * as_t[0:NSA_DH, :])
    for pr in range(hp // 2):
        packed = jnp.concatenate([outs[2 * pr], outs[2 * pr + 1]], axis=0).T
        y_ref[:, pr * LANES:(pr + 1) * LANES] = packed.astype(y_ref.dtype)


def _nsa(pm3, pg3, kcc, vct, ek, al, ovt):
    bsz, seq, _ = pm3.shape
    nq = seq // TQ
    gw = NSA_HPG * NSA_DH
    kv = lambda tile0: pl.BlockSpec((None, seq, LANES), lambda b, g, q: (b, 0, tile0 + g))
    c2 = lambda b, g, q: (0, 0)
    front = lambda b, g, q: (b, jnp.minimum(q, nq - 1), g)
    back = lambda b, g, q: (b, jnp.maximum(q - 1, 0), g)
    return pl.pallas_call(
        _nsa_kernel,
        grid=(bsz, NSA_GROUPS, nq + 1),
        in_specs=[pl.BlockSpec((None, TQ, NSA_HPG * LANES), front),
                  pl.BlockSpec((None, None, LANES, LANES), lambda b, g, q: (b, g, 0, 0)),
                  pl.BlockSpec((None, None, LANES, LANES), lambda b, g, q: (b, g, 0, 0)),
                  kv(_KVS0), kv(_KVW0),
                  pl.BlockSpec((None, TQ, LANES), front),
                  pl.BlockSpec((seq, LANES), c2),
                  pl.BlockSpec((NSA_HEADS, LANES), c2),
                  pl.BlockSpec((LANES, LANES), c2)],
        out_specs=pl.BlockSpec((None, TQ, gw), back),
        out_shape=jax.ShapeDtypeStruct((bsz, seq, NSA_HEADS * NSA_DH), BF16),
        scratch_shapes=[pltpu.VMEM((seq + LANES, LANES), BF16),
                        pltpu.VMEM((seq // LANES + 1, VT_ROWS, LANES), BF16),
                        pltpu.VMEM((seq + WINDOW, LANES), BF16),
                        pltpu.VMEM(((seq + WINDOW) // LANES, VT_ROWS, LANES), BF16),
                        pltpu.VMEM((2, NSA_HPG * TQ, LANES), BF16),
                        pltpu.VMEM((2, gw, TQ), F32), pltpu.VMEM((2, gw, TQ), F32),
                        pltpu.SMEM((2, SEL_CNT + 1), jnp.int32)],
        compiler_params=_cparams(("parallel", "arbitrary", "arbitrary")),
        name="nsa_attn",
    )(pm3, kcc, vct, pm3, pm3, pg3, ek, al, ovt)


def _nsa_tables(seq):
    t = np.arange(seq)
    ek = np.zeros((seq, LANES), np.float32)
    ek[t, SEL_LANE0 + t // SEL_BLOCK] = 1.0
    for c in range(3):
        ek[:, ALI_LANE0 + c] = t // SEL_BLOCK
        ek[:, ALI_LANE0 + 3 + c] = t % SEL_BLOCK
    slopes = jnp.exp2(-8.0 * jnp.arange(1, NSA_HEADS + 1, dtype=F32) / NSA_HEADS) * LOG2E
    s_hi, s_mid, s_lo = _split3(slopes)
    parts = [p.astype(F32) for p in (s_hi, s_mid, s_lo)]
    al = jnp.zeros((NSA_HEADS, LANES), F32)
    for c in range(3):
        al = al.at[:, ALI_LANE0 + c].set(parts[c] * SEL_BLOCK)
        al = al.at[:, ALI_LANE0 + 3 + c].set(parts[c])
    nb = seq // CMP_STRIDE
    cstart = np.arange(nb) * CMP_STRIDE
    cend = cstart + CMP_LEN - 1
    ekc = np.zeros((nb, LANES), np.float32)
    for c in range(3):
        ekc[:, ALI_LANE0 + c] = cend // SEL_BLOCK
        ekc[:, ALI_LANE0 + 3 + c] = cend % SEL_BLOCK
    sel = np.arange(seq // SEL_BLOCK)
    ovt = np.zeros((LANES, max(nb, LANES)), np.float32)
    ovl = (cstart[:, None] < (sel[None, :] + 1) * SEL_BLOCK) & (cstart[:, None] + CMP_LEN > sel[None, :] * SEL_BLOCK)
    ovl[nb - 1, :] = False
    ovt[SEL_LANE0:SEL_LANE0 + len(sel), :nb] = ovl.T
    return jnp.asarray(ek, BF16), al, jnp.asarray(ekc), jnp.asarray(ovt, BF16)


def _s5_disc_kernel(are_ref, aim_ref, ldt_ref, bre_ref, bim_ref, abr_ref, abi_ref, bbr_ref, bbi_ref):
    a_re = are_ref[...]
    a_im = aim_ref[...]
    dt = jnp.exp(ldt_ref[...])
    er = jnp.exp(a_re * dt)
    abar_re = er * jnp.cos(a_im * dt)
    abar_im = er * jnp.sin(a_im * dt)
    lam2 = a_re * a_re + a_im * a_im
    cr = ((abar_re - 1.0) * a_re + abar_im * a_im) / lam2
    ci = (abar_im * a_re - (abar_re - 1.0) * a_im) / lam2
    b_re = bre_ref[...]
    b_im = bim_ref[...]
    abr_ref[...] = abar_re
    abi_ref[...] = abar_im
    bbr_ref[...] = cr * b_re - ci * b_im
    bbi_ref[...] = cr * b_im + ci * b_re


def _s5_disc(a_re_rep, a_im_rep, ldt_rep, b_re_t, b_im_t):
    shp = jax.ShapeDtypeStruct(a_re_rep.shape, F32)
    return pl.pallas_call(_s5_disc_kernel, out_shape=[shp] * 4, name="s5_disc")(
        a_re_rep, a_im_rep, ldt_rep, b_re_t, b_im_t)


S5_SLABG = 4
S5_TT = 64
S5_PITCH = S5_TT + 4
S5_DIAG = 256
S5_NSTATE_PER_DIAG = S5_DIAG // S5_GROUP_CH * S5_STATE


def _s5_kernel(u_ref, bre_ref, bim_ref, ar_ref, ai_ref, cre_ref, cim_ref, d_ref, wg_ref, o_ref,
               xr_ref, xi_ref, sr_ref, si_ref):
    nb, tt, wd = u_ref.shape
    pitch = sr_ref.shape[1] // nb
    gcols = S5_SLABG * LANES
    ngroups = S5_NSTATE // gcols

    @pl.when(pl.program_id(0) == 0)
    def _():
        xr_ref[...] = jnp.zeros(xr_ref.shape, F32)
        xi_ref[...] = jnp.zeros(xi_ref.shape, F32)

    u = u_ref[...].reshape(nb * tt, wd)

    chan = lambda kg: slice(kg * gcols // S5_NSTATE_PER_DIAG * S5_DIAG,
                            (kg * gcols // S5_NSTATE_PER_DIAG + 1) * S5_DIAG)
    for kg in range(ngroups):
        cols = slice(kg * gcols, (kg + 1) * gcols)
        br = _dot(u[:, chan(kg)], bre_ref[chan(kg), cols])
        bi = _dot(u[:, chan(kg)], bim_ref[chan(kg), cols])
        for j in range(S5_SLABG):
            for b in range(nb):
                sr_ref[kg * S5_SLABG + j, b * pitch:b * pitch + tt, :] = br[b * tt:(b + 1) * tt, j * LANES:(j + 1) * LANES]
                si_ref[kg * S5_SLABG + j, b * pitch:b * pitch + tt, :] = bi[b * tt:(b + 1) * tt, j * LANES:(j + 1) * LANES]

    for kg in range(ngroups):
        cols = slice(kg * gcols, (kg + 1) * gcols)
        slabs = range(kg * S5_SLABG, (kg + 1) * S5_SLABG)
        ar = ar_ref[:, cols]
        ai = ai_ref[:, cols]

        def body(t, carry, slabs=slabs, ar=ar, ai=ai):
            xr, xi = carry
            rows = pl.ds(t, nb, stride=pitch)
            nxr = ar * xr - ai * xi + jnp.concatenate([sr_ref[k, rows, :] for k in slabs], axis=1)
            nxi = ar * xi + ai * xr + jnp.concatenate([si_ref[k, rows, :] for k in slabs], axis=1)
            for j, k in enumerate(slabs):
                sr_ref[k, rows, :] = nxr[:, j * LANES:(j + 1) * LANES]
                si_ref[k, rows, :] = nxi[:, j * LANES:(j + 1) * LANES]
            return nxr, nxi

        xr, xi = lax.fori_loop(0, tt, body, (xr_ref[:, cols], xi_ref[:, cols]), unroll=4)
        xr_ref[:, cols] = xr
        xi_ref[:, cols] = xi

    y_blocks = [jnp.zeros((nb * tt, S5_DIAG), F32) for _ in range(wd // S5_DIAG)]
    for kg in range(ngroups):
        cols = slice(kg * gcols, (kg + 1) * gcols)
        gather = lambda ref: jnp.concatenate(
            [jnp.concatenate([ref[kg * S5_SLABG + j, b * pitch:b * pitch + tt, :] for j in range(S5_SLABG)], axis=1)
             for b in range(nb)], axis=0).astype(BF16)
        blk = chan(kg).start // S5_DIAG
        y_blocks[blk] = (y_blocks[blk] + _dot(gather(sr_ref), cre_ref[cols, chan(kg)])
                         - _dot(gather(si_ref), cim_ref[cols, chan(kg)]))
    y = jnp.concatenate(y_blocks, axis=1)
    y = _gelu(y + d_ref[...] * u.astype(F32))
    z = _dot(y.astype(BF16), wg_ref[...])
    o = z[:, :S5_WIDTH] * _sigmoid(z[:, S5_WIDTH:])
    o_ref[...] = o.reshape(nb, tt, wd).astype(o_ref.dtype)


def _s5(pm3, bbd_re, bbd_im, ar8, ai8, cbd_re, cbd_im, d_row, w_glu):
    bsz, seq, _ = pm3.shape
    tt = S5_TT
    c2 = lambda i: (0, 0)
    nslab = S5_NSTATE // LANES
    return pl.pallas_call(
        _s5_kernel,
        grid=(seq // tt,),
        in_specs=[pl.BlockSpec((bsz, tt, S5_WIDTH), lambda i: (0, i, _SU0 // S5_WIDTH)),
                  pl.BlockSpec((S5_WIDTH, S5_NSTATE), c2), pl.BlockSpec((S5_WIDTH, S5_NSTATE), c2),
                  pl.BlockSpec((8, S5_NSTATE), c2), pl.BlockSpec((8, S5_NSTATE), c2),
                  pl.BlockSpec((S5_NSTATE, S5_WIDTH), c2), pl.BlockSpec((S5_NSTATE, S5_WIDTH), c2),
                  pl.BlockSpec((1, S5_WIDTH), c2), pl.BlockSpec((S5_WIDTH, 2 * S5_WIDTH), c2)],
        out_specs=pl.BlockSpec((bsz, tt, S5_WIDTH), lambda i: (0, i, 0)),
        out_shape=jax.ShapeDtypeStruct((bsz, seq, S5_WIDTH), BF16),
        scratch_shapes=[pltpu.VMEM((8, S5_NSTATE), F32), pltpu.VMEM((8, S5_NSTATE), F32),
                        pltpu.VMEM((nslab, bsz * S5_PITCH, LANES), F32),
                        pltpu.VMEM((nslab, bsz * S5_PITCH, LANES), F32)],
        compiler_params=_cparams(("arbitrary",)),
        name="s5_scan",
    )(pm3, bbd_re, bbd_im, ar8, ai8, cbd_re, cbd_im, d_row, w_glu)


def _pad_cols(w, n):
    return jnp.pad(w, ((0, 0), (0, n - w.shape[1])))


def _cd_proj_weights(w_in):
    d = w_in.shape[0]
    nw, nkv = NSA_HEADS * NSA_DH, NSA_GROUPS * NSA_DH
    offs = np.cumsum((0, nw, nkv, nkv, nkv, nkv, nkv, nkv, 3 * NSA_HEADS, nw, S5_WIDTH, S5_WIDTH))
    seg = lambda i: w_in[:, offs[i]:offs[i + 1]]
    pair = lambda a, b: jnp.stack([a.reshape(d, -1, NSA_DH), b.reshape(d, -1, NSA_DH)], axis=2).reshape(d, -1)
    w_main = jnp.concatenate(
        [pair(seg(0) * (NSA_DH ** -0.5 * LOG2E), seg(8)), pair(seg(1), seg(2)), pair(seg(3), seg(4)),
         pair(seg(5), seg(6)), seg(9), seg(10)], axis=1).astype(BF16)
    w_gate = seg(7).reshape(d, NSA_GROUPS, NSA_HPG, 3).transpose(0, 1, 3, 2).reshape(d, NSA_GROUPS, 3 * NSA_HPG)
    w_gate = jnp.pad(w_gate, ((0, 0), (0, 0), (0, LANES - 3 * NSA_HPG))).reshape(d, NSA_GROUPS * LANES).astype(BF16)
    return w_gate, w_main


def _layer0(h2, p4, layer, bsz, seq, norm_pre, norm_post, pe_proj, pe_gate,
            w_in, ret_norm, conv_w, conv_b, gate_b, ml_norm, w_out):
    d = h2.shape[1]
    g0 = 5120
    w_main = jnp.concatenate([w_in[:, :g0], w_in[:, g0 + 2 * ML_HEADS:]], axis=1).astype(BF16)
    w_gate = _pad_cols(w_in[:, g0:g0 + 2 * ML_HEADS], LANES).astype(BF16)
    pm, pg = _norm_proj(h2, norm_pre.reshape(1, d), w_gate, w_main)
    mix = _ab_mixer(pm, pg, conv_w, conv_b.reshape(1, -1), _pad_cols(gate_b.reshape(1, -1), LANES),
                    ret_norm.reshape(1, -1), ml_norm.reshape(1, -1), bsz, seq)
    ka = RET_HEADS * RET_DV
    nt = seq // 512
    return _out_pe(mix, lambda b, s: (b * nt + s, 0), ka, mix, lambda b, s: (b * nt + s, 1), ML_HEADS * ML_DV,
                   None, None, h2, p4, layer, w_out[:ka].astype(BF16), w_out[ka:].astype(BF16),
                   norm_post.reshape(1, d), pe_gate.astype(BF16), pe_proj.astype(BF16), bsz, seq)


def _layer1(h2, p4, layer, bsz, seq, norm_pre, norm_post, pe_proj, pe_gate,
            w_in, pos_k, pos_v, w1_k, w2_k, w1_v, w2_v,
            a_re, a_im, log_dt, b_re, b_im, c_re, c_im, d_skip, w_glu, w_out):
    d = h2.shape[1]
    nw = NSA_HEADS * NSA_DH
    pm, pg = _norm_proj(h2, norm_pre.reshape(1, d), *_cd_proj_weights(w_in))

    nb = seq // CMP_STRIDE
    pm3 = pm.reshape(bsz, seq, CD_MAIN)
    padl = lambda a, lo=True: jnp.pad(a, ((0, 0),) * (a.ndim - 1)
                                      + ((0, LANES - a.shape[-1]) if lo else (LANES - a.shape[-1], 0),))
    w1 = lambda w, lo: jnp.pad(w.reshape(CMP_LEN, NSA_DH, CMP_HIDDEN),
                               ((0, 0), (0, LANES - NSA_DH) if lo else (LANES - NSA_DH, 0), (0, 0))).astype(BF16)
    ek, al, ekc, ovt = _nsa_tables(seq)
    kcc, vct = _compress(pm3, padl(pos_k), padl(pos_v, False), w1(w1_k, True), w1(w1_v, False),
                         padl(w2_k).astype(BF16), padl(w2_v).T.astype(BF16), ekc)
    if nb < LANES:
        kcc = jnp.pad(kcc, ((0, 0), (0, 0), (0, LANES - nb), (0, 0)))
        vct = jnp.pad(vct, ((0, 0), (0, 0), (0, 0), (0, LANES - nb)))
    y_nsa = _nsa(pm3, pg.reshape(bsz, seq, NSA_GROUPS * LANES), kcc, vct, ek, al, ovt)

    rep = lambda a: jnp.repeat(a, S5_GROUP_CH, axis=0)
    ldt = jnp.broadcast_to(rep(log_dt[:, None]), (S5_WIDTH, S5_STATE))
    bt = lambda b: b.transpose(0, 2, 1).reshape(S5_WIDTH, S5_STATE)
    abr, abi, bbr, bbi = _s5_disc(rep(a_re), rep(a_im), ldt, bt(b_re), bt(b_im))
    eye = jnp.eye(S5_GROUPS, dtype=F32)
    bd_in = lambda bb: (bb.reshape(S5_GROUPS, S5_GROUP_CH, 1, S5_STATE)
                        * eye[:, None, :, None]).reshape(S5_WIDTH, S5_NSTATE).astype(BF16)
    bd_out = lambda c: (c.transpose(0, 2, 1).reshape(S5_GROUPS, S5_STATE, 1, S5_GROUP_CH)
                        * eye[:, None, :, None]).reshape(S5_NSTATE, S5_WIDTH).astype(BF16)
    row8 = lambda a: jnp.broadcast_to(a[::S5_GROUP_CH].reshape(1, S5_NSTATE), (8, S5_NSTATE))
    y_s5 = _s5(pm3, bd_in(bbr), bd_in(bbi), row8(abr), row8(abi),
               bd_out(c_re), bd_out(c_im), d_skip.reshape(1, -1), w_glu.astype(BF16))

    nt = seq // 512
    szb = _SZ0 // S5_WIDTH
    return _out_pe(y_nsa.reshape(bsz * seq, nw), lambda b, s: (b * nt + s, 0), nw,
                   y_s5.reshape(bsz * seq, S5_WIDTH), lambda b, s: (b * nt + s, 0), S5_WIDTH,
                   pm, lambda b, s: (b * nt + s, szb),
                   h2, p4, layer, w_out[:nw].astype(BF16), w_out[nw:].astype(BF16),
                   norm_post.reshape(1, d), pe_gate.astype(BF16), pe_proj.astype(BF16), bsz, seq)


def kernel(x, p, norm_pre, norm_post, pe_proj, pe_gate, ab_w_in, ret_norm, ml_conv_w, ml_conv_b, ml_gate_b,
           ml_norm, ab_w_out, cd_w_in, cmp_pos_k, cmp_pos_v, cmp_w1_k, cmp_w2_k, cmp_w1_v, cmp_w2_v,
           s5_a_re, s5_a_im, s5_log_dt, s5_b_re, s5_b_im, s5_c_re, s5_c_im, s5_d, s5_w_glu, cd_w_out):
    bsz, seq, d = x.shape
    assert bsz == 8, "the S5 scan maps the batch onto the 8 sublanes of a vreg"
    assert seq % max(KCH, 1024) == 0
    depth = p.shape[0]
    h2 = x.reshape(bsz * seq, d)
    for i in range(depth):
        j = i // 2
        if i % 2 == 0:
            h2 = _layer0(h2, p, i, bsz, seq, norm_pre[i], norm_post[i], pe_proj[i], pe_gate[i],
                         ab_w_in[j], ret_norm[j], ml_conv_w[j], ml_conv_b[j], ml_gate_b[j], ml_norm[j], ab_w_out[j])
        else:
            h2 = _layer1(h2, p, i, bsz, seq, norm_pre[i], norm_post[i], pe_proj[i], pe_gate[i],
                         cd_w_in[j], cmp_pos_k[j], cmp_pos_v[j], cmp_w1_k[j], cmp_w2_k[j], cmp_w1_v[j], cmp_w2_v[j],
                         s5_a_re[j], s5_a_im[j], s5_log_dt[j], s5_b_re[j], s5_b_im[j], s5_c_re[j], s5_c_im[j],
                         s5_d[j], s5_w_glu[j], cd_w_out[j])
    return h2.reshape(bsz, seq, d)
```

```python
import functools
import math

import numpy as np
import jax
import jax.numpy as jnp
from jax import lax
from jax.experimental import pallas as pl
from jax.experimental.pallas import tpu as pltpu

F32 = jnp.float32
BF16 = jnp.bfloat16
EPS = 1e-6
NEG = -1e30
BIG = 1e30

LANES = 128
CHUNK = 128
RET_HEADS, RET_DK, RET_DV = 4, 128, 256
ML_HEADS, ML_DK, ML_DV = 4, 128, 256
ML_CONV = 4
NSA_HEADS, NSA_GROUPS, NSA_DH = 16, 4, 64
NSA_HPG = NSA_HEADS // NSA_GROUPS
CMP_STRIDE, CMP_LEN, CMP_HIDDEN = 16, 32, 128
SEL_BLOCK, SEL_TOPN, WINDOW = 64, 4, 512
S5_GROUPS, S5_GROUP_CH, S5_STATE = 32, 16, 64
S5_WIDTH = S5_GROUPS * S5_GROUP_CH
S5_NSTATE = S5_GROUPS * S5_STATE

TQ = 256
KCH = 512
SEL_LANE0 = 64
ALI_LANE0 = 96
VMEM_LIMIT = 56 * 1024 * 1024


def _cparams(sem):
    return pltpu.CompilerParams(dimension_semantics=sem, vmem_limit_bytes=VMEM_LIMIT)


def _dot(a, b):
    return jnp.dot(a, b, preferred_element_type=F32)


def _dot_nt(a, b):
    return lax.dot_general(a, b, (((1,), (1,)), ((), ())), preferred_element_type=F32)


def _split3(x):
    hi = x.astype(BF16)
    r1 = x - hi.astype(F32)
    mid = r1.astype(BF16)
    lo = (r1 - mid.astype(F32)).astype(BF16)
    return hi, mid, lo


def _dot_f32_lhs(x, m):
    hi, mid, lo = _split3(x)
    return _dot(hi, m) + _dot(mid, m) + _dot(lo, m)


def _sigmoid(x):
    return 1.0 / (1.0 + jnp.exp(-x))


def _silu(x):
    return x * _sigmoid(x)


def _gelu(x):
    return 0.5 * x * (1.0 + jnp.tanh(math.sqrt(2.0 / math.pi) * (x + 0.044715 * (x * x * x))))


def _rms(x, g):
    return x * lax.rsqrt(jnp.mean(x * x, -1, keepdims=True) + EPS) * g


PROJ_TN = 1024


def _norm_proj_kernel(h_ref, g_ref, wg_ref, wm_ref, om_ref, og_ref):
    ub = _rms(h_ref[...], g_ref[...]).astype(BF16)
    nm = om_ref.shape[1]
    for c0 in range(0, nm, PROJ_TN):
        c1 = min(c0 + PROJ_TN, nm)
        om_ref[:, c0:c1] = _dot(ub, wm_ref[:, c0:c1]).astype(om_ref.dtype)
    og_ref[...] = _dot(ub, wg_ref[...])


def _norm_proj(h2, g, w_gate, w_main, tm=512):
    n, d = h2.shape
    nm = w_main.shape[1]
    ng = w_gate.shape[1]
    once = pl.Buffered(1)
    return pl.pallas_call(
        _norm_proj_kernel,
        grid=(n // tm,),
        in_specs=[pl.BlockSpec((tm, d), lambda i: (i, 0)),
                  pl.BlockSpec((1, d), lambda i: (0, 0)),
                  pl.BlockSpec((d, ng), lambda i: (0, 0), pipeline_mode=once),
                  pl.BlockSpec((d, nm), lambda i: (0, 0), pipeline_mode=once)],
        out_specs=[pl.BlockSpec((tm, nm), lambda i: (i, 0)),
                   pl.BlockSpec((tm, ng), lambda i: (i, 0))],
        out_shape=[jax.ShapeDtypeStruct((n, nm), BF16), jax.ShapeDtypeStruct((n, ng), F32)],
        compiler_params=_cparams(("parallel",)),
        name="norm_proj",
    )(h2, g, w_gate, w_main)


def _out_pe_kernel(gated, *refs):
    if gated:
        ma_ref, mb_ref, sz_ref, h_ref, p_ref, wa_ref, wb_ref, gp_ref, wg_ref, wp_ref, o_ref = refs
        sz = sz_ref[...].astype(F32)
        mb = (mb_ref[...].astype(F32) * _silu(sz)).astype(BF16)
    else:
        ma_ref, mb_ref, h_ref, p_ref, wa_ref, wb_ref, gp_ref, wg_ref, wp_ref, o_ref = refs
        mb = mb_ref[...]
    y = _dot(ma_ref[...], wa_ref[...]) + _dot(mb, wb_ref[...])
    h1 = h_ref[...] + _rms(y, gp_ref[...])
    gate = _sigmoid(_dot(h1.astype(BF16), wg_ref[...]))
    pp = _dot(p_ref[...].astype(BF16), wp_ref[...])
    o_ref[...] = h1 + gate * pp


def _out_pe(mix_a, a_map, ka, mix_b, b_map, kb, sz, sz_map, h2, p4, layer, w_a, w_b, g_post, w_gate, w_pe,
            bsz, seq, tm=512):
    n, d = h2.shape
    pe = p4.shape[-1]
    nt = seq // tm
    row = lambda b, s: (b * nt + s, 0)
    const = lambda b, s: (0, 0)
    gated = sz is not None
    in_specs = [pl.BlockSpec((tm, ka), a_map), pl.BlockSpec((tm, kb), b_map)]
    args = [mix_a, mix_b]
    if gated:
        in_specs.append(pl.BlockSpec((tm, kb), sz_map))
        args.append(sz)
    in_specs += [pl.BlockSpec((tm, d), row), pl.BlockSpec((None, None, tm, pe), lambda b, s: (layer, b, s, 0)),
                 pl.BlockSpec((ka, d), const), pl.BlockSpec((kb, d), const), pl.BlockSpec((1, d), const),
                 pl.BlockSpec((d, d), const), pl.BlockSpec((pe, d), const)]
    args += [h2, p4, w_a, w_b, g_post, w_gate, w_pe]
    return pl.pallas_call(
        functools.partial(_out_pe_kernel, gated),
        grid=(bsz, nt),
        in_specs=in_specs,
        out_specs=pl.BlockSpec((tm, d), row),
        out_shape=jax.ShapeDtypeStruct((n, d), F32),
        compiler_params=_cparams(("parallel", "arbitrary")),
        name="out_pe_gated" if gated else "out_pe",
    )(*args)


_RQ, _RK, _RV, _RG = 0, 512, 1024, 2048
_MQ, _MK, _MV, _MO, _MZ = 3072, 3584, 4096, 5120, 6144
AB_MAIN = 7168


def _head_norm(y, g):
    yc = y - jnp.mean(y, -1, keepdims=True)
    return yc * lax.rsqrt(jnp.mean(yc * yc, -1, keepdims=True) + EPS) * g


def _ab_kernel(pm_ref, pg_ref, cw_ref, cb_ref, gb_ref, rn_ref, mn_ref, mix_ref,
               ext_ref, r_ref, c_ref, n_ref, m_ref):
    L = CHUNK
    c_idx = pl.program_id(1)

    @pl.when(c_idx == 0)
    def _():
        ext_ref[0:8, :] = jnp.zeros((8, ext_ref.shape[1]), F32)
        r_ref[...] = jnp.zeros(r_ref.shape, F32)
        c_ref[...] = jnp.zeros(c_ref.shape, F32)
        n_ref[...] = jnp.zeros(n_ref.shape, F32)
        m_ref[...] = jnp.full(m_ref.shape, NEG, F32)

    row = lax.broadcasted_iota(jnp.int32, (L, L), 0)
    col = lax.broadcasted_iota(jnp.int32, (L, L), 1)
    causal = row >= col
    diff = (row - col).astype(F32)
    tcol = lax.broadcasted_iota(jnp.int32, (L, 1), 0).astype(F32)

    for h in range(RET_HEADS):
        lg = math.log1p(-(2.0 ** (-5.0 - h)))
        scale = RET_DK ** -0.5
        decay = jnp.where(causal, jnp.exp(jnp.maximum(diff, 0.0) * lg), 0.0) * scale
        q = pm_ref[:, _RQ + h * RET_DK:_RQ + (h + 1) * RET_DK]
        k = pm_ref[:, _RK + h * RET_DK:_RK + (h + 1) * RET_DK]
        v = pm_ref[:, _RV + h * RET_DV:_RV + (h + 1) * RET_DV]
        sc = _dot_nt(q, k) * decay
        xi = jnp.exp((tcol + 1.0) * lg) * scale
        qx = (q.astype(F32) * xi).astype(BF16)
        r_prev = r_ref[h]
        y = _dot(sc.astype(BF16), v) + _dot(qx, r_prev.astype(BF16))
        zeta = jnp.exp((L - 1.0 - tcol) * lg)
        kz_t = (k.astype(F32) * zeta).T.astype(BF16)
        r_ref[h] = r_prev * math.exp(L * lg) + _dot(kz_t, v)
        g = pm_ref[:, _RG + h * RET_DV:_RG + (h + 1) * RET_DV].astype(F32)
        out = _head_norm(y, rn_ref[:, h * RET_DV:(h + 1) * RET_DV]) * _silu(g)
        mix_ref[:, h * RET_DV:(h + 1) * RET_DV] = out.astype(mix_ref.dtype)

    nqk = 2 * ML_HEADS * ML_DK
    x = pm_ref[:, _MQ:_MQ + nqk].astype(F32)
    ext_ref[8:8 + L, :] = x
    acc = cb_ref[...] + cw_ref[ML_CONV - 1:ML_CONV, :] * x
    for j in range(ML_CONV - 1):
        acc = acc + cw_ref[j:j + 1, :] * ext_ref[pl.ds(8 - (ML_CONV - 1) + j, L), :]
    ext_ref[0:8, :] = x[L - 8:L, :]
    qk = _silu(acc)

    gt = (pg_ref[...] + gb_ref[...]).T
    g8 = gt[0:8, :]
    logf8 = jnp.minimum(g8, 0.0) - jnp.log(1.0 + jnp.exp(-jnp.abs(g8)))
    triu = jnp.where(row <= col, 1.0, 0.0).astype(BF16)
    bcum8 = _dot_f32_lhs(logf8, triu)
    b_t = pltpu.roll(bcum8, 4, axis=0)
    b_last = b_t[:, L - 1:L]
    a_t = b_last - b_t + g8
    mu = jnp.max(a_t, axis=1, keepdims=True)
    w_t = jnp.exp(a_t - mu)
    r_t = g8 - b_t
    zpad = jnp.zeros((L - 8, L), F32)
    bcol = jnp.concatenate([b_t, zpad], axis=0).T
    wcol = jnp.concatenate([w_t, zpad], axis=0).T
    m_prev = m_ref[...][:, 0:1]
    m_new = jnp.maximum(b_last + m_prev, mu)
    sp = jnp.exp(b_last + m_prev - m_new)
    scn = jnp.exp(mu - m_new)
    m_ref[...] = jnp.broadcast_to(m_new, m_ref.shape)

    for h in range(ML_HEADS):
        bc = bcol[:, h:h + 1]
        log_d = jnp.where(causal, bc + r_t[h:h + 1, :], NEG)
        inter = bc + m_prev[h:h + 1, :]
        m_t = jnp.maximum(inter, jnp.max(log_d, axis=1, keepdims=True))
        dmat = jnp.exp(log_d - m_t)
        w_int = jnp.exp(inter - m_t)
        qh = qk[:, h * ML_DK:(h + 1) * ML_DK]
        kh = qk[:, ML_HEADS * ML_DK + h * ML_DK:ML_HEADS * ML_DK + (h + 1) * ML_DK] * (ML_DK ** -0.5)
        qb = qh.astype(BF16)
        v = pm_ref[:, _MV + h * ML_DV:_MV + (h + 1) * ML_DV]
        s = _dot_nt(qb, kh.astype(BF16)) * dmat
        c_prev = c_ref[h]
        n_prev = n_ref[h:h + 1, :]
        num = _dot(s.astype(BF16), v) + w_int * _dot(qb, c_prev.astype(BF16))
        den = jnp.sum(s, axis=1, keepdims=True) + w_int * jnp.sum(qh * n_prev, axis=1, keepdims=True)
        den = jnp.maximum(jnp.abs(den), jnp.exp(-m_t))
        hcell = num * (1.0 / den)
        o = pm_ref[:, _MO + h * ML_DV:_MO + (h + 1) * ML_DV].astype(F32)
        y = _sigmoid(o) * hcell
        z = pm_ref[:, _MZ + h * ML_DV:_MZ + (h + 1) * ML_DV].astype(F32)
        out = _head_norm(y, mn_ref[:, h * ML_DV:(h + 1) * ML_DV]) * _silu(z)
        mix_ref[:, RET_HEADS * RET_DV + h * ML_DV:RET_HEADS * RET_DV + (h + 1) * ML_DV] = out.astype(mix_ref.dtype)
        kw = kh * wcol[:, h:h + 1]
        kv = _dot(kw.T.astype(BF16), v)
        ksum = jnp.sum(kw, axis=0, keepdims=True)
        sp_h = sp[h:h + 1, :]
        sc_h = scn[h:h + 1, :]
        c_ref[h] = sp_h * c_prev + sc_h * kv
        n_ref[h:h + 1, :] = sp_h * n_prev + sc_h * ksum


def _ab_mixer(pm, pg, conv_w, conv_b, gate_b, ret_norm, ml_norm, bsz, seq):
    n = pm.shape[0]
    nc = seq // CHUNK
    nqk = 2 * ML_HEADS * ML_DK
    row = lambda b, c: (b * nc + c, 0)
    const = lambda b, c: (0, 0)
    nmix = RET_HEADS * RET_DV + ML_HEADS * ML_DV
    return pl.pallas_call(
        _ab_kernel,
        grid=(bsz, nc),
        in_specs=[pl.BlockSpec((CHUNK, AB_MAIN), row), pl.BlockSpec((CHUNK, LANES), row),
                  pl.BlockSpec((ML_CONV, nqk), const), pl.BlockSpec((1, nqk), const),
                  pl.BlockSpec((1, LANES), const),
                  pl.BlockSpec((1, RET_HEADS * RET_DV), const), pl.BlockSpec((1, ML_HEADS * ML_DV), const)],
        out_specs=pl.BlockSpec((CHUNK, nmix), row),
        out_shape=jax.ShapeDtypeStruct((n, nmix), BF16),
        scratch_shapes=[pltpu.VMEM((CHUNK + 8, nqk), F32),
                        pltpu.VMEM((RET_HEADS, RET_DK, RET_DV), F32),
                        pltpu.VMEM((ML_HEADS, ML_DK, ML_DV), F32),
                        pltpu.VMEM((8, ML_DK), F32),
                        pltpu.VMEM((8, LANES), F32)],
        compiler_params=_cparams(("parallel", "arbitrary")),
        name="ab_mixer",
    )(pm, pg, conv_w, conv_b, gate_b, ret_norm, ml_norm)


_QNZ0 = 0
_KVC0 = NSA_HEADS
_KVS0 = _KVC0 + NSA_GROUPS
_KVW0 = _KVS0 + NSA_GROUPS
_SU0 = (_KVW0 + NSA_GROUPS) * LANES
_SZ0 = _SU0 + S5_WIDTH
CD_MAIN = _SZ0 + S5_WIDTH
SLOTW = NSA_GROUPS * LANES
VT_ROWS = NSA_DH + 16
LOG2E = math.log2(math.e)


def _compress_kernel(x_ref, pk_ref, pv_ref, w1k_ref, w1v_ref, w2k_ref, w2vt_ref, ekc_ref,
                     ko_ref, vot_ref, xs_ref):
    seq = x_ref.shape[0]
    nb = seq // CMP_STRIDE
    for g in range(NSA_GROUPS):
        xs_ref[...] = x_ref[:, g * LANES:(g + 1) * LANES].astype(F32)
        acc = [jnp.zeros((nb, CMP_HIDDEN), F32) for _ in range(4)]
        for r in range(CMP_STRIDE):
            xr = xs_ref[pl.ds(r, nb, stride=CMP_STRIDE), :]
            for kind, (p_ref, w1_ref) in enumerate(((pk_ref, w1k_ref), (pv_ref, w1v_ref))):
                acc[2 * kind] = acc[2 * kind] + _dot((xr + p_ref[r:r + 1, :]).astype(BF16), w1_ref[r])
                acc[2 * kind + 1] = acc[2 * kind + 1] + _dot(
                    (xr + p_ref[CMP_STRIDE + r:CMP_STRIDE + r + 1, :]).astype(BF16), w1_ref[CMP_STRIDE + r])
        hk = _gelu(acc[0] + pltpu.roll(acc[1], nb - 1, axis=0)).astype(BF16)
        hv = _gelu(acc[2] + pltpu.roll(acc[3], nb - 1, axis=0)).astype(BF16)
        ko_ref[g] = (_dot(hk, w2k_ref[...]) + ekc_ref[...]).astype(ko_ref.dtype)
        vot_ref[g] = _dot_nt(w2vt_ref[...], hv).astype(vot_ref.dtype)


def _compress(pm3, pos_k, pos_v, w1k, w1v, w2k, w2vt, ekc):
    bsz, seq, _ = pm3.shape
    nb = seq // CMP_STRIDE
    c3 = lambda b: (0, 0, 0)
    c2 = lambda b: (0, 0)
    return pl.pallas_call(
        _compress_kernel,
        grid=(bsz,),
        in_specs=[pl.BlockSpec((None, seq, SLOTW), lambda b: (b, 0, _KVC0 * LANES // SLOTW)),
                  pl.BlockSpec((CMP_LEN, LANES), c2), pl.BlockSpec((CMP_LEN, LANES), c2),
                  pl.BlockSpec((CMP_LEN, LANES, CMP_HIDDEN), c3), pl.BlockSpec((CMP_LEN, LANES, CMP_HIDDEN), c3),
                  pl.BlockSpec((CMP_HIDDEN, LANES), c2), pl.BlockSpec((LANES, CMP_HIDDEN), c2),
                  pl.BlockSpec((nb, LANES), c2)],
        out_specs=[pl.BlockSpec((None, NSA_GROUPS, nb, LANES), lambda b: (b, 0, 0, 0)),
                   pl.BlockSpec((None, NSA_GROUPS, LANES, nb), lambda b: (b, 0, 0, 0))],
        out_shape=[jax.ShapeDtypeStruct((bsz, NSA_GROUPS, nb, LANES), BF16),
                   jax.ShapeDtypeStruct((bsz, NSA_GROUPS, LANES, nb), BF16)],
        scratch_shapes=[pltpu.VMEM((seq, LANES), F32)],
        compiler_params=_cparams(("parallel",)),
        name="nsa_compress",
    )(pm3, pos_k, pos_v, w1k, w1v, w2k, w2vt, ekc)


def _nsa_kernel(q_ref, kc_ref, vct_ref, kvs_ref, kvw_ref, pg_ref,
                ek_ref, al_ref, ovt_ref, y_ref,
                ksa_ref, vsat_ref, kwa_ref, vwat_ref, qa_scr, part_scr, gzs_scr, sel_smem):
    seq = kvs_ref.shape[0]
    g_idx = pl.program_id(1)
    qi = pl.program_id(2)
    wtiles = WINDOW // LANES

    @pl.when(qi == 0)
    def _():
        ek = ek_ref[...]
        lane_k = lax.broadcasted_iota(jnp.int32, (seq, LANES), 1)
        zero = jnp.zeros((seq, LANES), BF16)
        ksa_ref[0:seq, :] = jnp.where(lane_k < NSA_DH, kvs_ref[...], zero) + ek
        lane_d = lax.broadcasted_iota(jnp.int32, (LANES, LANES), 1)
        ksa_ref[seq:seq + LANES, :] = jnp.where((lane_d >= SEL_LANE0) & (lane_d < SEL_LANE0 + seq // SEL_BLOCK),
                                                1.0, 0.0).astype(BF16)
        vsat_ref[seq // LANES] = jnp.zeros((VT_ROWS, LANES), BF16)
        ek_w = jnp.where(lane_k >= ALI_LANE0, ek, zero)
        kwa_ref[0:WINDOW, :] = jnp.zeros((WINDOW, LANES), BF16)
        kwa_ref[WINDOW:WINDOW + seq, :] = jnp.where(lane_k < NSA_DH, kvw_ref[...], zero) + ek_w
        ones_rows = jnp.where(lax.broadcasted_iota(jnp.int32, (VT_ROWS - NSA_DH, LANES), 0) == 0, 1.0, 0.0)
        for i in range(wtiles):
            vwat_ref[i] = jnp.zeros((VT_ROWS, LANES), BF16)
        for i in range(seq // LANES):
            rws = slice(i * LANES, (i + 1) * LANES)
            for src_ref, dst_ref, j in ((kvs_ref, vsat_ref, i), (kvw_ref, vwat_ref, wtiles + i)):
                v_t = src_ref[rws, :].astype(F32).T[NSA_DH:2 * NSA_DH, :]
                dst_ref[j] = jnp.concatenate([v_t, ones_rows], axis=0).astype(BF16)

    nq = seq // TQ
    refs = (q_ref, kc_ref, vct_ref, pg_ref, al_ref, ovt_ref, y_ref, ksa_ref, vsat_ref, kwa_ref, vwat_ref,
            qa_scr, part_scr, gzs_scr, sel_smem)
    for step in range(nq + 1):
        pl.when(qi == step)(functools.partial(_nsa_step, step, nq, seq // SEL_BLOCK, g_idx, refs))


SEL_DYN = 2
SEL_CNT = 7


def _sel_compactable(qi):
    return (qi + 1) * TQ // LANES > 1 + SEL_DYN + TQ // LANES


def _nsa_step(step, nq, nsel, g_idx, refs):
    def run(compact):
        stages = []
        if step < nq:
            stages.append(_nsa_front(nsel, step, g_idx, refs))
        if step > 0:
            stages.append(_nsa_back(step - 1, compact, refs))
        while stages:
            for st in list(stages):
                if next(st, "done") == "done":
                    stages.remove(st)

    if step > 0 and _sel_compactable(step - 1):
        fits = refs[-1][(step - 1) & 1, SEL_CNT] <= SEL_DYN
        pl.when(fits)(functools.partial(run, True))
        pl.when(jnp.logical_not(fits))(functools.partial(run, False))
    else:
        run(False)


def _tile_iotas():
    rows = NSA_HPG * TQ
    jrow = lax.broadcasted_iota(jnp.int32, (LANES, rows), 0)
    tq = lax.broadcasted_iota(jnp.int32, (LANES, rows), 1) & (TQ - 1)
    return jrow, tq, jrow - tq


def _nsa_front(nsel, qi, g_idx, refs):
    (q_ref, kc_ref, vct_ref, pg_ref, al_ref, ovt_ref, _, _, _, kwa_ref, vwat_ref,
     qa_scr, part_scr, gzs_scr, sel_smem) = refs
    hp = NSA_HPG
    span = WINDOW + TQ
    t0 = qi * TQ
    tw = t0
    early = t0 < WINDOW
    par = qi & 1
    jrow, tq, kmt = _tile_iotas()

    qnz = [q_ref[:, hh * LANES:(hh + 1) * LANES].astype(F32) for hh in range(hp)]
    lane_q = lax.broadcasted_iota(jnp.int32, (TQ, LANES), 1)
    qal = jnp.concatenate([jnp.where(lane_q < NSA_DH, qnz[hh], al_ref[pl.ds(g_idx * hp + hh, 1), :])
                           for hh in range(hp)], axis=0)
    qalb = qal.astype(BF16)
    s_c = _dot_nt(kc_ref[...], qalb)
    s_w = _dot_nt(kwa_ref[pl.ds(tw, span), :], qalb)
    yield

    s = jnp.where(tq - jrow * CMP_STRIDE >= (CMP_LEN - 1) - t0, s_c, NEG)
    e = jnp.exp2(s - jnp.max(s, axis=0, keepdims=True))
    inv = 1.0 / jnp.sum(e, axis=0, keepdims=True)
    if early:
        inv = jnp.where(t0 + tq[0:1, :] >= CMP_LEN - 1, inv, 0.0)
    p = e * inv
    oc_t = _dot(vct_ref[...], p.astype(BF16))
    psum_t = p[:, 0:TQ]
    for hh in range(1, hp):
        psum_t = psum_t + p[:, hh * TQ:(hh + 1) * TQ]
    p_hi = psum_t.astype(BF16)
    p_lo = (psum_t - p_hi.astype(F32)).astype(BF16)
    imp_t = _dot(ovt_ref[...], p_hi) + _dot(ovt_ref[...], p_lo)
    yield

    blocks = [s_w[i * LANES:(i + 1) * LANES, :] for i in range(span // LANES)]
    for i in range(span // LANES):
        if i * LANES < TQ:
            blocks[i] = jnp.where(kmt > -i * LANES, blocks[i], NEG)
        if (i + 1) * LANES > WINDOW:
            blocks[i] = jnp.where(kmt <= WINDOW - i * LANES, blocks[i], NEG)
        if early:
            blocks[i] = jnp.where(jrow >= WINDOW - t0 - i * LANES, blocks[i], NEG)
    s = jnp.concatenate(blocks, axis=0)
    p = jnp.exp2(s - jnp.max(s, axis=0, keepdims=True)).astype(BF16)
    vw_t = jnp.concatenate([vwat_ref[qi * (TQ // LANES) + i] for i in range(span // LANES)], axis=1)
    accw_t = _dot(vw_t, p)
    yield

    score = imp_t[SEL_LANE0:SEL_LANE0 + nsel, :]
    nrow = lax.broadcasted_iota(jnp.int32, (nsel, TQ), 0)
    own = (t0 + lax.broadcasted_iota(jnp.int32, (nsel, TQ), 1)) >> int(math.log2(SEL_BLOCK))
    valid = nrow <= own
    forced = (nrow == 0) | (nrow == own)
    score = jnp.where(forced, BIG, jnp.where(valid, score, -1.0))
    chosen = jnp.zeros((nsel, TQ), F32)
    nrow_f = nrow.astype(F32)
    for _ in range(SEL_TOPN):
        mx = jnp.max(score, axis=0, keepdims=True)
        first = jnp.min(jnp.where(score == mx, nrow_f, 4.0 * LANES), axis=0, keepdims=True)
        hit = nrow_f == first
        chosen = jnp.where(hit, 1.0, chosen)
        score = jnp.where(hit, -3.0, score)
    mask_t = jnp.where(chosen > 0.0, jnp.where(valid, 0.0, NEG), NEG)
    if _sel_compactable(qi):
        used_blk = jnp.max(jnp.where(mask_t == 0.0, 1.0, 0.0), axis=1, keepdims=True)
        blk_per_tile = LANES // SEL_BLOCK
        for slot in range(SEL_DYN):
            sel_smem[par, slot] = jnp.int32(nsel // blk_per_tile)
        cnt = jnp.int32(0)
        for i in range(1, t0 // LANES):
            used = jnp.max(used_blk[i * blk_per_tile:(i + 1) * blk_per_tile, :]) > 0.0

            @pl.when(used & (cnt < SEL_DYN))
            def _(i=i, cnt=cnt):
                sel_smem[par, cnt] = jnp.int32(i)

            cnt = cnt + used.astype(jnp.int32)
        sel_smem[par, SEL_CNT] = cnt
    maskcols = jnp.concatenate([jnp.zeros((SEL_LANE0, TQ), F32), mask_t,
                                jnp.zeros((LANES - SEL_LANE0 - nsel, TQ), F32)], axis=0).T

    qa_scr[par] = (qal + jnp.concatenate([maskcols] * hp, axis=0)).astype(BF16)

    gt_t = _sigmoid(pg_ref[...]).T
    for hh in range(hp):
        g_c, g_s, g_w = [gt_t[j * hp + hh:j * hp + hh + 1, :] for j in range(3)]
        hcols = slice(hh * TQ, (hh + 1) * TQ)
        aw_t = accw_t[:, hcols]
        gz = _silu(qnz[hh].T[NSA_DH:2 * NSA_DH, :])
        o = g_c * oc_t[0:NSA_DH, hcols] + (g_w * (1.0 / aw_t[NSA_DH:NSA_DH + 1, :])) * aw_t[0:NSA_DH, :]
        part_scr[par, hh * NSA_DH:(hh + 1) * NSA_DH, :] = o * gz
        gzs_scr[par, hh * NSA_DH:(hh + 1) * NSA_DH, :] = g_s * gz


def _nsa_back(qi, compact, refs):
    _, _, _, _, _, _, y_ref, ksa_ref, vsat_ref, _, _, qa_scr, part_scr, gzs_scr, sel_smem = refs
    hp = NSA_HPG
    t0 = qi * TQ
    par = qi & 1
    _, _, kmt = _tile_iotas()
    qa = qa_scr[par]
    ntiles = (t0 + TQ) // LANES
    if compact:
        tiles = [0] + [sel_smem[par, j] for j in range(SEL_DYN)] + list(range(t0 // LANES, ntiles))
    else:
        tiles = list(range(ntiles))
    per = KCH // LANES
    chunks = [tiles[i:i + per] for i in range(0, len(tiles), per)]
    nk = len(chunks)

    def k_tile(i):
        if isinstance(i, int):
            return ksa_ref[i * LANES:(i + 1) * LANES, :]
        return ksa_ref[pl.ds(pl.multiple_of(i * LANES, LANES), LANES), :]

    def scores(c):
        s = _dot_nt(jnp.concatenate([k_tile(i) for i in chunks[c]], axis=0), qa)
        if any(isinstance(i, int) and (i + 1) * LANES > t0 for i in chunks[c]):
            s = jnp.concatenate(
                [jnp.where(kmt <= t0 - i * LANES, s[n * LANES:(n + 1) * LANES, :], NEG)
                 if isinstance(i, int) and (i + 1) * LANES > t0 else s[n * LANES:(n + 1) * LANES, :]
                 for n, i in enumerate(chunks[c])], axis=0)
        return s

    def weighted(c, s):
        mc = jnp.max(s, axis=0, keepdims=True)
        v_t = jnp.concatenate([vsat_ref[i] for i in chunks[c]], axis=1)
        return mc, _dot(v_t, jnp.exp2(s - mc).astype(BF16))

    parts = []
    s_next = scores(0)
    for c in range(nk):
        s_cur = s_next
        if c + 1 < nk:
            s_next = scores(c + 1)
        yield
        parts.append(weighted(c, s_cur))
    yield
    m = parts[0][0]
    for mc, _ in parts[1:]:
        m = jnp.maximum(m, mc)
    accs_t = None
    for mc, d in parts:
        d = d if nk == 1 else jnp.exp2(mc - m) * d
        accs_t = d if accs_t is None else accs_t + d

    outs = []
    for hh in range(hp):
        as_t = accs_t[:, hh * TQ:(hh + 1) * TQ]
        rws = slice(hh * NSA_DH, (hh + 1) * NSA_DH)
        outs.append(part_scr[par, rws, :] + (gzs_scr[par, rws, :] * (1.0 / as_t[NSA_DH:NSA_DH + 1, :])) * as_t[0:NSA_DH, :])
    for pr in range(hp // 2):
        packed = jnp.concatenate([outs[2 * pr], outs[2 * pr + 1]], axis=0).T
        y_ref[:, pr * LANES:(pr + 1) * LANES] = packed.astype(y_ref.dtype)


def _nsa(pm3, pg3, kcc, vct, ek, al, ovt):
    bsz, seq, _ = pm3.shape
    nq = seq // TQ
    gw = NSA_HPG * NSA_DH
    kv = lambda tile0: pl.BlockSpec((None, seq, LANES), lambda b, g, q: (b, 0, tile0 + g))
    c2 = lambda b, g, q: (0, 0)
    front = lambda b, g, q: (b, jnp.minimum(q, nq - 1), g)
    back = lambda b, g, q: (b, jnp.maximum(q - 1, 0), g)
    return pl.pallas_call(
        _nsa_kernel,
        grid=(bsz, NSA_GROUPS, nq + 1),
        in_specs=[pl.BlockSpec((None, TQ, NSA_HPG * LANES), front),
                  pl.BlockSpec((None, None, LANES, LANES), lambda b, g, q: (b, g, 0, 0)),
                  pl.BlockSpec((None, None, LANES, LANES), lambda b, g, q: (b, g, 0, 0)),
                  kv(_KVS0), kv(_KVW0),
                  pl.BlockSpec((None, TQ, LANES), front),
                  pl.BlockSpec((seq, LANES), c2),
                  pl.BlockSpec((NSA_HEADS, LANES), c2),
                  pl.BlockSpec((LANES, LANES), c2)],
        out_specs=pl.BlockSpec((None, TQ, gw), back),
        out_shape=jax.ShapeDtypeStruct((bsz, seq, NSA_HEADS * NSA_DH), BF16),
        scratch_shapes=[pltpu.VMEM((seq + LANES, LANES), BF16),
                        pltpu.VMEM((seq // LANES + 1, VT_ROWS, LANES), BF16),
                        pltpu.VMEM((seq + WINDOW, LANES), BF16),
                        pltpu.VMEM(((seq + WINDOW) // LANES, VT_ROWS, LANES), BF16),
                        pltpu.VMEM((2, NSA_HPG * TQ, LANES), BF16),
                        pltpu.VMEM((2, gw, TQ), F32), pltpu.VMEM((2, gw, TQ), F32),
                        pltpu.SMEM((2, SEL_CNT + 1), jnp.int32)],
        compiler_params=_cparams(("parallel", "arbitrary", "arbitrary")),
        name="nsa_attn",
    )(pm3, kcc, vct, pm3, pm3, pg3, ek, al, ovt)


def _nsa_tables(seq):
    t = np.arange(seq)
    ek = np.zeros((seq, LANES), np.float32)
    ek[t, SEL_LANE0 + t // SEL_BLOCK] = 1.0
    for c in range(3):
        ek[:, ALI_LANE0 + c] = t // SEL_BLOCK
        ek[:, ALI_LANE0 + 3 + c] = t % SEL_BLOCK
    slopes = jnp.exp2(-8.0 * jnp.arange(1, NSA_HEADS + 1, dtype=F32) / NSA_HEADS) * LOG2E
    s_hi, s_mid, s_lo = _split3(slopes)
    parts = [p.astype(F32) for p in (s_hi, s_mid, s_lo)]
    al = jnp.pad(jnp.stack([p * SEL_BLOCK for p in parts] + parts, axis=1),
                 ((0, 0), (ALI_LANE0, LANES - ALI_LANE0 - 2 * len(parts))))
    nb = seq // CMP_STRIDE
    cstart = np.arange(nb) * CMP_STRIDE
    cend = cstart + CMP_LEN - 1
    ekc = np.zeros((nb, LANES), np.float32)
    for c in range(3):
        ekc[:, ALI_LANE0 + c] = cend // SEL_BLOCK
        ekc[:, ALI_LANE0 + 3 + c] = cend % SEL_BLOCK
    sel = np.arange(seq // SEL_BLOCK)
    ovt = np.zeros((LANES, max(nb, LANES)), np.float32)
    ovl = (cstart[:, None] < (sel[None, :] + 1) * SEL_BLOCK) & (cstart[:, None] + CMP_LEN > sel[None, :] * SEL_BLOCK)
    ovl[nb - 1, :] = False
    ovt[SEL_LANE0:SEL_LANE0 + len(sel), :nb] = ovl.T
    return jnp.asarray(ek, BF16), al, jnp.asarray(ekc), jnp.asarray(ovt, BF16)


def _s5_disc_kernel(are_ref, aim_ref, ldt_ref, bre_ref, bim_ref, abr_ref, abi_ref, bbr_ref, bbi_ref):
    a_re = are_ref[...]
    a_im = aim_ref[...]
    dt = jnp.exp(ldt_ref[...])
    er = jnp.exp(a_re * dt)
    abar_re = er * jnp.cos(a_im * dt)
    abar_im = er * jnp.sin(a_im * dt)
    lam2 = a_re * a_re + a_im * a_im
    cr = ((abar_re - 1.0) * a_re + abar_im * a_im) / lam2
    ci = (abar_im * a_re - (abar_re - 1.0) * a_im) / lam2
    b_re = bre_ref[...]
    b_im = bim_ref[...]
    abr_ref[...] = abar_re
    abi_ref[...] = abar_im
    bbr_ref[...] = cr * b_re - ci * b_im
    bbi_ref[...] = cr * b_im + ci * b_re


def _s5_disc(a_re_rep, a_im_rep, ldt_rep, b_re_t, b_im_t):
    shp = jax.ShapeDtypeStruct(a_re_rep.shape, F32)
    return pl.pallas_call(_s5_disc_kernel, out_shape=[shp] * 4, name="s5_disc")(
        a_re_rep, a_im_rep, ldt_rep, b_re_t, b_im_t)


S5_SLABG = 4
S5_TT = 64
S5_PITCH = S5_TT + 4
S5_DIAG = 256
S5_NSTATE_PER_DIAG = S5_DIAG // S5_GROUP_CH * S5_STATE


def _s5_kernel(u_ref, bre_ref, bim_ref, ar_ref, ai_ref, cre_ref, cim_ref, d_ref, wg_ref, o_ref,
               xr_ref, xi_ref, sr_ref, si_ref):
    nb, tt, wd = u_ref.shape
    pitch = sr_ref.shape[1] // nb
    gcols = S5_SLABG * LANES
    ngroups = S5_NSTATE // gcols

    @pl.when(pl.program_id(0) == 0)
    def _():
        xr_ref[...] = jnp.zeros(xr_ref.shape, F32)
        xi_ref[...] = jnp.zeros(xi_ref.shape, F32)

    u = u_ref[...].reshape(nb * tt, wd)

    chan = lambda kg: slice(kg * gcols // S5_NSTATE_PER_DIAG * S5_DIAG,
                            (kg * gcols // S5_NSTATE_PER_DIAG + 1) * S5_DIAG)
    for kg in range(ngroups):
        cols = slice(kg * gcols, (kg + 1) * gcols)
        br = _dot(u[:, chan(kg)], bre_ref[chan(kg), cols])
        bi = _dot(u[:, chan(kg)], bim_ref[chan(kg), cols])
        for j in range(S5_SLABG):
            for b in range(nb):
                sr_ref[kg * S5_SLABG + j, b * pitch:b * pitch + tt, :] = br[b * tt:(b + 1) * tt, j * LANES:(j + 1) * LANES]
                si_ref[kg * S5_SLABG + j, b * pitch:b * pitch + tt, :] = bi[b * tt:(b + 1) * tt, j * LANES:(j + 1) * LANES]

    for kg in range(ngroups):
        cols = slice(kg * gcols, (kg + 1) * gcols)
        slabs = range(kg * S5_SLABG, (kg + 1) * S5_SLABG)
        ar = ar_ref[:, cols]
        ai = ai_ref[:, cols]

        def body(t, carry, slabs=slabs, ar=ar, ai=ai):
            xr, xi = carry
            rows = pl.ds(t, nb, stride=pitch)
            nxr = ar * xr - ai * xi + jnp.concatenate([sr_ref[k, rows, :] for k in slabs], axis=1)
            nxi = ar * xi + ai * xr + jnp.concatenate([si_ref[k, rows, :] for k in slabs], axis=1)
            for j, k in enumerate(slabs):
                sr_ref[k, rows, :] = nxr[:, j * LANES:(j + 1) * LANES]
                si_ref[k, rows, :] = nxi[:, j * LANES:(j + 1) * LANES]
            return nxr, nxi

        xr, xi = lax.fori_loop(0, tt, body, (xr_ref[:, cols], xi_ref[:, cols]), unroll=4)
        xr_ref[:, cols] = xr
        xi_ref[:, cols] = xi

    y_blocks = [jnp.zeros((nb * tt, S5_DIAG), F32) for _ in range(wd // S5_DIAG)]
    for kg in range(ngroups):
        cols = slice(kg * gcols, (kg + 1) * gcols)
        gather = lambda ref: jnp.concatenate(
            [jnp.concatenate([ref[kg * S5_SLABG + j, b * pitch:b * pitch + tt, :] for j in range(S5_SLABG)], axis=1)
             for b in range(nb)], axis=0).astype(BF16)
        blk = chan(kg).start // S5_DIAG
        y_blocks[blk] = (y_blocks[blk] + _dot(gather(sr_ref), cre_ref[cols, chan(kg)])
                         - _dot(gather(si_ref), cim_ref[cols, chan(kg)]))
    y = jnp.concatenate(y_blocks, axis=1)
    y = _gelu(y + d_ref[...] * u.astype(F32))
    z = _dot(y.astype(BF16), wg_ref[...])
    o = z[:, :S5_WIDTH] * _sigmoid(z[:, S5_WIDTH:])
    o_ref[...] = o.reshape(nb, tt, wd).astype(o_ref.dtype)


def _s5(pm3, bbd_re, bbd_im, ar8, ai8, cbd_re, cbd_im, d_row, w_glu):
    bsz, seq, _ = pm3.shape
    tt = S5_TT
    c2 = lambda i: (0, 0)
    nslab = S5_NSTATE // LANES
    return pl.pallas_call(
        _s5_kernel,
        grid=(seq // tt,),
        in_specs=[pl.BlockSpec((bsz, tt, S5_WIDTH), lambda i: (0, i, _SU0 // S5_WIDTH)),
                  pl.BlockSpec((S5_WIDTH, S5_NSTATE), c2), pl.BlockSpec((S5_WIDTH, S5_NSTATE), c2),
                  pl.BlockSpec((8, S5_NSTATE), c2), pl.BlockSpec((8, S5_NSTATE), c2),
                  pl.BlockSpec((S5_NSTATE, S5_WIDTH), c2), pl.BlockSpec((S5_NSTATE, S5_WIDTH), c2),
                  pl.BlockSpec((1, S5_WIDTH), c2), pl.BlockSpec((S5_WIDTH, 2 * S5_WIDTH), c2)],
        out_specs=pl.BlockSpec((bsz, tt, S5_WIDTH), lambda i: (0, i, 0)),
        out_shape=jax.ShapeDtypeStruct((bsz, seq, S5_WIDTH), BF16),
        scratch_shapes=[pltpu.VMEM((8, S5_NSTATE), F32), pltpu.VMEM((8, S5_NSTATE), F32),
                        pltpu.VMEM((nslab, bsz * S5_PITCH, LANES), F32),
                        pltpu.VMEM((nslab, bsz * S5_PITCH, LANES), F32)],
        compiler_params=_cparams(("arbitrary",)),
        name="s5_scan",
    )(pm3, bbd_re, bbd_im, ar8, ai8, cbd_re, cbd_im, d_row, w_glu)


def _pad_cols(w, n):
    return jnp.pad(w, ((0, 0), (0, n - w.shape[1])))


def _cd_proj_weights(w_in):
    d = w_in.shape[0]
    nw, nkv = NSA_HEADS * NSA_DH, NSA_GROUPS * NSA_DH
    offs = np.cumsum((0, nw, nkv, nkv, nkv, nkv, nkv, nkv, 3 * NSA_HEADS, nw, S5_WIDTH, S5_WIDTH))
    seg = lambda i: w_in[:, offs[i]:offs[i + 1]]
    def pair(ia, ib, scale=None):
        pieces = []
        for c in range(0, int(offs[ia + 1] - offs[ia]), NSA_DH):
            a = w_in[:, offs[ia] + c:offs[ia] + c + NSA_DH]
            pieces += [a if scale is None else a * scale, w_in[:, offs[ib] + c:offs[ib] + c + NSA_DH]]
        return pieces
    w_main = jnp.concatenate(pair(0, 8, NSA_DH ** -0.5 * LOG2E) + pair(1, 2) + pair(3, 4) + pair(5, 6)
                             + [seg(9), seg(10)], axis=1).astype(BF16)
    w_gate = seg(7).reshape(d, NSA_GROUPS, NSA_HPG, 3).transpose(0, 1, 3, 2).reshape(d, NSA_GROUPS, 3 * NSA_HPG)
    w_gate = jnp.pad(w_gate, ((0, 0), (0, 0), (0, LANES - 3 * NSA_HPG))).reshape(d, NSA_GROUPS * LANES).astype(BF16)
    return w_gate, w_main


def _layer0(h2, p4, layer, bsz, seq, norm_pre, norm_post, pe_proj, pe_gate,
            w_in, ret_norm, conv_w, conv_b, gate_b, ml_norm, w_out):
    d = h2.shape[1]
    g0 = 5120
    w_main = jnp.concatenate([w_in[:, :g0], w_in[:, g0 + 2 * ML_HEADS:]], axis=1).astype(BF16)
    w_gate = _pad_cols(w_in[:, g0:g0 + 2 * ML_HEADS], LANES).astype(BF16)
    pm, pg = _norm_proj(h2, norm_pre.reshape(1, d), w_gate, w_main)
    mix = _ab_mixer(pm, pg, conv_w, conv_b.reshape(1, -1), _pad_cols(gate_b.reshape(1, -1), LANES),
                    ret_norm.reshape(1, -1), ml_norm.reshape(1, -1), bsz, seq)
    ka = RET_HEADS * RET_DV
    nt = seq // 512
    return _out_pe(mix, lambda b, s: (b * nt + s, 0), ka, mix, lambda b, s: (b * nt + s, 1), ML_HEADS * ML_DV,
                   None, None, h2, p4, layer, w_out[:ka].astype(BF16), w_out[ka:].astype(BF16),
                   norm_post.reshape(1, d), pe_gate.astype(BF16), pe_proj.astype(BF16), bsz, seq)


def _layer1(h2, p4, layer, bsz, seq, norm_pre, norm_post, pe_proj, pe_gate,
            w_in, pos_k, pos_v, w1_k, w2_k, w1_v, w2_v,
            a_re, a_im, log_dt, b_re, b_im, c_re, c_im, d_skip, w_glu, w_out):
    d = h2.shape[1]
    nw = NSA_HEADS * NSA_DH
    pm, pg = _norm_proj(h2, norm_pre.reshape(1, d), *_cd_proj_weights(w_in))

    nb = seq // CMP_STRIDE
    pm3 = pm.reshape(bsz, seq, CD_MAIN)
    padl = lambda a, lo=True: jnp.pad(a, ((0, 0),) * (a.ndim - 1)
                                      + ((0, LANES - a.shape[-1]) if lo else (LANES - a.shape[-1], 0),))
    w1 = lambda w, lo: jnp.pad(w.reshape(CMP_LEN, NSA_DH, CMP_HIDDEN),
                               ((0, 0), (0, LANES - NSA_DH) if lo else (LANES - NSA_DH, 0), (0, 0))).astype(BF16)
    ek, al, ekc, ovt = _nsa_tables(seq)
    kcc, vct = _compress(pm3, padl(pos_k), padl(pos_v, False), w1(w1_k, True), w1(w1_v, False),
                         padl(w2_k).astype(BF16), padl(w2_v).T.astype(BF16), ekc)
    if nb < LANES:
        kcc = jnp.pad(kcc, ((0, 0), (0, 0), (0, LANES - nb), (0, 0)))
        vct = jnp.pad(vct, ((0, 0), (0, 0), (0, 0), (0, LANES - nb)))
    y_nsa = _nsa(pm3, pg.reshape(bsz, seq, NSA_GROUPS * LANES), kcc, vct, ek, al, ovt)

    rep = lambda a: jnp.repeat(a, S5_GROUP_CH, axis=0)
    ldt = jnp.broadcast_to(rep(log_dt[:, None]), (S5_WIDTH, S5_STATE))
    bt = lambda b: b.transpose(0, 2, 1).reshape(S5_WIDTH, S5_STATE)
    abr, abi, bbr, bbi = _s5_disc(rep(a_re), rep(a_im), ldt, bt(b_re), bt(b_im))
    gi = np.arange(S5_WIDTH) // S5_GROUP_CH
    gs = np.arange(S5_NSTATE) // S5_STATE
    diag_in = jnp.asarray(gi[:, None] == gs[None, :])
    bd_in = lambda bb: jnp.where(diag_in, jnp.tile(bb, (1, S5_GROUPS)), 0.0).astype(BF16)
    bd_out = lambda c: jnp.where(diag_in.T, jnp.tile(c.transpose(0, 2, 1).reshape(S5_NSTATE, S5_GROUP_CH),
                                                     (1, S5_GROUPS)), 0.0).astype(BF16)
    row8 = lambda a: jnp.broadcast_to(a[::S5_GROUP_CH].reshape(1, S5_NSTATE), (8, S5_NSTATE))
    y_s5 = _s5(pm3, bd_in(bbr), bd_in(bbi), row8(abr), row8(abi),
               bd_out(c_re), bd_out(c_im), d_skip.reshape(1, -1), w_glu.astype(BF16))

    nt = seq // 512
    szb = _SZ0 // S5_WIDTH
    return _out_pe(y_nsa.reshape(bsz * seq, nw), lambda b, s: (b * nt + s, 0), nw,
                   y_s5.reshape(bsz * seq, S5_WIDTH), lambda b, s: (b * nt + s, 0), S5_WIDTH,
                   pm, lambda b, s: (b * nt + s, szb),
                   h2, p4, layer, w_out[:nw].astype(BF16), w_out[nw:].astype(BF16),
                   norm_post.reshape(1, d), pe_gate.astype(BF16), pe_proj.astype(BF16), bsz, seq)


def kernel(x, p, norm_pre, norm_post, pe_proj, pe_gate, ab_w_in, ret_norm, ml_conv_w, ml_conv_b, ml_gate_b,
           ml_norm, ab_w_out, cd_w_in, cmp_pos_k, cmp_pos_v, cmp_w1_k, cmp_w2_k, cmp_w1_v, cmp_w2_v,
           s5_a_re, s5_a_im, s5_log_dt, s5_b_re, s5_b_im, s5_c_re, s5_c_im, s5_d, s5_w_glu, cd_w_out):
    bsz, seq, d = x.shape
    assert bsz == 8, "the S5 scan maps the batch onto the 8 sublanes of a vreg"
    assert seq % max(KCH, 1024) == 0
    depth = p.shape[0]
    h2 = x.reshape(bsz * seq, d)
    for i in range(depth):
        j = i // 2
        if i % 2 == 0:
            h2 = _layer0(h2, p, i, bsz, seq, norm_pre[i], norm_post[i], pe_proj[i], pe_gate[i],
                         ab_w_in[j], ret_norm[j], ml_conv_w[j], ml_conv_b[j], ml_gate_b[j], ml_norm[j], ab_w_out[j])
        else:
            h2 = _layer1(h2, p, i, bsz, seq, norm_pre[i], norm_post[i], pe_proj[i], pe_gate[i],
                         cd_w_in[j], cmp_pos_k[j], cmp_pos_v[j], cmp_w1_k[j], cmp_w2_k[j], cmp_w1_v[j], cmp_w2_v[j],
                         s5_a_re[j], s5_a_im[j], s5_log_dt[j], s5_b_re[j], s5_b_im[j], s5_c_re[j], s5_c_im[j],
                         s5_d[j], s5_w_glu[j], cd_w_out[j])
    return h2.reshape(bsz, seq, d)
```

```python
import functools
import math

import numpy as np
import jax
import jax.numpy as jnp
from jax import lax
from jax.experimental import pallas as pl
from jax.experimental.pallas import tpu as pltpu

F32 = jnp.float32
BF16 = jnp.bfloat16
EPS = 1e-6
NEG = -1e30
BIG = 1e30

LANES = 128
CHUNK = 128
RET_HEADS, RET_DK, RET_DV = 4, 128, 256
ML_HEADS, ML_DK, ML_DV = 4, 128, 256
ML_CONV = 4
NSA_HEADS, NSA_GROUPS, NSA_DH = 16, 4, 64
NSA_HPG = NSA_HEADS // NSA_GROUPS
CMP_STRIDE, CMP_LEN, CMP_HIDDEN = 16, 32, 128
SEL_BLOCK, SEL_TOPN, WINDOW = 64, 4, 512
S5_GROUPS, S5_GROUP_CH, S5_STATE = 32, 16, 64
S5_WIDTH = S5_GROUPS * S5_GROUP_CH
S5_NSTATE = S5_GROUPS * S5_STATE

TQ = 256
KCH = 512
SEL_LANE0 = 64
ALI_LANE0 = 96
VMEM_LIMIT = 56 * 1024 * 1024


def _cparams(sem):
    return pltpu.CompilerParams(dimension_semantics=sem, vmem_limit_bytes=VMEM_LIMIT)


def _dot(a, b):
    return jnp.dot(a, b, preferred_element_type=F32)


def _dot_nt(a, b):
    return lax.dot_general(a, b, (((1,), (1,)), ((), ())), preferred_element_type=F32)


def _split3(x):
    hi = x.astype(BF16)
    r1 = x - hi.astype(F32)
    mid = r1.astype(BF16)
    lo = (r1 - mid.astype(F32)).astype(BF16)
    return hi, mid, lo


def _dot_f32_lhs(x, m):
    hi, mid, lo = _split3(x)
    return _dot(hi, m) + _dot(mid, m) + _dot(lo, m)


def _sigmoid(x):
    return 1.0 / (1.0 + jnp.exp(-x))


def _silu(x):
    return x * _sigmoid(x)


def _gelu(x):
    return 0.5 * x * (1.0 + jnp.tanh(math.sqrt(2.0 / math.pi) * (x + 0.044715 * (x * x * x))))


def _rms(x, g):
    return x * lax.rsqrt(jnp.mean(x * x, -1, keepdims=True) + EPS) * g


PROJ_TN = 1024


def _norm_proj_kernel(h_ref, g_ref, wg_ref, wm_ref, om_ref, og_ref):
    ub = _rms(h_ref[...], g_ref[...]).astype(BF16)
    nm = om_ref.shape[1]
    for c0 in range(0, nm, PROJ_TN):
        c1 = min(c0 + PROJ_TN, nm)
        om_ref[:, c0:c1] = _dot(ub, wm_ref[:, c0:c1]).astype(om_ref.dtype)
    og_ref[...] = _dot(ub, wg_ref[...])


def _norm_proj(h2, g, w_gate, w_main, tm=512):
    n, d = h2.shape
    nm = w_main.shape[1]
    ng = w_gate.shape[1]
    once = pl.Buffered(1)
    return pl.pallas_call(
        _norm_proj_kernel,
        grid=(n // tm,),
        in_specs=[pl.BlockSpec((tm, d), lambda i: (i, 0)),
                  pl.BlockSpec((1, d), lambda i: (0, 0)),
                  pl.BlockSpec((d, ng), lambda i: (0, 0), pipeline_mode=once),
                  pl.BlockSpec((d, nm), lambda i: (0, 0), pipeline_mode=once)],
        out_specs=[pl.BlockSpec((tm, nm), lambda i: (i, 0)),
                   pl.BlockSpec((tm, ng), lambda i: (i, 0))],
        out_shape=[jax.ShapeDtypeStruct((n, nm), BF16), jax.ShapeDtypeStruct((n, ng), F32)],
        compiler_params=_cparams(("parallel",)),
        name="norm_proj",
    )(h2, g, w_gate, w_main)


def _out_pe_kernel(gated, *refs):
    if gated:
        ma_ref, mb_ref, sz_ref, h_ref, p_ref, wa_ref, wb_ref, gp_ref, wg_ref, wp_ref, o_ref = refs
        sz = sz_ref[...].astype(F32)
        mb = (mb_ref[...].astype(F32) * _silu(sz)).astype(BF16)
    else:
        ma_ref, mb_ref, h_ref, p_ref, wa_ref, wb_ref, gp_ref, wg_ref, wp_ref, o_ref = refs
        mb = mb_ref[...]
    y = _dot(ma_ref[...], wa_ref[...]) + _dot(mb, wb_ref[...])
    h1 = h_ref[...] + _rms(y, gp_ref[...])
    gate = _sigmoid(_dot(h1.astype(BF16), wg_ref[...]))
    pp = _dot(p_ref[...].astype(BF16), wp_ref[...])
    o_ref[...] = h1 + gate * pp


def _out_pe(mix_a, a_map, ka, mix_b, b_map, kb, sz, sz_map, h2, p4, layer, w_a, w_b, g_post, w_gate, w_pe,
            bsz, seq, tm=512):
    n, d = h2.shape
    pe = p4.shape[-1]
    nt = seq // tm
    row = lambda b, s: (b * nt + s, 0)
    const = lambda b, s: (0, 0)
    gated = sz is not None
    in_specs = [pl.BlockSpec((tm, ka), a_map), pl.BlockSpec((tm, kb), b_map)]
    args = [mix_a, mix_b]
    if gated:
        in_specs.append(pl.BlockSpec((tm, kb), sz_map))
        args.append(sz)
    in_specs += [pl.BlockSpec((tm, d), row), pl.BlockSpec((None, None, tm, pe), lambda b, s: (layer, b, s, 0)),
                 pl.BlockSpec((ka, d), const), pl.BlockSpec((kb, d), const), pl.BlockSpec((1, d), const),
                 pl.BlockSpec((d, d), const), pl.BlockSpec((pe, d), const)]
    args += [h2, p4, w_a, w_b, g_post, w_gate, w_pe]
    return pl.pallas_call(
        functools.partial(_out_pe_kernel, gated),
        grid=(bsz, nt),
        in_specs=in_specs,
        out_specs=pl.BlockSpec((tm, d), row),
        out_shape=jax.ShapeDtypeStruct((n, d), F32),
        compiler_params=_cparams(("parallel", "arbitrary")),
        name="out_pe_gated" if gated else "out_pe",
    )(*args)


_RQ, _RK, _RV, _RG = 0, 512, 1024, 2048
_MQ, _MK, _MV, _MO, _MZ = 3072, 3584, 4096, 5120, 6144
AB_MAIN = 7168


def _head_norm(y, g):
    yc = y - jnp.mean(y, -1, keepdims=True)
    return yc * lax.rsqrt(jnp.mean(yc * yc, -1, keepdims=True) + EPS) * g


def _ab_kernel(pm_ref, pg_ref, cw_ref, cb_ref, gb_ref, rn_ref, mn_ref, mix_ref,
               ext_ref, r_ref, c_ref, n_ref, m_ref):
    L = CHUNK
    c_idx = pl.program_id(1)

    @pl.when(c_idx == 0)
    def _():
        ext_ref[0:8, :] = jnp.zeros((8, ext_ref.shape[1]), F32)
        r_ref[...] = jnp.zeros(r_ref.shape, F32)
        c_ref[...] = jnp.zeros(c_ref.shape, F32)
        n_ref[...] = jnp.zeros(n_ref.shape, F32)
        m_ref[...] = jnp.full(m_ref.shape, NEG, F32)

    row = lax.broadcasted_iota(jnp.int32, (L, L), 0)
    col = lax.broadcasted_iota(jnp.int32, (L, L), 1)
    causal = row >= col
    diff = (row - col).astype(F32)
    tcol = lax.broadcasted_iota(jnp.int32, (L, 1), 0).astype(F32)

    for h in range(RET_HEADS):
        lg = math.log1p(-(2.0 ** (-5.0 - h)))
        scale = RET_DK ** -0.5
        decay = jnp.where(causal, jnp.exp(jnp.maximum(diff, 0.0) * lg), 0.0) * scale
        q = pm_ref[:, _RQ + h * RET_DK:_RQ + (h + 1) * RET_DK]
        k = pm_ref[:, _RK + h * RET_DK:_RK + (h + 1) * RET_DK]
        v = pm_ref[:, _RV + h * RET_DV:_RV + (h + 1) * RET_DV]
        sc = _dot_nt(q, k) * decay
        xi = jnp.exp((tcol + 1.0) * lg) * scale
        qx = (q.astype(F32) * xi).astype(BF16)
        r_prev = r_ref[h]
        y = _dot(sc.astype(BF16), v) + _dot(qx, r_prev.astype(BF16))
        zeta = jnp.exp((L - 1.0 - tcol) * lg)
        kz_t = (k.astype(F32) * zeta).T.astype(BF16)
        r_ref[h] = r_prev * math.exp(L * lg) + _dot(kz_t, v)
        g = pm_ref[:, _RG + h * RET_DV:_RG + (h + 1) * RET_DV].astype(F32)
        out = _head_norm(y, rn_ref[:, h * RET_DV:(h + 1) * RET_DV]) * _silu(g)
        mix_ref[:, h * RET_DV:(h + 1) * RET_DV] = out.astype(mix_ref.dtype)

    nqk = 2 * ML_HEADS * ML_DK
    x = pm_ref[:, _MQ:_MQ + nqk].astype(F32)
    ext_ref[8:8 + L, :] = x
    acc = cb_ref[...] + cw_ref[ML_CONV - 1:ML_CONV, :] * x
    for j in range(ML_CONV - 1):
        acc = acc + cw_ref[j:j + 1, :] * ext_ref[pl.ds(8 - (ML_CONV - 1) + j, L), :]
    ext_ref[0:8, :] = x[L - 8:L, :]
    qk = _silu(acc)

    gt = (pg_ref[...] + gb_ref[...]).T
    g8 = gt[0:8, :]
    logf8 = jnp.minimum(g8, 0.0) - jnp.log(1.0 + jnp.exp(-jnp.abs(g8)))
    triu = jnp.where(row <= col, 1.0, 0.0).astype(BF16)
    bcum8 = _dot_f32_lhs(logf8, triu)
    b_t = pltpu.roll(bcum8, 4, axis=0)
    b_last = b_t[:, L - 1:L]
    a_t = b_last - b_t + g8
    mu = jnp.max(a_t, axis=1, keepdims=True)
    w_t = jnp.exp(a_t - mu)
    r_t = g8 - b_t
    zpad = jnp.zeros((L - 8, L), F32)
    bcol = jnp.concatenate([b_t, zpad], axis=0).T
    wcol = jnp.concatenate([w_t, zpad], axis=0).T
    m_prev = m_ref[...][:, 0:1]
    m_new = jnp.maximum(b_last + m_prev, mu)
    sp = jnp.exp(b_last + m_prev - m_new)
    scn = jnp.exp(mu - m_new)
    m_ref[...] = jnp.broadcast_to(m_new, m_ref.shape)

    for h in range(ML_HEADS):
        bc = bcol[:, h:h + 1]
        log_d = jnp.where(causal, bc + r_t[h:h + 1, :], NEG)
        inter = bc + m_prev[h:h + 1, :]
        m_t = jnp.maximum(inter, jnp.max(log_d, axis=1, keepdims=True))
        dmat = jnp.exp(log_d - m_t)
        w_int = jnp.exp(inter - m_t)
        qh = qk[:, h * ML_DK:(h + 1) * ML_DK]
        kh = qk[:, ML_HEADS * ML_DK + h * ML_DK:ML_HEADS * ML_DK + (h + 1) * ML_DK] * (ML_DK ** -0.5)
        qb = qh.astype(BF16)
        v = pm_ref[:, _MV + h * ML_DV:_MV + (h + 1) * ML_DV]
        s = _dot_nt(qb, kh.astype(BF16)) * dmat
        c_prev = c_ref[h]
        n_prev = n_ref[h:h + 1, :]
        num = _dot(s.astype(BF16), v) + w_int * _dot(qb, c_prev.astype(BF16))
        den = jnp.sum(s, axis=1, keepdims=True) + w_int * jnp.sum(qh * n_prev, axis=1, keepdims=True)
        den = jnp.maximum(jnp.abs(den), jnp.exp(-m_t))
        hcell = num * (1.0 / den)
        o = pm_ref[:, _MO + h * ML_DV:_MO + (h + 1) * ML_DV].astype(F32)
        y = _sigmoid(o) * hcell
        z = pm_ref[:, _MZ + h * ML_DV:_MZ + (h + 1) * ML_DV].astype(F32)
        out = _head_norm(y, mn_ref[:, h * ML_DV:(h + 1) * ML_DV]) * _silu(z)
        mix_ref[:, RET_HEADS * RET_DV + h * ML_DV:RET_HEADS * RET_DV + (h + 1) * ML_DV] = out.astype(mix_ref.dtype)
        kw = kh * wcol[:, h:h + 1]
        kv = _dot(kw.T.astype(BF16), v)
        ksum = jnp.sum(kw, axis=0, keepdims=True)
        sp_h = sp[h:h + 1, :]
        sc_h = scn[h:h + 1, :]
        c_ref[h] = sp_h * c_prev + sc_h * kv
        n_ref[h:h + 1, :] = sp_h * n_prev + sc_h * ksum


def _ab_mixer(pm, pg, conv_w, conv_b, gate_b, ret_norm, ml_norm, bsz, seq):
    n = pm.shape[0]
    nc = seq // CHUNK
    nqk = 2 * ML_HEADS * ML_DK
    row = lambda b, c: (b * nc + c, 0)
    const = lambda b, c: (0, 0)
    nmix = RET_HEADS * RET_DV + ML_HEADS * ML_DV
    return pl.pallas_call(
        _ab_kernel,
        grid=(bsz, nc),
        in_specs=[pl.BlockSpec((CHUNK, AB_MAIN), row), pl.BlockSpec((CHUNK, LANES), row),
                  pl.BlockSpec((ML_CONV, nqk), const), pl.BlockSpec((1, nqk), const),
                  pl.BlockSpec((1, LANES), const),
                  pl.BlockSpec((1, RET_HEADS * RET_DV), const), pl.BlockSpec((1, ML_HEADS * ML_DV), const)],
        out_specs=pl.BlockSpec((CHUNK, nmix), row),
        out_shape=jax.ShapeDtypeStruct((n, nmix), BF16),
        scratch_shapes=[pltpu.VMEM((CHUNK + 8, nqk), F32),
                        pltpu.VMEM((RET_HEADS, RET_DK, RET_DV), F32),
                        pltpu.VMEM((ML_HEADS, ML_DK, ML_DV), F32),
                        pltpu.VMEM((8, ML_DK), F32),
                        pltpu.VMEM((8, LANES), F32)],
        compiler_params=_cparams(("parallel", "arbitrary")),
        name="ab_mixer",
    )(pm, pg, conv_w, conv_b, gate_b, ret_norm, ml_norm)


_QNZ0 = 0
_KVC0 = NSA_HEADS
_KVS0 = _KVC0 + NSA_GROUPS
_KVW0 = _KVS0 + NSA_GROUPS
_SU0 = (_KVW0 + NSA_GROUPS) * LANES
_SZ0 = _SU0 + S5_WIDTH
CD_MAIN = _SZ0 + S5_WIDTH
SLOTW = NSA_GROUPS * LANES
VT_ROWS = NSA_DH + 16
LOG2E = math.log2(math.e)


def _compress_kernel(x_ref, pk_ref, pv_ref, w1k_ref, w1v_ref, w2k_ref, w2vt_ref, ekc_ref,
                     ko_ref, vot_ref, xs_ref):
    seq = x_ref.shape[0]
    nb = seq // CMP_STRIDE
    for g in range(NSA_GROUPS):
        xs_ref[...] = x_ref[:, g * LANES:(g + 1) * LANES].astype(F32)
        acc = [jnp.zeros((nb, CMP_HIDDEN), F32) for _ in range(4)]
        for r in range(CMP_STRIDE):
            xr = xs_ref[pl.ds(r, nb, stride=CMP_STRIDE), :]
            for kind, (p_ref, w1_ref) in enumerate(((pk_ref, w1k_ref), (pv_ref, w1v_ref))):
                acc[2 * kind] = acc[2 * kind] + _dot((xr + p_ref[r:r + 1, :]).astype(BF16), w1_ref[r])
                acc[2 * kind + 1] = acc[2 * kind + 1] + _dot(
                    (xr + p_ref[CMP_STRIDE + r:CMP_STRIDE + r + 1, :]).astype(BF16), w1_ref[CMP_STRIDE + r])
        hk = _gelu(acc[0] + pltpu.roll(acc[1], nb - 1, axis=0)).astype(BF16)
        hv = _gelu(acc[2] + pltpu.roll(acc[3], nb - 1, axis=0)).astype(BF16)
        ko_ref[g] = (_dot(hk, w2k_ref[...]) + ekc_ref[...]).astype(ko_ref.dtype)
        vot_ref[g] = _dot_nt(w2vt_ref[...], hv).astype(vot_ref.dtype)


def _compress(pm3, pos_k, pos_v, w1k, w1v, w2k, w2vt, ekc):
    bsz, seq, _ = pm3.shape
    nb = seq // CMP_STRIDE
    c3 = lambda b: (0, 0, 0)
    c2 = lambda b: (0, 0)
    return pl.pallas_call(
        _compress_kernel,
        grid=(bsz,),
        in_specs=[pl.BlockSpec((None, seq, SLOTW), lambda b: (b, 0, _KVC0 * LANES // SLOTW)),
                  pl.BlockSpec((CMP_LEN, LANES), c2), pl.BlockSpec((CMP_LEN, LANES), c2),
                  pl.BlockSpec((CMP_LEN, LANES, CMP_HIDDEN), c3), pl.BlockSpec((CMP_LEN, LANES, CMP_HIDDEN), c3),
                  pl.BlockSpec((CMP_HIDDEN, LANES), c2), pl.BlockSpec((LANES, CMP_HIDDEN), c2),
                  pl.BlockSpec((nb, LANES), c2)],
        out_specs=[pl.BlockSpec((None, NSA_GROUPS, nb, LANES), lambda b: (b, 0, 0, 0)),
                   pl.BlockSpec((None, NSA_GROUPS, LANES, nb), lambda b: (b, 0, 0, 0))],
        out_shape=[jax.ShapeDtypeStruct((bsz, NSA_GROUPS, nb, LANES), BF16),
                   jax.ShapeDtypeStruct((bsz, NSA_GROUPS, LANES, nb), BF16)],
        scratch_shapes=[pltpu.VMEM((seq, LANES), F32)],
        compiler_params=_cparams(("parallel",)),
        name="nsa_compress",
    )(pm3, pos_k, pos_v, w1k, w1v, w2k, w2vt, ekc)


def _nsa_kernel(q_ref, kc_ref, vct_ref, kvs_ref, kvw_ref, pg_ref,
                ek_ref, al_ref, ovt_ref, y_ref,
                ksa_ref, vsat_ref, kwa_ref, vwat_ref, qa_scr, part_scr, gzs_scr, sel_smem):
    seq = kvs_ref.shape[0]
    g_idx = pl.program_id(1)
    qi = pl.program_id(2)
    wtiles = WINDOW // LANES

    @pl.when(qi == 0)
    def _():
        ek = ek_ref[...]
        lane_k = lax.broadcasted_iota(jnp.int32, (seq, LANES), 1)
        zero = jnp.zeros((seq, LANES), BF16)
        ksa_ref[0:seq, :] = jnp.where(lane_k < NSA_DH, kvs_ref[...], zero) + ek
        lane_d = lax.broadcasted_iota(jnp.int32, (LANES, LANES), 1)
        ksa_ref[seq:seq + LANES, :] = jnp.where((lane_d >= SEL_LANE0) & (lane_d < SEL_LANE0 + seq // SEL_BLOCK),
                                                1.0, 0.0).astype(BF16)
        vsat_ref[seq // LANES] = jnp.zeros((VT_ROWS, LANES), BF16)
        ek_w = jnp.where(lane_k >= ALI_LANE0, ek, zero)
        kwa_ref[0:WINDOW, :] = jnp.zeros((WINDOW, LANES), BF16)
        kwa_ref[WINDOW:WINDOW + seq, :] = jnp.where(lane_k < NSA_DH, kvw_ref[...], zero) + ek_w
        ones_rows = jnp.where(lax.broadcasted_iota(jnp.int32, (VT_ROWS - NSA_DH, LANES), 0) == 0, 1.0, 0.0)
        for i in range(wtiles):
            vwat_ref[i] = jnp.zeros((VT_ROWS, LANES), BF16)
        for i in range(seq // LANES):
            rws = slice(i * LANES, (i + 1) * LANES)
            for src_ref, dst_ref, j in ((kvs_ref, vsat_ref, i), (kvw_ref, vwat_ref, wtiles + i)):
                v_t = src_ref[rws, :].astype(F32).T[NSA_DH:2 * NSA_DH, :]
                dst_ref[j] = jnp.concatenate([v_t, ones_rows], axis=0).astype(BF16)

    nq = seq // TQ
    refs = (q_ref, kc_ref, vct_ref, pg_ref, al_ref, ovt_ref, y_ref, ksa_ref, vsat_ref, kwa_ref, vwat_ref,
            qa_scr, part_scr, gzs_scr, sel_smem)
    for step in range(nq + 1):
        pl.when(qi == step)(functools.partial(_nsa_step, step, nq, seq // SEL_BLOCK, g_idx, refs))


SEL_DYN = 2
SEL_CNT = 7


def _sel_compactable(qi):
    return (qi + 1) * TQ // LANES > 1 + SEL_DYN + TQ // LANES


def _nsa_step(step, nq, nsel, g_idx, refs):
    def run(compact):
        stages = []
        if step < nq:
            stages.append(_nsa_front(nsel, step, g_idx, refs))
        if step > 0:
            stages.append(_nsa_back(step - 1, compact, refs))
        while stages:
            for st in list(stages):
                if next(st, "done") == "done":
                    stages.remove(st)

    if step > 0 and _sel_compactable(step - 1):
        fits = refs[-1][(step - 1) & 1, SEL_CNT] <= SEL_DYN
        pl.when(fits)(functools.partial(run, True))
        pl.when(jnp.logical_not(fits))(functools.partial(run, False))
    else:
        run(False)


def _tile_iotas():
    rows = NSA_HPG * TQ
    jrow = lax.broadcasted_iota(jnp.int32, (LANES, rows), 0)
    tq = lax.broadcasted_iota(jnp.int32, (LANES, rows), 1) & (TQ - 1)
    return jrow, tq, jrow - tq


def _nsa_front(nsel, qi, g_idx, refs):
    (q_ref, kc_ref, vct_ref, pg_ref, al_ref, ovt_ref, _, _, _, kwa_ref, vwat_ref,
     qa_scr, part_scr, gzs_scr, sel_smem) = refs
    hp = NSA_HPG
    span = WINDOW + TQ
    t0 = qi * TQ
    tw = t0
    early = t0 < WINDOW
    par = qi & 1
    jrow, tq, kmt = _tile_iotas()

    qnz = [q_ref[:, hh * LANES:(hh + 1) * LANES].astype(F32) for hh in range(hp)]
    lane_q = lax.broadcasted_iota(jnp.int32, (TQ, LANES), 1)
    qal = jnp.concatenate([jnp.where(lane_q < NSA_DH, qnz[hh], al_ref[pl.ds(g_idx * hp + hh, 1), :])
                           for hh in range(hp)], axis=0)
    qalb = qal.astype(BF16)
    s_c = _dot_nt(kc_ref[...], qalb)
    s_w = _dot_nt(kwa_ref[pl.ds(tw, span), :], qalb)
    yield

    s = jnp.where(tq - jrow * CMP_STRIDE >= (CMP_LEN - 1) - t0, s_c, NEG)
    e = jnp.exp2(s - jnp.max(s, axis=0, keepdims=True))
    inv = 1.0 / jnp.sum(e, axis=0, keepdims=True)
    if early:
        inv = jnp.where(t0 + tq[0:1, :] >= CMP_LEN - 1, inv, 0.0)
    p = e * inv
    oc_t = _dot(vct_ref[...], p.astype(BF16))
    psum_t = p[:, 0:TQ]
    for hh in range(1, hp):
        psum_t = psum_t + p[:, hh * TQ:(hh + 1) * TQ]
    p_hi = psum_t.astype(BF16)
    p_lo = (psum_t - p_hi.astype(F32)).astype(BF16)
    imp_t = _dot(ovt_ref[...], p_hi) + _dot(ovt_ref[...], p_lo)
    yield

    blocks = [s_w[i * LANES:(i + 1) * LANES, :] for i in range(span // LANES)]
    for i in range(span // LANES):
        if i * LANES < TQ:
            blocks[i] = jnp.where(kmt > -i * LANES, blocks[i], NEG)
        if (i + 1) * LANES > WINDOW:
            blocks[i] = jnp.where(kmt <= WINDOW - i * LANES, blocks[i], NEG)
        if early:
            blocks[i] = jnp.where(jrow >= WINDOW - t0 - i * LANES, blocks[i], NEG)
    s = jnp.concatenate(blocks, axis=0)
    p = jnp.exp2(s - jnp.max(s, axis=0, keepdims=True)).astype(BF16)
    vw_t = jnp.concatenate([vwat_ref[qi * (TQ // LANES) + i] for i in range(span // LANES)], axis=1)
    accw_t = _dot(vw_t, p)
    yield

    score = imp_t[SEL_LANE0:SEL_LANE0 + nsel, :]
    nrow = lax.broadcasted_iota(jnp.int32, (nsel, TQ), 0)
    own = (t0 + lax.broadcasted_iota(jnp.int32, (nsel, TQ), 1)) >> int(math.log2(SEL_BLOCK))
    valid = nrow <= own
    forced = (nrow == 0) | (nrow == own)
    score = jnp.where(forced, BIG, jnp.where(valid, score, -1.0))
    chosen = jnp.zeros((nsel, TQ), F32)
    nrow_f = nrow.astype(F32)
    for _ in range(SEL_TOPN):
        mx = jnp.max(score, axis=0, keepdims=True)
        first = jnp.min(jnp.where(score == mx, nrow_f, 4.0 * LANES), axis=0, keepdims=True)
        hit = nrow_f == first
        chosen = jnp.where(hit, 1.0, chosen)
        score = jnp.where(hit, -3.0, score)
    mask_t = jnp.where(chosen > 0.0, jnp.where(valid, 0.0, NEG), NEG)
    if _sel_compactable(qi):
        used_blk = jnp.max(jnp.where(mask_t == 0.0, 1.0, 0.0), axis=1, keepdims=True)
        blk_per_tile = LANES // SEL_BLOCK
        for slot in range(SEL_DYN):
            sel_smem[par, slot] = jnp.int32(nsel // blk_per_tile)
        cnt = jnp.int32(0)
        for i in range(1, t0 // LANES):
            used = jnp.max(used_blk[i * blk_per_tile:(i + 1) * blk_per_tile, :]) > 0.0

            @pl.when(used & (cnt < SEL_DYN))
            def _(i=i, cnt=cnt):
                sel_smem[par, cnt] = jnp.int32(i)

            cnt = cnt + used.astype(jnp.int32)
        sel_smem[par, SEL_CNT] = cnt
    maskcols = jnp.concatenate([jnp.zeros((SEL_LANE0, TQ), F32), mask_t,
                                jnp.zeros((LANES - SEL_LANE0 - nsel, TQ), F32)], axis=0).T

    qa_scr[par] = (qal + jnp.concatenate([maskcols] * hp, axis=0)).astype(BF16)

    gt_t = _sigmoid(pg_ref[...]).T
    for hh in range(hp):
        g_c, g_s, g_w = [gt_t[j * hp + hh:j * hp + hh + 1, :] for j in range(3)]
        hcols = slice(hh * TQ, (hh + 1) * TQ)
        aw_t = accw_t[:, hcols]
        gz = _silu(qnz[hh].T[NSA_DH:2 * NSA_DH, :])
        o = g_c * oc_t[0:NSA_DH, hcols] + (g_w * (1.0 / aw_t[NSA_DH:NSA_DH + 1, :])) * aw_t[0:NSA_DH, :]
        part_scr[par, hh * NSA_DH:(hh + 1) * NSA_DH, :] = o * gz
        gzs_scr[par, hh * NSA_DH:(hh + 1) * NSA_DH, :] = g_s * gz


def _nsa_back(qi, compact, refs):
    _, _, _, _, _, _, y_ref, ksa_ref, vsat_ref, _, _, qa_scr, part_scr, gzs_scr, sel_smem = refs
    hp = NSA_HPG
    t0 = qi * TQ
    par = qi & 1
    _, _, kmt = _tile_iotas()
    qa = qa_scr[par]
    ntiles = (t0 + TQ) // LANES
    if compact:
        tiles = [0] + [sel_smem[par, j] for j in range(SEL_DYN)] + list(range(t0 // LANES, ntiles))
    else:
        tiles = list(range(ntiles))
    per = KCH // LANES
    chunks = [tiles[i:i + per] for i in range(0, len(tiles), per)]
    nk = len(chunks)

    def k_tile(i):
        if isinstance(i, int):
            return ksa_ref[i * LANES:(i + 1) * LANES, :]
        return ksa_ref[pl.ds(pl.multiple_of(i * LANES, LANES), LANES), :]

    def scores(c):
        s = _dot_nt(jnp.concatenate([k_tile(i) for i in chunks[c]], axis=0), qa)
        if any(isinstance(i, int) and (i + 1) * LANES > t0 for i in chunks[c]):
            s = jnp.concatenate(
                [jnp.where(kmt <= t0 - i * LANES, s[n * LANES:(n + 1) * LANES, :], NEG)
                 if isinstance(i, int) and (i + 1) * LANES > t0 else s[n * LANES:(n + 1) * LANES, :]
                 for n, i in enumerate(chunks[c])], axis=0)
        return s

    def weighted(c, s):
        mc = jnp.max(s, axis=0, keepdims=True)
        v_t = jnp.concatenate([vsat_ref[i] for i in chunks[c]], axis=1)
        return mc, _dot(v_t, jnp.exp2(s - mc).astype(BF16))

    parts = []
    s_next = scores(0)
    for c in range(nk):
        s_cur = s_next
        if c + 1 < nk:
            s_next = scores(c + 1)
        yield
        parts.append(weighted(c, s_cur))
    yield
    m = parts[0][0]
    for mc, _ in parts[1:]:
        m = jnp.maximum(m, mc)
    accs_t = None
    for mc, d in parts:
        d = d if nk == 1 else jnp.exp2(mc - m) * d
        accs_t = d if accs_t is None else accs_t + d

    outs = []
    for hh in range(hp):
        as_t = accs_t[:, hh * TQ:(hh + 1) * TQ]
        rws = slice(hh * NSA_DH, (hh + 1) * NSA_DH)
        outs.append(part_scr[par, rws, :] + (gzs_scr[par, rws, :] * (1.0 / as_t[NSA_DH:NSA_DH + 1, :])) * as_t[0:NSA_DH, :])
    for pr in range(hp // 2):
        packed = jnp.concatenate([outs[2 * pr], outs[2 * pr + 1]], axis=0).T
        y_ref[:, pr * LANES:(pr + 1) * LANES] = packed.astype(y_ref.dtype)


def _nsa(pm3, pg3, kcc, vct, ek, al, ovt):
    bsz, seq, _ = pm3.shape
    nq = seq // TQ
    gw = NSA_HPG * NSA_DH
    kv = lambda tile0: pl.BlockSpec((None, seq, LANES), lambda b, g, q: (b, 0, tile0 + g))
    c2 = lambda b, g, q: (0, 0)
    front = lambda b, g, q: (b, jnp.minimum(q, nq - 1), g)
    back = lambda b, g, q: (b, jnp.maximum(q - 1, 0), g)
    return pl.pallas_call(
        _nsa_kernel,
        grid=(bsz, NSA_GROUPS, nq + 1),
        in_specs=[pl.BlockSpec((None, TQ, NSA_HPG * LANES), front),
                  pl.BlockSpec((None, None, LANES, LANES), lambda b, g, q: (b, g, 0, 0)),
                  pl.BlockSpec((None, None, LANES, LANES), lambda b, g, q: (b, g, 0, 0)),
                  kv(_KVS0), kv(_KVW0),
                  pl.BlockSpec((None, TQ, LANES), front),
                  pl.BlockSpec((seq, LANES), c2),
                  pl.BlockSpec((NSA_HEADS, LANES), c2),
                  pl.BlockSpec((LANES, LANES), c2)],
        out_specs=pl.BlockSpec((None, TQ, gw), back),
        out_shape=jax.ShapeDtypeStruct((bsz, seq, NSA_HEADS * NSA_DH), BF16),
        scratch_shapes=[pltpu.VMEM((seq + LANES, LANES), BF16),
                        pltpu.VMEM((seq // LANES + 1, VT_ROWS, LANES), BF16),
                        pltpu.VMEM((seq + WINDOW, LANES), BF16),
                        pltpu.VMEM(((seq + WINDOW) // LANES, VT_ROWS, LANES), BF16),
                        pltpu.VMEM((2, NSA_HPG * TQ, LANES), BF16),
                        pltpu.VMEM((2, gw, TQ), F32), pltpu.VMEM((2, gw, TQ), F32),
                        pltpu.SMEM((2, SEL_CNT + 1), jnp.int32)],
        compiler_params=_cparams(("parallel", "arbitrary", "arbitrary")),
        name="nsa_attn",
    )(pm3, kcc, vct, pm3, pm3, pg3, ek, al, ovt)


def _nsa_tables(seq):
    t = np.arange(seq)
    ek = np.zeros((seq, LANES), np.float32)
    ek[t, SEL_LANE0 + t // SEL_BLOCK] = 1.0
    for c in range(3):
        ek[:, ALI_LANE0 + c] = t // SEL_BLOCK
        ek[:, ALI_LANE0 + 3 + c] = t % SEL_BLOCK
    slopes = jnp.exp2(-8.0 * jnp.arange(1, NSA_HEADS + 1, dtype=F32) / NSA_HEADS) * LOG2E
    s_hi, s_mid, s_lo = _split3(slopes)
    parts = [p.astype(F32) for p in (s_hi, s_mid, s_lo)]
    al = jnp.pad(jnp.stack([p * SEL_BLOCK for p in parts] + parts, axis=1),
                 ((0, 0), (ALI_LANE0, LANES - ALI_LANE0 - 2 * len(parts))))
    nb = seq // CMP_STRIDE
    cstart = np.arange(nb) * CMP_STRIDE
    cend = cstart + CMP_LEN - 1
    ekc = np.zeros((nb, LANES), np.float32)
    for c in range(3):
        ekc[:, ALI_LANE0 + c] = cend // SEL_BLOCK
        ekc[:, ALI_LANE0 + 3 + c] = cend % SEL_BLOCK
    sel = np.arange(seq // SEL_BLOCK)
    ovt = np.zeros((LANES, max(nb, LANES)), np.float32)
    ovl = (cstart[:, None] < (sel[None, :] + 1) * SEL_BLOCK) & (cstart[:, None] + CMP_LEN > sel[None, :] * SEL_BLOCK)
    ovl[nb - 1, :] = False
    ovt[SEL_LANE0:SEL_LANE0 + len(sel), :nb] = ovl.T
    return jnp.asarray(ek, BF16), al, jnp.asarray(ekc), jnp.asarray(ovt, BF16)


def _s5_disc_kernel(are_ref, aim_ref, ldt_ref, bre_ref, bim_ref, abr_ref, abi_ref, bbr_ref, bbi_ref):
    a_re = are_ref[...]
    a_im = aim_ref[...]
    dt = jnp.exp(ldt_ref[...])
    er = jnp.exp(a_re * dt)
    abar_re = er * jnp.cos(a_im * dt)
    abar_im = er * jnp.sin(a_im * dt)
    lam2 = a_re * a_re + a_im * a_im
    cr = ((abar_re - 1.0) * a_re + abar_im * a_im) / lam2
    ci = (abar_im * a_re - (abar_re - 1.0) * a_im) / lam2
    b_re = bre_ref[...]
    b_im = bim_ref[...]
    abr_ref[...] = abar_re
    abi_ref[...] = abar_im
    bbr_ref[...] = cr * b_re - ci * b_im
    bbi_ref[...] = cr * b_im + ci * b_re


def _s5_disc(a_re_rep, a_im_rep, ldt_rep, b_re_t, b_im_t):
    shp = jax.ShapeDtypeStruct(a_re_rep.shape, F32)
    return pl.pallas_call(_s5_disc_kernel, out_shape=[shp] * 4, name="s5_disc")(
        a_re_rep, a_im_rep, ldt_rep, b_re_t, b_im_t)


S5_SLABG = 4
S5_TT = 64
S5_PITCH = S5_TT + 4
S5_DIAG = 256
S5_NSTATE_PER_DIAG = S5_DIAG // S5_GROUP_CH * S5_STATE


def _s5_kernel(u_ref, bre_ref, bim_ref, ar_ref, ai_ref, cre_ref, cim_ref, d_ref, wg_ref, o_ref,
               xr_ref, xi_ref, sr_ref, si_ref):
    nb, tt, wd = u_ref.shape
    pitch = sr_ref.shape[1] // nb
    gcols = S5_SLABG * LANES
    ngroups = S5_NSTATE // gcols

    @pl.when(pl.program_id(0) == 0)
    def _():
        xr_ref[...] = jnp.zeros(xr_ref.shape, F32)
        xi_ref[...] = jnp.zeros(xi_ref.shape, F32)

    u = u_ref[...].reshape(nb * tt, wd)

    chan = lambda kg: slice(kg * gcols // S5_NSTATE_PER_DIAG * S5_DIAG,
                            (kg * gcols // S5_NSTATE_PER_DIAG + 1) * S5_DIAG)
    for kg in range(ngroups):
        cols = slice(kg * gcols, (kg + 1) * gcols)
        br = _dot(u[:, chan(kg)], bre_ref[chan(kg), cols])
        bi = _dot(u[:, chan(kg)], bim_ref[chan(kg), cols])
        for j in range(S5_SLABG):
            for b in range(nb):
                sr_ref[kg * S5_SLABG + j, b * pitch:b * pitch + tt, :] = br[b * tt:(b + 1) * tt, j * LANES:(j + 1) * LANES]
                si_ref[kg * S5_SLABG + j, b * pitch:b * pitch + tt, :] = bi[b * tt:(b + 1) * tt, j * LANES:(j + 1) * LANES]

    for kg in range(ngroups):
        cols = slice(kg * gcols, (kg + 1) * gcols)
        slabs = range(kg * S5_SLABG, (kg + 1) * S5_SLABG)
        ar = ar_ref[:, cols]
        ai = ai_ref[:, cols]

        def body(t, carry, slabs=slabs, ar=ar, ai=ai):
            xr, xi = carry
            rows = pl.ds(t, nb, stride=pitch)
            nxr = ar * xr - ai * xi + jnp.concatenate([sr_ref[k, rows, :] for k in slabs], axis=1)
            nxi = ar * xi + ai * xr + jnp.concatenate([si_ref[k, rows, :] for k in slabs], axis=1)
            for j, k in enumerate(slabs):
                sr_ref[k, rows, :] = nxr[:, j * LANES:(j + 1) * LANES]
                si_ref[k, rows, :] = nxi[:, j * LANES:(j + 1) * LANES]
            return nxr, nxi

        xr, xi = lax.fori_loop(0, tt, body, (xr_ref[:, cols], xi_ref[:, cols]), unroll=4)
        xr_ref[:, cols] = xr
        xi_ref[:, cols] = xi

    y_blocks = [jnp.zeros((nb * tt, S5_DIAG), F32) for _ in range(wd // S5_DIAG)]
    for kg in range(ngroups):
        cols = slice(kg * gcols, (kg + 1) * gcols)
        gather = lambda ref: jnp.concatenate(
            [jnp.concatenate([ref[kg * S5_SLABG + j, b * pitch:b * pitch + tt, :] for j in range(S5_SLABG)], axis=1)
             for b in range(nb)], axis=0).astype(BF16)
        blk = chan(kg).start // S5_DIAG
        y_blocks[blk] = (y_blocks[blk] + _dot(gather(sr_ref), cre_ref[cols, chan(kg)])
                         - _dot(gather(si_ref), cim_ref[cols, chan(kg)]))
    y = jnp.concatenate(y_blocks, axis=1)
    y = _gelu(y + d_ref[...] * u.astype(F32))
    z = _dot(y.astype(BF16), wg_ref[...])
    o = z[:, :S5_WIDTH] * _sigmoid(z[:, S5_WIDTH:])
    o_ref[...] = o.reshape(nb, tt, wd).astype(o_ref.dtype)


def _s5(pm3, bbd_re, bbd_im, ar8, ai8, cbd_re, cbd_im, d_row, w_glu):
    bsz, seq, _ = pm3.shape
    tt = S5_TT
    c2 = lambda i: (0, 0)
    nslab = S5_NSTATE // LANES
    return pl.pallas_call(
        _s5_kernel,
        grid=(seq // tt,),
        in_specs=[pl.BlockSpec((bsz, tt, S5_WIDTH), lambda i: (0, i, _SU0 // S5_WIDTH)),
                  pl.BlockSpec((S5_WIDTH, S5_NSTATE), c2), pl.BlockSpec((S5_WIDTH, S5_NSTATE), c2),
                  pl.BlockSpec((8, S5_NSTATE), c2), pl.BlockSpec((8, S5_NSTATE), c2),
                  pl.BlockSpec((S5_NSTATE, S5_WIDTH), c2), pl.BlockSpec((S5_NSTATE, S5_WIDTH), c2),
                  pl.BlockSpec((1, S5_WIDTH), c2), pl.BlockSpec((S5_WIDTH, 2 * S5_WIDTH), c2)],
        out_specs=pl.BlockSpec((bsz, tt, S5_WIDTH), lambda i: (0, i, 0)),
        out_shape=jax.ShapeDtypeStruct((bsz, seq, S5_WIDTH), BF16),
        scratch_shapes=[pltpu.VMEM((8, S5_NSTATE), F32), pltpu.VMEM((8, S5_NSTATE), F32),
                        pltpu.VMEM((nslab, bsz * S5_PITCH, LANES), F32),
                        pltpu.VMEM((nslab, bsz * S5_PITCH, LANES), F32)],
        compiler_params=_cparams(("arbitrary",)),
        name="s5_scan",
    )(pm3, bbd_re, bbd_im, ar8, ai8, cbd_re, cbd_im, d_row, w_glu)


def _pad_cols(w, n):
    return jnp.pad(w, ((0, 0), (0, n - w.shape[1])))


def _relayout_kernel(groups, w_ref, o_ref):
    o0 = 0
    for group in groups:
        vals = [w_ref[:, s0:s0 + wd] if sc is None else w_ref[:, s0:s0 + wd] * sc for s0, wd, sc in group]
        v = vals[0] if len(vals) == 1 else jnp.concatenate(vals, axis=1)
        o_ref[:, o0:o0 + v.shape[1]] = v.astype(o_ref.dtype)
        o0 += v.shape[1]


def _relayout_cols(w, groups, tm=256):
    d, nin = w.shape
    nout = sum(wd for group in groups for _, wd, _ in group)
    return pl.pallas_call(
        functools.partial(_relayout_kernel, groups),
        grid=(d // tm,),
        in_specs=[pl.BlockSpec((tm, nin), lambda i: (i, 0))],
        out_specs=pl.BlockSpec((tm, nout), lambda i: (i, 0)),
        out_shape=jax.ShapeDtypeStruct((d, nout), BF16),
        compiler_params=_cparams(("parallel",)),
        name="relayout_cols",
    )(w)


def _cd_proj_weights(w_in):
    d = w_in.shape[0]
    nw, nkv = NSA_HEADS * NSA_DH, NSA_GROUPS * NSA_DH
    offs = np.cumsum((0, nw, nkv, nkv, nkv, nkv, nkv, nkv, 3 * NSA_HEADS, nw, S5_WIDTH, S5_WIDTH))
    seg = lambda i: w_in[:, offs[i]:offs[i + 1]]
    def pair(ia, ib, scale=None):
        return [[(int(offs[ia]) + c, NSA_DH, scale), (int(offs[ib]) + c, NSA_DH, None)]
                for c in range(0, int(offs[ia + 1] - offs[ia]), NSA_DH)]
    w_main = _relayout_cols(w_in, pair(0, 8, NSA_DH ** -0.5 * LOG2E) + pair(1, 2) + pair(3, 4) + pair(5, 6)
                            + [[(int(offs[9]), S5_WIDTH, None)], [(int(offs[10]), S5_WIDTH, None)]])
    w_gate = seg(7).reshape(d, NSA_GROUPS, NSA_HPG, 3).transpose(0, 1, 3, 2).reshape(d, NSA_GROUPS, 3 * NSA_HPG)
    w_gate = jnp.pad(w_gate, ((0, 0), (0, 0), (0, LANES - 3 * NSA_HPG))).reshape(d, NSA_GROUPS * LANES).astype(BF16)
    return w_gate, w_main


def _layer0(h2, p4, layer, bsz, seq, norm_pre, norm_post, pe_proj, pe_gate,
            w_in, ret_norm, conv_w, conv_b, gate_b, ml_norm, w_out):
    d = h2.shape[1]
    g0 = 5120
    after = g0 + 2 * ML_HEADS
    w_main = _relayout_cols(w_in, [[(c, PROJ_TN, None)] for c in range(0, g0, PROJ_TN)]
                            + [[(c, PROJ_TN, None)] for c in range(after, w_in.shape[1], PROJ_TN)])
    w_gate = _pad_cols(w_in[:, g0:g0 + 2 * ML_HEADS], LANES).astype(BF16)
    pm, pg = _norm_proj(h2, norm_pre.reshape(1, d), w_gate, w_main)
    mix = _ab_mixer(pm, pg, conv_w, conv_b.reshape(1, -1), _pad_cols(gate_b.reshape(1, -1), LANES),
                    ret_norm.reshape(1, -1), ml_norm.reshape(1, -1), bsz, seq)
    ka = RET_HEADS * RET_DV
    nt = seq // 512
    return _out_pe(mix, lambda b, s: (b * nt + s, 0), ka, mix, lambda b, s: (b * nt + s, 1), ML_HEADS * ML_DV,
                   None, None, h2, p4, layer, w_out[:ka].astype(BF16), w_out[ka:].astype(BF16),
                   norm_post.reshape(1, d), pe_gate.astype(BF16), pe_proj.astype(BF16), bsz, seq)


def _layer1(h2, p4, layer, bsz, seq, norm_pre, norm_post, pe_proj, pe_gate,
            w_in, pos_k, pos_v, w1_k, w2_k, w1_v, w2_v,
            a_re, a_im, log_dt, b_re, b_im, c_re, c_im, d_skip, w_glu, w_out):
    d = h2.shape[1]
    nw = NSA_HEADS * NSA_DH
    pm, pg = _norm_proj(h2, norm_pre.reshape(1, d), *_cd_proj_weights(w_in))

    nb = seq // CMP_STRIDE
    pm3 = pm.reshape(bsz, seq, CD_MAIN)
    padl = lambda a, lo=True: jnp.pad(a, ((0, 0),) * (a.ndim - 1)
                                      + ((0, LANES - a.shape[-1]) if lo else (LANES - a.shape[-1], 0),))
    w1 = lambda w, lo: jnp.pad(w.reshape(CMP_LEN, NSA_DH, CMP_HIDDEN),
                               ((0, 0), (0, LANES - NSA_DH) if lo else (LANES - NSA_DH, 0), (0, 0))).astype(BF16)
    ek, al, ekc, ovt = _nsa_tables(seq)
    kcc, vct = _compress(pm3, padl(pos_k), padl(pos_v, False), w1(w1_k, True), w1(w1_v, False),
                         padl(w2_k).astype(BF16), padl(w2_v).T.astype(BF16), ekc)
    if nb < LANES:
        kcc = jnp.pad(kcc, ((0, 0), (0, 0), (0, LANES - nb), (0, 0)))
        vct = jnp.pad(vct, ((0, 0), (0, 0), (0, 0), (0, LANES - nb)))
    y_nsa = _nsa(pm3, pg.reshape(bsz, seq, NSA_GROUPS * LANES), kcc, vct, ek, al, ovt)

    rep = lambda a: jnp.repeat(a, S5_GROUP_CH, axis=0)
    ldt = jnp.broadcast_to(rep(log_dt[:, None]), (S5_WIDTH, S5_STATE))
    bt = lambda b: b.transpose(0, 2, 1).reshape(S5_WIDTH, S5_STATE)
    abr, abi, bbr, bbi = _s5_disc(rep(a_re), rep(a_im), ldt, bt(b_re), bt(b_im))
    gi = np.arange(S5_WIDTH) // S5_GROUP_CH
    gs = np.arange(S5_NSTATE) // S5_STATE
    diag_in = jnp.asarray(gi[:, None] == gs[None, :])
    bd_in = lambda bb: jnp.where(diag_in, jnp.tile(bb, (1, S5_GROUPS)), 0.0).astype(BF16)
    bd_out = lambda c: jnp.where(diag_in.T, jnp.tile(c.transpose(0, 2, 1).reshape(S5_NSTATE, S5_GROUP_CH),
                                                     (1, S5_GROUPS)), 0.0).astype(BF16)
    row8 = lambda a: jnp.broadcast_to(a[::S5_GROUP_CH].reshape(1, S5_NSTATE), (8, S5_NSTATE))
    y_s5 = _s5(pm3, bd_in(bbr), bd_in(bbi), row8(abr), row8(abi),
               bd_out(c_re), bd_out(c_im), d_skip.reshape(1, -1), w_glu.astype(BF16))

    nt = seq // 512
    szb = _SZ0 // S5_WIDTH
    return _out_pe(y_nsa.reshape(bsz * seq, nw), lambda b, s: (b * nt + s, 0), nw,
                   y_s5.reshape(bsz * seq, S5_WIDTH), lambda b, s: (b * nt + s, 0), S5_WIDTH,
                   pm, lambda b, s: (b * nt + s, szb),
                   h2, p4, layer, w_out[:nw].astype(BF16), w_out[nw:].astype(BF16),
                   norm_post.reshape(1, d), pe_gate.astype(BF16), pe_proj.astype(BF16), bsz, seq)


def kernel(x, p, norm_pre, norm_post, pe_proj, pe_gate, ab_w_in, ret_norm, ml_conv_w, ml_conv_b, ml_gate_b,
           ml_norm, ab_w_out, cd_w_in, cmp_pos_k, cmp_pos_v, cmp_w1_k, cmp_w2_k, cmp_w1_v, cmp_w2_v,
           s5_a_re, s5_a_im, s5_log_dt, s5_b_re, s5_b_im, s5_c_re, s5_c_im, s5_d, s5_w_glu, cd_w_out):
    bsz, seq, d = x.shape
    assert bsz == 8, "the S5 scan maps the batch onto the 8 sublanes of a vreg"
    assert seq % max(KCH, 1024) == 0
    depth = p.shape[0]
    h2 = x.reshape(bsz * seq, d)
    for i in range(depth):
        j = i // 2
        if i % 2 == 0:
            h2 = _layer0(h2, p, i, bsz, seq, norm_pre[i], norm_post[i], pe_proj[i], pe_gate[i],
                         ab_w_in[j], ret_norm[j], ml_conv_w[j], ml_conv_b[j], ml_gate_b[j], ml_norm[j], ab_w_out[j])
        else:
            h2 = _layer1(h2, p, i, bsz, seq, norm_pre[i], norm_post[i], pe_proj[i], pe_gate[i],
                         cd_w_in[j], cmp_pos_k[j], cmp_pos_v[j], cmp_w1_k[j], cmp_w2_k[j], cmp_w1_v[j], cmp_w2_v[j],
                         s5_a_re[j], s5_a_im[j], s5_log_dt[j], s5_b_re[j], s5_b_im[j], s5_c_re[j], s5_c_im[j],
                         s5_d[j], s5_w_glu[j], cd_w_out[j])
    return h2.reshape(bsz, seq, d)
```

```python
import functools
import math

import numpy as np
import jax
import jax.numpy as jnp
from jax import lax
from jax.experimental import pallas as pl
from jax.experimental.pallas import tpu as pltpu

F32 = jnp.float32
BF16 = jnp.bfloat16
EPS = 1e-6
NEG = -1e30
BIG = 1e30

LANES = 128
CHUNK = 128
RET_HEADS, RET_DK, RET_DV = 4, 128, 256
ML_HEADS, ML_DK, ML_DV = 4, 128, 256
ML_CONV = 4
NSA_HEADS, NSA_GROUPS, NSA_DH = 16, 4, 64
NSA_HPG = NSA_HEADS // NSA_GROUPS
CMP_STRIDE, CMP_LEN, CMP_HIDDEN = 16, 32, 128
SEL_BLOCK, SEL_TOPN, WINDOW = 64, 4, 512
S5_GROUPS, S5_GROUP_CH, S5_STATE = 32, 16, 64
S5_WIDTH = S5_GROUPS * S5_GROUP_CH
S5_NSTATE = S5_GROUPS * S5_STATE

TQ = 256
KCH = 512
SEL_LANE0 = 64
ALI_LANE0 = 96
VMEM_LIMIT = 56 * 1024 * 1024


def _cparams(sem):
    return pltpu.CompilerParams(dimension_semantics=sem, vmem_limit_bytes=VMEM_LIMIT)


def _dot(a, b):
    return jnp.dot(a, b, preferred_element_type=F32)


def _dot_nt(a, b):
    return lax.dot_general(a, b, (((1,), (1,)), ((), ())), preferred_element_type=F32)


def _split3(x):
    hi = x.astype(BF16)
    r1 = x - hi.astype(F32)
    mid = r1.astype(BF16)
    lo = (r1 - mid.astype(F32)).astype(BF16)
    return hi, mid, lo


def _dot_f32_lhs(x, m):
    hi, mid, lo = _split3(x)
    return _dot(hi, m) + _dot(mid, m) + _dot(lo, m)


def _sigmoid(x):
    return 1.0 / (1.0 + jnp.exp(-x))


def _silu(x):
    return x * _sigmoid(x)


def _gelu(x):
    return 0.5 * x * (1.0 + jnp.tanh(math.sqrt(2.0 / math.pi) * (x + 0.044715 * (x * x * x))))


def _rms(x, g):
    return x * lax.rsqrt(jnp.mean(x * x, -1, keepdims=True) + EPS) * g


PROJ_TN = 1024


def _norm_proj_kernel(h_ref, g_ref, wg_ref, wm_ref, om_ref, og_ref):
    ub = _rms(h_ref[...], g_ref[...]).astype(BF16)
    nm = om_ref.shape[1]
    for c0 in range(0, nm, PROJ_TN):
        c1 = min(c0 + PROJ_TN, nm)
        om_ref[:, c0:c1] = _dot_nt(ub, wm_ref[c0:c1, :]).astype(om_ref.dtype)
    og_ref[...] = _dot_nt(ub, wg_ref[...])


def _norm_proj(h2, g, w_gate_t, w_main_t, tm=512):
    n, d = h2.shape
    nm = w_main_t.shape[0]
    ng = w_gate_t.shape[0]
    once = pl.Buffered(1)
    return pl.pallas_call(
        _norm_proj_kernel,
        grid=(n // tm,),
        in_specs=[pl.BlockSpec((tm, d), lambda i: (i, 0)),
                  pl.BlockSpec((1, d), lambda i: (0, 0)),
                  pl.BlockSpec((ng, d), lambda i: (0, 0), pipeline_mode=once),
                  pl.BlockSpec((nm, d), lambda i: (0, 0), pipeline_mode=once)],
        out_specs=[pl.BlockSpec((tm, nm), lambda i: (i, 0)),
                   pl.BlockSpec((tm, ng), lambda i: (i, 0))],
        out_shape=[jax.ShapeDtypeStruct((n, nm), BF16), jax.ShapeDtypeStruct((n, ng), F32)],
        compiler_params=_cparams(("parallel",)),
        name="norm_proj",
    )(h2, g, w_gate_t, w_main_t)


def _out_pe_kernel(gated, *refs):
    if gated:
        ma_ref, mb_ref, sz_ref, h_ref, p_ref, wa_ref, wb_ref, gp_ref, wg_ref, wp_ref, o_ref = refs
        sz = sz_ref[...].astype(F32)
        mb = (mb_ref[...].astype(F32) * _silu(sz)).astype(BF16)
    else:
        ma_ref, mb_ref, h_ref, p_ref, wa_ref, wb_ref, gp_ref, wg_ref, wp_ref, o_ref = refs
        mb = mb_ref[...]
    y = _dot(ma_ref[...], wa_ref[...]) + _dot(mb, wb_ref[...])
    h1 = h_ref[...] + _rms(y, gp_ref[...])
    gate = _sigmoid(_dot(h1.astype(BF16), wg_ref[...]))
    pp = _dot(p_ref[...].astype(BF16), wp_ref[...])
    o_ref[...] = h1 + gate * pp


def _out_pe(mix_a, a_map, ka, mix_b, b_map, kb, sz, sz_map, h2, p4, layer, w_a, w_b, g_post, w_gate, w_pe,
            bsz, seq, tm=512):
    n, d = h2.shape
    pe = p4.shape[-1]
    nt = seq // tm
    row = lambda b, s: (b * nt + s, 0)
    const = lambda b, s: (0, 0)
    gated = sz is not None
    in_specs = [pl.BlockSpec((tm, ka), a_map), pl.BlockSpec((tm, kb), b_map)]
    args = [mix_a, mix_b]
    if gated:
        in_specs.append(pl.BlockSpec((tm, kb), sz_map))
        args.append(sz)
    in_specs += [pl.BlockSpec((tm, d), row), pl.BlockSpec((None, None, tm, pe), lambda b, s: (layer, b, s, 0)),
                 pl.BlockSpec((ka, d), const), pl.BlockSpec((kb, d), const), pl.BlockSpec((1, d), const),
                 pl.BlockSpec((d, d), const), pl.BlockSpec((pe, d), const)]
    args += [h2, p4, w_a, w_b, g_post, w_gate, w_pe]
    return pl.pallas_call(
        functools.partial(_out_pe_kernel, gated),
        grid=(bsz, nt),
        in_specs=in_specs,
        out_specs=pl.BlockSpec((tm, d), row),
        out_shape=jax.ShapeDtypeStruct((n, d), F32),
        compiler_params=_cparams(("parallel", "arbitrary")),
        name="out_pe_gated" if gated else "out_pe",
    )(*args)


_RQ, _RK, _RV, _RG = 0, 512, 1024, 2048
_MQ, _MK, _MV, _MO, _MZ = 3072, 3584, 4096, 5120, 6144
AB_MAIN = 7168


def _head_norm(y, g):
    yc = y - jnp.mean(y, -1, keepdims=True)
    return yc * lax.rsqrt(jnp.mean(yc * yc, -1, keepdims=True) + EPS) * g


def _ab_kernel(pm_ref, pg_ref, cw_ref, cb_ref, gb_ref, rn_ref, mn_ref, mix_ref,
               ext_ref, r_ref, c_ref, n_ref, m_ref):
    L = CHUNK
    c_idx = pl.program_id(1)

    @pl.when(c_idx == 0)
    def _():
        ext_ref[0:8, :] = jnp.zeros((8, ext_ref.shape[1]), F32)
        r_ref[...] = jnp.zeros(r_ref.shape, F32)
        c_ref[...] = jnp.zeros(c_ref.shape, F32)
        n_ref[...] = jnp.zeros(n_ref.shape, F32)
        m_ref[...] = jnp.full(m_ref.shape, NEG, F32)

    row = lax.broadcasted_iota(jnp.int32, (L, L), 0)
    col = lax.broadcasted_iota(jnp.int32, (L, L), 1)
    causal = row >= col
    diff = (row - col).astype(F32)
    tcol = lax.broadcasted_iota(jnp.int32, (L, 1), 0).astype(F32)

    for h in range(RET_HEADS):
        lg = math.log1p(-(2.0 ** (-5.0 - h)))
        scale = RET_DK ** -0.5
        decay = jnp.where(causal, jnp.exp(jnp.maximum(diff, 0.0) * lg), 0.0) * scale
        q = pm_ref[:, _RQ + h * RET_DK:_RQ + (h + 1) * RET_DK]
        k = pm_ref[:, _RK + h * RET_DK:_RK + (h + 1) * RET_DK]
        v = pm_ref[:, _RV + h * RET_DV:_RV + (h + 1) * RET_DV]
        sc = _dot_nt(q, k) * decay
        xi = jnp.exp((tcol + 1.0) * lg) * scale
        qx = (q.astype(F32) * xi).astype(BF16)
        r_prev = r_ref[h]
        y = _dot(sc.astype(BF16), v) + _dot(qx, r_prev.astype(BF16))
        zeta = jnp.exp((L - 1.0 - tcol) * lg)
        kz_t = (k.astype(F32) * zeta).T.astype(BF16)
        r_ref[h] = r_prev * math.exp(L * lg) + _dot(kz_t, v)
        g = pm_ref[:, _RG + h * RET_DV:_RG + (h + 1) * RET_DV].astype(F32)
        out = _head_norm(y, rn_ref[:, h * RET_DV:(h + 1) * RET_DV]) * _silu(g)
        mix_ref[:, h * RET_DV:(h + 1) * RET_DV] = out.astype(mix_ref.dtype)

    nqk = 2 * ML_HEADS * ML_DK
    x = pm_ref[:, _MQ:_MQ + nqk].astype(F32)
    ext_ref[8:8 + L, :] = x
    acc = cb_ref[...] + cw_ref[ML_CONV - 1:ML_CONV, :] * x
    for j in range(ML_CONV - 1):
        acc = acc + cw_ref[j:j + 1, :] * ext_ref[pl.ds(8 - (ML_CONV - 1) + j, L), :]
    ext_ref[0:8, :] = x[L - 8:L, :]
    qk = _silu(acc)

    gt = (pg_ref[...] + gb_ref[...]).T
    g8 = gt[0:8, :]
    logf8 = jnp.minimum(g8, 0.0) - jnp.log(1.0 + jnp.exp(-jnp.abs(g8)))
    triu = jnp.where(row <= col, 1.0, 0.0).astype(BF16)
    bcum8 = _dot_f32_lhs(logf8, triu)
    b_t = pltpu.roll(bcum8, 4, axis=0)
    b_last = b_t[:, L - 1:L]
    a_t = b_last - b_t + g8
    mu = jnp.max(a_t, axis=1, keepdims=True)
    w_t = jnp.exp(a_t - mu)
    r_t = g8 - b_t
    zpad = jnp.zeros((L - 8, L), F32)
    bcol = jnp.concatenate([b_t, zpad], axis=0).T
    wcol = jnp.concatenate([w_t, zpad], axis=0).T
    m_prev = m_ref[...][:, 0:1]
    m_new = jnp.maximum(b_last + m_prev, mu)
    sp = jnp.exp(b_last + m_prev - m_new)
    scn = jnp.exp(mu - m_new)
    m_ref[...] = jnp.broadcast_to(m_new, m_ref.shape)

    for h in range(ML_HEADS):
        bc = bcol[:, h:h + 1]
        log_d = jnp.where(causal, bc + r_t[h:h + 1, :], NEG)
        inter = bc + m_prev[h:h + 1, :]
        m_t = jnp.maximum(inter, jnp.max(log_d, axis=1, keepdims=True))
        dmat = jnp.exp(log_d - m_t)
        w_int = jnp.exp(inter - m_t)
        qh = qk[:, h * ML_DK:(h + 1) * ML_DK]
        kh = qk[:, ML_HEADS * ML_DK + h * ML_DK:ML_HEADS * ML_DK + (h + 1) * ML_DK] * (ML_DK ** -0.5)
        qb = qh.astype(BF16)
        v = pm_ref[:, _MV + h * ML_DV:_MV + (h + 1) * ML_DV]
        s = _dot_nt(qb, kh.astype(BF16)) * dmat
        c_prev = c_ref[h]
        n_prev = n_ref[h:h + 1, :]
        num = _dot(s.astype(BF16), v) + w_int * _dot(qb, c_prev.astype(BF16))
        den = jnp.sum(s, axis=1, keepdims=True) + w_int * jnp.sum(qh * n_prev, axis=1, keepdims=True)
        den = jnp.maximum(jnp.abs(den), jnp.exp(-m_t))
        hcell = num * (1.0 / den)
        o = pm_ref[:, _MO + h * ML_DV:_MO + (h + 1) * ML_DV].astype(F32)
        y = _sigmoid(o) * hcell
        z = pm_ref[:, _MZ + h * ML_DV:_MZ + (h + 1) * ML_DV].astype(F32)
        out = _head_norm(y, mn_ref[:, h * ML_DV:(h + 1) * ML_DV]) * _silu(z)
        mix_ref[:, RET_HEADS * RET_DV + h * ML_DV:RET_HEADS * RET_DV + (h + 1) * ML_DV] = out.astype(mix_ref.dtype)
        kw = kh * wcol[:, h:h + 1]
        kv = _dot(kw.T.astype(BF16), v)
        ksum = jnp.sum(kw, axis=0, keepdims=True)
        sp_h = sp[h:h + 1, :]
        sc_h = scn[h:h + 1, :]
        c_ref[h] = sp_h * c_prev + sc_h * kv
        n_ref[h:h + 1, :] = sp_h * n_prev + sc_h * ksum


def _ab_mixer(pm, pg, conv_w, conv_b, gate_b, ret_norm, ml_norm, bsz, seq):
    n = pm.shape[0]
    nc = seq // CHUNK
    nqk = 2 * ML_HEADS * ML_DK
    row = lambda b, c: (b * nc + c, 0)
    const = lambda b, c: (0, 0)
    nmix = RET_HEADS * RET_DV + ML_HEADS * ML_DV
    return pl.pallas_call(
        _ab_kernel,
        grid=(bsz, nc),
        in_specs=[pl.BlockSpec((CHUNK, AB_MAIN), row), pl.BlockSpec((CHUNK, LANES), row),
                  pl.BlockSpec((ML_CONV, nqk), const), pl.BlockSpec((1, nqk), const),
                  pl.BlockSpec((1, LANES), const),
                  pl.BlockSpec((1, RET_HEADS * RET_DV), const), pl.BlockSpec((1, ML_HEADS * ML_DV), const)],
        out_specs=pl.BlockSpec((CHUNK, nmix), row),
        out_shape=jax.ShapeDtypeStruct((n, nmix), BF16),
        scratch_shapes=[pltpu.VMEM((CHUNK + 8, nqk), F32),
                        pltpu.VMEM((RET_HEADS, RET_DK, RET_DV), F32),
                        pltpu.VMEM((ML_HEADS, ML_DK, ML_DV), F32),
                        pltpu.VMEM((8, ML_DK), F32),
                        pltpu.VMEM((8, LANES), F32)],
        compiler_params=_cparams(("parallel", "arbitrary")),
        name="ab_mixer",
    )(pm, pg, conv_w, conv_b, gate_b, ret_norm, ml_norm)


_QNZ0 = 0
_KVC0 = NSA_HEADS
_KVS0 = _KVC0 + NSA_GROUPS
_KVW0 = _KVS0 + NSA_GROUPS
_SU0 = (_KVW0 + NSA_GROUPS) * LANES
_SZ0 = _SU0 + S5_WIDTH
CD_MAIN = _SZ0 + S5_WIDTH
SLOTW = NSA_GROUPS * LANES
VT_ROWS = NSA_DH + 16
LOG2E = math.log2(math.e)


def _compress_kernel(x_ref, pk_ref, pv_ref, w1k_ref, w1v_ref, w2k_ref, w2vt_ref, ekc_ref,
                     ko_ref, vot_ref, xs_ref):
    seq = x_ref.shape[0]
    nb = seq // CMP_STRIDE
    for g in range(NSA_GROUPS):
        xs_ref[...] = x_ref[:, g * LANES:(g + 1) * LANES].astype(F32)
        acc = [jnp.zeros((nb, CMP_HIDDEN), F32) for _ in range(4)]
        for r in range(CMP_STRIDE):
            xr = xs_ref[pl.ds(r, nb, stride=CMP_STRIDE), :]
            for kind, (p_ref, w1_ref) in enumerate(((pk_ref, w1k_ref), (pv_ref, w1v_ref))):
                acc[2 * kind] = acc[2 * kind] + _dot((xr + p_ref[r:r + 1, :]).astype(BF16), w1_ref[r])
                acc[2 * kind + 1] = acc[2 * kind + 1] + _dot(
                    (xr + p_ref[CMP_STRIDE + r:CMP_STRIDE + r + 1, :]).astype(BF16), w1_ref[CMP_STRIDE + r])
        hk = _gelu(acc[0] + pltpu.roll(acc[1], nb - 1, axis=0)).astype(BF16)
        hv = _gelu(acc[2] + pltpu.roll(acc[3], nb - 1, axis=0)).astype(BF16)
        ko_ref[g] = (_dot(hk, w2k_ref[...]) + ekc_ref[...]).astype(ko_ref.dtype)
        vot_ref[g] = _dot_nt(w2vt_ref[...], hv).astype(vot_ref.dtype)


def _compress(pm3, pos_k, pos_v, w1k, w1v, w2k, w2vt, ekc):
    bsz, seq, _ = pm3.shape
    nb = seq // CMP_STRIDE
    c3 = lambda b: (0, 0, 0)
    c2 = lambda b: (0, 0)
    return pl.pallas_call(
        _compress_kernel,
        grid=(bsz,),
        in_specs=[pl.BlockSpec((None, seq, SLOTW), lambda b: (b, 0, _KVC0 * LANES // SLOTW)),
                  pl.BlockSpec((CMP_LEN, LANES), c2), pl.BlockSpec((CMP_LEN, LANES), c2),
                  pl.BlockSpec((CMP_LEN, LANES, CMP_HIDDEN), c3), pl.BlockSpec((CMP_LEN, LANES, CMP_HIDDEN), c3),
                  pl.BlockSpec((CMP_HIDDEN, LANES), c2), pl.BlockSpec((LANES, CMP_HIDDEN), c2),
                  pl.BlockSpec((nb, LANES), c2)],
        out_specs=[pl.BlockSpec((None, NSA_GROUPS, nb, LANES), lambda b: (b, 0, 0, 0)),
                   pl.BlockSpec((None, NSA_GROUPS, LANES, nb), lambda b: (b, 0, 0, 0))],
        out_shape=[jax.ShapeDtypeStruct((bsz, NSA_GROUPS, nb, LANES), BF16),
                   jax.ShapeDtypeStruct((bsz, NSA_GROUPS, LANES, nb), BF16)],
        scratch_shapes=[pltpu.VMEM((seq, LANES), F32)],
        compiler_params=_cparams(("parallel",)),
        name="nsa_compress",
    )(pm3, pos_k, pos_v, w1k, w1v, w2k, w2vt, ekc)


def _nsa_kernel(q_ref, kc_ref, vct_ref, kvs_ref, kvw_ref, pg_ref,
                ek_ref, al_ref, ovt_ref, y_ref,
                ksa_ref, vsat_ref, kwa_ref, vwat_ref, qa_scr, part_scr, gzs_scr, sel_smem):
    seq = kvs_ref.shape[0]
    g_idx = pl.program_id(1)
    qi = pl.program_id(2)
    wtiles = WINDOW // LANES

    @pl.when(qi == 0)
    def _():
        ek = ek_ref[...]
        lane_k = lax.broadcasted_iota(jnp.int32, (seq, LANES), 1)
        zero = jnp.zeros((seq, LANES), BF16)
        ksa_ref[0:seq, :] = jnp.where(lane_k < NSA_DH, kvs_ref[...], zero) + ek
        lane_d = lax.broadcasted_iota(jnp.int32, (LANES, LANES), 1)
        ksa_ref[seq:seq + LANES, :] = jnp.where((lane_d >= SEL_LANE0) & (lane_d < SEL_LANE0 + seq // SEL_BLOCK),
                                                1.0, 0.0).astype(BF16)
        vsat_ref[seq // LANES] = jnp.zeros((VT_ROWS, LANES), BF16)
        ek_w = jnp.where(lane_k >= ALI_LANE0, ek, zero)
        kwa_ref[0:WINDOW, :] = jnp.zeros((WINDOW, LANES), BF16)
        kwa_ref[WINDOW:WINDOW + seq, :] = jnp.where(lane_k < NSA_DH, kvw_ref[...], zero) + ek_w
        ones_rows = jnp.where(lax.broadcasted_iota(jnp.int32, (VT_ROWS - NSA_DH, LANES), 0) == 0, 1.0, 0.0)
        for i in range(wtiles):
            vwat_ref[i] = jnp.zeros((VT_ROWS, LANES), BF16)
        for i in range(seq // LANES):
            rws = slice(i * LANES, (i + 1) * LANES)
            for src_ref, dst_ref, j in ((kvs_ref, vsat_ref, i), (kvw_ref, vwat_ref, wtiles + i)):
                v_t = src_ref[rws, :].astype(F32).T[NSA_DH:2 * NSA_DH, :]
                dst_ref[j] = jnp.concatenate([v_t, ones_rows], axis=0).astype(BF16)

    nq = seq // TQ
    refs = (q_ref, kc_ref, vct_ref, pg_ref, al_ref, ovt_ref, y_ref, ksa_ref, vsat_ref, kwa_ref, vwat_ref,
            qa_scr, part_scr, gzs_scr, sel_smem)
    for step in range(nq + 1):
        pl.when(qi == step)(functools.partial(_nsa_step, step, nq, seq // SEL_BLOCK, g_idx, refs))


SEL_DYN = 2
SEL_CNT = 7


def _sel_compactable(qi):
    return (qi + 1) * TQ // LANES > 1 + SEL_DYN + TQ // LANES


def _nsa_step(step, nq, nsel, g_idx, refs):
    def run(compact):
        stages = []
        if step < nq:
            stages.append(_nsa_front(nsel, step, g_idx, refs))
        if step > 0:
            stages.append(_nsa_back(step - 1, compact, refs))
        while stages:
            for st in list(stages):
                if next(st, "done") == "done":
                    stages.remove(st)

    if step > 0 and _sel_compactable(step - 1):
        fits = refs[-1][(step - 1) & 1, SEL_CNT] <= SEL_DYN
        pl.when(fits)(functools.partial(run, True))
        pl.when(jnp.logical_not(fits))(functools.partial(run, False))
    else:
        run(False)


def _tile_iotas():
    rows = NSA_HPG * TQ
    jrow = lax.broadcasted_iota(jnp.int32, (LANES, rows), 0)
    tq = lax.broadcasted_iota(jnp.int32, (LANES, rows), 1) & (TQ - 1)
    return jrow, tq, jrow - tq


def _nsa_front(nsel, qi, g_idx, refs):
    (q_ref, kc_ref, vct_ref, pg_ref, al_ref, ovt_ref, _, _, _, kwa_ref, vwat_ref,
     qa_scr, part_scr, gzs_scr, sel_smem) = refs
    hp = NSA_HPG
    span = WINDOW + TQ
    t0 = qi * TQ
    tw = t0
    early = t0 < WINDOW
    par = qi & 1
    jrow, tq, kmt = _tile_iotas()

    qnz = [q_ref[:, hh * LANES:(hh + 1) * LANES].astype(F32) for hh in range(hp)]
    lane_q = lax.broadcasted_iota(jnp.int32, (TQ, LANES), 1)
    qal = jnp.concatenate([jnp.where(lane_q < NSA_DH, qnz[hh], al_ref[pl.ds(g_idx * hp + hh, 1), :])
                           for hh in range(hp)], axis=0)
    qalb = qal.astype(BF16)
    s_c = _dot_nt(kc_ref[...], qalb)
    s_w = _dot_nt(kwa_ref[pl.ds(tw, span), :], qalb)
    yield

    s = jnp.where(tq - jrow * CMP_STRIDE >= (CMP_LEN - 1) - t0, s_c, NEG)
    e = jnp.exp2(s - jnp.max(s, axis=0, keepdims=True))
    inv = 1.0 / jnp.sum(e, axis=0, keepdims=True)
    if early:
        inv = jnp.where(t0 + tq[0:1, :] >= CMP_LEN - 1, inv, 0.0)
    p = e * inv
    oc_t = _dot(vct_ref[...], p.astype(BF16))
    psum_t = p[:, 0:TQ]
    for hh in range(1, hp):
        psum_t = psum_t + p[:, hh * TQ:(hh + 1) * TQ]
    p_hi = psum_t.astype(BF16)
    p_lo = (psum_t - p_hi.astype(F32)).astype(BF16)
    imp_t = _dot(ovt_ref[...], p_hi) + _dot(ovt_ref[...], p_lo)
    yield

    blocks = [s_w[i * LANES:(i + 1) * LANES, :] for i in range(span // LANES)]
    for i in range(span // LANES):
        if i * LANES < TQ:
            blocks[i] = jnp.where(kmt > -i * LANES, blocks[i], NEG)
        if (i + 1) * LANES > WINDOW:
            blocks[i] = jnp.where(kmt <= WINDOW - i * LANES, blocks[i], NEG)
        if early:
            blocks[i] = jnp.where(jrow >= WINDOW - t0 - i * LANES, blocks[i], NEG)
    s = jnp.concatenate(blocks, axis=0)
    p = jnp.exp2(s - jnp.max(s, axis=0, keepdims=True)).astype(BF16)
    vw_t = jnp.concatenate([vwat_ref[qi * (TQ // LANES) + i] for i in range(span // LANES)], axis=1)
    accw_t = _dot(vw_t, p)
    yield

    score = imp_t[SEL_LANE0:SEL_LANE0 + nsel, :]
    nrow = lax.broadcasted_iota(jnp.int32, (nsel, TQ), 0)
    own = (t0 + lax.broadcasted_iota(jnp.int32, (nsel, TQ), 1)) >> int(math.log2(SEL_BLOCK))
    valid = nrow <= own
    forced = (nrow == 0) | (nrow == own)
    score = jnp.where(forced, BIG, jnp.where(valid, score, -1.0))
    chosen = jnp.zeros((nsel, TQ), F32)
    nrow_f = nrow.astype(F32)
    for _ in range(SEL_TOPN):
        mx = jnp.max(score, axis=0, keepdims=True)
        first = jnp.min(jnp.where(score == mx, nrow_f, 4.0 * LANES), axis=0, keepdims=True)
        hit = nrow_f == first
        chosen = jnp.where(hit, 1.0, chosen)
        score = jnp.where(hit, -3.0, score)
    mask_t = jnp.where(chosen > 0.0, jnp.where(valid, 0.0, NEG), NEG)
    if _sel_compactable(qi):
        used_blk = jnp.max(jnp.where(mask_t == 0.0, 1.0, 0.0), axis=1, keepdims=True)
        blk_per_tile = LANES // SEL_BLOCK
        for slot in range(SEL_DYN):
            sel_smem[par, slot] = jnp.int32(nsel // blk_per_tile)
        cnt = jnp.int32(0)
        for i in range(1, t0 // LANES):
            used = jnp.max(used_blk[i * blk_per_tile:(i + 1) * blk_per_tile, :]) > 0.0

            @pl.when(used & (cnt < SEL_DYN))
            def _(i=i, cnt=cnt):
                sel_smem[par, cnt] = jnp.int32(i)

            cnt = cnt + used.astype(jnp.int32)
        sel_smem[par, SEL_CNT] = cnt
    maskcols = jnp.concatenate([jnp.zeros((SEL_LANE0, TQ), F32), mask_t,
                                jnp.zeros((LANES - SEL_LANE0 - nsel, TQ), F32)], axis=0).T

    qa_scr[par] = (qal + jnp.concatenate([maskcols] * hp, axis=0)).astype(BF16)

    gt_t = _sigmoid(pg_ref[...]).T
    for hh in range(hp):
        g_c, g_s, g_w = [gt_t[j * hp + hh:j * hp + hh + 1, :] for j in range(3)]
        hcols = slice(hh * TQ, (hh + 1) * TQ)
        aw_t = accw_t[:, hcols]
        gz = _silu(qnz[hh].T[NSA_DH:2 * NSA_DH, :])
        o = g_c * oc_t[0:NSA_DH, hcols] + (g_w * (1.0 / aw_t[NSA_DH:NSA_DH + 1, :])) * aw_t[0:NSA_DH, :]
        part_scr[par, hh * NSA_DH:(hh + 1) * NSA_DH, :] = o * gz
        gzs_scr[par, hh * NSA_DH:(hh + 1) * NSA_DH, :] = g_s * gz


def _nsa_back(qi, compact, refs):
    _, _, _, _, _, _, y_ref, ksa_ref, vsat_ref, _, _, qa_scr, part_scr, gzs_scr, sel_smem = refs
    hp = NSA_HPG
    t0 = qi * TQ
    par = qi & 1
    _, _, kmt = _tile_iotas()
    qa = qa_scr[par]
    ntiles = (t0 + TQ) // LANES
    if compact:
        tiles = [0] + [sel_smem[par, j] for j in range(SEL_DYN)] + list(range(t0 // LANES, ntiles))
    else:
        tiles = list(range(ntiles))
    per = KCH // LANES
    chunks = [tiles[i:i + per] for i in range(0, len(tiles), per)]
    nk = len(chunks)

    def k_tile(i):
        if isinstance(i, int):
            return ksa_ref[i * LANES:(i + 1) * LANES, :]
        return ksa_ref[pl.ds(pl.multiple_of(i * LANES, LANES), LANES), :]

    def scores(c):
        s = _dot_nt(jnp.concatenate([k_tile(i) for i in chunks[c]], axis=0), qa)
        if any(isinstance(i, int) and (i + 1) * LANES > t0 for i in chunks[c]):
            s = jnp.concatenate(
                [jnp.where(kmt <= t0 - i * LANES, s[n * LANES:(n + 1) * LANES, :], NEG)
                 if isinstance(i, int) and (i + 1) * LANES > t0 else s[n * LANES:(n + 1) * LANES, :]
                 for n, i in enumerate(chunks[c])], axis=0)
        return s

    def weighted(c, s):
        mc = jnp.max(s, axis=0, keepdims=True)
        v_t = jnp.concatenate([vsat_ref[i] for i in chunks[c]], axis=1)
        return mc, _dot(v_t, jnp.exp2(s - mc).astype(BF16))

    parts = []
    s_next = scores(0)
    for c in range(nk):
        s_cur = s_next
        if c + 1 < nk:
            s_next = scores(c + 1)
        yield
        parts.append(weighted(c, s_cur))
    yield
    m = parts[0][0]
    for mc, _ in parts[1:]:
        m = jnp.maximum(m, mc)
    accs_t = None
    for mc, d in parts:
        d = d if nk == 1 else jnp.exp2(mc - m) * d
        accs_t = d if accs_t is None else accs_t + d

    outs = []
    for hh in range(hp):
        as_t = accs_t[:, hh * TQ:(hh + 1) * TQ]
        rws = slice(hh * NSA_DH, (hh + 1) * NSA_DH)
        outs.append(part_scr[par, rws, :] + (gzs_scr[par, rws, :] * (1.0 / as_t[NSA_DH:NSA_DH + 1, :])) * as_t[0:NSA_DH, :])
    for pr in range(hp // 2):
        packed = jnp.concatenate([outs[2 * pr], outs[2 * pr + 1]], axis=0).T
        y_ref[:, pr * LANES:(pr + 1) * LANES] = packed.astype(y_ref.dtype)


def _nsa(pm3, pg3, kcc, vct, ek, al, ovt):
    bsz, seq, _ = pm3.shape
    nq = seq // TQ
    gw = NSA_HPG * NSA_DH
    kv = lambda tile0: pl.BlockSpec((None, seq, LANES), lambda b, g, q: (b, 0, tile0 + g))
    c2 = lambda b, g, q: (0, 0)
    front = lambda b, g, q: (b, jnp.minimum(q, nq - 1), g)
    back = lambda b, g, q: (b, jnp.maximum(q - 1, 0), g)
    return pl.pallas_call(
        _nsa_kernel,
        grid=(bsz, NSA_GROUPS, nq + 1),
        in_specs=[pl.BlockSpec((None, TQ, NSA_HPG * LANES), front),
                  pl.BlockSpec((None, None, LANES, LANES), lambda b, g, q: (b, g, 0, 0)),
                  pl.BlockSpec((None, None, LANES, LANES), lambda b, g, q: (b, g, 0, 0)),
                  kv(_KVS0), kv(_KVW0),
                  pl.BlockSpec((None, TQ, LANES), front),
                  pl.BlockSpec((seq, LANES), c2),
                  pl.BlockSpec((NSA_HEADS, LANES), c2),
                  pl.BlockSpec((LANES, LANES), c2)],
        out_specs=pl.BlockSpec((None, TQ, gw), back),
        out_shape=jax.ShapeDtypeStruct((bsz, seq, NSA_HEADS * NSA_DH), BF16),
        scratch_shapes=[pltpu.VMEM((seq + LANES, LANES), BF16),
                        pltpu.VMEM((seq // LANES + 1, VT_ROWS, LANES), BF16),
                        pltpu.VMEM((seq + WINDOW, LANES), BF16),
                        pltpu.VMEM(((seq + WINDOW) // LANES, VT_ROWS, LANES), BF16),
                        pltpu.VMEM((2, NSA_HPG * TQ, LANES), BF16),
                        pltpu.VMEM((2, gw, TQ), F32), pltpu.VMEM((2, gw, TQ), F32),
                        pltpu.SMEM((2, SEL_CNT + 1), jnp.int32)],
        compiler_params=_cparams(("parallel", "arbitrary", "arbitrary")),
        name="nsa_attn",
    )(pm3, kcc, vct, pm3, pm3, pg3, ek, al, ovt)


def _nsa_tables(seq):
    t = np.arange(seq)
    ek = np.zeros((seq, LANES), np.float32)
    ek[t, SEL_LANE0 + t // SEL_BLOCK] = 1.0
    for c in range(3):
        ek[:, ALI_LANE0 + c] = t // SEL_BLOCK
        ek[:, ALI_LANE0 + 3 + c] = t % SEL_BLOCK
    slopes = jnp.exp2(-8.0 * jnp.arange(1, NSA_HEADS + 1, dtype=F32) / NSA_HEADS) * LOG2E
    s_hi, s_mid, s_lo = _split3(slopes)
    parts = [p.astype(F32) for p in (s_hi, s_mid, s_lo)]
    al = jnp.pad(jnp.stack([p * SEL_BLOCK for p in parts] + parts, axis=1),
                 ((0, 0), (ALI_LANE0, LANES - ALI_LANE0 - 2 * len(parts))))
    nb = seq // CMP_STRIDE
    cstart = np.arange(nb) * CMP_STRIDE
    cend = cstart + CMP_LEN - 1
    ekc = np.zeros((nb, LANES), np.float32)
    for c in range(3):
        ekc[:, ALI_LANE0 + c] = cend // SEL_BLOCK
        ekc[:, ALI_LANE0 + 3 + c] = cend % SEL_BLOCK
    sel = np.arange(seq // SEL_BLOCK)
    ovt = np.zeros((LANES, max(nb, LANES)), np.float32)
    ovl = (cstart[:, None] < (sel[None, :] + 1) * SEL_BLOCK) & (cstart[:, None] + CMP_LEN > sel[None, :] * SEL_BLOCK)
    ovl[nb - 1, :] = False
    ovt[SEL_LANE0:SEL_LANE0 + len(sel), :nb] = ovl.T
    return jnp.asarray(ek, BF16), al, jnp.asarray(ekc), jnp.asarray(ovt, BF16)


def _s5_disc_kernel(are_ref, aim_ref, ldt_ref, bre_ref, bim_ref, abr_ref, abi_ref, bbr_ref, bbi_ref):
    a_re = are_ref[...]
    a_im = aim_ref[...]
    dt = jnp.exp(ldt_ref[...])
    er = jnp.exp(a_re * dt)
    abar_re = er * jnp.cos(a_im * dt)
    abar_im = er * jnp.sin(a_im * dt)
    lam2 = a_re * a_re + a_im * a_im
    cr = ((abar_re - 1.0) * a_re + abar_im * a_im) / lam2
    ci = (abar_im * a_re - (abar_re - 1.0) * a_im) / lam2
    b_re = bre_ref[...]
    b_im = bim_ref[...]
    abr_ref[...] = abar_re
    abi_ref[...] = abar_im
    bbr_ref[...] = cr * b_re - ci * b_im
    bbi_ref[...] = cr * b_im + ci * b_re


def _s5_disc(a_re_rep, a_im_rep, ldt_rep, b_re_t, b_im_t):
    shp = jax.ShapeDtypeStruct(a_re_rep.shape, F32)
    return pl.pallas_call(_s5_disc_kernel, out_shape=[shp] * 4, name="s5_disc")(
        a_re_rep, a_im_rep, ldt_rep, b_re_t, b_im_t)


S5_SLABG = 4
S5_TT = 64
S5_PITCH = S5_TT + 4
S5_DIAG = 256
S5_NSTATE_PER_DIAG = S5_DIAG // S5_GROUP_CH * S5_STATE


def _s5_kernel(u_ref, bre_ref, bim_ref, ar_ref, ai_ref, cre_ref, cim_ref, d_ref, wg_ref, o_ref,
               xr_ref, xi_ref, sr_ref, si_ref):
    nb, tt, wd = u_ref.shape
    pitch = sr_ref.shape[1] // nb
    gcols = S5_SLABG * LANES
    ngroups = S5_NSTATE // gcols

    @pl.when(pl.program_id(0) == 0)
    def _():
        xr_ref[...] = jnp.zeros(xr_ref.shape, F32)
        xi_ref[...] = jnp.zeros(xi_ref.shape, F32)

    u = u_ref[...].reshape(nb * tt, wd)

    chan = lambda kg: slice(kg * gcols // S5_NSTATE_PER_DIAG * S5_DIAG,
                            (kg * gcols // S5_NSTATE_PER_DIAG + 1) * S5_DIAG)
    for kg in range(ngroups):
        cols = slice(kg * gcols, (kg + 1) * gcols)
        br = _dot(u[:, chan(kg)], bre_ref[chan(kg), cols])
        bi = _dot(u[:, chan(kg)], bim_ref[chan(kg), cols])
        for j in range(S5_SLABG):
            for b in range(nb):
                sr_ref[kg * S5_SLABG + j, b * pitch:b * pitch + tt, :] = br[b * tt:(b + 1) * tt, j * LANES:(j + 1) * LANES]
                si_ref[kg * S5_SLABG + j, b * pitch:b * pitch + tt, :] = bi[b * tt:(b + 1) * tt, j * LANES:(j + 1) * LANES]

    for kg in range(ngroups):
        cols = slice(kg * gcols, (kg + 1) * gcols)
        slabs = range(kg * S5_SLABG, (kg + 1) * S5_SLABG)
        ar = ar_ref[:, cols]
        ai = ai_ref[:, cols]

        def body(t, carry, slabs=slabs, ar=ar, ai=ai):
            xr, xi = carry
            rows = pl.ds(t, nb, stride=pitch)
            nxr = ar * xr - ai * xi + jnp.concatenate([sr_ref[k, rows, :] for k in slabs], axis=1)
            nxi = ar * xi + ai * xr + jnp.concatenate([si_ref[k, rows, :] for k in slabs], axis=1)
            for j, k in enumerate(slabs):
                sr_ref[k, rows, :] = nxr[:, j * LANES:(j + 1) * LANES]
                si_ref[k, rows, :] = nxi[:, j * LANES:(j + 1) * LANES]
            return nxr, nxi

        xr, xi = lax.fori_loop(0, tt, body, (xr_ref[:, cols], xi_ref[:, cols]), unroll=4)
        xr_ref[:, cols] = xr
        xi_ref[:, cols] = xi

    y_blocks = [jnp.zeros((nb * tt, S5_DIAG), F32) for _ in range(wd // S5_DIAG)]
    for kg in range(ngroups):
        cols = slice(kg * gcols, (kg + 1) * gcols)
        gather = lambda ref: jnp.concatenate(
            [jnp.concatenate([ref[kg * S5_SLABG + j, b * pitch:b * pitch + tt, :] for j in range(S5_SLABG)], axis=1)
             for b in range(nb)], axis=0).astype(BF16)
        blk = chan(kg).start // S5_DIAG
        y_blocks[blk] = (y_blocks[blk] + _dot(gather(sr_ref), cre_ref[cols, chan(kg)])
                         - _dot(gather(si_ref), cim_ref[cols, chan(kg)]))
    y = jnp.concatenate(y_blocks, axis=1)
    y = _gelu(y + d_ref[...] * u.astype(F32))
    z = _dot(y.astype(BF16), wg_ref[...])
    o = z[:, :S5_WIDTH] * _sigmoid(z[:, S5_WIDTH:])
    o_ref[...] = o.reshape(nb, tt, wd).astype(o_ref.dtype)


def _s5(pm3, bbd_re, bbd_im, ar8, ai8, cbd_re, cbd_im, d_row, w_glu):
    bsz, seq, _ = pm3.shape
    tt = S5_TT
    c2 = lambda i: (0, 0)
    nslab = S5_NSTATE // LANES
    return pl.pallas_call(
        _s5_kernel,
        grid=(seq // tt,),
        in_specs=[pl.BlockSpec((bsz, tt, S5_WIDTH), lambda i: (0, i, _SU0 // S5_WIDTH)),
                  pl.BlockSpec((S5_WIDTH, S5_NSTATE), c2), pl.BlockSpec((S5_WIDTH, S5_NSTATE), c2),
                  pl.BlockSpec((8, S5_NSTATE), c2), pl.BlockSpec((8, S5_NSTATE), c2),
                  pl.BlockSpec((S5_NSTATE, S5_WIDTH), c2), pl.BlockSpec((S5_NSTATE, S5_WIDTH), c2),
                  pl.BlockSpec((1, S5_WIDTH), c2), pl.BlockSpec((S5_WIDTH, 2 * S5_WIDTH), c2)],
        out_specs=pl.BlockSpec((bsz, tt, S5_WIDTH), lambda i: (0, i, 0)),
        out_shape=jax.ShapeDtypeStruct((bsz, seq, S5_WIDTH), BF16),
        scratch_shapes=[pltpu.VMEM((8, S5_NSTATE), F32), pltpu.VMEM((8, S5_NSTATE), F32),
                        pltpu.VMEM((nslab, bsz * S5_PITCH, LANES), F32),
                        pltpu.VMEM((nslab, bsz * S5_PITCH, LANES), F32)],
        compiler_params=_cparams(("arbitrary",)),
        name="s5_scan",
    )(pm3, bbd_re, bbd_im, ar8, ai8, cbd_re, cbd_im, d_row, w_glu)


def _pad_cols(w, n):
    return jnp.pad(w, ((0, 0), (0, n - w.shape[1])))


def _relayout_kernel(pieces, w_ref, o_ref):
    o0 = 0
    for s0, n, sc in pieces:
        v = w_ref[s0:s0 + n, :]
        o_ref[o0:o0 + n, :] = (v if sc is None else v * sc).astype(o_ref.dtype)
        o0 += n


def _relayout_rows(w_t, pieces, tn=256):
    nin, d = w_t.shape
    nout = sum(n for _, n, _ in pieces)
    return pl.pallas_call(
        functools.partial(_relayout_kernel, pieces),
        grid=(d // tn,),
        in_specs=[pl.BlockSpec((nin, tn), lambda i: (0, i))],
        out_specs=pl.BlockSpec((nout, tn), lambda i: (0, i)),
        out_shape=jax.ShapeDtypeStruct((nout, d), BF16),
        compiler_params=_cparams(("parallel",)),
        name="relayout_rows",
    )(w_t)


def _cd_proj_weights(w_in):
    w_t = w_in.T
    d = w_in.shape[0]
    nw, nkv = NSA_HEADS * NSA_DH, NSA_GROUPS * NSA_DH
    offs = [int(o) for o in np.cumsum((0, nw, nkv, nkv, nkv, nkv, nkv, nkv, 3 * NSA_HEADS, nw, S5_WIDTH, S5_WIDTH))]
    def pair(ia, ib, scale=None):
        return [pc for c in range(0, offs[ia + 1] - offs[ia], NSA_DH)
                for pc in ((offs[ia] + c, NSA_DH, scale), (offs[ib] + c, NSA_DH, None))]
    w_main = _relayout_rows(w_t, pair(0, 8, NSA_DH ** -0.5 * LOG2E) + pair(1, 2) + pair(3, 4) + pair(5, 6)
                            + [(offs[9], S5_WIDTH, None), (offs[10], S5_WIDTH, None)])
    w_gate = w_t[offs[7]:offs[8]].reshape(NSA_GROUPS, NSA_HPG, 3, d).transpose(0, 2, 1, 3)
    w_gate = jnp.pad(w_gate.reshape(NSA_GROUPS, 3 * NSA_HPG, d), ((0, 0), (0, LANES - 3 * NSA_HPG), (0, 0)))
    return w_gate.reshape(NSA_GROUPS * LANES, d).astype(BF16), w_main


def _layer0(h2, p4, layer, bsz, seq, norm_pre, norm_post, pe_proj, pe_gate,
            w_in, ret_norm, conv_w, conv_b, gate_b, ml_norm, w_out):
    d = h2.shape[1]
    g0 = 5120
    after = g0 + 2 * ML_HEADS
    w_t = w_in.T
    w_main = _relayout_rows(w_t, [(0, g0, None), (after, w_in.shape[1] - after, None)])
    w_gate = jnp.pad(w_t[g0:after], ((0, LANES - 2 * ML_HEADS), (0, 0))).astype(BF16)
    pm, pg = _norm_proj(h2, norm_pre.reshape(1, d), w_gate, w_main)
    mix = _ab_mixer(pm, pg, conv_w, conv_b.reshape(1, -1), _pad_cols(gate_b.reshape(1, -1), LANES),
                    ret_norm.reshape(1, -1), ml_norm.reshape(1, -1), bsz, seq)
    ka = RET_HEADS * RET_DV
    nt = seq // 512
    return _out_pe(mix, lambda b, s: (b * nt + s, 0), ka, mix, lambda b, s: (b * nt + s, 1), ML_HEADS * ML_DV,
                   None, None, h2, p4, layer, w_out[:ka].astype(BF16), w_out[ka:].astype(BF16),
                   norm_post.reshape(1, d), pe_gate.astype(BF16), pe_proj.astype(BF16), bsz, seq)


def _layer1(h2, p4, layer, bsz, seq, norm_pre, norm_post, pe_proj, pe_gate,
            w_in, pos_k, pos_v, w1_k, w2_k, w1_v, w2_v,
            a_re, a_im, log_dt, b_re, b_im, c_re, c_im, d_skip, w_glu, w_out):
    d = h2.shape[1]
    nw = NSA_HEADS * NSA_DH
    pm, pg = _norm_proj(h2, norm_pre.reshape(1, d), *_cd_proj_weights(w_in))

    nb = seq // CMP_STRIDE
    pm3 = pm.reshape(bsz, seq, CD_MAIN)
    padl = lambda a, lo=True: jnp.pad(a, ((0, 0),) * (a.ndim - 1)
                                      + ((0, LANES - a.shape[-1]) if lo else (LANES - a.shape[-1], 0),))
    w1 = lambda w, lo: jnp.pad(w.reshape(CMP_LEN, NSA_DH, CMP_HIDDEN),
                               ((0, 0), (0, LANES - NSA_DH) if lo else (LANES - NSA_DH, 0), (0, 0))).astype(BF16)
    ek, al, ekc, ovt = _nsa_tables(seq)
    kcc, vct = _compress(pm3, padl(pos_k), padl(pos_v, False), w1(w1_k, True), w1(w1_v, False),
                         padl(w2_k).astype(BF16), padl(w2_v).T.astype(BF16), ekc)
    if nb < LANES:
        kcc = jnp.pad(kcc, ((0, 0), (0, 0), (0, LANES - nb), (0, 0)))
        vct = jnp.pad(vct, ((0, 0), (0, 0), (0, 0), (0, LANES - nb)))
    y_nsa = _nsa(pm3, pg.reshape(bsz, seq, NSA_GROUPS * LANES), kcc, vct, ek, al, ovt)

    rep = lambda a: jnp.repeat(a, S5_GROUP_CH, axis=0)
    ldt = jnp.broadcast_to(rep(log_dt[:, None]), (S5_WIDTH, S5_STATE))
    bt = lambda b: b.transpose(0, 2, 1).reshape(S5_WIDTH, S5_STATE)
    abr, abi, bbr, bbi = _s5_disc(rep(a_re), rep(a_im), ldt, bt(b_re), bt(b_im))
    gi = np.arange(S5_WIDTH) // S5_GROUP_CH
    gs = np.arange(S5_NSTATE) // S5_STATE
    diag_in = jnp.asarray(gi[:, None] == gs[None, :])
    bd_in = lambda bb: jnp.where(diag_in, jnp.tile(bb, (1, S5_GROUPS)), 0.0).astype(BF16)
    bd_out = lambda c: jnp.where(diag_in.T, jnp.tile(c.transpose(0, 2, 1).reshape(S5_NSTATE, S5_GROUP_CH),
                                                     (1, S5_GROUPS)), 0.0).astype(BF16)
    row8 = lambda a: jnp.broadcast_to(a[::S5_GROUP_CH].reshape(1, S5_NSTATE), (8, S5_NSTATE))
    y_s5 = _s5(pm3, bd_in(bbr), bd_in(bbi), row8(abr), row8(abi),
               bd_out(c_re), bd_out(c_im), d_skip.reshape(1, -1), w_glu.astype(BF16))

    nt = seq // 512
    szb = _SZ0 // S5_WIDTH
    return _out_pe(y_nsa.reshape(bsz * seq, nw), lambda b, s: (b * nt + s, 0), nw,
                   y_s5.reshape(bsz * seq, S5_WIDTH), lambda b, s: (b * nt + s, 0), S5_WIDTH,
                   pm, lambda b, s: (b * nt + s, szb),
                   h2, p4, layer, w_out[:nw].astype(BF16), w_out[nw:].astype(BF16),
                   norm_post.reshape(1, d), pe_gate.astype(BF16), pe_proj.astype(BF16), bsz, seq)


def kernel(x, p, norm_pre, norm_post, pe_proj, pe_gate, ab_w_in, ret_norm, ml_conv_w, ml_conv_b, ml_gate_b,
           ml_norm, ab_w_out, cd_w_in, cmp_pos_k, cmp_pos_v, cmp_w1_k, cmp_w2_k, cmp_w1_v, cmp_w2_v,
           s5_a_re, s5_a_im, s5_log_dt, s5_b_re, s5_b_im, s5_c_re, s5_c_im, s5_d, s5_w_glu, cd_w_out):
    bsz, seq, d = x.shape
    assert bsz == 8, "the S5 scan maps the batch onto the 8 sublanes of a vreg"
    assert seq % max(KCH, 1024) == 0
    depth = p.shape[0]
    h2 = x.reshape(bsz * seq, d)
    for i in range(depth):
        j = i // 2
        if i % 2 == 0:
            h2 = _layer0(h2, p, i, bsz, seq, norm_pre[i], norm_post[i], pe_proj[i], pe_gate[i],
                         ab_w_in[j], ret_norm[j], ml_conv_w[j], ml_conv_b[j], ml_gate_b[j], ml_norm[j], ab_w_out[j])
        else:
            h2 = _layer1(h2, p, i, bsz, seq, norm_pre[i], norm_post[i], pe_proj[i], pe_gate[i],
                         cd_w_in[j], cmp_pos_k[j], cmp_pos_v[j], cmp_w1_k[j], cmp_w2_k[j], cmp_w1_v[j], cmp_w2_v[j],
                         s5_a_re[j], s5_a_im[j], s5_log_dt[j], s5_b_re[j], s5_b_im[j], s5_c_re[j], s5_c_im[j],
                         s5_d[j], s5_w_glu[j], cd_w_out[j])
    return h2.reshape(bsz, seq, d)
```

```python
import functools
import math

import numpy as np
import jax
import jax.numpy as jnp
from jax import lax
from jax.experimental import pallas as pl
from jax.experimental.pallas import tpu as pltpu

F32 = jnp.float32
BF16 = jnp.bfloat16
EPS = 1e-6
NEG = -1e30
BIG = 1e30

LANES = 128
CHUNK = 128
RET_HEADS, RET_DK, RET_DV = 4, 128, 256
ML_HEADS, ML_DK, ML_DV = 4, 128, 256
ML_CONV = 4
NSA_HEADS, NSA_GROUPS, NSA_DH = 16, 4, 64
NSA_HPG = NSA_HEADS // NSA_GROUPS
CMP_STRIDE, CMP_LEN, CMP_HIDDEN = 16, 32, 128
SEL_BLOCK, SEL_TOPN, WINDOW = 64, 4, 512
S5_GROUPS, S5_GROUP_CH, S5_STATE = 32, 16, 64
S5_WIDTH = S5_GROUPS * S5_GROUP_CH
S5_NSTATE = S5_GROUPS * S5_STATE

TQ = 256
KCH = 512
SEL_LANE0 = 64
ALI_LANE0 = 96
VMEM_LIMIT = 56 * 1024 * 1024


def _cparams(sem):
    return pltpu.CompilerParams(dimension_semantics=sem, vmem_limit_bytes=VMEM_LIMIT)


def _dot(a, b):
    return jnp.dot(a, b, preferred_element_type=F32)


def _dot_nt(a, b):
    return lax.dot_general(a, b, (((1,), (1,)), ((), ())), preferred_element_type=F32)


def _split3(x):
    hi = x.astype(BF16)
    r1 = x - hi.astype(F32)
    mid = r1.astype(BF16)
    lo = (r1 - mid.astype(F32)).astype(BF16)
    return hi, mid, lo


def _dot_f32_lhs(x, m):
    hi, mid, lo = _split3(x)
    return _dot(hi, m) + _dot(mid, m) + _dot(lo, m)


def _sigmoid(x):
    return 1.0 / (1.0 + jnp.exp(-x))


def _silu(x):
    return x * _sigmoid(x)


def _gelu(x):
    return 0.5 * x * (1.0 + jnp.tanh(math.sqrt(2.0 / math.pi) * (x + 0.044715 * (x * x * x))))


def _rms(x, g):
    return x * lax.rsqrt(jnp.mean(x * x, -1, keepdims=True) + EPS) * g


PROJ_TN = 1024


def _norm_proj_kernel(h_ref, g_ref, wg_ref, wm_ref, om_ref, og_ref):
    ub = _rms(h_ref[...], g_ref[...]).astype(BF16)
    nm = om_ref.shape[1]
    for c0 in range(0, nm, PROJ_TN):
        c1 = min(c0 + PROJ_TN, nm)
        om_ref[:, c0:c1] = _dot_nt(ub, wm_ref[c0:c1, :]).astype(om_ref.dtype)
    og_ref[...] = _dot_nt(ub, wg_ref[...])


def _norm_proj(h2, g, w_gate_t, w_main_t, tm=512):
    n, d = h2.shape
    nm = w_main_t.shape[0]
    ng = w_gate_t.shape[0]
    once = pl.Buffered(1)
    return pl.pallas_call(
        _norm_proj_kernel,
        grid=(n // tm,),
        in_specs=[pl.BlockSpec((tm, d), lambda i: (i, 0)),
                  pl.BlockSpec((1, d), lambda i: (0, 0)),
                  pl.BlockSpec((ng, d), lambda i: (0, 0), pipeline_mode=once),
                  pl.BlockSpec((nm, d), lambda i: (0, 0), pipeline_mode=once)],
        out_specs=[pl.BlockSpec((tm, nm), lambda i: (i, 0)),
                   pl.BlockSpec((tm, ng), lambda i: (i, 0))],
        out_shape=[jax.ShapeDtypeStruct((n, nm), BF16), jax.ShapeDtypeStruct((n, ng), F32)],
        compiler_params=_cparams(("parallel",)),
        name="norm_proj",
    )(h2, g, w_gate_t, w_main_t)


def _out_pe_kernel(gated, *refs):
    if gated:
        ma_ref, mb_ref, sz_ref, h_ref, p_ref, wa_ref, wb_ref, gp_ref, wg_ref, wp_ref, o_ref = refs
        sz = sz_ref[...].astype(F32)
        mb = (mb_ref[...].astype(F32) * _silu(sz)).astype(BF16)
    else:
        ma_ref, mb_ref, h_ref, p_ref, wa_ref, wb_ref, gp_ref, wg_ref, wp_ref, o_ref = refs
        mb = mb_ref[...]
    y = _dot(ma_ref[...], wa_ref[...]) + _dot(mb, wb_ref[...])
    h1 = h_ref[...] + _rms(y, gp_ref[...])
    gate = _sigmoid(_dot(h1.astype(BF16), wg_ref[...]))
    pp = _dot(p_ref[...].astype(BF16), wp_ref[...])
    o_ref[...] = h1 + gate * pp


def _out_pe(mix_a, a_map, ka, mix_b, b_map, kb, sz, sz_map, h2, p4, layer, w_a, w_b, g_post, w_gate, w_pe,
            bsz, seq, tm=512):
    n, d = h2.shape
    pe = p4.shape[-1]
    nt = seq // tm
    row = lambda b, s: (b * nt + s, 0)
    const = lambda b, s: (0, 0)
    gated = sz is not None
    in_specs = [pl.BlockSpec((tm, ka), a_map), pl.BlockSpec((tm, kb), b_map)]
    args = [mix_a, mix_b]
    if gated:
        in_specs.append(pl.BlockSpec((tm, kb), sz_map))
        args.append(sz)
    in_specs += [pl.BlockSpec((tm, d), row), pl.BlockSpec((None, None, tm, pe), lambda b, s: (layer, b, s, 0)),
                 pl.BlockSpec((ka, d), const), pl.BlockSpec((kb, d), const), pl.BlockSpec((1, d), const),
                 pl.BlockSpec((d, d), const), pl.BlockSpec((pe, d), const)]
    args += [h2, p4, w_a, w_b, g_post, w_gate, w_pe]
    return pl.pallas_call(
        functools.partial(_out_pe_kernel, gated),
        grid=(bsz, nt),
        in_specs=in_specs,
        out_specs=pl.BlockSpec((tm, d), row),
        out_shape=jax.ShapeDtypeStruct((n, d), F32),
        compiler_params=_cparams(("parallel", "arbitrary")),
        name="out_pe_gated" if gated else "out_pe",
    )(*args)


_RQ, _RK, _RV, _RG = 0, 512, 1024, 2048
_MQ, _MK, _MV, _MO, _MZ = 3072, 3584, 4096, 5120, 6144
AB_MAIN = 7168


def _head_norm(y, g):
    yc = y - jnp.mean(y, -1, keepdims=True)
    return yc * lax.rsqrt(jnp.mean(yc * yc, -1, keepdims=True) + EPS) * g


def _ab_kernel(pm_ref, pg_ref, cw_ref, cb_ref, gb_ref, rn_ref, mn_ref, mix_ref,
               ext_ref, r_ref, c_ref, n_ref, m_ref):
    L = CHUNK
    c_idx = pl.program_id(1)

    @pl.when(c_idx == 0)
    def _():
        ext_ref[0:8, :] = jnp.zeros((8, ext_ref.shape[1]), F32)
        r_ref[...] = jnp.zeros(r_ref.shape, F32)
        c_ref[...] = jnp.zeros(c_ref.shape, F32)
        n_ref[...] = jnp.zeros(n_ref.shape, F32)
        m_ref[...] = jnp.full(m_ref.shape, NEG, F32)

    row = lax.broadcasted_iota(jnp.int32, (L, L), 0)
    col = lax.broadcasted_iota(jnp.int32, (L, L), 1)
    causal = row >= col
    diff = (row - col).astype(F32)
    tcol = lax.broadcasted_iota(jnp.int32, (L, 1), 0).astype(F32)

    for h in range(RET_HEADS):
        lg = math.log1p(-(2.0 ** (-5.0 - h)))
        scale = RET_DK ** -0.5
        decay = jnp.where(causal, jnp.exp(jnp.maximum(diff, 0.0) * lg), 0.0) * scale
        q = pm_ref[:, _RQ + h * RET_DK:_RQ + (h + 1) * RET_DK]
        k = pm_ref[:, _RK + h * RET_DK:_RK + (h + 1) * RET_DK]
        v = pm_ref[:, _RV + h * RET_DV:_RV + (h + 1) * RET_DV]
        sc = _dot_nt(q, k) * decay
        xi = jnp.exp((tcol + 1.0) * lg) * scale
        qx = (q.astype(F32) * xi).astype(BF16)
        r_prev = r_ref[h]
        y = _dot(sc.astype(BF16), v) + _dot(qx, r_prev.astype(BF16))
        zeta = jnp.exp((L - 1.0 - tcol) * lg)
        kz_t = (k.astype(F32) * zeta).T.astype(BF16)
        r_ref[h] = r_prev * math.exp(L * lg) + _dot(kz_t, v)
        g = pm_ref[:, _RG + h * RET_DV:_RG + (h + 1) * RET_DV].astype(F32)
        out = _head_norm(y, rn_ref[:, h * RET_DV:(h + 1) * RET_DV]) * _silu(g)
        mix_ref[:, h * RET_DV:(h + 1) * RET_DV] = out.astype(mix_ref.dtype)

    nqk = 2 * ML_HEADS * ML_DK
    x = pm_ref[:, _MQ:_MQ + nqk].astype(F32)
    ext_ref[8:8 + L, :] = x
    acc = cb_ref[...] + cw_ref[ML_CONV - 1:ML_CONV, :] * x
    for j in range(ML_CONV - 1):
        acc = acc + cw_ref[j:j + 1, :] * ext_ref[pl.ds(8 - (ML_CONV - 1) + j, L), :]
    ext_ref[0:8, :] = x[L - 8:L, :]
    qk = _silu(acc)

    gt = (pg_ref[...] + gb_ref[...]).T
    g8 = gt[0:8, :]
    logf8 = jnp.minimum(g8, 0.0) - jnp.log(1.0 + jnp.exp(-jnp.abs(g8)))
    triu = jnp.where(row <= col, 1.0, 0.0).astype(BF16)
    bcum8 = _dot_f32_lhs(logf8, triu)
    b_t = pltpu.roll(bcum8, 4, axis=0)
    b_last = b_t[:, L - 1:L]
    a_t = b_last - b_t + g8
    mu = jnp.max(a_t, axis=1, keepdims=True)
    w_t = jnp.exp(a_t - mu)
    r_t = g8 - b_t
    zpad = jnp.zeros((L - 8, L), F32)
    bcol = jnp.concatenate([b_t, zpad], axis=0).T
    wcol = jnp.concatenate([w_t, zpad], axis=0).T
    m_prev = m_ref[...][:, 0:1]
    m_new = jnp.maximum(b_last + m_prev, mu)
    sp = jnp.exp(b_last + m_prev - m_new)
    scn = jnp.exp(mu - m_new)
    m_ref[...] = jnp.broadcast_to(m_new, m_ref.shape)

    for h in range(ML_HEADS):
        bc = bcol[:, h:h + 1]
        log_d = jnp.where(causal, bc + r_t[h:h + 1, :], NEG)
        inter = bc + m_prev[h:h + 1, :]
        m_t = jnp.maximum(inter, jnp.max(log_d, axis=1, keepdims=True))
        dmat = jnp.exp(log_d - m_t)
        w_int = jnp.exp(inter - m_t)
        qh = qk[:, h * ML_DK:(h + 1) * ML_DK]
        kh = qk[:, ML_HEADS * ML_DK + h * ML_DK:ML_HEADS * ML_DK + (h + 1) * ML_DK] * (ML_DK ** -0.5)
        qb = qh.astype(BF16)
        v = pm_ref[:, _MV + h * ML_DV:_MV + (h + 1) * ML_DV]
        s = _dot_nt(qb, kh.astype(BF16)) * dmat
        c_prev = c_ref[h]
        n_prev = n_ref[h:h + 1, :]
        num = _dot(s.astype(BF16), v) + w_int * _dot(qb, c_prev.astype(BF16))
        den = jnp.sum(s, axis=1, keepdims=True) + w_int * jnp.sum(qh * n_prev, axis=1, keepdims=True)
        den = jnp.maximum(jnp.abs(den), jnp.exp(-m_t))
        hcell = num * (1.0 / den)
        o = pm_ref[:, _MO + h * ML_DV:_MO + (h + 1) * ML_DV].astype(F32)
        y = _sigmoid(o) * hcell
        z = pm_ref[:, _MZ + h * ML_DV:_MZ + (h + 1) * ML_DV].astype(F32)
        out = _head_norm(y, mn_ref[:, h * ML_DV:(h + 1) * ML_DV]) * _silu(z)
        mix_ref[:, RET_HEADS * RET_DV + h * ML_DV:RET_HEADS * RET_DV + (h + 1) * ML_DV] = out.astype(mix_ref.dtype)
        kw = kh * wcol[:, h:h + 1]
        kv = _dot(kw.T.astype(BF16), v)
        ksum = jnp.sum(kw, axis=0, keepdims=True)
        sp_h = sp[h:h + 1, :]
        sc_h = scn[h:h + 1, :]
        c_ref[h] = sp_h * c_prev + sc_h * kv
        n_ref[h:h + 1, :] = sp_h * n_prev + sc_h * ksum


def _ab_mixer(pm, pg, conv_w, conv_b, gate_b, ret_norm, ml_norm, bsz, seq):
    n = pm.shape[0]
    nc = seq // CHUNK
    nqk = 2 * ML_HEADS * ML_DK
    row = lambda b, c: (b * nc + c, 0)
    const = lambda b, c: (0, 0)
    nmix = RET_HEADS * RET_DV + ML_HEADS * ML_DV
    return pl.pallas_call(
        _ab_kernel,
        grid=(bsz, nc),
        in_specs=[pl.BlockSpec((CHUNK, AB_MAIN), row), pl.BlockSpec((CHUNK, LANES), row),
                  pl.BlockSpec((ML_CONV, nqk), const), pl.BlockSpec((1, nqk), const),
                  pl.BlockSpec((1, LANES), const),
                  pl.BlockSpec((1, RET_HEADS * RET_DV), const), pl.BlockSpec((1, ML_HEADS * ML_DV), const)],
        out_specs=pl.BlockSpec((CHUNK, nmix), row),
        out_shape=jax.ShapeDtypeStruct((n, nmix), BF16),
        scratch_shapes=[pltpu.VMEM((CHUNK + 8, nqk), F32),
                        pltpu.VMEM((RET_HEADS, RET_DK, RET_DV), F32),
                        pltpu.VMEM((ML_HEADS, ML_DK, ML_DV), F32),
                        pltpu.VMEM((8, ML_DK), F32),
                        pltpu.VMEM((8, LANES), F32)],
        compiler_params=_cparams(("parallel", "arbitrary")),
        name="ab_mixer",
    )(pm, pg, conv_w, conv_b, gate_b, ret_norm, ml_norm)


_QNZ0 = 0
_KVC0 = NSA_HEADS
_KVS0 = _KVC0 + NSA_GROUPS
_KVW0 = _KVS0 + NSA_GROUPS
_SU0 = (_KVW0 + NSA_GROUPS) * LANES
_SZ0 = _SU0 + S5_WIDTH
CD_MAIN = _SZ0 + S5_WIDTH
SLOTW = NSA_GROUPS * LANES
VT_ROWS = NSA_DH + 16
LOG2E = math.log2(math.e)


def _compress_kernel(x_ref, pk_ref, pv_ref, w1k_ref, w1v_ref, w2k_ref, w2vt_ref, ekc_ref,
                     ko_ref, vot_ref, xs_ref):
    seq = x_ref.shape[0]
    nb = seq // CMP_STRIDE
    for g in range(NSA_GROUPS):
        xs_ref[...] = x_ref[:, g * LANES:(g + 1) * LANES].astype(F32)
        acc = [jnp.zeros((nb, CMP_HIDDEN), F32) for _ in range(4)]
        for r in range(CMP_STRIDE):
            xr = xs_ref[pl.ds(r, nb, stride=CMP_STRIDE), :]
            for kind, (p_ref, w1_ref) in enumerate(((pk_ref, w1k_ref), (pv_ref, w1v_ref))):
                acc[2 * kind] = acc[2 * kind] + _dot((xr + p_ref[r:r + 1, :]).astype(BF16), w1_ref[r])
                acc[2 * kind + 1] = acc[2 * kind + 1] + _dot(
                    (xr + p_ref[CMP_STRIDE + r:CMP_STRIDE + r + 1, :]).astype(BF16), w1_ref[CMP_STRIDE + r])
        hk = _gelu(acc[0] + pltpu.roll(acc[1], nb - 1, axis=0)).astype(BF16)
        hv = _gelu(acc[2] + pltpu.roll(acc[3], nb - 1, axis=0)).astype(BF16)
        ko_ref[g] = (_dot(hk, w2k_ref[...]) + ekc_ref[...]).astype(ko_ref.dtype)
        vot_ref[g] = _dot_nt(w2vt_ref[...], hv).astype(vot_ref.dtype)


def _compress(pm3, pos_k, pos_v, w1k, w1v, w2k, w2vt, ekc):
    bsz, seq, _ = pm3.shape
    nb = seq // CMP_STRIDE
    c3 = lambda b: (0, 0, 0)
    c2 = lambda b: (0, 0)
    return pl.pallas_call(
        _compress_kernel,
        grid=(bsz,),
        in_specs=[pl.BlockSpec((None, seq, SLOTW), lambda b: (b, 0, _KVC0 * LANES // SLOTW)),
                  pl.BlockSpec((CMP_LEN, LANES), c2), pl.BlockSpec((CMP_LEN, LANES), c2),
                  pl.BlockSpec((CMP_LEN, LANES, CMP_HIDDEN), c3), pl.BlockSpec((CMP_LEN, LANES, CMP_HIDDEN), c3),
                  pl.BlockSpec((CMP_HIDDEN, LANES), c2), pl.BlockSpec((LANES, CMP_HIDDEN), c2),
                  pl.BlockSpec((nb, LANES), c2)],
        out_specs=[pl.BlockSpec((None, NSA_GROUPS, nb, LANES), lambda b: (b, 0, 0, 0)),
                   pl.BlockSpec((None, NSA_GROUPS, LANES, nb), lambda b: (b, 0, 0, 0))],
        out_shape=[jax.ShapeDtypeStruct((bsz, NSA_GROUPS, nb, LANES), BF16),
                   jax.ShapeDtypeStruct((bsz, NSA_GROUPS, LANES, nb), BF16)],
        scratch_shapes=[pltpu.VMEM((seq, LANES), F32)],
        compiler_params=_cparams(("parallel",)),
        name="nsa_compress",
    )(pm3, pos_k, pos_v, w1k, w1v, w2k, w2vt, ekc)


def _nsa_kernel(q_ref, kc_ref, vct_ref, kvs_ref, kvw_ref, pg_ref,
                ek_ref, al_ref, ovt_ref, y_ref,
                ksa_ref, vsat_ref, kwa_ref, vwat_ref, qa_scr, part_scr, gzs_scr, sel_smem):
    seq = kvs_ref.shape[0]
    g_idx = pl.program_id(1)
    qi = pl.program_id(2)
    wtiles = WINDOW // LANES

    @pl.when(qi == 0)
    def _():
        ek = ek_ref[...]
        lane_k = lax.broadcasted_iota(jnp.int32, (seq, LANES), 1)
        zero = jnp.zeros((seq, LANES), BF16)
        ksa_ref[0:seq, :] = jnp.where(lane_k < NSA_DH, kvs_ref[...], zero) + ek
        lane_d = lax.broadcasted_iota(jnp.int32, (LANES, LANES), 1)
        ksa_ref[seq:seq + LANES, :] = jnp.where((lane_d >= SEL_LANE0) & (lane_d < SEL_LANE0 + seq // SEL_BLOCK),
                                                1.0, 0.0).astype(BF16)
        vsat_ref[seq // LANES] = jnp.zeros((VT_ROWS, LANES), BF16)
        ek_w = jnp.where(lane_k >= ALI_LANE0, ek, zero)
        kwa_ref[0:WINDOW, :] = jnp.zeros((WINDOW, LANES), BF16)
        kwa_ref[WINDOW:WINDOW + seq, :] = jnp.where(lane_k < NSA_DH, kvw_ref[...], zero) + ek_w
        ones_rows = jnp.where(lax.broadcasted_iota(jnp.int32, (VT_ROWS - NSA_DH, LANES), 0) == 0, 1.0, 0.0)
        for i in range(wtiles):
            vwat_ref[i] = jnp.zeros((VT_ROWS, LANES), BF16)
        for i in range(seq // LANES):
            rws = slice(i * LANES, (i + 1) * LANES)
            for src_ref, dst_ref, j in ((kvs_ref, vsat_ref, i), (kvw_ref, vwat_ref, wtiles + i)):
                v_t = src_ref[rws, :].astype(F32).T[NSA_DH:2 * NSA_DH, :]
                dst_ref[j] = jnp.concatenate([v_t, ones_rows], axis=0).astype(BF16)

    nq = seq // TQ
    refs = (q_ref, kc_ref, vct_ref, pg_ref, al_ref, ovt_ref, y_ref, ksa_ref, vsat_ref, kwa_ref, vwat_ref,
            qa_scr, part_scr, gzs_scr, sel_smem)
    for step in range(nq + 1):
        pl.when(qi == step)(functools.partial(_nsa_step, step, nq, seq // SEL_BLOCK, g_idx, refs))


SEL_DYN = 2
SEL_CNT = 7


def _sel_compactable(qi):
    return (qi + 1) * TQ // LANES > 1 + SEL_DYN + TQ // LANES


def _nsa_step(step, nq, nsel, g_idx, refs):
    def run(compact):
        stages = []
        if step < nq:
            stages.append(_nsa_front(nsel, step, g_idx, refs))
        if step > 0:
            stages.append(_nsa_back(step - 1, compact, refs))
        while stages:
            for st in list(stages):
                if next(st, "done") == "done":
                    stages.remove(st)

    if step > 0 and _sel_compactable(step - 1):
        fits = refs[-1][(step - 1) & 1, SEL_CNT] <= SEL_DYN
        pl.when(fits)(functools.partial(run, True))
        pl.when(jnp.logical_not(fits))(functools.partial(run, False))
    else:
        run(False)


def _tile_iotas():
    rows = NSA_HPG * TQ
    jrow = lax.broadcasted_iota(jnp.int32, (LANES, rows), 0)
    tq = lax.broadcasted_iota(jnp.int32, (LANES, rows), 1) & (TQ - 1)
    return jrow, tq, jrow - tq


def _nsa_front(nsel, qi, g_idx, refs):
    (q_ref, kc_ref, vct_ref, pg_ref, al_ref, ovt_ref, _, _, _, kwa_ref, vwat_ref,
     qa_scr, part_scr, gzs_scr, sel_smem) = refs
    hp = NSA_HPG
    span = WINDOW + TQ
    t0 = qi * TQ
    tw = t0
    early = t0 < WINDOW
    par = qi & 1
    jrow, tq, kmt = _tile_iotas()

    qnz = [q_ref[:, hh * LANES:(hh + 1) * LANES].astype(F32) for hh in range(hp)]
    lane_q = lax.broadcasted_iota(jnp.int32, (TQ, LANES), 1)
    qal = jnp.concatenate([jnp.where(lane_q < NSA_DH, qnz[hh], al_ref[pl.ds(g_idx * hp + hh, 1), :])
                           for hh in range(hp)], axis=0)
    qalb = qal.astype(BF16)
    s_c = _dot_nt(kc_ref[...], qalb)
    wfirst = max(0, WINDOW - t0) // LANES
    s_w = _dot_nt(kwa_ref[tw + wfirst * LANES:tw + span, :], qalb)
    yield

    s = jnp.where(tq - jrow * CMP_STRIDE >= (CMP_LEN - 1) - t0, s_c, NEG)
    e = jnp.exp2(s - jnp.max(s, axis=0, keepdims=True))
    inv = 1.0 / jnp.sum(e, axis=0, keepdims=True)
    if early:
        inv = jnp.where(t0 + tq[0:1, :] >= CMP_LEN - 1, inv, 0.0)
    p = e * inv
    oc_t = _dot(vct_ref[...], p.astype(BF16))
    psum_t = p[:, 0:TQ]
    for hh in range(1, hp):
        psum_t = psum_t + p[:, hh * TQ:(hh + 1) * TQ]
    p_hi = psum_t.astype(BF16)
    p_lo = (psum_t - p_hi.astype(F32)).astype(BF16)
    imp_t = _dot(ovt_ref[...], p_hi) + _dot(ovt_ref[...], p_lo)
    yield

    blocks = []
    for i in range(wfirst, span // LANES):
        blk = s_w[(i - wfirst) * LANES:(i - wfirst + 1) * LANES, :]
        if i * LANES < TQ:
            blk = jnp.where(kmt > -i * LANES, blk, NEG)
        if (i + 1) * LANES > WINDOW:
            blk = jnp.where(kmt <= WINDOW - i * LANES, blk, NEG)
        blocks.append(blk)
    s = jnp.concatenate(blocks, axis=0)
    p = jnp.exp2(s - jnp.max(s, axis=0, keepdims=True)).astype(BF16)
    vw_t = jnp.concatenate([vwat_ref[qi * (TQ // LANES) + i] for i in range(wfirst, span // LANES)], axis=1)
    accw_t = _dot(vw_t, p)
    yield

    score = imp_t[SEL_LANE0:SEL_LANE0 + nsel, :]
    nrow = lax.broadcasted_iota(jnp.int32, (nsel, TQ), 0)
    own = (t0 + lax.broadcasted_iota(jnp.int32, (nsel, TQ), 1)) >> int(math.log2(SEL_BLOCK))
    valid = nrow <= own
    forced = (nrow == 0) | (nrow == own)
    score = jnp.where(forced, BIG, jnp.where(valid, score, -1.0))
    chosen = jnp.zeros((nsel, TQ), F32)
    nrow_f = nrow.astype(F32)
    for _ in range(SEL_TOPN):
        mx = jnp.max(score, axis=0, keepdims=True)
        first = jnp.min(jnp.where(score == mx, nrow_f, 4.0 * LANES), axis=0, keepdims=True)
        hit = nrow_f == first
        chosen = jnp.where(hit, 1.0, chosen)
        score = jnp.where(hit, -3.0, score)
    mask_t = jnp.where(chosen > 0.0, jnp.where(valid, 0.0, NEG), NEG)
    if _sel_compactable(qi):
        used_blk = jnp.max(jnp.where(mask_t == 0.0, 1.0, 0.0), axis=1, keepdims=True)
        blk_per_tile = LANES // SEL_BLOCK
        for slot in range(SEL_DYN):
            sel_smem[par, slot] = jnp.int32(nsel // blk_per_tile)
        cnt = jnp.int32(0)
        for i in range(1, t0 // LANES):
            used = jnp.max(used_blk[i * blk_per_tile:(i + 1) * blk_per_tile, :]) > 0.0

            @pl.when(used & (cnt < SEL_DYN))
            def _(i=i, cnt=cnt):
                sel_smem[par, cnt] = jnp.int32(i)

            cnt = cnt + used.astype(jnp.int32)
        sel_smem[par, SEL_CNT] = cnt
    maskcols = jnp.concatenate([jnp.zeros((SEL_LANE0, TQ), F32), mask_t,
                                jnp.zeros((LANES - SEL_LANE0 - nsel, TQ), F32)], axis=0).T

    qa_scr[par] = (qal + jnp.concatenate([maskcols] * hp, axis=0)).astype(BF16)

    gt_t = _sigmoid(pg_ref[...]).T
    for hh in range(hp):
        g_c, g_s, g_w = [gt_t[j * hp + hh:j * hp + hh + 1, :] for j in range(3)]
        hcols = slice(hh * TQ, (hh + 1) * TQ)
        aw_t = accw_t[:, hcols]
        gz = _silu(qnz[hh].T[NSA_DH:2 * NSA_DH, :])
        o = g_c * oc_t[0:NSA_DH, hcols] + (g_w * (1.0 / aw_t[NSA_DH:NSA_DH + 1, :])) * aw_t[0:NSA_DH, :]
        part_scr[par, hh * NSA_DH:(hh + 1) * NSA_DH, :] = o * gz
        gzs_scr[par, hh * NSA_DH:(hh + 1) * NSA_DH, :] = g_s * gz


def _nsa_back(qi, compact, refs):
    _, _, _, _, _, _, y_ref, ksa_ref, vsat_ref, _, _, qa_scr, part_scr, gzs_scr, sel_smem = refs
    hp = NSA_HPG
    t0 = qi * TQ
    par = qi & 1
    _, _, kmt = _tile_iotas()
    qa = qa_scr[par]
    ntiles = (t0 + TQ) // LANES
    if compact:
        tiles = [0] + [sel_smem[par, j] for j in range(SEL_DYN)] + list(range(t0 // LANES, ntiles))
    else:
        tiles = list(range(ntiles))
    per = len(tiles) if compact else KCH // LANES
    chunks = [tiles[i:i + per] for i in range(0, len(tiles), per)]
    nk = len(chunks)

    def k_tile(i):
        if isinstance(i, int):
            return ksa_ref[i * LANES:(i + 1) * LANES, :]
        return ksa_ref[pl.ds(pl.multiple_of(i * LANES, LANES), LANES), :]

    def scores(c):
        s = _dot_nt(jnp.concatenate([k_tile(i) for i in chunks[c]], axis=0), qa)
        if any(isinstance(i, int) and (i + 1) * LANES > t0 for i in chunks[c]):
            s = jnp.concatenate(
                [jnp.where(kmt <= t0 - i * LANES, s[n * LANES:(n + 1) * LANES, :], NEG)
                 if isinstance(i, int) and (i + 1) * LANES > t0 else s[n * LANES:(n + 1) * LANES, :]
                 for n, i in enumerate(chunks[c])], axis=0)
        return s

    def weighted(c, s):
        mc = jnp.max(s, axis=0, keepdims=True)
        v_t = jnp.concatenate([vsat_ref[i] for i in chunks[c]], axis=1)
        return mc, _dot(v_t, jnp.exp2(s - mc).astype(BF16))

    parts = []
    s_next = scores(0)
    for c in range(nk):
        s_cur = s_next
        if c + 1 < nk:
            s_next = scores(c + 1)
        yield
        parts.append(weighted(c, s_cur))
    yield
    m = parts[0][0]
    for mc, _ in parts[1:]:
        m = jnp.maximum(m, mc)
    accs_t = None
    for mc, d in parts:
        d = d if nk == 1 else jnp.exp2(mc - m) * d
        accs_t = d if accs_t is None else accs_t + d

    outs = []
    for hh in range(hp):
        as_t = accs_t[:, hh * TQ:(hh + 1) * TQ]
        rws = slice(hh * NSA_DH, (hh + 1) * NSA_DH)
        outs.append(part_scr[par, rws, :] + (gzs_scr[par, rws, :] * (1.0 / as_t[NSA_DH:NSA_DH + 1, :])) * as_t[0:NSA_DH, :])
    for pr in range(hp // 2):
        packed = jnp.concatenate([outs[2 * pr], outs[2 * pr + 1]], axis=0).T
        y_ref[:, pr * LANES:(pr + 1) * LANES] = packed.astype(y_ref.dtype)


def _nsa(pm3, pg3, kcc, vct, ek, al, ovt):
    bsz, seq, _ = pm3.shape
    nq = seq // TQ
    gw = NSA_HPG * NSA_DH
    kv = lambda tile0: pl.BlockSpec((None, seq, LANES), lambda b, g, q: (b, 0, tile0 + g))
    c2 = lambda b, g, q: (0, 0)
    front = lambda b, g, q: (b, jnp.minimum(q, nq - 1), g)
    back = lambda b, g, q: (b, jnp.maximum(q - 1, 0), g)
    return pl.pallas_call(
        _nsa_kernel,
        grid=(bsz, NSA_GROUPS, nq + 1),
        in_specs=[pl.BlockSpec((None, TQ, NSA_HPG * LANES), front),
                  pl.BlockSpec((None, None, LANES, LANES), lambda b, g, q: (b, g, 0, 0)),
                  pl.BlockSpec((None, None, LANES, LANES), lambda b, g, q: (b, g, 0, 0)),
                  kv(_KVS0), kv(_KVW0),
                  pl.BlockSpec((None, TQ, LANES), front),
                  pl.BlockSpec((seq, LANES), c2),
                  pl.BlockSpec((NSA_HEADS, LANES), c2),
                  pl.BlockSpec((LANES, LANES), c2)],
        out_specs=pl.BlockSpec((None, TQ, gw), back),
        out_shape=jax.ShapeDtypeStruct((bsz, seq, NSA_HEADS * NSA_DH), BF16),
        scratch_shapes=[pltpu.VMEM((seq + LANES, LANES), BF16),
                        pltpu.VMEM((seq // LANES + 1, VT_ROWS, LANES), BF16),
                        pltpu.VMEM((seq + WINDOW, LANES), BF16),
                        pltpu.VMEM(((seq + WINDOW) // LANES, VT_ROWS, LANES), BF16),
                        pltpu.VMEM((2, NSA_HPG * TQ, LANES), BF16),
                        pltpu.VMEM((2, gw, TQ), F32), pltpu.VMEM((2, gw, TQ), F32),
                        pltpu.SMEM((2, SEL_CNT + 1), jnp.int32)],
        compiler_params=_cparams(("parallel", "arbitrary", "arbitrary")),
        name="nsa_attn",
    )(pm3, kcc, vct, pm3, pm3, pg3, ek, al, ovt)


def _nsa_tables(seq):
    t = np.arange(seq)
    ek = np.zeros((seq, LANES), np.float32)
    ek[t, SEL_LANE0 + t // SEL_BLOCK] = 1.0
    for c in range(3):
        ek[:, ALI_LANE0 + c] = t // SEL_BLOCK
        ek[:, ALI_LANE0 + 3 + c] = t % SEL_BLOCK
    slopes = jnp.exp2(-8.0 * jnp.arange(1, NSA_HEADS + 1, dtype=F32) / NSA_HEADS) * LOG2E
    s_hi, s_mid, s_lo = _split3(slopes)
    parts = [p.astype(F32) for p in (s_hi, s_mid, s_lo)]
    al = jnp.pad(jnp.stack([p * SEL_BLOCK for p in parts] + parts, axis=1),
                 ((0, 0), (ALI_LANE0, LANES - ALI_LANE0 - 2 * len(parts))))
    nb = seq // CMP_STRIDE
    cstart = np.arange(nb) * CMP_STRIDE
    cend = cstart + CMP_LEN - 1
    ekc = np.zeros((nb, LANES), np.float32)
    for c in range(3):
        ekc[:, ALI_LANE0 + c] = cend // SEL_BLOCK
        ekc[:, ALI_LANE0 + 3 + c] = cend % SEL_BLOCK
    sel = np.arange(seq // SEL_BLOCK)
    ovt = np.zeros((LANES, max(nb, LANES)), np.float32)
    ovl = (cstart[:, None] < (sel[None, :] + 1) * SEL_BLOCK) & (cstart[:, None] + CMP_LEN > sel[None, :] * SEL_BLOCK)
    ovl[nb - 1, :] = False
    ovt[SEL_LANE0:SEL_LANE0 + len(sel), :nb] = ovl.T
    return jnp.asarray(ek, BF16), al, jnp.asarray(ekc), jnp.asarray(ovt, BF16)


def _s5_disc_kernel(are_ref, aim_ref, ldt_ref, bre_ref, bim_ref, abr_ref, abi_ref, bbr_ref, bbi_ref):
    a_re = are_ref[...]
    a_im = aim_ref[...]
    dt = jnp.exp(ldt_ref[...])
    er = jnp.exp(a_re * dt)
    abar_re = er * jnp.cos(a_im * dt)
    abar_im = er * jnp.sin(a_im * dt)
    lam2 = a_re * a_re + a_im * a_im
    cr = ((abar_re - 1.0) * a_re + abar_im * a_im) / lam2
    ci = (abar_im * a_re - (abar_re - 1.0) * a_im) / lam2
    b_re = bre_ref[...]
    b_im = bim_ref[...]
    abr_ref[...] = abar_re
    abi_ref[...] = abar_im
    bbr_ref[...] = cr * b_re - ci * b_im
    bbi_ref[...] = cr * b_im + ci * b_re


def _s5_disc(a_re_rep, a_im_rep, ldt_rep, b_re_t, b_im_t):
    shp = jax.ShapeDtypeStruct(a_re_rep.shape, F32)
    return pl.pallas_call(_s5_disc_kernel, out_shape=[shp] * 4, name="s5_disc")(
        a_re_rep, a_im_rep, ldt_rep, b_re_t, b_im_t)


S5_SLABG = 4
S5_TT = 64
S5_PITCH = S5_TT + 4
S5_DIAG = 256
S5_NSTATE_PER_DIAG = S5_DIAG // S5_GROUP_CH * S5_STATE


def _s5_kernel(u_ref, bre_ref, bim_ref, ar_ref, ai_ref, cre_ref, cim_ref, d_ref, wg_ref, o_ref,
               xr_ref, xi_ref, sr_ref, si_ref):
    nb, tt, wd = u_ref.shape
    pitch = sr_ref.shape[1] // nb
    gcols = S5_SLABG * LANES
    ngroups = S5_NSTATE // gcols

    @pl.when(pl.program_id(0) == 0)
    def _():
        xr_ref[...] = jnp.zeros(xr_ref.shape, F32)
        xi_ref[...] = jnp.zeros(xi_ref.shape, F32)

    u = u_ref[...].reshape(nb * tt, wd)

    chan = lambda kg: slice(kg * gcols // S5_NSTATE_PER_DIAG * S5_DIAG,
                            (kg * gcols // S5_NSTATE_PER_DIAG + 1) * S5_DIAG)
    for kg in range(ngroups):
        cols = slice(kg * gcols, (kg + 1) * gcols)
        br = _dot(u[:, chan(kg)], bre_ref[chan(kg), cols])
        bi = _dot(u[:, chan(kg)], bim_ref[chan(kg), cols])
        for j in range(S5_SLABG):
            for b in range(nb):
                sr_ref[kg * S5_SLABG + j, b * pitch:b * pitch + tt, :] = br[b * tt:(b + 1) * tt, j * LANES:(j + 1) * LANES]
                si_ref[kg * S5_SLABG + j, b * pitch:b * pitch + tt, :] = bi[b * tt:(b + 1) * tt, j * LANES:(j + 1) * LANES]

    for kg in range(ngroups):
        cols = slice(kg * gcols, (kg + 1) * gcols)
        slabs = range(kg * S5_SLABG, (kg + 1) * S5_SLABG)
        ar = ar_ref[:, cols]
        ai = ai_ref[:, cols]

        def body(t, carry, slabs=slabs, ar=ar, ai=ai):
            xr, xi = carry
            rows = pl.ds(t, nb, stride=pitch)
            nxr = ar * xr - ai * xi + jnp.concatenate([sr_ref[k, rows, :] for k in slabs], axis=1)
            nxi = ar * xi + ai * xr + jnp.concatenate([si_ref[k, rows, :] for k in slabs], axis=1)
            for j, k in enumerate(slabs):
                sr_ref[k, rows, :] = nxr[:, j * LANES:(j + 1) * LANES]
                si_ref[k, rows, :] = nxi[:, j * LANES:(j + 1) * LANES]
            return nxr, nxi

        xr, xi = lax.fori_loop(0, tt, body, (xr_ref[:, cols], xi_ref[:, cols]), unroll=4)
        xr_ref[:, cols] = xr
        xi_ref[:, cols] = xi

    y_blocks = [jnp.zeros((nb * tt, S5_DIAG), F32) for _ in range(wd // S5_DIAG)]
    for kg in range(ngroups):
        cols = slice(kg * gcols, (kg + 1) * gcols)
        gather = lambda ref: jnp.concatenate(
            [jnp.concatenate([ref[kg * S5_SLABG + j, b * pitch:b * pitch + tt, :] for j in range(S5_SLABG)], axis=1)
             for b in range(nb)], axis=0).astype(BF16)
        blk = chan(kg).start // S5_DIAG
        y_blocks[blk] = (y_blocks[blk] + _dot(gather(sr_ref), cre_ref[cols, chan(kg)])
                         - _dot(gather(si_ref), cim_ref[cols, chan(kg)]))
    y = jnp.concatenate(y_blocks, axis=1)
    y = _gelu(y + d_ref[...] * u.astype(F32))
    z = _dot(y.astype(BF16), wg_ref[...])
    o = z[:, :S5_WIDTH] * _sigmoid(z[:, S5_WIDTH:])
    o_ref[...] = o.reshape(nb, tt, wd).astype(o_ref.dtype)


def _s5(pm3, bbd_re, bbd_im, ar8, ai8, cbd_re, cbd_im, d_row, w_glu):
    bsz, seq, _ = pm3.shape
    tt = S5_TT
    c2 = lambda i: (0, 0)
    nslab = S5_NSTATE // LANES
    return pl.pallas_call(
        _s5_kernel,
        grid=(seq // tt,),
        in_specs=[pl.BlockSpec((bsz, tt, S5_WIDTH), lambda i: (0, i, _SU0 // S5_WIDTH)),
                  pl.BlockSpec((S5_WIDTH, S5_NSTATE), c2), pl.BlockSpec((S5_WIDTH, S5_NSTATE), c2),
                  pl.BlockSpec((8, S5_NSTATE), c2), pl.BlockSpec((8, S5_NSTATE), c2),
                  pl.BlockSpec((S5_NSTATE, S5_WIDTH), c2), pl.BlockSpec((S5_NSTATE, S5_WIDTH), c2),
                  pl.BlockSpec((1, S5_WIDTH), c2), pl.BlockSpec((S5_WIDTH, 2 * S5_WIDTH), c2)],
        out_specs=pl.BlockSpec((bsz, tt, S5_WIDTH), lambda i: (0, i, 0)),
        out_shape=jax.ShapeDtypeStruct((bsz, seq, S5_WIDTH), BF16),
        scratch_shapes=[pltpu.VMEM((8, S5_NSTATE), F32), pltpu.VMEM((8, S5_NSTATE), F32),
                        pltpu.VMEM((nslab, bsz * S5_PITCH, LANES), F32),
                        pltpu.VMEM((nslab, bsz * S5_PITCH, LANES), F32)],
        compiler_params=_cparams(("arbitrary",)),
        name="s5_scan",
    )(pm3, bbd_re, bbd_im, ar8, ai8, cbd_re, cbd_im, d_row, w_glu)


def _pad_cols(w, n):
    return jnp.pad(w, ((0, 0), (0, n - w.shape[1])))


def _relayout_kernel(pieces, w_ref, o_ref):
    o0 = 0
    for s0, n, sc in pieces:
        v = w_ref[s0:s0 + n, :]
        o_ref[o0:o0 + n, :] = (v if sc is None else v * sc).astype(o_ref.dtype)
        o0 += n


def _relayout_rows(w_t, pieces, tn=256):
    nin, d = w_t.shape
    nout = sum(n for _, n, _ in pieces)
    return pl.pallas_call(
        functools.partial(_relayout_kernel, pieces),
        grid=(d // tn,),
        in_specs=[pl.BlockSpec((nin, tn), lambda i: (0, i))],
        out_specs=pl.BlockSpec((nout, tn), lambda i: (0, i)),
        out_shape=jax.ShapeDtypeStruct((nout, d), BF16),
        compiler_params=_cparams(("parallel",)),
        name="relayout_rows",
    )(w_t)


def _cd_proj_weights(w_in):
    w_t = w_in.T
    d = w_in.shape[0]
    nw, nkv = NSA_HEADS * NSA_DH, NSA_GROUPS * NSA_DH
    offs = [int(o) for o in np.cumsum((0, nw, nkv, nkv, nkv, nkv, nkv, nkv, 3 * NSA_HEADS, nw, S5_WIDTH, S5_WIDTH))]
    def pair(ia, ib, scale=None):
        return [pc for c in range(0, offs[ia + 1] - offs[ia], NSA_DH)
                for pc in ((offs[ia] + c, NSA_DH, scale), (offs[ib] + c, NSA_DH, None))]
    w_main = _relayout_rows(w_t, pair(0, 8, NSA_DH ** -0.5 * LOG2E) + pair(1, 2) + pair(3, 4) + pair(5, 6)
                            + [(offs[9], S5_WIDTH, None), (offs[10], S5_WIDTH, None)])
    w_gate = w_t[offs[7]:offs[8]].reshape(NSA_GROUPS, NSA_HPG, 3, d).transpose(0, 2, 1, 3)
    w_gate = jnp.pad(w_gate.reshape(NSA_GROUPS, 3 * NSA_HPG, d), ((0, 0), (0, LANES - 3 * NSA_HPG), (0, 0)))
    return w_gate.reshape(NSA_GROUPS * LANES, d).astype(BF16), w_main


def _layer0(h2, p4, layer, bsz, seq, norm_pre, norm_post, pe_proj, pe_gate,
            w_in, ret_norm, conv_w, conv_b, gate_b, ml_norm, w_out):
    d = h2.shape[1]
    g0 = 5120
    after = g0 + 2 * ML_HEADS
    w_t = w_in.T
    w_main = _relayout_rows(w_t, [(0, g0, None), (after, w_in.shape[1] - after, None)])
    w_gate = jnp.pad(w_t[g0:after], ((0, LANES - 2 * ML_HEADS), (0, 0))).astype(BF16)
    pm, pg = _norm_proj(h2, norm_pre.reshape(1, d), w_gate, w_main)
    mix = _ab_mixer(pm, pg, conv_w, conv_b.reshape(1, -1), _pad_cols(gate_b.reshape(1, -1), LANES),
                    ret_norm.reshape(1, -1), ml_norm.reshape(1, -1), bsz, seq)
    ka = RET_HEADS * RET_DV
    nt = seq // 512
    return _out_pe(mix, lambda b, s: (b * nt + s, 0), ka, mix, lambda b, s: (b * nt + s, 1), ML_HEADS * ML_DV,
                   None, None, h2, p4, layer, w_out[:ka].astype(BF16), w_out[ka:].astype(BF16),
                   norm_post.reshape(1, d), pe_gate.astype(BF16), pe_proj.astype(BF16), bsz, seq)


def _layer1(h2, p4, layer, bsz, seq, norm_pre, norm_post, pe_proj, pe_gate,
            w_in, pos_k, pos_v, w1_k, w2_k, w1_v, w2_v,
            a_re, a_im, log_dt, b_re, b_im, c_re, c_im, d_skip, w_glu, w_out):
    d = h2.shape[1]
    nw = NSA_HEADS * NSA_DH
    pm, pg = _norm_proj(h2, norm_pre.reshape(1, d), *_cd_proj_weights(w_in))

    nb = seq // CMP_STRIDE
    pm3 = pm.reshape(bsz, seq, CD_MAIN)
    padl = lambda a, lo=True: jnp.pad(a, ((0, 0),) * (a.ndim - 1)
                                      + ((0, LANES - a.shape[-1]) if lo else (LANES - a.shape[-1], 0),))
    w1 = lambda w, lo: jnp.pad(w.reshape(CMP_LEN, NSA_DH, CMP_HIDDEN),
                               ((0, 0), (0, LANES - NSA_DH) if lo else (LANES - NSA_DH, 0), (0, 0))).astype(BF16)
    ek, al, ekc, ovt = _nsa_tables(seq)
    kcc, vct = _compress(pm3, padl(pos_k), padl(pos_v, False), w1(w1_k, True), w1(w1_v, False),
                         padl(w2_k).astype(BF16), padl(w2_v).T.astype(BF16), ekc)
    if nb < LANES:
        kcc = jnp.pad(kcc, ((0, 0), (0, 0), (0, LANES - nb), (0, 0)))
        vct = jnp.pad(vct, ((0, 0), (0, 0), (0, 0), (0, LANES - nb)))
    y_nsa = _nsa(pm3, pg.reshape(bsz, seq, NSA_GROUPS * LANES), kcc, vct, ek, al, ovt)

    rep = lambda a: jnp.repeat(a, S5_GROUP_CH, axis=0)
    ldt = jnp.broadcast_to(rep(log_dt[:, None]), (S5_WIDTH, S5_STATE))
    bt = lambda b: b.transpose(0, 2, 1).reshape(S5_WIDTH, S5_STATE)
    abr, abi, bbr, bbi = _s5_disc(rep(a_re), rep(a_im), ldt, bt(b_re), bt(b_im))
    gi = np.arange(S5_WIDTH) // S5_GROUP_CH
    gs = np.arange(S5_NSTATE) // S5_STATE
    diag_in = jnp.asarray(gi[:, None] == gs[None, :])
    bd_in = lambda bb: jnp.where(diag_in, jnp.tile(bb, (1, S5_GROUPS)), 0.0).astype(BF16)
    bd_out = lambda c: jnp.where(diag_in.T, jnp.tile(c.transpose(0, 2, 1).reshape(S5_NSTATE, S5_GROUP_CH),
                                                     (1, S5_GROUPS)), 0.0).astype(BF16)
    row8 = lambda a: jnp.broadcast_to(a[::S5_GROUP_CH].reshape(1, S5_NSTATE), (8, S5_NSTATE))
    y_s5 = _s5(pm3, bd_in(bbr), bd_in(bbi), row8(abr), row8(abi),
               bd_out(c_re), bd_out(c_im), d_skip.reshape(1, -1), w_glu.astype(BF16))

    nt = seq // 512
    szb = _SZ0 // S5_WIDTH
    return _out_pe(y_nsa.reshape(bsz * seq, nw), lambda b, s: (b * nt + s, 0), nw,
                   y_s5.reshape(bsz * seq, S5_WIDTH), lambda b, s: (b * nt + s, 0), S5_WIDTH,
                   pm, lambda b, s: (b * nt + s, szb),
                   h2, p4, layer, w_out[:nw].astype(BF16), w_out[nw:].astype(BF16),
                   norm_post.reshape(1, d), pe_gate.astype(BF16), pe_proj.astype(BF16), bsz, seq)


def kernel(x, p, norm_pre, norm_post, pe_proj, pe_gate, ab_w_in, ret_norm, ml_conv_w, ml_conv_b, ml_gate_b,
           ml_norm, ab_w_out, cd_w_in, cmp_pos_k, cmp_pos_v, cmp_w1_k, cmp_w2_k, cmp_w1_v, cmp_w2_v,
           s5_a_re, s5_a_im, s5_log_dt, s5_b_re, s5_b_im, s5_c_re, s5_c_im, s5_d, s5_w_glu, cd_w_out):
    bsz, seq, d = x.shape
    assert bsz == 8, "the S5 scan maps the batch onto the 8 sublanes of a vreg"
    assert seq % max(KCH, 1024) == 0
    depth = p.shape[0]
    h2 = x.reshape(bsz * seq, d)
    for i in range(depth):
        j = i // 2
        if i % 2 == 0:
            h2 = _layer0(h2, p, i, bsz, seq, norm_pre[i], norm_post[i], pe_proj[i], pe_gate[i],
                         ab_w_in[j], ret_norm[j], ml_conv_w[j], ml_conv_b[j], ml_gate_b[j], ml_norm[j], ab_w_out[j])
        else:
            h2 = _layer1(h2, p, i, bsz, seq, norm_pre[i], norm_post[i], pe_proj[i], pe_gate[i],
                         cd_w_in[j], cmp_pos_k[j], cmp_pos_v[j], cmp_w1_k[j], cmp_w2_k[j], cmp_w1_v[j], cmp_w2_v[j],
                         s5_a_re[j], s5_a_im[j], s5_log_dt[j], s5_b_re[j], s5_b_im[j], s5_c_re[j], s5_c_im[j],
                         s5_d[j], s5_w_glu[j], cd_w_out[j])
    return h2.reshape(bsz, seq, d)
```

```python
import functools
import math

import numpy as np
import jax
import jax.numpy as jnp
from jax import lax
from jax.experimental import pallas as pl
from jax.experimental.pallas import tpu as pltpu

F32 = jnp.float32
BF16 = jnp.bfloat16
EPS = 1e-6
NEG = -1e30
BIG = 1e30

LANES = 128
CHUNK = 128
RET_HEADS, RET_DK, RET_DV = 4, 128, 256
ML_HEADS, ML_DK, ML_DV = 4, 128, 256
ML_CONV = 4
NSA_HEADS, NSA_GROUPS, NSA_DH = 16, 4, 64
NSA_HPG = NSA_HEADS // NSA_GROUPS
CMP_STRIDE, CMP_LEN, CMP_HIDDEN = 16, 32, 128
SEL_BLOCK, SEL_TOPN, WINDOW = 64, 4, 512
S5_GROUPS, S5_GROUP_CH, S5_STATE = 32, 16, 64
S5_WIDTH = S5_GROUPS * S5_GROUP_CH
S5_NSTATE = S5_GROUPS * S5_STATE

TQ = 256
KCH = 512
SEL_LANE0 = 64
ALI_LANE0 = 96
VMEM_LIMIT = 56 * 1024 * 1024


def _cparams(sem):
    return pltpu.CompilerParams(dimension_semantics=sem, vmem_limit_bytes=VMEM_LIMIT)


def _dot(a, b):
    return jnp.dot(a, b, preferred_element_type=F32)


def _dot_nt(a, b):
    return lax.dot_general(a, b, (((1,), (1,)), ((), ())), preferred_element_type=F32)


def _split3(x):
    hi = x.astype(BF16)
    r1 = x - hi.astype(F32)
    mid = r1.astype(BF16)
    lo = (r1 - mid.astype(F32)).astype(BF16)
    return hi, mid, lo


def _dot_f32_lhs(x, m):
    hi, mid, lo = _split3(x)
    return _dot(hi, m) + _dot(mid, m) + _dot(lo, m)


def _sigmoid(x):
    return 1.0 / (1.0 + jnp.exp(-x))


def _silu(x):
    return x * _sigmoid(x)


def _gelu(x):
    return 0.5 * x * (1.0 + jnp.tanh(math.sqrt(2.0 / math.pi) * (x + 0.044715 * (x * x * x))))


def _rms(x, g):
    return x * lax.rsqrt(jnp.mean(x * x, -1, keepdims=True) + EPS) * g


PROJ_TN = 1024


def _norm_proj_kernel(h_ref, g_ref, wg_ref, wm_ref, om_ref, og_ref):
    ub = _rms(h_ref[...], g_ref[...]).astype(BF16)
    nm = om_ref.shape[1]
    for c0 in range(0, nm, PROJ_TN):
        c1 = min(c0 + PROJ_TN, nm)
        om_ref[:, c0:c1] = _dot_nt(ub, wm_ref[c0:c1, :]).astype(om_ref.dtype)
    og_ref[...] = _dot_nt(ub, wg_ref[...])


def _norm_proj(h2, g, w_gate_t, w_main_t, tm=512):
    n, d = h2.shape
    nm = w_main_t.shape[0]
    ng = w_gate_t.shape[0]
    once = pl.Buffered(1)
    return pl.pallas_call(
        _norm_proj_kernel,
        grid=(n // tm,),
        in_specs=[pl.BlockSpec((tm, d), lambda i: (i, 0)),
                  pl.BlockSpec((1, d), lambda i: (0, 0)),
                  pl.BlockSpec((ng, d), lambda i: (0, 0), pipeline_mode=once),
                  pl.BlockSpec((nm, d), lambda i: (0, 0), pipeline_mode=once)],
        out_specs=[pl.BlockSpec((tm, nm), lambda i: (i, 0)),
                   pl.BlockSpec((tm, ng), lambda i: (i, 0))],
        out_shape=[jax.ShapeDtypeStruct((n, nm), BF16), jax.ShapeDtypeStruct((n, ng), F32)],
        compiler_params=_cparams(("parallel",)),
        name="norm_proj",
    )(h2, g, w_gate_t, w_main_t)


def _out_pe_kernel(gated, *refs):
    if gated:
        ma_ref, mb_ref, sz_ref, h_ref, p_ref, wa_ref, wb_ref, gp_ref, wg_ref, wp_ref, o_ref = refs
        sz = sz_ref[...].astype(F32)
        mb = (mb_ref[...].astype(F32) * _silu(sz)).astype(BF16)
    else:
        ma_ref, mb_ref, h_ref, p_ref, wa_ref, wb_ref, gp_ref, wg_ref, wp_ref, o_ref = refs
        mb = mb_ref[...]
    y = _dot(ma_ref[...], wa_ref[...]) + _dot(mb, wb_ref[...])
    h1 = h_ref[...] + _rms(y, gp_ref[...])
    gate = _sigmoid(_dot(h1.astype(BF16), wg_ref[...]))
    pp = _dot(p_ref[...].astype(BF16), wp_ref[...])
    o_ref[...] = h1 + gate * pp


def _out_pe(mix_a, a_map, ka, mix_b, b_map, kb, sz, sz_map, h2, p4, layer, w_a, w_b, g_post, w_gate, w_pe,
            bsz, seq, tm=512):
    n, d = h2.shape
    pe = p4.shape[-1]
    nt = seq // tm
    row = lambda b, s: (b * nt + s, 0)
    const = lambda b, s: (0, 0)
    gated = sz is not None
    in_specs = [pl.BlockSpec((tm, ka), a_map), pl.BlockSpec((tm, kb), b_map)]
    args = [mix_a, mix_b]
    if gated:
        in_specs.append(pl.BlockSpec((tm, kb), sz_map))
        args.append(sz)
    in_specs += [pl.BlockSpec((tm, d), row), pl.BlockSpec((None, None, tm, pe), lambda b, s: (layer, b, s, 0)),
                 pl.BlockSpec((ka, d), const), pl.BlockSpec((kb, d), const), pl.BlockSpec((1, d), const),
                 pl.BlockSpec((d, d), const), pl.BlockSpec((pe, d), const)]
    args += [h2, p4, w_a, w_b, g_post, w_gate, w_pe]
    return pl.pallas_call(
        functools.partial(_out_pe_kernel, gated),
        grid=(bsz, nt),
        in_specs=in_specs,
        out_specs=pl.BlockSpec((tm, d), row),
        out_shape=jax.ShapeDtypeStruct((n, d), F32),
        compiler_params=_cparams(("parallel", "arbitrary")),
        name="out_pe_gated" if gated else "out_pe",
    )(*args)


_RQ, _RK, _RV, _RG = 0, 512, 1024, 2048
_MQ, _MK, _MV, _MO, _MZ = 3072, 3584, 4096, 5120, 6144
AB_MAIN = 7168


def _head_norm(y, g):
    yc = y - jnp.mean(y, -1, keepdims=True)
    return yc * lax.rsqrt(jnp.mean(yc * yc, -1, keepdims=True) + EPS) * g


def _ab_kernel(pm_ref, pg_ref, cw_ref, cb_ref, gb_ref, rn_ref, mn_ref, mix_ref,
               ext_ref, r_ref, c_ref, n_ref, m_ref):
    L = CHUNK
    c_idx = pl.program_id(1)

    @pl.when(c_idx == 0)
    def _():
        ext_ref[0:8, :] = jnp.zeros((8, ext_ref.shape[1]), F32)
        r_ref[...] = jnp.zeros(r_ref.shape, F32)
        c_ref[...] = jnp.zeros(c_ref.shape, F32)
        n_ref[...] = jnp.zeros(n_ref.shape, F32)
        m_ref[...] = jnp.full(m_ref.shape, NEG, F32)

    row = lax.broadcasted_iota(jnp.int32, (L, L), 0)
    col = lax.broadcasted_iota(jnp.int32, (L, L), 1)
    causal = row >= col
    diff = (row - col).astype(F32)
    tcol = lax.broadcasted_iota(jnp.int32, (L, 1), 0).astype(F32)

    for h in range(RET_HEADS):
        lg = math.log1p(-(2.0 ** (-5.0 - h)))
        scale = RET_DK ** -0.5
        decay = jnp.where(causal, jnp.exp(jnp.maximum(diff, 0.0) * lg), 0.0) * scale
        q = pm_ref[:, _RQ + h * RET_DK:_RQ + (h + 1) * RET_DK]
        k = pm_ref[:, _RK + h * RET_DK:_RK + (h + 1) * RET_DK]
        v = pm_ref[:, _RV + h * RET_DV:_RV + (h + 1) * RET_DV]
        sc = _dot_nt(q, k) * decay
        xi = jnp.exp((tcol + 1.0) * lg) * scale
        qx = (q.astype(F32) * xi).astype(BF16)
        r_prev = r_ref[h]
        y = _dot(sc.astype(BF16), v) + _dot(qx, r_prev.astype(BF16))
        zeta = jnp.exp((L - 1.0 - tcol) * lg)
        kz_t = (k.astype(F32) * zeta).T.astype(BF16)
        r_ref[h] = r_prev * math.exp(L * lg) + _dot(kz_t, v)
        g = pm_ref[:, _RG + h * RET_DV:_RG + (h + 1) * RET_DV].astype(F32)
        out = _head_norm(y, rn_ref[:, h * RET_DV:(h + 1) * RET_DV]) * _silu(g)
        mix_ref[:, h * RET_DV:(h + 1) * RET_DV] = out.astype(mix_ref.dtype)

    nqk = 2 * ML_HEADS * ML_DK
    x = pm_ref[:, _MQ:_MQ + nqk].astype(F32)
    ext_ref[8:8 + L, :] = x
    acc = cb_ref[...] + cw_ref[ML_CONV - 1:ML_CONV, :] * x
    for j in range(ML_CONV - 1):
        acc = acc + cw_ref[j:j + 1, :] * ext_ref[pl.ds(8 - (ML_CONV - 1) + j, L), :]
    ext_ref[0:8, :] = x[L - 8:L, :]
    qk = _silu(acc)

    gt = (pg_ref[...] + gb_ref[...]).T
    g8 = gt[0:8, :]
    logf8 = jnp.minimum(g8, 0.0) - jnp.log(1.0 + jnp.exp(-jnp.abs(g8)))
    triu = jnp.where(row <= col, 1.0, 0.0).astype(BF16)
    bcum8 = _dot_f32_lhs(logf8, triu)
    b_t = pltpu.roll(bcum8, 4, axis=0)
    b_last = b_t[:, L - 1:L]
    a_t = b_last - b_t + g8
    mu = jnp.max(a_t, axis=1, keepdims=True)
    w_t = jnp.exp(a_t - mu)
    r_t = g8 - b_t
    zpad = jnp.zeros((L - 8, L), F32)
    bcol = jnp.concatenate([b_t, zpad], axis=0).T
    wcol = jnp.concatenate([w_t, zpad], axis=0).T
    m_prev = m_ref[...][:, 0:1]
    m_new = jnp.maximum(b_last + m_prev, mu)
    sp = jnp.exp(b_last + m_prev - m_new)
    scn = jnp.exp(mu - m_new)
    m_ref[...] = jnp.broadcast_to(m_new, m_ref.shape)

    for h in range(ML_HEADS):
        bc = bcol[:, h:h + 1]
        log_d = jnp.where(causal, bc + r_t[h:h + 1, :], NEG)
        inter = bc + m_prev[h:h + 1, :]
        m_t = jnp.maximum(inter, jnp.max(log_d, axis=1, keepdims=True))
        dmat = jnp.exp(log_d - m_t)
        w_int = jnp.exp(inter - m_t)
        qh = qk[:, h * ML_DK:(h + 1) * ML_DK]
        kh = qk[:, ML_HEADS * ML_DK + h * ML_DK:ML_HEADS * ML_DK + (h + 1) * ML_DK] * (ML_DK ** -0.5)
        qb = qh.astype(BF16)
        v = pm_ref[:, _MV + h * ML_DV:_MV + (h + 1) * ML_DV]
        s = _dot_nt(qb, kh.astype(BF16)) * dmat
        c_prev = c_ref[h]
        n_prev = n_ref[h:h + 1, :]
        num = _dot(s.astype(BF16), v) + w_int * _dot(qb, c_prev.astype(BF16))
        den = jnp.sum(s, axis=1, keepdims=True) + w_int * jnp.sum(qh * n_prev, axis=1, keepdims=True)
        den = jnp.maximum(jnp.abs(den), jnp.exp(-m_t))
        hcell = num * (1.0 / den)
        o = pm_ref[:, _MO + h * ML_DV:_MO + (h + 1) * ML_DV].astype(F32)
        y = _sigmoid(o) * hcell
        z = pm_ref[:, _MZ + h * ML_DV:_MZ + (h + 1) * ML_DV].astype(F32)
        out = _head_norm(y, mn_ref[:, h * ML_DV:(h + 1) * ML_DV]) * _silu(z)
        mix_ref[:, RET_HEADS * RET_DV + h * ML_DV:RET_HEADS * RET_DV + (h + 1) * ML_DV] = out.astype(mix_ref.dtype)
        kw = kh * wcol[:, h:h + 1]
        kv = _dot(kw.T.astype(BF16), v)
        ksum = jnp.sum(kw, axis=0, keepdims=True)
        sp_h = sp[h:h + 1, :]
        sc_h = scn[h:h + 1, :]
        c_ref[h] = sp_h * c_prev + sc_h * kv
        n_ref[h:h + 1, :] = sp_h * n_prev + sc_h * ksum


def _ab_mixer(pm, pg, conv_w, conv_b, gate_b, ret_norm, ml_norm, bsz, seq):
    n = pm.shape[0]
    nc = seq // CHUNK
    nqk = 2 * ML_HEADS * ML_DK
    row = lambda b, c: (b * nc + c, 0)
    const = lambda b, c: (0, 0)
    nmix = RET_HEADS * RET_DV + ML_HEADS * ML_DV
    return pl.pallas_call(
        _ab_kernel,
        grid=(bsz, nc),
        in_specs=[pl.BlockSpec((CHUNK, AB_MAIN), row), pl.BlockSpec((CHUNK, LANES), row),
                  pl.BlockSpec((ML_CONV, nqk), const), pl.BlockSpec((1, nqk), const),
                  pl.BlockSpec((1, LANES), const),
                  pl.BlockSpec((1, RET_HEADS * RET_DV), const), pl.BlockSpec((1, ML_HEADS * ML_DV), const)],
        out_specs=pl.BlockSpec((CHUNK, nmix), row),
        out_shape=jax.ShapeDtypeStruct((n, nmix), BF16),
        scratch_shapes=[pltpu.VMEM((CHUNK + 8, nqk), F32),
                        pltpu.VMEM((RET_HEADS, RET_DK, RET_DV), F32),
                        pltpu.VMEM((ML_HEADS, ML_DK, ML_DV), F32),
                        pltpu.VMEM((8, ML_DK), F32),
                        pltpu.VMEM((8, LANES), F32)],
        compiler_params=_cparams(("parallel", "arbitrary")),
        name="ab_mixer",
    )(pm, pg, conv_w, conv_b, gate_b, ret_norm, ml_norm)


_QNZ0 = 0
_KVC0 = NSA_HEADS
_KVS0 = _KVC0 + NSA_GROUPS
_KVW0 = _KVS0 + NSA_GROUPS
_SU0 = (_KVW0 + NSA_GROUPS) * LANES
_SZ0 = _SU0 + S5_WIDTH
CD_MAIN = _SZ0 + S5_WIDTH
SLOTW = NSA_GROUPS * LANES
VT_ROWS = NSA_DH + 16
LOG2E = math.log2(math.e)


def _compress_kernel(x_ref, pk_ref, pv_ref, w1k_ref, w1v_ref, w2k_ref, w2vt_ref, ekc_ref,
                     ko_ref, vot_ref, xs_ref):
    seq = x_ref.shape[0]
    nb = seq // CMP_STRIDE
    for g in range(NSA_GROUPS):
        xs_ref[...] = x_ref[:, g * LANES:(g + 1) * LANES].astype(F32)
        acc = [jnp.zeros((nb, CMP_HIDDEN), F32) for _ in range(4)]
        for r in range(CMP_STRIDE):
            xr = xs_ref[pl.ds(r, nb, stride=CMP_STRIDE), :]
            for kind, (p_ref, w1_ref) in enumerate(((pk_ref, w1k_ref), (pv_ref, w1v_ref))):
                acc[2 * kind] = acc[2 * kind] + _dot((xr + p_ref[r:r + 1, :]).astype(BF16), w1_ref[r])
                acc[2 * kind + 1] = acc[2 * kind + 1] + _dot(
                    (xr + p_ref[CMP_STRIDE + r:CMP_STRIDE + r + 1, :]).astype(BF16), w1_ref[CMP_STRIDE + r])
        hk = _gelu(acc[0] + pltpu.roll(acc[1], nb - 1, axis=0)).astype(BF16)
        hv = _gelu(acc[2] + pltpu.roll(acc[3], nb - 1, axis=0)).astype(BF16)
        ko_ref[g] = (_dot(hk, w2k_ref[...]) + ekc_ref[...]).astype(ko_ref.dtype)
        vot_ref[g] = _dot_nt(w2vt_ref[...], hv).astype(vot_ref.dtype)


def _compress(pm3, pos_k, pos_v, w1k, w1v, w2k, w2vt, ekc):
    bsz, seq, _ = pm3.shape
    nb = seq // CMP_STRIDE
    c3 = lambda b: (0, 0, 0)
    c2 = lambda b: (0, 0)
    return pl.pallas_call(
        _compress_kernel,
        grid=(bsz,),
        in_specs=[pl.BlockSpec((None, seq, SLOTW), lambda b: (b, 0, _KVC0 * LANES // SLOTW)),
                  pl.BlockSpec((CMP_LEN, LANES), c2), pl.BlockSpec((CMP_LEN, LANES), c2),
                  pl.BlockSpec((CMP_LEN, LANES, CMP_HIDDEN), c3), pl.BlockSpec((CMP_LEN, LANES, CMP_HIDDEN), c3),
                  pl.BlockSpec((CMP_HIDDEN, LANES), c2), pl.BlockSpec((LANES, CMP_HIDDEN), c2),
                  pl.BlockSpec((nb, LANES), c2)],
        out_specs=[pl.BlockSpec((None, NSA_GROUPS, nb, LANES), lambda b: (b, 0, 0, 0)),
                   pl.BlockSpec((None, NSA_GROUPS, LANES, nb), lambda b: (b, 0, 0, 0))],
        out_shape=[jax.ShapeDtypeStruct((bsz, NSA_GROUPS, nb, LANES), BF16),
                   jax.ShapeDtypeStruct((bsz, NSA_GROUPS, LANES, nb), BF16)],
        scratch_shapes=[pltpu.VMEM((seq, LANES), F32)],
        compiler_params=_cparams(("parallel",)),
        name="nsa_compress",
    )(pm3, pos_k, pos_v, w1k, w1v, w2k, w2vt, ekc)


def _nsa_kernel(q_ref, kc_ref, vct_ref, kvs_ref, kvw_ref, pg_ref,
                ek_ref, al_ref, ovt_ref, y_ref,
                ksa_ref, vsat_ref, kwa_ref, vwat_ref, qa_scr, part_scr, gzs_scr, sel_smem):
    seq = kvs_ref.shape[0]
    g_idx = pl.program_id(1)
    qi = pl.program_id(2)
    wtiles = WINDOW // LANES

    @pl.when(qi == 0)
    def _():
        ek = ek_ref[...]
        lane_k = lax.broadcasted_iota(jnp.int32, (seq, LANES), 1)
        zero = jnp.zeros((seq, LANES), BF16)
        ksa_ref[0:seq, :] = jnp.where(lane_k < NSA_DH, kvs_ref[...], zero) + ek
        lane_d = lax.broadcasted_iota(jnp.int32, (LANES, LANES), 1)
        ksa_ref[seq:seq + LANES, :] = jnp.where((lane_d >= SEL_LANE0) & (lane_d < SEL_LANE0 + seq // SEL_BLOCK),
                                                1.0, 0.0).astype(BF16)
        vsat_ref[seq // LANES] = jnp.zeros((VT_ROWS, LANES), BF16)
        ek_w = jnp.where(lane_k >= ALI_LANE0, ek, zero)
        kwa_ref[0:WINDOW, :] = jnp.zeros((WINDOW, LANES), BF16)
        kwa_ref[WINDOW:WINDOW + seq, :] = jnp.where(lane_k < NSA_DH, kvw_ref[...], zero) + ek_w
        ones_rows = jnp.where(lax.broadcasted_iota(jnp.int32, (VT_ROWS - NSA_DH, LANES), 0) == 0, 1.0, 0.0)
        for i in range(wtiles):
            vwat_ref[i] = jnp.zeros((VT_ROWS, LANES), BF16)
        for i in range(seq // LANES):
            rws = slice(i * LANES, (i + 1) * LANES)
            for src_ref, dst_ref, j in ((kvs_ref, vsat_ref, i), (kvw_ref, vwat_ref, wtiles + i)):
                v_t = src_ref[rws, :].astype(F32).T[NSA_DH:2 * NSA_DH, :]
                dst_ref[j] = jnp.concatenate([v_t, ones_rows], axis=0).astype(BF16)

    nq = seq // TQ
    refs = (q_ref, kc_ref, vct_ref, pg_ref, al_ref, ovt_ref, y_ref, ksa_ref, vsat_ref, kwa_ref, vwat_ref,
            qa_scr, part_scr, gzs_scr, sel_smem)
    for step in range(nq + 1):
        pl.when(qi == step)(functools.partial(_nsa_step, step, nq, seq // SEL_BLOCK, g_idx, refs))


SEL_DYN = 2
SEL_CNT = 7


def _sel_compactable(qi):
    return (qi + 1) * TQ // LANES > 1 + SEL_DYN + TQ // LANES


def _nsa_step(step, nq, nsel, g_idx, refs):
    def run(compact):
        stages = []
        if step < nq:
            stages.append(_nsa_front(nsel, step, g_idx, refs))
        if step > 0:
            stages.append(_nsa_back(step - 1, compact, refs))
        while stages:
            for st in list(stages):
                if next(st, "done") == "done":
                    stages.remove(st)

    if step > 0 and _sel_compactable(step - 1):
        cnt = refs[-1][(step - 1) & 1, SEL_CNT]
        for ndyn in range(1, SEL_DYN + 1):
            lo = 0 if ndyn == 1 else ndyn
            pl.when((cnt >= lo) & (cnt <= ndyn))(functools.partial(run, ndyn))
        pl.when(cnt > SEL_DYN)(functools.partial(run, 0))
    else:
        run(0)


def _tile_iotas():
    rows = NSA_HPG * TQ
    jrow = lax.broadcasted_iota(jnp.int32, (LANES, rows), 0)
    tq = lax.broadcasted_iota(jnp.int32, (LANES, rows), 1) & (TQ - 1)
    return jrow, tq, jrow - tq


def _nsa_front(nsel, qi, g_idx, refs):
    (q_ref, kc_ref, vct_ref, pg_ref, al_ref, ovt_ref, _, _, _, kwa_ref, vwat_ref,
     qa_scr, part_scr, gzs_scr, sel_smem) = refs
    hp = NSA_HPG
    span = WINDOW + TQ
    t0 = qi * TQ
    tw = t0
    early = t0 < WINDOW
    par = qi & 1
    jrow, tq, kmt = _tile_iotas()

    qnz = [q_ref[:, hh * LANES:(hh + 1) * LANES].astype(F32) for hh in range(hp)]
    lane_q = lax.broadcasted_iota(jnp.int32, (TQ, LANES), 1)
    qal = jnp.concatenate([jnp.where(lane_q < NSA_DH, qnz[hh], al_ref[pl.ds(g_idx * hp + hh, 1), :])
                           for hh in range(hp)], axis=0)
    qalb = qal.astype(BF16)
    s_c = _dot_nt(kc_ref[...], qalb)
    wfirst = max(0, WINDOW - t0) // LANES
    s_w = _dot_nt(kwa_ref[tw + wfirst * LANES:tw + span, :], qalb)
    yield

    s = jnp.where(tq - jrow * CMP_STRIDE >= (CMP_LEN - 1) - t0, s_c, NEG)
    e = jnp.exp2(s - jnp.max(s, axis=0, keepdims=True))
    inv = 1.0 / jnp.sum(e, axis=0, keepdims=True)
    if early:
        inv = jnp.where(t0 + tq[0:1, :] >= CMP_LEN - 1, inv, 0.0)
    p = e * inv
    oc_t = _dot(vct_ref[...], p.astype(BF16))
    psum_t = p[:, 0:TQ]
    for hh in range(1, hp):
        psum_t = psum_t + p[:, hh * TQ:(hh + 1) * TQ]
    p_hi = psum_t.astype(BF16)
    p_lo = (psum_t - p_hi.astype(F32)).astype(BF16)
    imp_t = _dot(ovt_ref[...], p_hi) + _dot(ovt_ref[...], p_lo)
    yield

    blocks = []
    for i in range(wfirst, span // LANES):
        blk = s_w[(i - wfirst) * LANES:(i - wfirst + 1) * LANES, :]
        if i * LANES < TQ:
            blk = jnp.where(kmt > -i * LANES, blk, NEG)
        if (i + 1) * LANES > WINDOW:
            blk = jnp.where(kmt <= WINDOW - i * LANES, blk, NEG)
        blocks.append(blk)
    s = jnp.concatenate(blocks, axis=0)
    p = jnp.exp2(s - jnp.max(s, axis=0, keepdims=True)).astype(BF16)
    vw_t = jnp.concatenate([vwat_ref[qi * (TQ // LANES) + i] for i in range(wfirst, span // LANES)], axis=1)
    accw_t = _dot(vw_t, p)
    yield

    score = imp_t[SEL_LANE0:SEL_LANE0 + nsel, :]
    nrow = lax.broadcasted_iota(jnp.int32, (nsel, TQ), 0)
    own = (t0 + lax.broadcasted_iota(jnp.int32, (nsel, TQ), 1)) >> int(math.log2(SEL_BLOCK))
    valid = nrow <= own
    forced = (nrow == 0) | (nrow == own)
    score = jnp.where(forced, BIG, jnp.where(valid, score, -1.0))
    chosen = jnp.zeros((nsel, TQ), F32)
    nrow_f = nrow.astype(F32)
    for _ in range(SEL_TOPN):
        mx = jnp.max(score, axis=0, keepdims=True)
        first = jnp.min(jnp.where(score == mx, nrow_f, 4.0 * LANES), axis=0, keepdims=True)
        hit = nrow_f == first
        chosen = jnp.where(hit, 1.0, chosen)
        score = jnp.where(hit, -3.0, score)
    mask_t = jnp.where(chosen > 0.0, jnp.where(valid, 0.0, NEG), NEG)
    if _sel_compactable(qi):
        used_blk = jnp.max(jnp.where(mask_t == 0.0, 1.0, 0.0), axis=1, keepdims=True)
        blk_per_tile = LANES // SEL_BLOCK
        for slot in range(SEL_DYN):
            sel_smem[par, slot] = jnp.int32(nsel // blk_per_tile)
        cnt = jnp.int32(0)
        for i in range(1, t0 // LANES):
            used = jnp.max(used_blk[i * blk_per_tile:(i + 1) * blk_per_tile, :]) > 0.0

            @pl.when(used & (cnt < SEL_DYN))
            def _(i=i, cnt=cnt):
                sel_smem[par, cnt] = jnp.int32(i)

            cnt = cnt + used.astype(jnp.int32)
        sel_smem[par, SEL_CNT] = cnt
    maskcols = jnp.concatenate([jnp.zeros((SEL_LANE0, TQ), F32), mask_t,
                                jnp.zeros((LANES - SEL_LANE0 - nsel, TQ), F32)], axis=0).T

    qa_scr[par] = (qal + jnp.concatenate([maskcols] * hp, axis=0)).astype(BF16)

    gt_t = _sigmoid(pg_ref[...]).T
    for hh in range(hp):
        g_c, g_s, g_w = [gt_t[j * hp + hh:j * hp + hh + 1, :] for j in range(3)]
        hcols = slice(hh * TQ, (hh + 1) * TQ)
        aw_t = accw_t[:, hcols]
        gz = _silu(qnz[hh].T[NSA_DH:2 * NSA_DH, :])
        o = g_c * oc_t[0:NSA_DH, hcols] + (g_w * (1.0 / aw_t[NSA_DH:NSA_DH + 1, :])) * aw_t[0:NSA_DH, :]
        part_scr[par, hh * NSA_DH:(hh + 1) * NSA_DH, :] = o * gz
        gzs_scr[par, hh * NSA_DH:(hh + 1) * NSA_DH, :] = g_s * gz


def _nsa_back(qi, compact, refs):
    _, _, _, _, _, _, y_ref, ksa_ref, vsat_ref, _, _, qa_scr, part_scr, gzs_scr, sel_smem = refs
    hp = NSA_HPG
    t0 = qi * TQ
    par = qi & 1
    _, _, kmt = _tile_iotas()
    qa = qa_scr[par]
    ntiles = (t0 + TQ) // LANES
    if compact:
        tiles = [0] + [sel_smem[par, j] for j in range(compact)] + list(range(t0 // LANES, ntiles))
    else:
        tiles = list(range(ntiles))
    per = len(tiles) if compact else KCH // LANES
    chunks = [tiles[i:i + per] for i in range(0, len(tiles), per)]
    nk = len(chunks)

    def k_tile(i):
        if isinstance(i, int):
            return ksa_ref[i * LANES:(i + 1) * LANES, :]
        return ksa_ref[pl.ds(pl.multiple_of(i * LANES, LANES), LANES), :]

    def scores(c):
        s = _dot_nt(jnp.concatenate([k_tile(i) for i in chunks[c]], axis=0), qa)
        if any(isinstance(i, int) and (i + 1) * LANES > t0 for i in chunks[c]):
            s = jnp.concatenate(
                [jnp.where(kmt <= t0 - i * LANES, s[n * LANES:(n + 1) * LANES, :], NEG)
                 if isinstance(i, int) and (i + 1) * LANES > t0 else s[n * LANES:(n + 1) * LANES, :]
                 for n, i in enumerate(chunks[c])], axis=0)
        return s

    def weighted(c, s):
        mc = jnp.max(s, axis=0, keepdims=True)
        v_t = jnp.concatenate([vsat_ref[i] for i in chunks[c]], axis=1)
        return mc, _dot(v_t, jnp.exp2(s - mc).astype(BF16))

    parts = []
    s_next = scores(0)
    for c in range(nk):
        s_cur = s_next
        if c + 1 < nk:
            s_next = scores(c + 1)
        yield
        parts.append(weighted(c, s_cur))
    yield
    m = parts[0][0]
    for mc, _ in parts[1:]:
        m = jnp.maximum(m, mc)
    accs_t = None
    for mc, d in parts:
        d = d if nk == 1 else jnp.exp2(mc - m) * d
        accs_t = d if accs_t is None else accs_t + d

    outs = []
    for hh in range(hp):
        as_t = accs_t[:, hh * TQ:(hh + 1) * TQ]
        rws = slice(hh * NSA_DH, (hh + 1) * NSA_DH)
        outs.append(part_scr[par, rws, :] + (gzs_scr[par, rws, :] * (1.0 / as_t[NSA_DH:NSA_DH + 1, :])) * as_t[0:NSA_DH, :])
    for pr in range(hp // 2):
        packed = jnp.concatenate([outs[2 * pr], outs[2 * pr + 1]], axis=0).T
        y_ref[:, pr * LANES:(pr + 1) * LANES] = packed.astype(y_ref.dtype)


def _nsa(pm3, pg3, kcc, vct, ek, al, ovt):
    bsz, seq, _ = pm3.shape
    nq = seq // TQ
    gw = NSA_HPG * NSA_DH
    kv = lambda tile0: pl.BlockSpec((None, seq, LANES), lambda b, g, q: (b, 0, tile0 + g))
    c2 = lambda b, g, q: (0, 0)
    front = lambda b, g, q: (b, jnp.minimum(q, nq - 1), g)
    back = lambda b, g, q: (b, jnp.maximum(q - 1, 0), g)
    return pl.pallas_call(
        _nsa_kernel,
        grid=(bsz, NSA_GROUPS, nq + 1),
        in_specs=[pl.BlockSpec((None, TQ, NSA_HPG * LANES), front),
                  pl.BlockSpec((None, None, LANES, LANES), lambda b, g, q: (b, g, 0, 0)),
                  pl.BlockSpec((None, None, LANES, LANES), lambda b, g, q: (b, g, 0, 0)),
                  kv(_KVS0), kv(_KVW0),
                  pl.BlockSpec((None, TQ, LANES), front),
                  pl.BlockSpec((seq, LANES), c2),
                  pl.BlockSpec((NSA_HEADS, LANES), c2),
                  pl.BlockSpec((LANES, LANES), c2)],
        out_specs=pl.BlockSpec((None, TQ, gw), back),
        out_shape=jax.ShapeDtypeStruct((bsz, seq, NSA_HEADS * NSA_DH), BF16),
        scratch_shapes=[pltpu.VMEM((seq + LANES, LANES), BF16),
                        pltpu.VMEM((seq // LANES + 1, VT_ROWS, LANES), BF16),
                        pltpu.VMEM((seq + WINDOW, LANES), BF16),
                        pltpu.VMEM(((seq + WINDOW) // LANES, VT_ROWS, LANES), BF16),
                        pltpu.VMEM((2, NSA_HPG * TQ, LANES), BF16),
                        pltpu.VMEM((2, gw, TQ), F32), pltpu.VMEM((2, gw, TQ), F32),
                        pltpu.SMEM((2, SEL_CNT + 1), jnp.int32)],
        compiler_params=_cparams(("parallel", "arbitrary", "arbitrary")),
        name="nsa_attn",
    )(pm3, kcc, vct, pm3, pm3, pg3, ek, al, ovt)


def _nsa_tables(seq):
    t = np.arange(seq)
    ek = np.zeros((seq, LANES), np.float32)
    ek[t, SEL_LANE0 + t // SEL_BLOCK] = 1.0
    for c in range(3):
        ek[:, ALI_LANE0 + c] = t // SEL_BLOCK
        ek[:, ALI_LANE0 + 3 + c] = t % SEL_BLOCK
    slopes = jnp.exp2(-8.0 * jnp.arange(1, NSA_HEADS + 1, dtype=F32) / NSA_HEADS) * LOG2E
    s_hi, s_mid, s_lo = _split3(slopes)
    parts = [p.astype(F32) for p in (s_hi, s_mid, s_lo)]
    al = jnp.pad(jnp.stack([p * SEL_BLOCK for p in parts] + parts, axis=1),
                 ((0, 0), (ALI_LANE0, LANES - ALI_LANE0 - 2 * len(parts))))
    nb = seq // CMP_STRIDE
    cstart = np.arange(nb) * CMP_STRIDE
    cend = cstart + CMP_LEN - 1
    ekc = np.zeros((nb, LANES), np.float32)
    for c in range(3):
        ekc[:, ALI_LANE0 + c] = cend // SEL_BLOCK
        ekc[:, ALI_LANE0 + 3 + c] = cend % SEL_BLOCK
    sel = np.arange(seq // SEL_BLOCK)
    ovt = np.zeros((LANES, max(nb, LANES)), np.float32)
    ovl = (cstart[:, None] < (sel[None, :] + 1) * SEL_BLOCK) & (cstart[:, None] + CMP_LEN > sel[None, :] * SEL_BLOCK)
    ovl[nb - 1, :] = False
    ovt[SEL_LANE0:SEL_LANE0 + len(sel), :nb] = ovl.T
    return jnp.asarray(ek, BF16), al, jnp.asarray(ekc), jnp.asarray(ovt, BF16)


def _s5_disc_kernel(are_ref, aim_ref, ldt_ref, bre_ref, bim_ref, abr_ref, abi_ref, bbr_ref, bbi_ref):
    a_re = are_ref[...]
    a_im = aim_ref[...]
    dt = jnp.exp(ldt_ref[...])
    er = jnp.exp(a_re * dt)
    abar_re = er * jnp.cos(a_im * dt)
    abar_im = er * jnp.sin(a_im * dt)
    lam2 = a_re * a_re + a_im * a_im
    cr = ((abar_re - 1.0) * a_re + abar_im * a_im) / lam2
    ci = (abar_im * a_re - (abar_re - 1.0) * a_im) / lam2
    b_re = bre_ref[...]
    b_im = bim_ref[...]
    abr_ref[...] = abar_re
    abi_ref[...] = abar_im
    bbr_ref[...] = cr * b_re - ci * b_im
    bbi_ref[...] = cr * b_im + ci * b_re


def _s5_disc(a_re_rep, a_im_rep, ldt_rep, b_re_t, b_im_t):
    shp = jax.ShapeDtypeStruct(a_re_rep.shape, F32)
    return pl.pallas_call(_s5_disc_kernel, out_shape=[shp] * 4, name="s5_disc")(
        a_re_rep, a_im_rep, ldt_rep, b_re_t, b_im_t)


S5_SLABG = 4
S5_TT = 128
S5_PITCH = S5_TT + 4
S5_DIAG = 256
S5_NSTATE_PER_DIAG = S5_DIAG // S5_GROUP_CH * S5_STATE


def _s5_kernel(u_ref, bre_ref, bim_ref, ar_ref, ai_ref, cre_ref, cim_ref, d_ref, wg_ref, o_ref,
               xr_ref, xi_ref, sr_ref, si_ref):
    nb, tt, wd = u_ref.shape
    pitch = sr_ref.shape[1] // nb
    gcols = S5_SLABG * LANES
    ngroups = S5_NSTATE // gcols

    @pl.when(pl.program_id(0) == 0)
    def _():
        xr_ref[...] = jnp.zeros(xr_ref.shape, F32)
        xi_ref[...] = jnp.zeros(xi_ref.shape, F32)

    u = u_ref[...].reshape(nb * tt, wd)

    chan = lambda kg: slice(kg * gcols // S5_NSTATE_PER_DIAG * S5_DIAG,
                            (kg * gcols // S5_NSTATE_PER_DIAG + 1) * S5_DIAG)
    for kg in range(ngroups):
        cols = slice(kg * gcols, (kg + 1) * gcols)
        br = _dot(u[:, chan(kg)], bre_ref[chan(kg), cols])
        bi = _dot(u[:, chan(kg)], bim_ref[chan(kg), cols])
        for j in range(S5_SLABG):
            for b in range(nb):
                sr_ref[kg * S5_SLABG + j, b * pitch:b * pitch + tt, :] = br[b * tt:(b + 1) * tt, j * LANES:(j + 1) * LANES]
                si_ref[kg * S5_SLABG + j, b * pitch:b * pitch + tt, :] = bi[b * tt:(b + 1) * tt, j * LANES:(j + 1) * LANES]

    for kg in range(ngroups):
        cols = slice(kg * gcols, (kg + 1) * gcols)
        slabs = range(kg * S5_SLABG, (kg + 1) * S5_SLABG)
        ar = ar_ref[:, cols]
        ai = ai_ref[:, cols]

        def body(t, carry, slabs=slabs, ar=ar, ai=ai):
            xr, xi = carry
            rows = pl.ds(t, nb, stride=pitch)
            nxr = ar * xr - ai * xi + jnp.concatenate([sr_ref[k, rows, :] for k in slabs], axis=1)
            nxi = ar * xi + ai * xr + jnp.concatenate([si_ref[k, rows, :] for k in slabs], axis=1)
            for j, k in enumerate(slabs):
                sr_ref[k, rows, :] = nxr[:, j * LANES:(j + 1) * LANES]
                si_ref[k, rows, :] = nxi[:, j * LANES:(j + 1) * LANES]
            return nxr, nxi

        xr, xi = lax.fori_loop(0, tt, body, (xr_ref[:, cols], xi_ref[:, cols]), unroll=4)
        xr_ref[:, cols] = xr
        xi_ref[:, cols] = xi

    y_blocks = [jnp.zeros((nb * tt, S5_DIAG), F32) for _ in range(wd // S5_DIAG)]
    for kg in range(ngroups):
        cols = slice(kg * gcols, (kg + 1) * gcols)
        gather = lambda ref: jnp.concatenate(
            [jnp.concatenate([ref[kg * S5_SLABG + j, b * pitch:b * pitch + tt, :] for j in range(S5_SLABG)], axis=1)
             for b in range(nb)], axis=0).astype(BF16)
        blk = chan(kg).start // S5_DIAG
        y_blocks[blk] = (y_blocks[blk] + _dot(gather(sr_ref), cre_ref[cols, chan(kg)])
                         - _dot(gather(si_ref), cim_ref[cols, chan(kg)]))
    y = jnp.concatenate(y_blocks, axis=1)
    y = _gelu(y + d_ref[...] * u.astype(F32))
    z = _dot(y.astype(BF16), wg_ref[...])
    o = z[:, :S5_WIDTH] * _sigmoid(z[:, S5_WIDTH:])
    o_ref[...] = o.reshape(nb, tt, wd).astype(o_ref.dtype)


def _s5(pm3, bbd_re, bbd_im, ar8, ai8, cbd_re, cbd_im, d_row, w_glu):
    bsz, seq, _ = pm3.shape
    tt = S5_TT
    c2 = lambda i: (0, 0)
    nslab = S5_NSTATE // LANES
    return pl.pallas_call(
        _s5_kernel,
        grid=(seq // tt,),
        in_specs=[pl.BlockSpec((bsz, tt, S5_WIDTH), lambda i: (0, i, _SU0 // S5_WIDTH)),
                  pl.BlockSpec((S5_WIDTH, S5_NSTATE), c2), pl.BlockSpec((S5_WIDTH, S5_NSTATE), c2),
                  pl.BlockSpec((8, S5_NSTATE), c2), pl.BlockSpec((8, S5_NSTATE), c2),
                  pl.BlockSpec((S5_NSTATE, S5_WIDTH), c2), pl.BlockSpec((S5_NSTATE, S5_WIDTH), c2),
                  pl.BlockSpec((1, S5_WIDTH), c2), pl.BlockSpec((S5_WIDTH, 2 * S5_WIDTH), c2)],
        out_specs=pl.BlockSpec((bsz, tt, S5_WIDTH), lambda i: (0, i, 0)),
        out_shape=jax.ShapeDtypeStruct((bsz, seq, S5_WIDTH), BF16),
        scratch_shapes=[pltpu.VMEM((8, S5_NSTATE), F32), pltpu.VMEM((8, S5_NSTATE), F32),
                        pltpu.VMEM((nslab, bsz * S5_PITCH, LANES), F32),
                        pltpu.VMEM((nslab, bsz * S5_PITCH, LANES), F32)],
        compiler_params=_cparams(("arbitrary",)),
        name="s5_scan",
    )(pm3, bbd_re, bbd_im, ar8, ai8, cbd_re, cbd_im, d_row, w_glu)


def _pad_cols(w, n):
    return jnp.pad(w, ((0, 0), (0, n - w.shape[1])))


def _relayout_kernel(pieces, w_ref, o_ref):
    o0 = 0
    for s0, n, sc in pieces:
        v = w_ref[s0:s0 + n, :]
        o_ref[o0:o0 + n, :] = (v if sc is None else v * sc).astype(o_ref.dtype)
        o0 += n


def _relayout_rows(w_t, pieces, tn=256):
    nin, d = w_t.shape
    nout = sum(n for _, n, _ in pieces)
    return pl.pallas_call(
        functools.partial(_relayout_kernel, pieces),
        grid=(d // tn,),
        in_specs=[pl.BlockSpec((nin, tn), lambda i: (0, i))],
        out_specs=pl.BlockSpec((nout, tn), lambda i: (0, i)),
        out_shape=jax.ShapeDtypeStruct((nout, d), BF16),
        compiler_params=_cparams(("parallel",)),
        name="relayout_rows",
    )(w_t)


def _cd_proj_weights(w_in):
    w_t = w_in.T
    d = w_in.shape[0]
    nw, nkv = NSA_HEADS * NSA_DH, NSA_GROUPS * NSA_DH
    offs = [int(o) for o in np.cumsum((0, nw, nkv, nkv, nkv, nkv, nkv, nkv, 3 * NSA_HEADS, nw, S5_WIDTH, S5_WIDTH))]
    def pair(ia, ib, scale=None):
        return [pc for c in range(0, offs[ia + 1] - offs[ia], NSA_DH)
                for pc in ((offs[ia] + c, NSA_DH, scale), (offs[ib] + c, NSA_DH, None))]
    w_main = _relayout_rows(w_t, pair(0, 8, NSA_DH ** -0.5 * LOG2E) + pair(1, 2) + pair(3, 4) + pair(5, 6)
                            + [(offs[9], S5_WIDTH, None), (offs[10], S5_WIDTH, None)])
    w_gate = w_t[offs[7]:offs[8]].reshape(NSA_GROUPS, NSA_HPG, 3, d).transpose(0, 2, 1, 3)
    w_gate = jnp.pad(w_gate.reshape(NSA_GROUPS, 3 * NSA_HPG, d), ((0, 0), (0, LANES - 3 * NSA_HPG), (0, 0)))
    return w_gate.reshape(NSA_GROUPS * LANES, d).astype(BF16), w_main


def _layer0(h2, p4, layer, bsz, seq, norm_pre, norm_post, pe_proj, pe_gate,
            w_in, ret_norm, conv_w, conv_b, gate_b, ml_norm, w_out):
    d = h2.shape[1]
    g0 = 5120
    after = g0 + 2 * ML_HEADS
    w_t = w_in.T
    w_main = _relayout_rows(w_t, [(0, g0, None), (after, w_in.shape[1] - after, None)])
    w_gate = jnp.pad(w_t[g0:after], ((0, LANES - 2 * ML_HEADS), (0, 0))).astype(BF16)
    pm, pg = _norm_proj(h2, norm_pre.reshape(1, d), w_gate, w_main)
    mix = _ab_mixer(pm, pg, conv_w, conv_b.reshape(1, -1), _pad_cols(gate_b.reshape(1, -1), LANES),
                    ret_norm.reshape(1, -1), ml_norm.reshape(1, -1), bsz, seq)
    ka = RET_HEADS * RET_DV
    nt = seq // 512
    return _out_pe(mix, lambda b, s: (b * nt + s, 0), ka, mix, lambda b, s: (b * nt + s, 1), ML_HEADS * ML_DV,
                   None, None, h2, p4, layer, w_out[:ka].astype(BF16), w_out[ka:].astype(BF16),
                   norm_post.reshape(1, d), pe_gate.astype(BF16), pe_proj.astype(BF16), bsz, seq)


def _layer1(h2, p4, layer, bsz, seq, norm_pre, norm_post, pe_proj, pe_gate,
            w_in, pos_k, pos_v, w1_k, w2_k, w1_v, w2_v,
            a_re, a_im, log_dt, b_re, b_im, c_re, c_im, d_skip, w_glu, w_out):
    d = h2.shape[1]
    nw = NSA_HEADS * NSA_DH
    pm, pg = _norm_proj(h2, norm_pre.reshape(1, d), *_cd_proj_weights(w_in))

    nb = seq // CMP_STRIDE
    pm3 = pm.reshape(bsz, seq, CD_MAIN)
    padl = lambda a, lo=True: jnp.pad(a, ((0, 0),) * (a.ndim - 1)
                                      + ((0, LANES - a.shape[-1]) if lo else (LANES - a.shape[-1], 0),))
    w1 = lambda w, lo: jnp.pad(w.reshape(CMP_LEN, NSA_DH, CMP_HIDDEN),
                               ((0, 0), (0, LANES - NSA_DH) if lo else (LANES - NSA_DH, 0), (0, 0))).astype(BF16)
    ek, al, ekc, ovt = _nsa_tables(seq)
    kcc, vct = _compress(pm3, padl(pos_k), padl(pos_v, False), w1(w1_k, True), w1(w1_v, False),
                         padl(w2_k).astype(BF16), padl(w2_v).T.astype(BF16), ekc)
    if nb < LANES:
        kcc = jnp.pad(kcc, ((0, 0), (0, 0), (0, LANES - nb), (0, 0)))
        vct = jnp.pad(vct, ((0, 0), (0, 0), (0, 0), (0, LANES - nb)))
    y_nsa = _nsa(pm3, pg.reshape(bsz, seq, NSA_GROUPS * LANES), kcc, vct, ek, al, ovt)

    rep = lambda a: jnp.repeat(a, S5_GROUP_CH, axis=0)
    ldt = jnp.broadcast_to(rep(log_dt[:, None]), (S5_WIDTH, S5_STATE))
    bt = lambda b: b.transpose(0, 2, 1).reshape(S5_WIDTH, S5_STATE)
    abr, abi, bbr, bbi = _s5_disc(rep(a_re), rep(a_im), ldt, bt(b_re), bt(b_im))
    gi = np.arange(S5_WIDTH) // S5_GROUP_CH
    gs = np.arange(S5_NSTATE) // S5_STATE
    diag_in = jnp.asarray(gi[:, None] == gs[None, :])
    bd_in = lambda bb: jnp.where(diag_in, jnp.tile(bb, (1, S5_GROUPS)), 0.0).astype(BF16)
    bd_out = lambda c: jnp.where(diag_in.T, jnp.tile(c.transpose(0, 2, 1).reshape(S5_NSTATE, S5_GROUP_CH),
                                                     (1, S5_GROUPS)), 0.0).astype(BF16)
    row8 = lambda a: jnp.broadcast_to(a[::S5_GROUP_CH].reshape(1, S5_NSTATE), (8, S5_NSTATE))
    y_s5 = _s5(pm3, bd_in(bbr), bd_in(bbi), row8(abr), row8(abi),
               bd_out(c_re), bd_out(c_im), d_skip.reshape(1, -1), w_glu.astype(BF16))

    nt = seq // 512
    szb = _SZ0 // S5_WIDTH
    return _out_pe(y_nsa.reshape(bsz * seq, nw), lambda b, s: (b * nt + s, 0), nw,
                   y_s5.reshape(bsz * seq, S5_WIDTH), lambda b, s: (b * nt + s, 0), S5_WIDTH,
                   pm, lambda b, s: (b * nt + s, szb),
                   h2, p4, layer, w_out[:nw].astype(BF16), w_out[nw:].astype(BF16),
                   norm_post.reshape(1, d), pe_gate.astype(BF16), pe_proj.astype(BF16), bsz, seq)


def kernel(x, p, norm_pre, norm_post, pe_proj, pe_gate, ab_w_in, ret_norm, ml_conv_w, ml_conv_b, ml_gate_b,
           ml_norm, ab_w_out, cd_w_in, cmp_pos_k, cmp_pos_v, cmp_w1_k, cmp_w2_k, cmp_w1_v, cmp_w2_v,
           s5_a_re, s5_a_im, s5_log_dt, s5_b_re, s5_b_im, s5_c_re, s5_c_im, s5_d, s5_w_glu, cd_w_out):
    bsz, seq, d = x.shape
    assert bsz == 8, "the S5 scan maps the batch onto the 8 sublanes of a vreg"
    assert seq % max(KCH, 1024) == 0
    depth = p.shape[0]
    h2 = x.reshape(bsz * seq, d)
    for i in range(depth):
        j = i // 2
        if i % 2 == 0:
            h2 = _layer0(h2, p, i, bsz, seq, norm_pre[i], norm_post[i], pe_proj[i], pe_gate[i],
                         ab_w_in[j], ret_norm[j], ml_conv_w[j], ml_conv_b[j], ml_gate_b[j], ml_norm[j], ab_w_out[j])
        else:
            h2 = _layer1(h2, p, i, bsz, seq, norm_pre[i], norm_post[i], pe_proj[i], pe_gate[i],
                         cd_w_in[j], cmp_pos_k[j], cmp_pos_v[j], cmp_w1_k[j], cmp_w2_k[j], cmp_w1_v[j], cmp_w2_v[j],
                         s5_a_re[j], s5_a_im[j], s5_log_dt[j], s5_b_re[j], s5_b_im[j], s5_c_re[j], s5_c_im[j],
                         s5_d[j], s5_w_glu[j], cd_w_out[j])
    return h2.reshape(bsz, seq, d)
```

```python
import functools
import math

import numpy as np
import jax
import jax.numpy as jnp
from jax import lax
from jax.experimental import pallas as pl
from jax.experimental.pallas import tpu as pltpu

F32 = jnp.float32
BF16 = jnp.bfloat16
EPS = 1e-6
NEG = -1e30
BIG = 1e30

LANES = 128
CHUNK = 128
RET_HEADS, RET_DK, RET_DV = 4, 128, 256
ML_HEADS, ML_DK, ML_DV = 4, 128, 256
ML_CONV = 4
NSA_HEADS, NSA_GROUPS, NSA_DH = 16, 4, 64
NSA_HPG = NSA_HEADS // NSA_GROUPS
CMP_STRIDE, CMP_LEN, CMP_HIDDEN = 16, 32, 128
SEL_BLOCK, SEL_TOPN, WINDOW = 64, 4, 512
S5_GROUPS, S5_GROUP_CH, S5_STATE = 32, 16, 64
S5_WIDTH = S5_GROUPS * S5_GROUP_CH
S5_NSTATE = S5_GROUPS * S5_STATE

TQ = 256
KCH = 512
SEL_LANE0 = 64
ALI_LANE0 = 96
VMEM_LIMIT = 56 * 1024 * 1024


def _cparams(sem):
    return pltpu.CompilerParams(dimension_semantics=sem, vmem_limit_bytes=VMEM_LIMIT)


def _dot(a, b):
    return jnp.dot(a, b, preferred_element_type=F32)


def _dot_nt(a, b):
    return lax.dot_general(a, b, (((1,), (1,)), ((), ())), preferred_element_type=F32)


def _split3(x):
    hi = x.astype(BF16)
    r1 = x - hi.astype(F32)
    mid = r1.astype(BF16)
    lo = (r1 - mid.astype(F32)).astype(BF16)
    return hi, mid, lo


def _dot_f32_lhs(x, m):
    hi, mid, lo = _split3(x)
    return _dot(hi, m) + _dot(mid, m) + _dot(lo, m)


def _sigmoid(x):
    return 1.0 / (1.0 + jnp.exp(-x))


def _silu(x):
    return x * _sigmoid(x)


def _gelu(x):
    return 0.5 * x * (1.0 + jnp.tanh(math.sqrt(2.0 / math.pi) * (x + 0.044715 * (x * x * x))))


def _rms(x, g):
    return x * lax.rsqrt(jnp.mean(x * x, -1, keepdims=True) + EPS) * g


PROJ_TN = 1024


def _norm_proj_kernel(h_ref, g_ref, wg_ref, wm_ref, om_ref, og_ref):
    ub = _rms(h_ref[...], g_ref[...]).astype(BF16)
    nm = om_ref.shape[1]
    for c0 in range(0, nm, PROJ_TN):
        c1 = min(c0 + PROJ_TN, nm)
        om_ref[:, c0:c1] = _dot_nt(ub, wm_ref[c0:c1, :]).astype(om_ref.dtype)
    og_ref[...] = _dot_nt(ub, wg_ref[...])


def _norm_proj(h2, g, w_gate_t, w_main_t, tm=512):
    n, d = h2.shape
    nm = w_main_t.shape[0]
    ng = w_gate_t.shape[0]
    once = pl.Buffered(1)
    return pl.pallas_call(
        _norm_proj_kernel,
        grid=(n // tm,),
        in_specs=[pl.BlockSpec((tm, d), lambda i: (i, 0)),
                  pl.BlockSpec((1, d), lambda i: (0, 0)),
                  pl.BlockSpec((ng, d), lambda i: (0, 0), pipeline_mode=once),
                  pl.BlockSpec((nm, d), lambda i: (0, 0), pipeline_mode=once)],
        out_specs=[pl.BlockSpec((tm, nm), lambda i: (i, 0)),
                   pl.BlockSpec((tm, ng), lambda i: (i, 0))],
        out_shape=[jax.ShapeDtypeStruct((n, nm), BF16), jax.ShapeDtypeStruct((n, ng), F32)],
        compiler_params=_cparams(("parallel",)),
        name="norm_proj",
    )(h2, g, w_gate_t, w_main_t)


def _out_pe_kernel(gated, *refs):
    if gated:
        ma_ref, mb_ref, sz_ref, h_ref, p_ref, wa_ref, wb_ref, gp_ref, wg_ref, wp_ref, o_ref = refs
        sz = sz_ref[...].astype(F32)
        mb = (mb_ref[...].astype(F32) * _silu(sz)).astype(BF16)
    else:
        ma_ref, mb_ref, h_ref, p_ref, wa_ref, wb_ref, gp_ref, wg_ref, wp_ref, o_ref = refs
        mb = mb_ref[...]
    y = _dot(ma_ref[...], wa_ref[...]) + _dot(mb, wb_ref[...])
    h1 = h_ref[...] + _rms(y, gp_ref[...])
    gate = _sigmoid(_dot(h1.astype(BF16), wg_ref[...]))
    pp = _dot(p_ref[...].astype(BF16), wp_ref[...])
    o_ref[...] = h1 + gate * pp


def _out_pe(mix_a, a_map, ka, mix_b, b_map, kb, sz, sz_map, h2, p4, layer, w_a, w_b, g_post, w_gate, w_pe,
            bsz, seq, tm=512):
    n, d = h2.shape
    pe = p4.shape[-1]
    nt = seq // tm
    row = lambda b, s: (b * nt + s, 0)
    const = lambda b, s: (0, 0)
    gated = sz is not None
    in_specs = [pl.BlockSpec((tm, ka), a_map), pl.BlockSpec((tm, kb), b_map)]
    args = [mix_a, mix_b]
    if gated:
        in_specs.append(pl.BlockSpec((tm, kb), sz_map))
        args.append(sz)
    in_specs += [pl.BlockSpec((tm, d), row), pl.BlockSpec((None, None, tm, pe), lambda b, s: (layer, b, s, 0)),
                 pl.BlockSpec((ka, d), const), pl.BlockSpec((kb, d), const), pl.BlockSpec((1, d), const),
                 pl.BlockSpec((d, d), const), pl.BlockSpec((pe, d), const)]
    args += [h2, p4, w_a, w_b, g_post, w_gate, w_pe]
    return pl.pallas_call(
        functools.partial(_out_pe_kernel, gated),
        grid=(bsz, nt),
        in_specs=in_specs,
        out_specs=pl.BlockSpec((tm, d), row),
        out_shape=jax.ShapeDtypeStruct((n, d), F32),
        compiler_params=_cparams(("parallel", "arbitrary")),
        name="out_pe_gated" if gated else "out_pe",
    )(*args)


_RQ, _RK, _RV, _RG = 0, 512, 1024, 2048
_MQ, _MK, _MV, _MO, _MZ = 3072, 3584, 4096, 5120, 6144
AB_MAIN = 7168


def _head_norm(y, g):
    yc = y - jnp.mean(y, -1, keepdims=True)
    return yc * lax.rsqrt(jnp.mean(yc * yc, -1, keepdims=True) + EPS) * g


def _ab_kernel(pm_ref, pg_ref, cw_ref, cb_ref, gb_ref, rn_ref, mn_ref, mix_ref,
               ext_ref, r_ref, c_ref, n_ref, m_ref):
    L = CHUNK
    c_idx = pl.program_id(1)

    @pl.when(c_idx == 0)
    def _():
        ext_ref[0:8, :] = jnp.zeros((8, ext_ref.shape[1]), F32)
        r_ref[...] = jnp.zeros(r_ref.shape, F32)
        c_ref[...] = jnp.zeros(c_ref.shape, F32)
        n_ref[...] = jnp.zeros(n_ref.shape, F32)
        m_ref[...] = jnp.full(m_ref.shape, NEG, F32)

    row = lax.broadcasted_iota(jnp.int32, (L, L), 0)
    col = lax.broadcasted_iota(jnp.int32, (L, L), 1)
    causal = row >= col
    diff = (row - col).astype(F32)
    tcol = lax.broadcasted_iota(jnp.int32, (L, 1), 0).astype(F32)

    for h in range(RET_HEADS):
        lg = math.log1p(-(2.0 ** (-5.0 - h)))
        scale = RET_DK ** -0.5
        decay = jnp.where(causal, jnp.exp(jnp.maximum(diff, 0.0) * lg), 0.0) * scale
        q = pm_ref[:, _RQ + h * RET_DK:_RQ + (h + 1) * RET_DK]
        k = pm_ref[:, _RK + h * RET_DK:_RK + (h + 1) * RET_DK]
        v = pm_ref[:, _RV + h * RET_DV:_RV + (h + 1) * RET_DV]
        sc = _dot_nt(q, k) * decay
        xi = jnp.exp((tcol + 1.0) * lg) * scale
        qx = (q.astype(F32) * xi).astype(BF16)
        r_prev = r_ref[h]
        y = _dot(sc.astype(BF16), v) + _dot(qx, r_prev.astype(BF16))
        zeta = jnp.exp((L - 1.0 - tcol) * lg)
        kz_t = (k.astype(F32) * zeta).T.astype(BF16)
        r_ref[h] = r_prev * math.exp(L * lg) + _dot(kz_t, v)
        g = pm_ref[:, _RG + h * RET_DV:_RG + (h + 1) * RET_DV].astype(F32)
        out = _head_norm(y, rn_ref[:, h * RET_DV:(h + 1) * RET_DV]) * _silu(g)
        mix_ref[:, h * RET_DV:(h + 1) * RET_DV] = out.astype(mix_ref.dtype)

    nqk = 2 * ML_HEADS * ML_DK
    x = pm_ref[:, _MQ:_MQ + nqk].astype(F32)
    ext_ref[8:8 + L, :] = x
    acc = cb_ref[...] + cw_ref[ML_CONV - 1:ML_CONV, :] * x
    for j in range(ML_CONV - 1):
        acc = acc + cw_ref[j:j + 1, :] * ext_ref[pl.ds(8 - (ML_CONV - 1) + j, L), :]
    ext_ref[0:8, :] = x[L - 8:L, :]
    qk = _silu(acc)

    gt = (pg_ref[...] + gb_ref[...]).T
    g8 = gt[0:8, :]
    logf8 = jnp.minimum(g8, 0.0) - jnp.log(1.0 + jnp.exp(-jnp.abs(g8)))
    triu = jnp.where(row <= col, 1.0, 0.0).astype(BF16)
    bcum8 = _dot_f32_lhs(logf8, triu)
    b_t = pltpu.roll(bcum8, 4, axis=0)
    b_last = b_t[:, L - 1:L]
    a_t = b_last - b_t + g8
    mu = jnp.max(a_t, axis=1, keepdims=True)
    w_t = jnp.exp(a_t - mu)
    r_t = g8 - b_t
    zpad = jnp.zeros((L - 8, L), F32)
    bcol = jnp.concatenate([b_t, zpad], axis=0).T
    wcol = jnp.concatenate([w_t, zpad], axis=0).T
    m_prev = m_ref[...][:, 0:1]
    m_new = jnp.maximum(b_last + m_prev, mu)
    sp = jnp.exp(b_last + m_prev - m_new)
    scn = jnp.exp(mu - m_new)
    m_ref[...] = jnp.broadcast_to(m_new, m_ref.shape)

    for h in range(ML_HEADS):
        bc = bcol[:, h:h + 1]
        log_d = jnp.where(causal, bc + r_t[h:h + 1, :], NEG)
        inter = bc + m_prev[h:h + 1, :]
        m_t = jnp.maximum(inter, jnp.max(log_d, axis=1, keepdims=True))
        dmat = jnp.exp(log_d - m_t)
        w_int = jnp.exp(inter - m_t)
        qh = qk[:, h * ML_DK:(h + 1) * ML_DK]
        kh = qk[:, ML_HEADS * ML_DK + h * ML_DK:ML_HEADS * ML_DK + (h + 1) * ML_DK] * (ML_DK ** -0.5)
        qb = qh.astype(BF16)
        v = pm_ref[:, _MV + h * ML_DV:_MV + (h + 1) * ML_DV]
        s = _dot_nt(qb, kh.astype(BF16)) * dmat
        c_prev = c_ref[h]
        n_prev = n_ref[h:h + 1, :]
        num = _dot(s.astype(BF16), v) + w_int * _dot(qb, c_prev.astype(BF16))
        den = jnp.sum(s, axis=1, keepdims=True) + w_int * jnp.sum(qh * n_prev, axis=1, keepdims=True)
        den = jnp.maximum(jnp.abs(den), jnp.exp(-m_t))
        hcell = num * (1.0 / den)
        o = pm_ref[:, _MO + h * ML_DV:_MO + (h + 1) * ML_DV].astype(F32)
        y = _sigmoid(o) * hcell
        z = pm_ref[:, _MZ + h * ML_DV:_MZ + (h + 1) * ML_DV].astype(F32)
        out = _head_norm(y, mn_ref[:, h * ML_DV:(h + 1) * ML_DV]) * _silu(z)
        mix_ref[:, RET_HEADS * RET_DV + h * ML_DV:RET_HEADS * RET_DV + (h + 1) * ML_DV] = out.astype(mix_ref.dtype)
        kw = kh * wcol[:, h:h + 1]
        kv = _dot(kw.T.astype(BF16), v)
        ksum = jnp.sum(kw, axis=0, keepdims=True)
        sp_h = sp[h:h + 1, :]
        sc_h = scn[h:h + 1, :]
        c_ref[h] = sp_h * c_prev + sc_h * kv
        n_ref[h:h + 1, :] = sp_h * n_prev + sc_h * ksum


def _ab_mixer(pm, pg, conv_w, conv_b, gate_b, ret_norm, ml_norm, bsz, seq):
    n = pm.shape[0]
    nc = seq // CHUNK
    nqk = 2 * ML_HEADS * ML_DK
    row = lambda b, c: (b * nc + c, 0)
    const = lambda b, c: (0, 0)
    nmix = RET_HEADS * RET_DV + ML_HEADS * ML_DV
    return pl.pallas_call(
        _ab_kernel,
        grid=(bsz, nc),
        in_specs=[pl.BlockSpec((CHUNK, AB_MAIN), row), pl.BlockSpec((CHUNK, LANES), row),
                  pl.BlockSpec((ML_CONV, nqk), const), pl.BlockSpec((1, nqk), const),
                  pl.BlockSpec((1, LANES), const),
                  pl.BlockSpec((1, RET_HEADS * RET_DV), const), pl.BlockSpec((1, ML_HEADS * ML_DV), const)],
        out_specs=pl.BlockSpec((CHUNK, nmix), row),
        out_shape=jax.ShapeDtypeStruct((n, nmix), BF16),
        scratch_shapes=[pltpu.VMEM((CHUNK + 8, nqk), F32),
                        pltpu.VMEM((RET_HEADS, RET_DK, RET_DV), F32),
                        pltpu.VMEM((ML_HEADS, ML_DK, ML_DV), F32),
                        pltpu.VMEM((8, ML_DK), F32),
                        pltpu.VMEM((8, LANES), F32)],
        compiler_params=_cparams(("parallel", "arbitrary")),
        name="ab_mixer",
    )(pm, pg, conv_w, conv_b, gate_b, ret_norm, ml_norm)


_QNZ0 = 0
_KVC0 = NSA_HEADS
_KVS0 = _KVC0 + NSA_GROUPS
_KVW0 = _KVS0 + NSA_GROUPS
_SU0 = (_KVW0 + NSA_GROUPS) * LANES
_SZ0 = _SU0 + S5_WIDTH
CD_MAIN = _SZ0 + S5_WIDTH
SLOTW = NSA_GROUPS * LANES
VT_ROWS = NSA_DH + 16
LOG2E = math.log2(math.e)


def _compress_kernel(x_ref, pk_ref, pv_ref, w1k_ref, w1v_ref, w2k_ref, w2vt_ref, ekc_ref,
                     ko_ref, vot_ref, xs_ref):
    seq = x_ref.shape[0]
    nb = seq // CMP_STRIDE
    for g in range(NSA_GROUPS):
        xs_ref[...] = x_ref[:, g * LANES:(g + 1) * LANES].astype(F32)
        acc = [jnp.zeros((nb, CMP_HIDDEN), F32) for _ in range(4)]
        for r in range(CMP_STRIDE):
            xr = xs_ref[pl.ds(r, nb, stride=CMP_STRIDE), :]
            for kind, (p_ref, w1_ref) in enumerate(((pk_ref, w1k_ref), (pv_ref, w1v_ref))):
                acc[2 * kind] = acc[2 * kind] + _dot((xr + p_ref[r:r + 1, :]).astype(BF16), w1_ref[r])
                acc[2 * kind + 1] = acc[2 * kind + 1] + _dot(
                    (xr + p_ref[CMP_STRIDE + r:CMP_STRIDE + r + 1, :]).astype(BF16), w1_ref[CMP_STRIDE + r])
        hk = _gelu(acc[0] + pltpu.roll(acc[1], nb - 1, axis=0)).astype(BF16)
        hv = _gelu(acc[2] + pltpu.roll(acc[3], nb - 1, axis=0)).astype(BF16)
        ko_ref[g] = (_dot(hk, w2k_ref[...]) + ekc_ref[...]).astype(ko_ref.dtype)
        vot_ref[g] = _dot_nt(w2vt_ref[...], hv).astype(vot_ref.dtype)


def _compress(pm3, pos_k, pos_v, w1k, w1v, w2k, w2vt, ekc):
    bsz, seq, _ = pm3.shape
    nb = seq // CMP_STRIDE
    c3 = lambda b: (0, 0, 0)
    c2 = lambda b: (0, 0)
    return pl.pallas_call(
        _compress_kernel,
        grid=(bsz,),
        in_specs=[pl.BlockSpec((None, seq, SLOTW), lambda b: (b, 0, _KVC0 * LANES // SLOTW)),
                  pl.BlockSpec((CMP_LEN, LANES), c2), pl.BlockSpec((CMP_LEN, LANES), c2),
                  pl.BlockSpec((CMP_LEN, LANES, CMP_HIDDEN), c3), pl.BlockSpec((CMP_LEN, LANES, CMP_HIDDEN), c3),
                  pl.BlockSpec((CMP_HIDDEN, LANES), c2), pl.BlockSpec((LANES, CMP_HIDDEN), c2),
                  pl.BlockSpec((nb, LANES), c2)],
        out_specs=[pl.BlockSpec((None, NSA_GROUPS, nb, LANES), lambda b: (b, 0, 0, 0)),
                   pl.BlockSpec((None, NSA_GROUPS, LANES, nb), lambda b: (b, 0, 0, 0))],
        out_shape=[jax.ShapeDtypeStruct((bsz, NSA_GROUPS, nb, LANES), BF16),
                   jax.ShapeDtypeStruct((bsz, NSA_GROUPS, LANES, nb), BF16)],
        scratch_shapes=[pltpu.VMEM((seq, LANES), F32)],
        compiler_params=_cparams(("parallel",)),
        name="nsa_compress",
    )(pm3, pos_k, pos_v, w1k, w1v, w2k, w2vt, ekc)


def _nsa_kernel(q_ref, kc_ref, vct_ref, kvs_ref, kvw_ref, pg_ref,
                ek_ref, al_ref, ovt_ref, y_ref,
                ksa_ref, vsat_ref, kwa_ref, vwat_ref, qa_scr, part_scr, gzs_scr, sel_smem):
    seq = kvs_ref.shape[0]
    g_idx = pl.program_id(1)
    qi = pl.program_id(2)
    wtiles = WINDOW // LANES

    @pl.when(qi == 0)
    def _():
        ek = ek_ref[...]
        lane_k = lax.broadcasted_iota(jnp.int32, (seq, LANES), 1)
        zero = jnp.zeros((seq, LANES), BF16)
        ksa_ref[0:seq, :] = jnp.where(lane_k < NSA_DH, kvs_ref[...], zero) + ek
        lane_d = lax.broadcasted_iota(jnp.int32, (LANES, LANES), 1)
        ksa_ref[seq:seq + LANES, :] = jnp.where((lane_d >= SEL_LANE0) & (lane_d < SEL_LANE0 + seq // SEL_BLOCK),
                                                1.0, 0.0).astype(BF16)
        vsat_ref[seq // LANES] = jnp.zeros((VT_ROWS, LANES), BF16)
        ek_w = jnp.where(lane_k >= ALI_LANE0, ek, zero)
        kwa_ref[0:WINDOW, :] = jnp.zeros((WINDOW, LANES), BF16)
        kwa_ref[WINDOW:WINDOW + seq, :] = jnp.where(lane_k < NSA_DH, kvw_ref[...], zero) + ek_w
        ones_rows = jnp.where(lax.broadcasted_iota(jnp.int32, (VT_ROWS - NSA_DH, LANES), 0) == 0, 1.0, 0.0)
        for i in range(wtiles):
            vwat_ref[i] = jnp.zeros((VT_ROWS, LANES), BF16)
        for i in range(seq // LANES):
            rws = slice(i * LANES, (i + 1) * LANES)
            for src_ref, dst_ref, j in ((kvs_ref, vsat_ref, i), (kvw_ref, vwat_ref, wtiles + i)):
                v_t = src_ref[rws, :].astype(F32).T[NSA_DH:2 * NSA_DH, :]
                dst_ref[j] = jnp.concatenate([v_t, ones_rows], axis=0).astype(BF16)

    nq = seq // TQ
    refs = (q_ref, kc_ref, vct_ref, pg_ref, al_ref, ovt_ref, y_ref, ksa_ref, vsat_ref, kwa_ref, vwat_ref,
            qa_scr, part_scr, gzs_scr, sel_smem)
    for step in range(nq + 1):
        pl.when(qi == step)(functools.partial(_nsa_step, step, nq, seq // SEL_BLOCK, g_idx, refs))


SEL_DYN = 2
SEL_CNT = 7


def _sel_compactable(qi):
    return (qi + 1) * TQ // LANES > 1 + SEL_DYN + TQ // LANES


def _nsa_step(step, nq, nsel, g_idx, refs):
    def run(compact):
        stages = []
        if step < nq:
            stages.append(_nsa_front(nsel, step, g_idx, refs))
        if step > 0:
            stages.append(_nsa_back(step - 1, compact, refs))
        while stages:
            for st in list(stages):
                if next(st, "done") == "done":
                    stages.remove(st)

    if step > 0 and _sel_compactable(step - 1):
        fits = refs[-1][(step - 1) & 1, SEL_CNT] <= SEL_DYN
        pl.when(fits)(functools.partial(run, True))
        pl.when(jnp.logical_not(fits))(functools.partial(run, False))
    else:
        run(False)


def _tile_iotas():
    rows = NSA_HPG * TQ
    jrow = lax.broadcasted_iota(jnp.int32, (LANES, rows), 0)
    tq = lax.broadcasted_iota(jnp.int32, (LANES, rows), 1) & (TQ - 1)
    return jrow, tq, jrow - tq


def _nsa_front(nsel, qi, g_idx, refs):
    (q_ref, kc_ref, vct_ref, pg_ref, al_ref, ovt_ref, _, _, _, kwa_ref, vwat_ref,
     qa_scr, part_scr, gzs_scr, sel_smem) = refs
    hp = NSA_HPG
    span = WINDOW + TQ
    t0 = qi * TQ
    tw = t0
    early = t0 < WINDOW
    par = qi & 1
    jrow, tq, kmt = _tile_iotas()

    qnz = [q_ref[:, hh * LANES:(hh + 1) * LANES].astype(F32) for hh in range(hp)]
    lane_q = lax.broadcasted_iota(jnp.int32, (TQ, LANES), 1)
    qal = jnp.concatenate([jnp.where(lane_q < NSA_DH, qnz[hh], al_ref[pl.ds(g_idx * hp + hh, 1), :])
                           for hh in range(hp)], axis=0)
    qalb = qal.astype(BF16)
    s_c = _dot_nt(kc_ref[...], qalb)
    s_w = _dot_nt(kwa_ref[pl.ds(tw, span), :], qalb)
    yield

    s = jnp.where(tq - jrow * CMP_STRIDE >= (CMP_LEN - 1) - t0, s_c, NEG)
    e = jnp.exp2(s - jnp.max(s, axis=0, keepdims=True))
    inv = 1.0 / jnp.sum(e, axis=0, keepdims=True)
    if early:
        inv = jnp.where(t0 + tq[0:1, :] >= CMP_LEN - 1, inv, 0.0)
    p = e * inv
    oc_t = _dot(vct_ref[...], p.astype(BF16))
    psum_t = p[:, 0:TQ]
    for hh in range(1, hp):
        psum_t = psum_t + p[:, hh * TQ:(hh + 1) * TQ]
    p_hi = psum_t.astype(BF16)
    p_lo = (psum_t - p_hi.astype(F32)).astype(BF16)
    imp_t = _dot(ovt_ref[...], p_hi) + _dot(ovt_ref[...], p_lo)
    yield

    blocks = [s_w[i * LANES:(i + 1) * LANES, :] for i in range(span // LANES)]
    for i in range(span // LANES):
        if i * LANES < TQ:
            blocks[i] = jnp.where(kmt > -i * LANES, blocks[i], NEG)
        if (i + 1) * LANES > WINDOW:
            blocks[i] = jnp.where(kmt <= WINDOW - i * LANES, blocks[i], NEG)
        if early:
            blocks[i] = jnp.where(jrow >= WINDOW - t0 - i * LANES, blocks[i], NEG)
    s = jnp.concatenate(blocks, axis=0)
    p = jnp.exp2(s - jnp.max(s, axis=0, keepdims=True)).astype(BF16)
    vw_t = jnp.concatenate([vwat_ref[qi * (TQ // LANES) + i] for i in range(span // LANES)], axis=1)
    accw_t = _dot(vw_t, p)
    yield

    score = imp_t[SEL_LANE0:SEL_LANE0 + nsel, :]
    nrow = lax.broadcasted_iota(jnp.int32, (nsel, TQ), 0)
    own = (t0 + lax.broadcasted_iota(jnp.int32, (nsel, TQ), 1)) >> int(math.log2(SEL_BLOCK))
    valid = nrow <= own
    forced = (nrow == 0) | (nrow == own)
    score = jnp.where(forced, BIG, jnp.where(valid, score, -1.0))
    chosen = jnp.zeros((nsel, TQ), F32)
    nrow_f = nrow.astype(F32)
    for _ in range(SEL_TOPN):
        mx = jnp.max(score, axis=0, keepdims=True)
        first = jnp.min(jnp.where(score == mx, nrow_f, 4.0 * LANES), axis=0, keepdims=True)
        hit = nrow_f == first
        chosen = jnp.where(hit, 1.0, chosen)
        score = jnp.where(hit, -3.0, score)
    mask_t = jnp.where(chosen > 0.0, jnp.where(valid, 0.0, NEG), NEG)
    if _sel_compactable(qi):
        used_blk = jnp.max(jnp.where(mask_t == 0.0, 1.0, 0.0), axis=1, keepdims=True)
        blk_per_tile = LANES // SEL_BLOCK
        for slot in range(SEL_DYN):
            sel_smem[par, slot] = jnp.int32(nsel // blk_per_tile)
        cnt = jnp.int32(0)
        for i in range(1, t0 // LANES):
            used = jnp.max(used_blk[i * blk_per_tile:(i + 1) * blk_per_tile, :]) > 0.0

            @pl.when(used & (cnt < SEL_DYN))
            def _(i=i, cnt=cnt):
                sel_smem[par, cnt] = jnp.int32(i)

            cnt = cnt + used.astype(jnp.int32)
        sel_smem[par, SEL_CNT] = cnt
    maskcols = jnp.concatenate([jnp.zeros((SEL_LANE0, TQ), F32), mask_t,
                                jnp.zeros((LANES - SEL_LANE0 - nsel, TQ), F32)], axis=0).T

    qa_scr[par] = (qal + jnp.concatenate([maskcols] * hp, axis=0)).astype(BF16)

    gt_t = _sigmoid(pg_ref[...]).T
    for hh in range(hp):
        g_c, g_s, g_w = [gt_t[j * hp + hh:j * hp + hh + 1, :] for j in range(3)]
        hcols = slice(hh * TQ, (hh + 1) * TQ)
        aw_t = accw_t[:, hcols]
        gz = _silu(qnz[hh].T[NSA_DH:2 * NSA_DH, :])
        o = g_c * oc_t[0:NSA_DH, hcols] + (g_w * (1.0 / aw_t[NSA_DH:NSA_DH + 1, :])) * aw_t[0:NSA_DH, :]
        part_scr[par, hh * NSA_DH:(hh + 1) * NSA_DH, :] = o * gz
        gzs_scr[par, hh * NSA_DH:(hh + 1) * NSA_DH, :] = g_s * gz


def _nsa_back(qi, compact, refs):
    _, _, _, _, _, _, y_ref, ksa_ref, vsat_ref, _, _, qa_scr, part_scr, gzs_scr, sel_smem = refs
    hp = NSA_HPG
    t0 = qi * TQ
    par = qi & 1
    _, _, kmt = _tile_iotas()
    qa = qa_scr[par]
    ntiles = (t0 + TQ) // LANES
    if compact:
        tiles = [0] + [sel_smem[par, j] for j in range(SEL_DYN)] + list(range(t0 // LANES, ntiles))
    else:
        tiles = list(range(ntiles))
    per = len(tiles) if compact else KCH // LANES
    chunks = [tiles[i:i + per] for i in range(0, len(tiles), per)]
    nk = len(chunks)

    def k_tile(i):
        if isinstance(i, int):
            return ksa_ref[i * LANES:(i + 1) * LANES, :]
        return ksa_ref[pl.ds(pl.multiple_of(i * LANES, LANES), LANES), :]

    hw = hp // 2
    kmt_h = kmt[:, 0:hw * TQ]

    def scores(c, qa_h):
        s = _dot_nt(jnp.concatenate([k_tile(i) for i in chunks[c]], axis=0), qa_h)
        if any(isinstance(i, int) and (i + 1) * LANES > t0 for i in chunks[c]):
            s = jnp.concatenate(
                [jnp.where(kmt_h <= t0 - i * LANES, s[n * LANES:(n + 1) * LANES, :], NEG)
                 if isinstance(i, int) and (i + 1) * LANES > t0 else s[n * LANES:(n + 1) * LANES, :]
                 for n, i in enumerate(chunks[c])], axis=0)
        return s

    def weighted(c, s):
        mc = jnp.max(s, axis=0, keepdims=True)
        v_t = jnp.concatenate([vsat_ref[i] for i in chunks[c]], axis=1)
        return mc, _dot(v_t, jnp.exp2(s - mc).astype(BF16))

    outs = []
    for hf in range(hp // hw):
        qa_h = qa[hf * hw * TQ:(hf + 1) * hw * TQ, :]
        parts = []
        s_next = scores(0, qa_h)
        for c in range(nk):
            s_cur = s_next
            if c + 1 < nk:
                s_next = scores(c + 1, qa_h)
            yield
            parts.append(weighted(c, s_cur))
        yield
        m = parts[0][0]
        for mc, _ in parts[1:]:
            m = jnp.maximum(m, mc)
        accs_t = None
        for mc, d in parts:
            d = d if nk == 1 else jnp.exp2(mc - m) * d
            accs_t = d if accs_t is None else accs_t + d
        for hl in range(hw):
            hh = hf * hw + hl
            as_t = accs_t[:, hl * TQ:(hl + 1) * TQ]
            rws = slice(hh * NSA_DH, (hh + 1) * NSA_DH)
            outs.append(part_scr[par, rws, :]
                        + (gzs_scr[par, rws, :] * (1.0 / as_t[NSA_DH:NSA_DH + 1, :])) * as_t[0:NSA_DH, :])
    for pr in range(hp // 2):
        packed = jnp.concatenate([outs[2 * pr], outs[2 * pr + 1]], axis=0).T
        y_ref[:, pr * LANES:(pr + 1) * LANES] = packed.astype(y_ref.dtype)


def _nsa(pm3, pg3, kcc, vct, ek, al, ovt):
    bsz, seq, _ = pm3.shape
    nq = seq // TQ
    gw = NSA_HPG * NSA_DH
    kv = lambda tile0: pl.BlockSpec((None, seq, LANES), lambda b, g, q: (b, 0, tile0 + g))
    c2 = lambda b, g, q: (0, 0)
    front = lambda b, g, q: (b, jnp.minimum(q, nq - 1), g)
    back = lambda b, g, q: (b, jnp.maximum(q - 1, 0), g)
    return pl.pallas_call(
        _nsa_kernel,
        grid=(bsz, NSA_GROUPS, nq + 1),
        in_specs=[pl.BlockSpec((None, TQ, NSA_HPG * LANES), front),
                  pl.BlockSpec((None, None, LANES, LANES), lambda b, g, q: (b, g, 0, 0)),
                  pl.BlockSpec((None, None, LANES, LANES), lambda b, g, q: (b, g, 0, 0)),
                  kv(_KVS0), kv(_KVW0),
                  pl.BlockSpec((None, TQ, LANES), front),
                  pl.BlockSpec((seq, LANES), c2),
                  pl.BlockSpec((NSA_HEADS, LANES), c2),
                  pl.BlockSpec((LANES, LANES), c2)],
        out_specs=pl.BlockSpec((None, TQ, gw), back),
        out_shape=jax.ShapeDtypeStruct((bsz, seq, NSA_HEADS * NSA_DH), BF16),
        scratch_shapes=[pltpu.VMEM((seq + LANES, LANES), BF16),
                        pltpu.VMEM((seq // LANES + 1, VT_ROWS, LANES), BF16),
                        pltpu.VMEM((seq + WINDOW, LANES), BF16),
                        pltpu.VMEM(((seq + WINDOW) // LANES, VT_ROWS, LANES), BF16),
                        pltpu.VMEM((2, NSA_HPG * TQ, LANES), BF16),
                        pltpu.VMEM((2, gw, TQ), F32), pltpu.VMEM((2, gw, TQ), F32),
                        pltpu.SMEM((2, SEL_CNT + 1), jnp.int32)],
        compiler_params=_cparams(("parallel", "arbitrary", "arbitrary")),
        name="nsa_attn",
    )(pm3, kcc, vct, pm3, pm3, pg3, ek, al, ovt)


def _nsa_tables(seq):
    t = np.arange(seq)
    ek = np.zeros((seq, LANES), np.float32)
    ek[t, SEL_LANE0 + t // SEL_BLOCK] = 1.0
    for c in range(3):
        ek[:, ALI_LANE0 + c] = t // SEL_BLOCK
        ek[:, ALI_LANE0 + 3 + c] = t % SEL_BLOCK
    slopes = jnp.exp2(-8.0 * jnp.arange(1, NSA_HEADS + 1, dtype=F32) / NSA_HEADS) * LOG2E
    s_hi, s_mid, s_lo = _split3(slopes)
    parts = [p.astype(F32) for p in (s_hi, s_mid, s_lo)]
    al = jnp.pad(jnp.stack([p * SEL_BLOCK for p in parts] + parts, axis=1),
                 ((0, 0), (ALI_LANE0, LANES - ALI_LANE0 - 2 * len(parts))))
    nb = seq // CMP_STRIDE
    cstart = np.arange(nb) * CMP_STRIDE
    cend = cstart + CMP_LEN - 1
    ekc = np.zeros((nb, LANES), np.float32)
    for c in range(3):
        ekc[:, ALI_LANE0 + c] = cend // SEL_BLOCK
        ekc[:, ALI_LANE0 + 3 + c] = cend % SEL_BLOCK
    sel = np.arange(seq // SEL_BLOCK)
    ovt = np.zeros((LANES, max(nb, LANES)), np.float32)
    ovl = (cstart[:, None] < (sel[None, :] + 1) * SEL_BLOCK) & (cstart[:, None] + CMP_LEN > sel[None, :] * SEL_BLOCK)
    ovl[nb - 1, :] = False
    ovt[SEL_LANE0:SEL_LANE0 + len(sel), :nb] = ovl.T
    return jnp.asarray(ek, BF16), al, jnp.asarray(ekc), jnp.asarray(ovt, BF16)


def _s5_disc_kernel(are_ref, aim_ref, ldt_ref, bre_ref, bim_ref, abr_ref, abi_ref, bbr_ref, bbi_ref):
    a_re = are_ref[...]
    a_im = aim_ref[...]
    dt = jnp.exp(ldt_ref[...])
    er = jnp.exp(a_re * dt)
    abar_re = er * jnp.cos(a_im * dt)
    abar_im = er * jnp.sin(a_im * dt)
    lam2 = a_re * a_re + a_im * a_im
    cr = ((abar_re - 1.0) * a_re + abar_im * a_im) / lam2
    ci = (abar_im * a_re - (abar_re - 1.0) * a_im) / lam2
    b_re = bre_ref[...]
    b_im = bim_ref[...]
    abr_ref[...] = abar_re
    abi_ref[...] = abar_im
    bbr_ref[...] = cr * b_re - ci * b_im
    bbi_ref[...] = cr * b_im + ci * b_re


def _s5_disc(a_re_rep, a_im_rep, ldt_rep, b_re_t, b_im_t):
    shp = jax.ShapeDtypeStruct(a_re_rep.shape, F32)
    return pl.pallas_call(_s5_disc_kernel, out_shape=[shp] * 4, name="s5_disc")(
        a_re_rep, a_im_rep, ldt_rep, b_re_t, b_im_t)


S5_SLABG = 4
S5_TT = 64
S5_PITCH = S5_TT + 4
S5_DIAG = 256
S5_NSTATE_PER_DIAG = S5_DIAG // S5_GROUP_CH * S5_STATE


def _s5_kernel(u_ref, bre_ref, bim_ref, ar_ref, ai_ref, cre_ref, cim_ref, d_ref, wg_ref, o_ref,
               xr_ref, xi_ref, sr_ref, si_ref):
    nb, tt, wd = u_ref.shape
    pitch = sr_ref.shape[1] // nb
    gcols = S5_SLABG * LANES
    ngroups = S5_NSTATE // gcols

    @pl.when(pl.program_id(0) == 0)
    def _():
        xr_ref[...] = jnp.zeros(xr_ref.shape, F32)
        xi_ref[...] = jnp.zeros(xi_ref.shape, F32)

    u = u_ref[...].reshape(nb * tt, wd)

    chan = lambda kg: slice(kg * gcols // S5_NSTATE_PER_DIAG * S5_DIAG,
                            (kg * gcols // S5_NSTATE_PER_DIAG + 1) * S5_DIAG)
    for kg in range(ngroups):
        cols = slice(kg * gcols, (kg + 1) * gcols)
        br = _dot(u[:, chan(kg)], bre_ref[chan(kg), cols])
        bi = _dot(u[:, chan(kg)], bim_ref[chan(kg), cols])
        for j in range(S5_SLABG):
            for b in range(nb):
                sr_ref[kg * S5_SLABG + j, b * pitch:b * pitch + tt, :] = br[b * tt:(b + 1) * tt, j * LANES:(j + 1) * LANES]
                si_ref[kg * S5_SLABG + j, b * pitch:b * pitch + tt, :] = bi[b * tt:(b + 1) * tt, j * LANES:(j + 1) * LANES]

    for kg in range(ngroups):
        cols = slice(kg * gcols, (kg + 1) * gcols)
        slabs = range(kg * S5_SLABG, (kg + 1) * S5_SLABG)
        ar = ar_ref[:, cols]
        ai = ai_ref[:, cols]

        def body(t, carry, slabs=slabs, ar=ar, ai=ai):
            xr, xi = carry
            rows = pl.ds(t, nb, stride=pitch)
            nxr = ar * xr - ai * xi + jnp.concatenate([sr_ref[k, rows, :] for k in slabs], axis=1)
            nxi = ar * xi + ai * xr + jnp.concatenate([si_ref[k, rows, :] for k in slabs], axis=1)
            for j, k in enumerate(slabs):
                sr_ref[k, rows, :] = nxr[:, j * LANES:(j + 1) * LANES]
                si_ref[k, rows, :] = nxi[:, j * LANES:(j + 1) * LANES]
            return nxr, nxi

        xr, xi = lax.fori_loop(0, tt, body, (xr_ref[:, cols], xi_ref[:, cols]), unroll=4)
        xr_ref[:, cols] = xr
        xi_ref[:, cols] = xi

    y_blocks = [jnp.zeros((nb * tt, S5_DIAG), F32) for _ in range(wd // S5_DIAG)]
    for kg in range(ngroups):
        cols = slice(kg * gcols, (kg + 1) * gcols)
        gather = lambda ref: jnp.concatenate(
            [jnp.concatenate([ref[kg * S5_SLABG + j, b * pitch:b * pitch + tt, :] for j in range(S5_SLABG)], axis=1)
             for b in range(nb)], axis=0).astype(BF16)
        blk = chan(kg).start // S5_DIAG
        y_blocks[blk] = (y_blocks[blk] + _dot(gather(sr_ref), cre_ref[cols, chan(kg)])
                         - _dot(gather(si_ref), cim_ref[cols, chan(kg)]))
    y = jnp.concatenate(y_blocks, axis=1)
    y = _gelu(y + d_ref[...] * u.astype(F32))
    z = _dot(y.astype(BF16), wg_ref[...])
    o = z[:, :S5_WIDTH] * _sigmoid(z[:, S5_WIDTH:])
    o_ref[...] = o.reshape(nb, tt, wd).astype(o_ref.dtype)


def _s5(pm3, bbd_re, bbd_im, ar8, ai8, cbd_re, cbd_im, d_row, w_glu):
    bsz, seq, _ = pm3.shape
    tt = S5_TT
    c2 = lambda i: (0, 0)
    nslab = S5_NSTATE // LANES
    return pl.pallas_call(
        _s5_kernel,
        grid=(seq // tt,),
        in_specs=[pl.BlockSpec((bsz, tt, S5_WIDTH), lambda i: (0, i, _SU0 // S5_WIDTH)),
                  pl.BlockSpec((S5_WIDTH, S5_NSTATE), c2), pl.BlockSpec((S5_WIDTH, S5_NSTATE), c2),
                  pl.BlockSpec((8, S5_NSTATE), c2), pl.BlockSpec((8, S5_NSTATE), c2),
                  pl.BlockSpec((S5_NSTATE, S5_WIDTH), c2), pl.BlockSpec((S5_NSTATE, S5_WIDTH), c2),
                  pl.BlockSpec((1, S5_WIDTH), c2), pl.BlockSpec((S5_WIDTH, 2 * S5_WIDTH), c2)],
        out_specs=pl.BlockSpec((bsz, tt, S5_WIDTH), lambda i: (0, i, 0)),
        out_shape=jax.ShapeDtypeStruct((bsz, seq, S5_WIDTH), BF16),
        scratch_shapes=[pltpu.VMEM((8, S5_NSTATE), F32), pltpu.VMEM((8, S5_NSTATE), F32),
                        pltpu.VMEM((nslab, bsz * S5_PITCH, LANES), F32),
                        pltpu.VMEM((nslab, bsz * S5_PITCH, LANES), F32)],
        compiler_params=_cparams(("arbitrary",)),
        name="s5_scan",
    )(pm3, bbd_re, bbd_im, ar8, ai8, cbd_re, cbd_im, d_row, w_glu)


def _pad_cols(w, n):
    return jnp.pad(w, ((0, 0), (0, n - w.shape[1])))


def _relayout_kernel(pieces, w_ref, o_ref):
    o0 = 0
    for s0, n, sc in pieces:
        v = w_ref[s0:s0 + n, :]
        o_ref[o0:o0 + n, :] = (v if sc is None else v * sc).astype(o_ref.dtype)
        o0 += n


def _relayout_rows(w_t, pieces, tn=256):
    nin, d = w_t.shape
    nout = sum(n for _, n, _ in pieces)
    return pl.pallas_call(
        functools.partial(_relayout_kernel, pieces),
        grid=(d // tn,),
        in_specs=[pl.BlockSpec((nin, tn), lambda i: (0, i))],
        out_specs=pl.BlockSpec((nout, tn), lambda i: (0, i)),
        out_shape=jax.ShapeDtypeStruct((nout, d), BF16),
        compiler_params=_cparams(("parallel",)),
        name="relayout_rows",
    )(w_t)


def _cd_proj_weights(w_in):
    w_t = w_in.T
    d = w_in.shape[0]
    nw, nkv = NSA_HEADS * NSA_DH, NSA_GROUPS * NSA_DH
    offs = [int(o) for o in np.cumsum((0, nw, nkv, nkv, nkv, nkv, nkv, nkv, 3 * NSA_HEADS, nw, S5_WIDTH, S5_WIDTH))]
    def pair(ia, ib, scale=None):
        return [pc for c in range(0, offs[ia + 1] - offs[ia], NSA_DH)
                for pc in ((offs[ia] + c, NSA_DH, scale), (offs[ib] + c, NSA_DH, None))]
    w_main = _relayout_rows(w_t, pair(0, 8, NSA_DH ** -0.5 * LOG2E) + pair(1, 2) + pair(3, 4) + pair(5, 6)
                            + [(offs[9], S5_WIDTH, None), (offs[10], S5_WIDTH, None)])
    w_gate = w_t[offs[7]:offs[8]].reshape(NSA_GROUPS, NSA_HPG, 3, d).transpose(0, 2, 1, 3)
    w_gate = jnp.pad(w_gate.reshape(NSA_GROUPS, 3 * NSA_HPG, d), ((0, 0), (0, LANES - 3 * NSA_HPG), (0, 0)))
    return w_gate.reshape(NSA_GROUPS * LANES, d).astype(BF16), w_main


def _layer0(h2, p4, layer, bsz, seq, norm_pre, norm_post, pe_proj, pe_gate,
            w_in, ret_norm, conv_w, conv_b, gate_b, ml_norm, w_out):
    d = h2.shape[1]
    g0 = 5120
    after = g0 + 2 * ML_HEADS
    w_t = w_in.T
    w_main = _relayout_rows(w_t, [(0, g0, None), (after, w_in.shape[1] - after, None)])
    w_gate = jnp.pad(w_t[g0:after], ((0, LANES - 2 * ML_HEADS), (0, 0))).astype(BF16)
    pm, pg = _norm_proj(h2, norm_pre.reshape(1, d), w_gate, w_main)
    mix = _ab_mixer(pm, pg, conv_w, conv_b.reshape(1, -1), _pad_cols(gate_b.reshape(1, -1), LANES),
                    ret_norm.reshape(1, -1), ml_norm.reshape(1, -1), bsz, seq)
    ka = RET_HEADS * RET_DV
    nt = seq // 512
    return _out_pe(mix, lambda b, s: (b * nt + s, 0), ka, mix, lambda b, s: (b * nt + s, 1), ML_HEADS * ML_DV,
                   None, None, h2, p4, layer, w_out[:ka].astype(BF16), w_out[ka:].astype(BF16),
                   norm_post.reshape(1, d), pe_gate.astype(BF16), pe_proj.astype(BF16), bsz, seq)


def _layer1(h2, p4, layer, bsz, seq, norm_pre, norm_post, pe_proj, pe_gate,
            w_in, pos_k, pos_v, w1_k, w2_k, w1_v, w2_v,
            a_re, a_im, log_dt, b_re, b_im, c_re, c_im, d_skip, w_glu, w_out):
    d = h2.shape[1]
    nw = NSA_HEADS * NSA_DH
    pm, pg = _norm_proj(h2, norm_pre.reshape(1, d), *_cd_proj_weights(w_in))

    nb = seq // CMP_STRIDE
    pm3 = pm.reshape(bsz, seq, CD_MAIN)
    padl = lambda a, lo=True: jnp.pad(a, ((0, 0),) * (a.ndim - 1)
                                      + ((0, LANES - a.shape[-1]) if lo else (LANES - a.shape[-1], 0),))
    w1 = lambda w, lo: jnp.pad(w.reshape(CMP_LEN, NSA_DH, CMP_HIDDEN),
                               ((0, 0), (0, LANES - NSA_DH) if lo else (LANES - NSA_DH, 0), (0, 0))).astype(BF16)
    ek, al, ekc, ovt = _nsa_tables(seq)
    kcc, vct = _compress(pm3, padl(pos_k), padl(pos_v, False), w1(w1_k, True), w1(w1_v, False),
                         padl(w2_k).astype(BF16), padl(w2_v).T.astype(BF16), ekc)
    if nb < LANES:
        kcc = jnp.pad(kcc, ((0, 0), (0, 0), (0, LANES - nb), (0, 0)))
        vct = jnp.pad(vct, ((0, 0), (0, 0), (0, 0), (0, LANES - nb)))
    y_nsa = _nsa(pm3, pg.reshape(bsz, seq, NSA_GROUPS * LANES), kcc, vct, ek, al, ovt)

    rep = lambda a: jnp.repeat(a, S5_GROUP_CH, axis=0)
    ldt = jnp.broadcast_to(rep(log_dt[:, None]), (S5_WIDTH, S5_STATE))
    bt = lambda b: b.transpose(0, 2, 1).reshape(S5_WIDTH, S5_STATE)
    abr, abi, bbr, bbi = _s5_disc(rep(a_re), rep(a_im), ldt, bt(b_re), bt(b_im))
    gi = np.arange(S5_WIDTH) // S5_GROUP_CH
    gs = np.arange(S5_NSTATE) // S5_STATE
    diag_in = jnp.asarray(gi[:, None] == gs[None, :])
    bd_in = lambda bb: jnp.where(diag_in, jnp.tile(bb, (1, S5_GROUPS)), 0.0).astype(BF16)
    bd_out = lambda c: jnp.where(diag_in.T, jnp.tile(c.transpose(0, 2, 1).reshape(S5_NSTATE, S5_GROUP_CH),
                                                     (1, S5_GROUPS)), 0.0).astype(BF16)
    row8 = lambda a: jnp.broadcast_to(a[::S5_GROUP_CH].reshape(1, S5_NSTATE), (8, S5_NSTATE))
    y_s5 = _s5(pm3, bd_in(bbr), bd_in(bbi), row8(abr), row8(abi),
               bd_out(c_re), bd_out(c_im), d_skip.reshape(1, -1), w_glu.astype(BF16))

    nt = seq // 512
    szb = _SZ0 // S5_WIDTH
    return _out_pe(y_nsa.reshape(bsz * seq, nw), lambda b, s: (b * nt + s, 0), nw,
                   y_s5.reshape(bsz * seq, S5_WIDTH), lambda b, s: (b * nt + s, 0), S5_WIDTH,
                   pm, lambda b, s: (b * nt + s, szb),
                   h2, p4, layer, w_out[:nw].astype(BF16), w_out[nw:].astype(BF16),
                   norm_post.reshape(1, d), pe_gate.astype(BF16), pe_proj.astype(BF16), bsz, seq)


def kernel(x, p, norm_pre, norm_post, pe_proj, pe_gate, ab_w_in, ret_norm, ml_conv_w, ml_conv_b, ml_gate_b,
           ml_norm, ab_w_out, cd_w_in, cmp_pos_k, cmp_pos_v, cmp_w1_k, cmp_w2_k, cmp_w1_v, cmp_w2_v,
           s5_a_re, s5_a_im, s5_log_dt, s5_b_re, s5_b_im, s5_c_re, s5_c_im, s5_d, s5_w_glu, cd_w_out):
    bsz, seq, d = x.shape
    assert bsz == 8, "the S5 scan maps the batch onto the 8 sublanes of a vreg"
    assert seq % max(KCH, 1024) == 0
    depth = p.shape[0]
    h2 = x.reshape(bsz * seq, d)
    for i in range(depth):
        j = i // 2
        if i % 2 == 0:
            h2 = _layer0(h2, p, i, bsz, seq, norm_pre[i], norm_post[i], pe_proj[i], pe_gate[i],
                         ab_w_in[j], ret_norm[j], ml_conv_w[j], ml_conv_b[j], ml_gate_b[j], ml_norm[j], ab_w_out[j])
        else:
            h2 = _layer1(h2, p, i, bsz, seq, norm_pre[i], norm_post[i], pe_proj[i], pe_gate[i],
                         cd_w_in[j], cmp_pos_k[j], cmp_pos_v[j], cmp_w1_k[j], cmp_w2_k[j], cmp_w1_v[j], cmp_w2_v[j],
                         s5_a_re[j], s5_a_im[j], s5_log_dt[j], s5_b_re[j], s5_b_im[j], s5_c_re[j], s5_c_im[j],
                         s5_d[j], s5_w_glu[j], cd_w_out[j])
    return h2.reshape(bsz, seq, d)
```

```python
import functools
import math

import numpy as np
import jax
import jax.numpy as jnp
from jax import lax
from jax.experimental import pallas as pl
from jax.experimental.pallas import tpu as pltpu

F32 = jnp.float32
BF16 = jnp.bfloat16
EPS = 1e-6
NEG = -1e30
BIG = 1e30

LANES = 128
CHUNK = 128
RET_HEADS, RET_DK, RET_DV = 4, 128, 256
ML_HEADS, ML_DK, ML_DV = 4, 128, 256
ML_CONV = 4
NSA_HEADS, NSA_GROUPS, NSA_DH = 16, 4, 64
NSA_HPG = NSA_HEADS // NSA_GROUPS
CMP_STRIDE, CMP_LEN, CMP_HIDDEN = 16, 32, 128
SEL_BLOCK, SEL_TOPN, WINDOW = 64, 4, 512
S5_GROUPS, S5_GROUP_CH, S5_STATE = 32, 16, 64
S5_WIDTH = S5_GROUPS * S5_GROUP_CH
S5_NSTATE = S5_GROUPS * S5_STATE

TQ = 256
KCH = 512
SEL_LANE0 = 64
ALI_LANE0 = 96
VMEM_LIMIT = 56 * 1024 * 1024


def _cparams(sem):
    return pltpu.CompilerParams(dimension_semantics=sem, vmem_limit_bytes=VMEM_LIMIT)


def _dot(a, b):
    return jnp.dot(a, b, preferred_element_type=F32)


def _dot_nt(a, b):
    return lax.dot_general(a, b, (((1,), (1,)), ((), ())), preferred_element_type=F32)


def _split3(x):
    hi = x.astype(BF16)
    r1 = x - hi.astype(F32)
    mid = r1.astype(BF16)
    lo = (r1 - mid.astype(F32)).astype(BF16)
    return hi, mid, lo


def _dot_f32_lhs(x, m):
    hi, mid, lo = _split3(x)
    return _dot(hi, m) + _dot(mid, m) + _dot(lo, m)


def _sigmoid(x):
    return 1.0 / (1.0 + jnp.exp(-x))


def _silu(x):
    return x * _sigmoid(x)


def _gelu(x):
    return 0.5 * x * (1.0 + jnp.tanh(math.sqrt(2.0 / math.pi) * (x + 0.044715 * (x * x * x))))


def _rms(x, g):
    return x * lax.rsqrt(jnp.mean(x * x, -1, keepdims=True) + EPS) * g


PROJ_TN = 1024


def _norm_proj_kernel(h_ref, g_ref, wg_ref, wm_ref, om_ref, og_ref):
    ub = _rms(h_ref[...], g_ref[...]).astype(BF16)
    nm = om_ref.shape[1]
    for c0 in range(0, nm, PROJ_TN):
        c1 = min(c0 + PROJ_TN, nm)
        om_ref[:, c0:c1] = _dot_nt(ub, wm_ref[c0:c1, :]).astype(om_ref.dtype)
    og_ref[...] = _dot_nt(ub, wg_ref[...])


def _norm_proj(h2, g, w_gate_t, w_main_t, tm=512):
    n, d = h2.shape
    nm = w_main_t.shape[0]
    ng = w_gate_t.shape[0]
    once = pl.Buffered(1)
    return pl.pallas_call(
        _norm_proj_kernel,
        grid=(n // tm,),
        in_specs=[pl.BlockSpec((tm, d), lambda i: (i, 0)),
                  pl.BlockSpec((1, d), lambda i: (0, 0)),
                  pl.BlockSpec((ng, d), lambda i: (0, 0), pipeline_mode=once),
                  pl.BlockSpec((nm, d), lambda i: (0, 0), pipeline_mode=once)],
        out_specs=[pl.BlockSpec((tm, nm), lambda i: (i, 0)),
                   pl.BlockSpec((tm, ng), lambda i: (i, 0))],
        out_shape=[jax.ShapeDtypeStruct((n, nm), BF16), jax.ShapeDtypeStruct((n, ng), F32)],
        compiler_params=_cparams(("parallel",)),
        name="norm_proj",
    )(h2, g, w_gate_t, w_main_t)


def _out_pe_kernel(gated, *refs):
    if gated:
        ma_ref, mb_ref, sz_ref, h_ref, p_ref, wa_ref, wb_ref, gp_ref, wg_ref, wp_ref, o_ref = refs
        sz = sz_ref[...].astype(F32)
        mb = (mb_ref[...].astype(F32) * _silu(sz)).astype(BF16)
    else:
        ma_ref, mb_ref, h_ref, p_ref, wa_ref, wb_ref, gp_ref, wg_ref, wp_ref, o_ref = refs
        mb = mb_ref[...]
    y = _dot(ma_ref[...], wa_ref[...]) + _dot(mb, wb_ref[...])
    h1 = h_ref[...] + _rms(y, gp_ref[...])
    gate = _sigmoid(_dot(h1.astype(BF16), wg_ref[...]))
    pp = _dot(p_ref[...].astype(BF16), wp_ref[...])
    o_ref[...] = h1 + gate * pp


def _out_pe(mix_a, a_map, ka, mix_b, b_map, kb, sz, sz_map, h2, p4, layer, w_a, w_b, g_post, w_gate, w_pe,
            bsz, seq, tm=512):
    n, d = h2.shape
    pe = p4.shape[-1]
    nt = seq // tm
    row = lambda b, s: (b * nt + s, 0)
    const = lambda b, s: (0, 0)
    gated = sz is not None
    in_specs = [pl.BlockSpec((tm, ka), a_map), pl.BlockSpec((tm, kb), b_map)]
    args = [mix_a, mix_b]
    if gated:
        in_specs.append(pl.BlockSpec((tm, kb), sz_map))
        args.append(sz)
    in_specs += [pl.BlockSpec((tm, d), row), pl.BlockSpec((None, None, tm, pe), lambda b, s: (layer, b, s, 0)),
                 pl.BlockSpec((ka, d), const), pl.BlockSpec((kb, d), const), pl.BlockSpec((1, d), const),
                 pl.BlockSpec((d, d), const), pl.BlockSpec((pe, d), const)]
    args += [h2, p4, w_a, w_b, g_post, w_gate, w_pe]
    return pl.pallas_call(
        functools.partial(_out_pe_kernel, gated),
        grid=(bsz, nt),
        in_specs=in_specs,
        out_specs=pl.BlockSpec((tm, d), row),
        out_shape=jax.ShapeDtypeStruct((n, d), F32),
        compiler_params=_cparams(("parallel", "arbitrary")),
        name="out_pe_gated" if gated else "out_pe",
    )(*args)


_RQ, _RK, _RV, _RG = 0, 512, 1024, 2048
_MQ, _MK, _MV, _MO, _MZ = 3072, 3584, 4096, 5120, 6144
AB_MAIN = 7168


def _head_norm(y, g):
    yc = y - jnp.mean(y, -1, keepdims=True)
    return yc * lax.rsqrt(jnp.mean(yc * yc, -1, keepdims=True) + EPS) * g


def _ab_kernel(pm_ref, pg_ref, cw_ref, cb_ref, gb_ref, rn_ref, mn_ref, mix_ref,
               ext_ref, r_ref, c_ref, n_ref, m_ref):
    L = CHUNK
    c_idx = pl.program_id(1)

    @pl.when(c_idx == 0)
    def _():
        ext_ref[0:8, :] = jnp.zeros((8, ext_ref.shape[1]), F32)
        r_ref[...] = jnp.zeros(r_ref.shape, F32)
        c_ref[...] = jnp.zeros(c_ref.shape, F32)
        n_ref[...] = jnp.zeros(n_ref.shape, F32)
        m_ref[...] = jnp.full(m_ref.shape, NEG, F32)

    row = lax.broadcasted_iota(jnp.int32, (L, L), 0)
    col = lax.broadcasted_iota(jnp.int32, (L, L), 1)
    causal = row >= col
    diff = (row - col).astype(F32)
    tcol = lax.broadcasted_iota(jnp.int32, (L, 1), 0).astype(F32)

    for h in range(RET_HEADS):
        lg = math.log1p(-(2.0 ** (-5.0 - h)))
        scale = RET_DK ** -0.5
        decay = jnp.where(causal, jnp.exp(jnp.maximum(diff, 0.0) * lg), 0.0) * scale
        q = pm_ref[:, _RQ + h * RET_DK:_RQ + (h + 1) * RET_DK]
        k = pm_ref[:, _RK + h * RET_DK:_RK + (h + 1) * RET_DK]
        v = pm_ref[:, _RV + h * RET_DV:_RV + (h + 1) * RET_DV]
        sc = _dot_nt(q, k) * decay
        xi = jnp.exp((tcol + 1.0) * lg) * scale
        qx = (q.astype(F32) * xi).astype(BF16)
        r_prev = r_ref[h]
        y = _dot(sc.astype(BF16), v) + _dot(qx, r_prev.astype(BF16))
        zeta = jnp.exp((L - 1.0 - tcol) * lg)
        kz_t = (k.astype(F32) * zeta).T.astype(BF16)
        r_ref[h] = r_prev * math.exp(L * lg) + _dot(kz_t, v)
        g = pm_ref[:, _RG + h * RET_DV:_RG + (h + 1) * RET_DV].astype(F32)
        out = _head_norm(y, rn_ref[:, h * RET_DV:(h + 1) * RET_DV]) * _silu(g)
        mix_ref[:, h * RET_DV:(h + 1) * RET_DV] = out.astype(mix_ref.dtype)

    nqk = 2 * ML_HEADS * ML_DK
    x = pm_ref[:, _MQ:_MQ + nqk].astype(F32)
    ext_ref[8:8 + L, :] = x
    acc = cb_ref[...] + cw_ref[ML_CONV - 1:ML_CONV, :] * x
    for j in range(ML_CONV - 1):
        acc = acc + cw_ref[j:j + 1, :] * ext_ref[pl.ds(8 - (ML_CONV - 1) + j, L), :]
    ext_ref[0:8, :] = x[L - 8:L, :]
    qk = _silu(acc)

    gt = (pg_ref[...] + gb_ref[...]).T
    g8 = gt[0:8, :]
    logf8 = jnp.minimum(g8, 0.0) - jnp.log(1.0 + jnp.exp(-jnp.abs(g8)))
    triu = jnp.where(row <= col, 1.0, 0.0).astype(BF16)
    bcum8 = _dot_f32_lhs(logf8, triu)
    b_t = pltpu.roll(bcum8, 4, axis=0)
    b_last = b_t[:, L - 1:L]
    a_t = b_last - b_t + g8
    mu = jnp.max(a_t, axis=1, keepdims=True)
    w_t = jnp.exp(a_t - mu)
    r_t = g8 - b_t
    zpad = jnp.zeros((L - 8, L), F32)
    bcol = jnp.concatenate([b_t, zpad], axis=0).T
    wcol = jnp.concatenate([w_t, zpad], axis=0).T
    m_prev = m_ref[...][:, 0:1]
    m_new = jnp.maximum(b_last + m_prev, mu)
    sp = jnp.exp(b_last + m_prev - m_new)
    scn = jnp.exp(mu - m_new)
    m_ref[...] = jnp.broadcast_to(m_new, m_ref.shape)

    for h in range(ML_HEADS):
        bc = bcol[:, h:h + 1]
        log_d = jnp.where(causal, bc + r_t[h:h + 1, :], NEG)
        inter = bc + m_prev[h:h + 1, :]
        m_t = jnp.maximum(inter, jnp.max(log_d, axis=1, keepdims=True))
        dmat = jnp.exp(log_d - m_t)
        w_int = jnp.exp(inter - m_t)
        qh = qk[:, h * ML_DK:(h + 1) * ML_DK]
        kh = qk[:, ML_HEADS * ML_DK + h * ML_DK:ML_HEADS * ML_DK + (h + 1) * ML_DK] * (ML_DK ** -0.5)
        qb = qh.astype(BF16)
        v = pm_ref[:, _MV + h * ML_DV:_MV + (h + 1) * ML_DV]
        s = _dot_nt(qb, kh.astype(BF16)) * dmat
        c_prev = c_ref[h]
        n_prev = n_ref[h:h + 1, :]
        num = _dot(s.astype(BF16), v) + w_int * _dot(qb, c_prev.astype(BF16))
        den = jnp.sum(s, axis=1, keepdims=True) + w_int * jnp.sum(qh * n_prev, axis=1, keepdims=True)
        den = jnp.maximum(jnp.abs(den), jnp.exp(-m_t))
        hcell = num * (1.0 / den)
        o = pm_ref[:, _MO + h * ML_DV:_MO + (h + 1) * ML_DV].astype(F32)
        y = _sigmoid(o) * hcell
        z = pm_ref[:, _MZ + h * ML_DV:_MZ + (h + 1) * ML_DV].astype(F32)
        out = _head_norm(y, mn_ref[:, h * ML_DV:(h + 1) * ML_DV]) * _silu(z)
        mix_ref[:, RET_HEADS * RET_DV + h * ML_DV:RET_HEADS * RET_DV + (h + 1) * ML_DV] = out.astype(mix_ref.dtype)
        kw = kh * wcol[:, h:h + 1]
        kv = _dot(kw.T.astype(BF16), v)
        ksum = jnp.sum(kw, axis=0, keepdims=True)
        sp_h = sp[h:h + 1, :]
        sc_h = scn[h:h + 1, :]
        c_ref[h] = sp_h * c_prev + sc_h * kv
        n_ref[h:h + 1, :] = sp_h * n_prev + sc_h * ksum


def _ab_mixer(pm, pg, conv_w, conv_b, gate_b, ret_norm, ml_norm, bsz, seq):
    n = pm.shape[0]
    nc = seq // CHUNK
    nqk = 2 * ML_HEADS * ML_DK
    row = lambda b, c: (b * nc + c, 0)
    const = lambda b, c: (0, 0)
    nmix = RET_HEADS * RET_DV + ML_HEADS * ML_DV
    return pl.pallas_call(
        _ab_kernel,
        grid=(bsz, nc),
        in_specs=[pl.BlockSpec((CHUNK, AB_MAIN), row), pl.BlockSpec((CHUNK, LANES), row),
                  pl.BlockSpec((ML_CONV, nqk), const), pl.BlockSpec((1, nqk), const),
                  pl.BlockSpec((1, LANES), const),
                  pl.BlockSpec((1, RET_HEADS * RET_DV), const), pl.BlockSpec((1, ML_HEADS * ML_DV), const)],
        out_specs=pl.BlockSpec((CHUNK, nmix), row),
        out_shape=jax.ShapeDtypeStruct((n, nmix), BF16),
        scratch_shapes=[pltpu.VMEM((CHUNK + 8, nqk), F32),
                        pltpu.VMEM((RET_HEADS, RET_DK, RET_DV), F32),
                        pltpu.VMEM((ML_HEADS, ML_DK, ML_DV), F32),
                        pltpu.VMEM((8, ML_DK), F32),
                        pltpu.VMEM((8, LANES), F32)],
        compiler_params=_cparams(("parallel", "arbitrary")),
        name="ab_mixer",
    )(pm, pg, conv_w, conv_b, gate_b, ret_norm, ml_norm)


_QNZ0 = 0
_KVC0 = NSA_HEADS
_KVS0 = _KVC0 + NSA_GROUPS
_KVW0 = _KVS0 + NSA_GROUPS
_SU0 = (_KVW0 + NSA_GROUPS) * LANES
_SZ0 = _SU0 + S5_WIDTH
CD_MAIN = _SZ0 + S5_WIDTH
SLOTW = NSA_GROUPS * LANES
VT_ROWS = NSA_DH + 16
LOG2E = math.log2(math.e)


def _compress_kernel(x_ref, pk_ref, pv_ref, w1k_ref, w1v_ref, w2k_ref, w2vt_ref, ekc_ref,
                     ko_ref, vot_ref, xs_ref):
    seq = x_ref.shape[0]
    nb = seq // CMP_STRIDE
    for g in range(NSA_GROUPS):
        xs_ref[...] = x_ref[:, g * LANES:(g + 1) * LANES].astype(F32)
        acc = [jnp.zeros((nb, CMP_HIDDEN), F32) for _ in range(4)]
        for r in range(CMP_STRIDE):
            xr = xs_ref[pl.ds(r, nb, stride=CMP_STRIDE), :]
            for kind, (p_ref, w1_ref) in enumerate(((pk_ref, w1k_ref), (pv_ref, w1v_ref))):
                acc[2 * kind] = acc[2 * kind] + _dot((xr + p_ref[r:r + 1, :]).astype(BF16), w1_ref[r])
                acc[2 * kind + 1] = acc[2 * kind + 1] + _dot(
                    (xr + p_ref[CMP_STRIDE + r:CMP_STRIDE + r + 1, :]).astype(BF16), w1_ref[CMP_STRIDE + r])
        hk = _gelu(acc[0] + pltpu.roll(acc[1], nb - 1, axis=0)).astype(BF16)
        hv = _gelu(acc[2] + pltpu.roll(acc[3], nb - 1, axis=0)).astype(BF16)
        ko_ref[g] = (_dot(hk, w2k_ref[...]) + ekc_ref[...]).astype(ko_ref.dtype)
        vot_ref[g] = _dot_nt(w2vt_ref[...], hv).astype(vot_ref.dtype)


def _compress(pm3, pos_k, pos_v, w1k, w1v, w2k, w2vt, ekc):
    bsz, seq, _ = pm3.shape
    nb = seq // CMP_STRIDE
    c3 = lambda b: (0, 0, 0)
    c2 = lambda b: (0, 0)
    return pl.pallas_call(
        _compress_kernel,
        grid=(bsz,),
        in_specs=[pl.BlockSpec((None, seq, SLOTW), lambda b: (b, 0, _KVC0 * LANES // SLOTW)),
                  pl.BlockSpec((CMP_LEN, LANES), c2), pl.BlockSpec((CMP_LEN, LANES), c2),
                  pl.BlockSpec((CMP_LEN, LANES, CMP_HIDDEN), c3), pl.BlockSpec((CMP_LEN, LANES, CMP_HIDDEN), c3),
                  pl.BlockSpec((CMP_HIDDEN, LANES), c2), pl.BlockSpec((LANES, CMP_HIDDEN), c2),
                  pl.BlockSpec((nb, LANES), c2)],
        out_specs=[pl.BlockSpec((None, NSA_GROUPS, nb, LANES), lambda b: (b, 0, 0, 0)),
                   pl.BlockSpec((None, NSA_GROUPS, LANES, nb), lambda b: (b, 0, 0, 0))],
        out_shape=[jax.ShapeDtypeStruct((bsz, NSA_GROUPS, nb, LANES), BF16),
                   jax.ShapeDtypeStruct((bsz, NSA_GROUPS, LANES, nb), BF16)],
        scratch_shapes=[pltpu.VMEM((seq, LANES), F32)],
        compiler_params=_cparams(("parallel",)),
        name="nsa_compress",
    )(pm3, pos_k, pos_v, w1k, w1v, w2k, w2vt, ekc)


def _nsa_kernel(q_ref, kc_ref, vct_ref, kvs_ref, kvw_ref, pg_ref,
                ek_ref, al_ref, ovt_ref, y_ref,
                ksa_ref, vsat_ref, kwa_ref, vwat_ref, qa_scr, part_scr, gzs_scr, sel_smem):
    seq = kvs_ref.shape[0]
    g_idx = pl.program_id(1)
    qi = pl.program_id(2)
    wtiles = WINDOW // LANES

    @pl.when(qi == 0)
    def _():
        ek = ek_ref[...]
        lane_k = lax.broadcasted_iota(jnp.int32, (seq, LANES), 1)
        zero = jnp.zeros((seq, LANES), BF16)
        ksa_ref[0:seq, :] = jnp.where(lane_k < NSA_DH, kvs_ref[...], zero) + ek
        lane_d = lax.broadcasted_iota(jnp.int32, (LANES, LANES), 1)
        ksa_ref[seq:seq + LANES, :] = jnp.where((lane_d >= SEL_LANE0) & (lane_d < SEL_LANE0 + seq // SEL_BLOCK),
                                                1.0, 0.0).astype(BF16)
        vsat_ref[seq // LANES] = jnp.zeros((VT_ROWS, LANES), BF16)
        ek_w = jnp.where(lane_k >= ALI_LANE0, ek, zero)
        kwa_ref[0:WINDOW, :] = jnp.zeros((WINDOW, LANES), BF16)
        kwa_ref[WINDOW:WINDOW + seq, :] = jnp.where(lane_k < NSA_DH, kvw_ref[...], zero) + ek_w
        ones_rows = jnp.where(lax.broadcasted_iota(jnp.int32, (VT_ROWS - NSA_DH, LANES), 0) == 0, 1.0, 0.0)
        for i in range(wtiles):
            vwat_ref[i] = jnp.zeros((VT_ROWS, LANES), BF16)
        for i in range(seq // LANES):
            rws = slice(i * LANES, (i + 1) * LANES)
            for src_ref, dst_ref, j in ((kvs_ref, vsat_ref, i), (kvw_ref, vwat_ref, wtiles + i)):
                v_t = src_ref[rws, :].astype(F32).T[NSA_DH:2 * NSA_DH, :]
                dst_ref[j] = jnp.concatenate([v_t, ones_rows], axis=0).astype(BF16)

    nq = seq // TQ
    refs = (q_ref, kc_ref, vct_ref, pg_ref, al_ref, ovt_ref, y_ref, ksa_ref, vsat_ref, kwa_ref, vwat_ref,
            qa_scr, part_scr, gzs_scr, sel_smem)
    for step in range(nq + 1):
        pl.when(qi == step)(functools.partial(_nsa_step, step, nq, seq // SEL_BLOCK, g_idx, refs))


SEL_DYN = 2
SEL_CNT = 7


def _sel_compactable(qi):
    return (qi + 1) * TQ // LANES > 1 + SEL_DYN + TQ // LANES


def _nsa_step(step, nq, nsel, g_idx, refs):
    def run(compact):
        stages = []
        if step < nq:
            stages.append(_nsa_front(nsel, step, g_idx, refs))
        if step > 0:
            stages.append(_nsa_back(step - 1, compact, refs))
        while stages:
            for st in list(stages):
                if next(st, "done") == "done":
                    stages.remove(st)

    if step > 0 and _sel_compactable(step - 1):
        fits = refs[-1][(step - 1) & 1, SEL_CNT] <= SEL_DYN
        pl.when(fits)(functools.partial(run, True))
        pl.when(jnp.logical_not(fits))(functools.partial(run, False))
    else:
        run(False)


def _tile_iotas():
    rows = NSA_HPG * TQ
    jrow = lax.broadcasted_iota(jnp.int32, (LANES, rows), 0)
    tq = lax.broadcasted_iota(jnp.int32, (LANES, rows), 1) & (TQ - 1)
    return jrow, tq, jrow - tq


def _nsa_front(nsel, qi, g_idx, refs):
    (q_ref, kc_ref, vct_ref, pg_ref, al_ref, ovt_ref, _, _, _, kwa_ref, vwat_ref,
     qa_scr, part_scr, gzs_scr, sel_smem) = refs
    hp = NSA_HPG
    span = WINDOW + TQ
    t0 = qi * TQ
    tw = t0
    early = t0 < WINDOW
    par = qi & 1
    jrow, tq, kmt = _tile_iotas()

    qnz = [q_ref[:, hh * LANES:(hh + 1) * LANES].astype(F32) for hh in range(hp)]
    lane_q = lax.broadcasted_iota(jnp.int32, (TQ, LANES), 1)
    qal = jnp.concatenate([jnp.where(lane_q < NSA_DH, qnz[hh], al_ref[pl.ds(g_idx * hp + hh, 1), :])
                           for hh in range(hp)], axis=0)
    qalb = qal.astype(BF16)
    s_c = _dot_nt(kc_ref[...], qalb)
    wfirst = max(0, WINDOW - t0) // LANES
    s_w = _dot_nt(kwa_ref[tw + wfirst * LANES:tw + span, :], qalb)
    yield

    s = jnp.where(tq - jrow * CMP_STRIDE >= (CMP_LEN - 1) - t0, s_c, NEG)
    e = jnp.exp2(s - jnp.max(s, axis=0, keepdims=True))
    inv = 1.0 / jnp.sum(e, axis=0, keepdims=True)
    if early:
        inv = jnp.where(t0 + tq[0:1, :] >= CMP_LEN - 1, inv, 0.0)
    p = e * inv
    oc_t = _dot(vct_ref[...], p.astype(BF16))
    psum_t = p[:, 0:TQ]
    for hh in range(1, hp):
        psum_t = psum_t + p[:, hh * TQ:(hh + 1) * TQ]
    p_hi = psum_t.astype(BF16)
    p_lo = (psum_t - p_hi.astype(F32)).astype(BF16)
    imp_t = _dot(ovt_ref[...], p_hi) + _dot(ovt_ref[...], p_lo)
    yield

    blocks = []
    for i in range(wfirst, span // LANES):
        blk = s_w[(i - wfirst) * LANES:(i - wfirst + 1) * LANES, :]
        if i * LANES < TQ:
            blk = jnp.where(kmt > -i * LANES, blk, NEG)
        if (i + 1) * LANES > WINDOW:
            blk = jnp.where(kmt <= WINDOW - i * LANES, blk, NEG)
        blocks.append(blk)
    s = jnp.concatenate(blocks, axis=0)
    p = jnp.exp2(s - jnp.max(s, axis=0, keepdims=True)).astype(BF16)
    vw_t = jnp.concatenate([vwat_ref[qi * (TQ // LANES) + i] for i in range(wfirst, span // LANES)], axis=1)
    accw_t = _dot(vw_t, p)
    yield

    score = imp_t[SEL_LANE0:SEL_LANE0 + nsel, :]
    nrow = lax.broadcasted_iota(jnp.int32, (nsel, TQ), 0)
    own = (t0 + lax.broadcasted_iota(jnp.int32, (nsel, TQ), 1)) >> int(math.log2(SEL_BLOCK))
    valid = nrow <= own
    forced = (nrow == 0) | (nrow == own)
    score = jnp.where(forced, BIG, jnp.where(valid, score, -1.0))
    chosen = jnp.zeros((nsel, TQ), F32)
    nrow_f = nrow.astype(F32)
    for _ in range(SEL_TOPN):
        mx = jnp.max(score, axis=0, keepdims=True)
        first = jnp.min(jnp.where(score == mx, nrow_f, 4.0 * LANES), axis=0, keepdims=True)
        hit = nrow_f == first
        chosen = jnp.where(hit, 1.0, chosen)
        score = jnp.where(hit, -3.0, score)
    mask_t = jnp.where(chosen > 0.0, jnp.where(valid, 0.0, NEG), NEG)
    if _sel_compactable(qi):
        used_blk = jnp.max(jnp.where(mask_t == 0.0, 1.0, 0.0), axis=1, keepdims=True)
        blk_per_tile = LANES // SEL_BLOCK
        for slot in range(SEL_DYN):
            sel_smem[par, slot] = jnp.int32(nsel // blk_per_tile)
        cnt = jnp.int32(0)
        for i in range(1, t0 // LANES):
            used = jnp.max(used_blk[i * blk_per_tile:(i + 1) * blk_per_tile, :]) > 0.0

            @pl.when(used & (cnt < SEL_DYN))
            def _(i=i, cnt=cnt):
                sel_smem[par, cnt] = jnp.int32(i)

            cnt = cnt + used.astype(jnp.int32)
        sel_smem[par, SEL_CNT] = cnt
    maskcols = jnp.concatenate([jnp.zeros((SEL_LANE0, TQ), F32), mask_t,
                                jnp.zeros((LANES - SEL_LANE0 - nsel, TQ), F32)], axis=0).T

    qa_scr[par] = (qal + jnp.concatenate([maskcols] * hp, axis=0)).astype(BF16)

    gt_t = _sigmoid(pg_ref[...]).T
    for hh in range(hp):
        g_c, g_s, g_w = [gt_t[j * hp + hh:j * hp + hh + 1, :] for j in range(3)]
        hcols = slice(hh * TQ, (hh + 1) * TQ)
        aw_t = accw_t[:, hcols]
        gz = _silu(qnz[hh].T[NSA_DH:2 * NSA_DH, :])
        o = g_c * oc_t[0:NSA_DH, hcols] + (g_w * (1.0 / aw_t[NSA_DH:NSA_DH + 1, :])) * aw_t[0:NSA_DH, :]
        part_scr[par, hh * NSA_DH:(hh + 1) * NSA_DH, :] = o * gz
        gzs_scr[par, hh * NSA_DH:(hh + 1) * NSA_DH, :] = g_s * gz


def _nsa_back(qi, compact, refs):
    _, _, _, _, _, _, y_ref, ksa_ref, vsat_ref, _, _, qa_scr, part_scr, gzs_scr, sel_smem = refs
    hp = NSA_HPG
    t0 = qi * TQ
    par = qi & 1
    _, _, kmt = _tile_iotas()
    qa = qa_scr[par]
    ntiles = (t0 + TQ) // LANES
    if compact:
        tiles = [0] + [sel_smem[par, j] for j in range(SEL_DYN)] + list(range(t0 // LANES, ntiles))
    else:
        tiles = list(range(ntiles))
    per = len(tiles) if compact else KCH // LANES
    chunks = [tiles[i:i + per] for i in range(0, len(tiles), per)]
    nk = len(chunks)

    def k_tile(i):
        if isinstance(i, int):
            return ksa_ref[i * LANES:(i + 1) * LANES, :]
        return ksa_ref[pl.ds(pl.multiple_of(i * LANES, LANES), LANES), :]

    def scores(c):
        s = _dot_nt(jnp.concatenate([k_tile(i) for i in chunks[c]], axis=0), qa)
        if any(isinstance(i, int) and (i + 1) * LANES > t0 for i in chunks[c]):
            s = jnp.concatenate(
                [jnp.where(kmt <= t0 - i * LANES, s[n * LANES:(n + 1) * LANES, :], NEG)
                 if isinstance(i, int) and (i + 1) * LANES > t0 else s[n * LANES:(n + 1) * LANES, :]
                 for n, i in enumerate(chunks[c])], axis=0)
        return s

    def weighted(c, s):
        mc = jnp.max(s, axis=0, keepdims=True)
        v_t = jnp.concatenate([vsat_ref[i] for i in chunks[c]], axis=1)
        return mc, _dot(v_t, jnp.exp2(s - mc).astype(BF16))

    parts = []
    s_next = scores(0)
    for c in range(nk):
        s_cur = s_next
        if c + 1 < nk:
            s_next = scores(c + 1)
        yield
        parts.append(weighted(c, s_cur))
    yield
    m = parts[0][0]
    for mc, _ in parts[1:]:
        m = jnp.maximum(m, mc)
    accs_t = None
    for mc, d in parts:
        d = d if nk == 1 else jnp.exp2(mc - m) * d
        accs_t = d if accs_t is None else accs_t + d

    outs = []
    for hh in range(hp):
        as_t = accs_t[:, hh * TQ:(hh + 1) * TQ]
        rws = slice(hh * NSA_DH, (hh + 1) * NSA_DH)
        outs.append(part_scr[par, rws, :] + (gzs_scr[par, rws, :] * (1.0 / as_t[NSA_DH:NSA_DH + 1, :])) * as_t[0:NSA_DH, :])
    for pr in range(hp // 2):
        packed = jnp.concatenate([outs[2 * pr], outs[2 * pr + 1]], axis=0).T
        y_ref[:, pr * LANES:(pr + 1) * LANES] = packed.astype(y_ref.dtype)


def _nsa(pm3, pg3, kcc, vct, ek, al, ovt):
    bsz, seq, _ = pm3.shape
    nq = seq // TQ
    gw = NSA_HPG * NSA_DH
    kv = lambda tile0: pl.BlockSpec((None, seq, LANES), lambda b, g, q: (b, 0, tile0 + g))
    c2 = lambda b, g, q: (0, 0)
    front = lambda b, g, q: (b, jnp.minimum(q, nq - 1), g)
    back = lambda b, g, q: (b, jnp.maximum(q - 1, 0), g)
    return pl.pallas_call(
        _nsa_kernel,
        grid=(bsz, NSA_GROUPS, nq + 1),
        in_specs=[pl.BlockSpec((None, TQ, NSA_HPG * LANES), front),
                  pl.BlockSpec((None, None, LANES, LANES), lambda b, g, q: (b, g, 0, 0)),
                  pl.BlockSpec((None, None, LANES, LANES), lambda b, g, q: (b, g, 0, 0)),
                  kv(_KVS0), kv(_KVW0),
                  pl.BlockSpec((None, TQ, LANES), front),
                  pl.BlockSpec((seq, LANES), c2),
                  pl.BlockSpec((NSA_HEADS, LANES), c2),
                  pl.BlockSpec((LANES, LANES), c2)],
        out_specs=pl.BlockSpec((None, TQ, gw), back),
        out_shape=jax.ShapeDtypeStruct((bsz, seq, NSA_HEADS * NSA_DH), BF16),
        scratch_shapes=[pltpu.VMEM((seq + LANES, LANES), BF16),
                        pltpu.VMEM((seq // LANES + 1, VT_ROWS, LANES), BF16),
                        pltpu.VMEM((seq + WINDOW, LANES), BF16),
                        pltpu.VMEM(((seq + WINDOW) // LANES, VT_ROWS, LANES), BF16),
                        pltpu.VMEM((2, NSA_HPG * TQ, LANES), BF16),
                        pltpu.VMEM((2, gw, TQ), F32), pltpu.VMEM((2, gw, TQ), F32),
                        pltpu.SMEM((2, SEL_CNT + 1), jnp.int32)],
        compiler_params=_cparams(("parallel", "arbitrary", "arbitrary")),
        name="nsa_attn",
    )(pm3, kcc, vct, pm3, pm3, pg3, ek, al, ovt)


def _nsa_tables(seq):
    t = np.arange(seq)
    ek = np.zeros((seq, LANES), np.float32)
    ek[t, SEL_LANE0 + t // SEL_BLOCK] = 1.0
    for c in range(3):
        ek[:, ALI_LANE0 + c] = t // SEL_BLOCK
        ek[:, ALI_LANE0 + 3 + c] = t % SEL_BLOCK
    slopes = jnp.exp2(-8.0 * jnp.arange(1, NSA_HEADS + 1, dtype=F32) / NSA_HEADS) * LOG2E
    s_hi, s_mid, s_lo = _split3(slopes)
    parts = [p.astype(F32) for p in (s_hi, s_mid, s_lo)]
    al = jnp.pad(jnp.stack([p * SEL_BLOCK for p in parts] + parts, axis=1),
                 ((0, 0), (ALI_LANE0, LANES - ALI_LANE0 - 2 * len(parts))))
    nb = seq // CMP_STRIDE
    cstart = np.arange(nb) * CMP_STRIDE
    cend = cstart + CMP_LEN - 1
    ekc = np.zeros((nb, LANES), np.float32)
    for c in range(3):
        ekc[:, ALI_LANE0 + c] = cend // SEL_BLOCK
        ekc[:, ALI_LANE0 + 3 + c] = cend % SEL_BLOCK
    sel = np.arange(seq // SEL_BLOCK)
    ovt = np.zeros((LANES, max(nb, LANES)), np.float32)
    ovl = (cstart[:, None] < (sel[None, :] + 1) * SEL_BLOCK) & (cstart[:, None] + CMP_LEN > sel[None, :] * SEL_BLOCK)
    ovl[nb - 1, :] = False
    ovt[SEL_LANE0:SEL_LANE0 + len(sel), :nb] = ovl.T
    return jnp.asarray(ek, BF16), al, jnp.asarray(ekc), jnp.asarray(ovt, BF16)


def _s5_disc_kernel(are_ref, aim_ref, ldt_ref, bre_ref, bim_ref, abr_ref, abi_ref, bbr_ref, bbi_ref):
    a_re = are_ref[...]
    a_im = aim_ref[...]
    dt = jnp.exp(ldt_ref[...])
    er = jnp.exp(a_re * dt)
    abar_re = er * jnp.cos(a_im * dt)
    abar_im = er * jnp.sin(a_im * dt)
    lam2 = a_re * a_re + a_im * a_im
    cr = ((abar_re - 1.0) * a_re + abar_im * a_im) / lam2
    ci = (abar_im * a_re - (abar_re - 1.0) * a_im) / lam2
    b_re = bre_ref[...]
    b_im = bim_ref[...]
    abr_ref[...] = abar_re
    abi_ref[...] = abar_im
    bbr_ref[...] = cr * b_re - ci * b_im
    bbi_ref[...] = cr * b_im + ci * b_re


def _s5_disc(a_re_rep, a_im_rep, ldt_rep, b_re_t, b_im_t):
    shp = jax.ShapeDtypeStruct(a_re_rep.shape, F32)
    return pl.pallas_call(_s5_disc_kernel, out_shape=[shp] * 4, name="s5_disc")(
        a_re_rep, a_im_rep, ldt_rep, b_re_t, b_im_t)


S5_SLABG = 4
S5_TT = 128
S5_PITCH = S5_TT + 4
S5_DIAG = 256
S5_NSTATE_PER_DIAG = S5_DIAG // S5_GROUP_CH * S5_STATE


def _s5_kernel(u_ref, bre_ref, bim_ref, ar_ref, ai_ref, cre_ref, cim_ref, d_ref, wg_ref, o_ref,
               xr_ref, xi_ref, sr_ref, si_ref):
    nb, tt, wd = u_ref.shape
    pitch = sr_ref.shape[1] // nb
    gcols = S5_SLABG * LANES
    ngroups = S5_NSTATE // gcols

    @pl.when(pl.program_id(0) == 0)
    def _():
        xr_ref[...] = jnp.zeros(xr_ref.shape, F32)
        xi_ref[...] = jnp.zeros(xi_ref.shape, F32)

    u = u_ref[...].reshape(nb * tt, wd)

    chan = lambda kg: slice(kg * gcols // S5_NSTATE_PER_DIAG * S5_DIAG,
                            (kg * gcols // S5_NSTATE_PER_DIAG + 1) * S5_DIAG)
    for kg in range(ngroups):
        cols = slice(kg * gcols, (kg + 1) * gcols)
        br = _dot(u[:, chan(kg)], bre_ref[chan(kg), cols])
        bi = _dot(u[:, chan(kg)], bim_ref[chan(kg), cols])
        for j in range(S5_SLABG):
            for b in range(nb):
                sr_ref[kg * S5_SLABG + j, b * pitch:b * pitch + tt, :] = br[b * tt:(b + 1) * tt, j * LANES:(j + 1) * LANES]
                si_ref[kg * S5_SLABG + j, b * pitch:b * pitch + tt, :] = bi[b * tt:(b + 1) * tt, j * LANES:(j + 1) * LANES]

    for kg in range(ngroups):
        cols = slice(kg * gcols, (kg + 1) * gcols)
        slabs = range(kg * S5_SLABG, (kg + 1) * S5_SLABG)
        ar = ar_ref[:, cols]
        ai = ai_ref[:, cols]

        def body(t, carry, slabs=slabs, ar=ar, ai=ai):
            xr, xi = carry
            rows = pl.ds(t, nb, stride=pitch)
            nxr = ar * xr - ai * xi + jnp.concatenate([sr_ref[k, rows, :] for k in slabs], axis=1)
            nxi = ar * xi + ai * xr + jnp.concatenate([si_ref[k, rows, :] for k in slabs], axis=1)
            for j, k in enumerate(slabs):
                sr_ref[k, rows, :] = nxr[:, j * LANES:(j + 1) * LANES]
                si_ref[k, rows, :] = nxi[:, j * LANES:(j + 1) * LANES]
            return nxr, nxi

        xr, xi = lax.fori_loop(0, tt, body, (xr_ref[:, cols], xi_ref[:, cols]), unroll=4)
        xr_ref[:, cols] = xr
        xi_ref[:, cols] = xi

    y_blocks = [jnp.zeros((nb * tt, S5_DIAG), F32) for _ in range(wd // S5_DIAG)]
    for kg in range(ngroups):
        cols = slice(kg * gcols, (kg + 1) * gcols)
        gather = lambda ref: jnp.concatenate(
            [jnp.concatenate([ref[kg * S5_SLABG + j, b * pitch:b * pitch + tt, :] for j in range(S5_SLABG)], axis=1)
             for b in range(nb)], axis=0).astype(BF16)
        blk = chan(kg).start // S5_DIAG
        y_blocks[blk] = (y_blocks[blk] + _dot(gather(sr_ref), cre_ref[cols, chan(kg)])
                         - _dot(gather(si_ref), cim_ref[cols, chan(kg)]))
    y = jnp.concatenate(y_blocks, axis=1)
    y = _gelu(y + d_ref[...] * u.astype(F32))
    z = _dot(y.astype(BF16), wg_ref[...])
    o = z[:, :S5_WIDTH] * _sigmoid(z[:, S5_WIDTH:])
    o_ref[...] = o.reshape(nb, tt, wd).astype(o_ref.dtype)


def _s5(pm3, bbd_re, bbd_im, ar8, ai8, cbd_re, cbd_im, d_row, w_glu):
    bsz, seq, _ = pm3.shape
    tt = S5_TT
    c2 = lambda i: (0, 0)
    nslab = S5_NSTATE // LANES
    return pl.pallas_call(
        _s5_kernel,
        grid=(seq // tt,),
        in_specs=[pl.BlockSpec((bsz, tt, S5_WIDTH), lambda i: (0, i, _SU0 // S5_WIDTH)),
                  pl.BlockSpec((S5_WIDTH, S5_NSTATE), c2), pl.BlockSpec((S5_WIDTH, S5_NSTATE), c2),
                  pl.BlockSpec((8, S5_NSTATE), c2), pl.BlockSpec((8, S5_NSTATE), c2),
                  pl.BlockSpec((S5_NSTATE, S5_WIDTH), c2), pl.BlockSpec((S5_NSTATE, S5_WIDTH), c2),
                  pl.BlockSpec((1, S5_WIDTH), c2), pl.BlockSpec((S5_WIDTH, 2 * S5_WIDTH), c2)],
        out_specs=pl.BlockSpec((bsz, tt, S5_WIDTH), lambda i: (0, i, 0)),
        out_shape=jax.ShapeDtypeStruct((bsz, seq, S5_WIDTH), BF16),
        scratch_shapes=[pltpu.VMEM((8, S5_NSTATE), F32), pltpu.VMEM((8, S5_NSTATE), F32),
                        pltpu.VMEM((nslab, bsz * S5_PITCH, LANES), F32),
                        pltpu.VMEM((nslab, bsz * S5_PITCH, LANES), F32)],
        compiler_params=_cparams(("arbitrary",)),
        name="s5_scan",
    )(pm3, bbd_re, bbd_im, ar8, ai8, cbd_re, cbd_im, d_row, w_glu)


def _pad_cols(w, n):
    return jnp.pad(w, ((0, 0), (0, n - w.shape[1])))


def _relayout_kernel(pieces, w_ref, o_ref):
    o0 = 0
    for s0, n, sc in pieces:
        v = w_ref[s0:s0 + n, :]
        o_ref[o0:o0 + n, :] = (v if sc is None else v * sc).astype(o_ref.dtype)
        o0 += n


def _relayout_rows(w_t, pieces, tn=256):
    nin, d = w_t.shape
    nout = sum(n for _, n, _ in pieces)
    return pl.pallas_call(
        functools.partial(_relayout_kernel, pieces),
        grid=(d // tn,),
        in_specs=[pl.BlockSpec((nin, tn), lambda i: (0, i))],
        out_specs=pl.BlockSpec((nout, tn), lambda i: (0, i)),
        out_shape=jax.ShapeDtypeStruct((nout, d), BF16),
        compiler_params=_cparams(("parallel",)),
        name="relayout_rows",
    )(w_t)


def _cd_proj_weights(w_in):
    w_t = w_in.T
    d = w_in.shape[0]
    nw, nkv = NSA_HEADS * NSA_DH, NSA_GROUPS * NSA_DH
    offs = [int(o) for o in np.cumsum((0, nw, nkv, nkv, nkv, nkv, nkv, nkv, 3 * NSA_HEADS, nw, S5_WIDTH, S5_WIDTH))]
    def pair(ia, ib, scale=None):
        return [pc for c in range(0, offs[ia + 1] - offs[ia], NSA_DH)
                for pc in ((offs[ia] + c, NSA_DH, scale), (offs[ib] + c, NSA_DH, None))]
    w_main = _relayout_rows(w_t, pair(0, 8, NSA_DH ** -0.5 * LOG2E) + pair(1, 2) + pair(3, 4) + pair(5, 6)
                            + [(offs[9], S5_WIDTH, None), (offs[10], S5_WIDTH, None)])
    w_gate = w_t[offs[7]:offs[8]].reshape(NSA_GROUPS, NSA_HPG, 3, d).transpose(0, 2, 1, 3)
    w_gate = jnp.pad(w_gate.reshape(NSA_GROUPS, 3 * NSA_HPG, d), ((0, 0), (0, LANES - 3 * NSA_HPG), (0, 0)))
    return w_gate.reshape(NSA_GROUPS * LANES, d).astype(BF16), w_main


def _layer0(h2, p4, layer, bsz, seq, norm_pre, norm_post, pe_proj, pe_gate,
            w_in, ret_norm, conv_w, conv_b, gate_b, ml_norm, w_out):
    d = h2.shape[1]
    g0 = 5120
    after = g0 + 2 * ML_HEADS
    w_t = w_in.T
    w_main = _relayout_rows(w_t, [(0, g0, None), (after, w_in.shape[1] - after, None)])
    w_gate = jnp.pad(w_t[g0:after], ((0, LANES - 2 * ML_HEADS), (0, 0))).astype(BF16)
    pm, pg = _norm_proj(h2, norm_pre.reshape(1, d), w_gate, w_main)
    mix = _ab_mixer(pm, pg, conv_w, conv_b.reshape(1, -1), _pad_cols(gate_b.reshape(1, -1), LANES),
                    ret_norm.reshape(1, -1), ml_norm.reshape(1, -1), bsz, seq)
    ka = RET_HEADS * RET_DV
    nt = seq // 512
    return _out_pe(mix, lambda b, s: (b * nt + s, 0), ka, mix, lambda b, s: (b * nt + s, 1), ML_HEADS * ML_DV,
                   None, None, h2, p4, layer, w_out[:ka].astype(BF16), w_out[ka:].astype(BF16),
                   norm_post.reshape(1, d), pe_gate.astype(BF16), pe_proj.astype(BF16), bsz, seq)


def _layer1(h2, p4, layer, bsz, seq, norm_pre, norm_post, pe_proj, pe_gate,
            w_in, pos_k, pos_v, w1_k, w2_k, w1_v, w2_v,
            a_re, a_im, log_dt, b_re, b_im, c_re, c_im, d_skip, w_glu, w_out):
    d = h2.shape[1]
    nw = NSA_HEADS * NSA_DH
    pm, pg = _norm_proj(h2, norm_pre.reshape(1, d), *_cd_proj_weights(w_in))

    nb = seq // CMP_STRIDE
    pm3 = pm.reshape(bsz, seq, CD_MAIN)
    padl = lambda a, lo=True: jnp.pad(a, ((0, 0),) * (a.ndim - 1)
                                      + ((0, LANES - a.shape[-1]) if lo else (LANES - a.shape[-1], 0),))
    w1 = lambda w, lo: jnp.pad(w.reshape(CMP_LEN, NSA_DH, CMP_HIDDEN),
                               ((0, 0), (0, LANES - NSA_DH) if lo else (LANES - NSA_DH, 0), (0, 0))).astype(BF16)
    ek, al, ekc, ovt = _nsa_tables(seq)
    kcc, vct = _compress(pm3, padl(pos_k), padl(pos_v, False), w1(w1_k, True), w1(w1_v, False),
                         padl(w2_k).astype(BF16), padl(w2_v).T.astype(BF16), ekc)
    if nb < LANES:
        kcc = jnp.pad(kcc, ((0, 0), (0, 0), (0, LANES - nb), (0, 0)))
        vct = jnp.pad(vct, ((0, 0), (0, 0), (0, 0), (0, LANES - nb)))
    y_nsa = _nsa(pm3, pg.reshape(bsz, seq, NSA_GROUPS * LANES), kcc, vct, ek, al, ovt)

    rep = lambda a: jnp.repeat(a, S5_GROUP_CH, axis=0)
    ldt = jnp.broadcast_to(rep(log_dt[:, None]), (S5_WIDTH, S5_STATE))
    bt = lambda b: b.transpose(0, 2, 1).reshape(S5_WIDTH, S5_STATE)
    abr, abi, bbr, bbi = _s5_disc(rep(a_re), rep(a_im), ldt, bt(b_re), bt(b_im))
    gi = np.arange(S5_WIDTH) // S5_GROUP_CH
    gs = np.arange(S5_NSTATE) // S5_STATE
    diag_in = jnp.asarray(gi[:, None] == gs[None, :])
    bd_in = lambda bb: jnp.where(diag_in, jnp.tile(bb, (1, S5_GROUPS)), 0.0).astype(BF16)
    bd_out = lambda c: jnp.where(diag_in.T, jnp.tile(c.transpose(0, 2, 1).reshape(S5_NSTATE, S5_GROUP_CH),
                                                     (1, S5_GROUPS)), 0.0).astype(BF16)
    row8 = lambda a: jnp.broadcast_to(a[::S5_GROUP_CH].reshape(1, S5_NSTATE), (8, S5_NSTATE))
    y_s5 = _s5(pm3, bd_in(bbr), bd_in(bbi), row8(abr), row8(abi),
               bd_out(c_re), bd_out(c_im), d_skip.reshape(1, -1), w_glu.astype(BF16))

    nt = seq // 512
    szb = _SZ0 // S5_WIDTH
    return _out_pe(y_nsa.reshape(bsz * seq, nw), lambda b, s: (b * nt + s, 0), nw,
                   y_s5.reshape(bsz * seq, S5_WIDTH), lambda b, s: (b * nt + s, 0), S5_WIDTH,
                   pm, lambda b, s: (b * nt + s, szb),
                   h2, p4, layer, w_out[:nw].astype(BF16), w_out[nw:].astype(BF16),
                   norm_post.reshape(1, d), pe_gate.astype(BF16), pe_proj.astype(BF16), bsz, seq)


def kernel(x, p, norm_pre, norm_post, pe_proj, pe_gate, ab_w_in, ret_norm, ml_conv_w, ml_conv_b, ml_gate_b,
           ml_norm, ab_w_out, cd_w_in, cmp_pos_k, cmp_pos_v, cmp_w1_k, cmp_w2_k, cmp_w1_v, cmp_w2_v,
           s5_a_re, s5_a_im, s5_log_dt, s5_b_re, s5_b_im, s5_c_re, s5_c_im, s5_d, s5_w_glu, cd_w_out):
    bsz, seq, d = x.shape
    assert bsz == 8, "the S5 scan maps the batch onto the 8 sublanes of a vreg"
    assert seq % max(KCH, 1024) == 0
    depth = p.shape[0]
    h2 = x.reshape(bsz * seq, d)
    for i in range(depth):
        j = i // 2
        if i % 2 == 0:
            h2 = _layer0(h2, p, i, bsz, seq, norm_pre[i], norm_post[i], pe_proj[i], pe_gate[i],
                         ab_w_in[j], ret_norm[j], ml_conv_w[j], ml_conv_b[j], ml_gate_b[j], ml_norm[j], ab_w_out[j])
        else:
            h2 = _layer1(h2, p, i, bsz, seq, norm_pre[i], norm_post[i], pe_proj[i], pe_gate[i],
                         cd_w_in[j], cmp_pos_k[j], cmp_pos_v[j], cmp_w1_k[j], cmp_w2_k[j], cmp_w1_v[j], cmp_w2_v[j],
                         s5_a_re[j], s5_a_im[j], s5_log_dt[j], s5_b_re[j], s5_b_im[j], s5_c_re[j], s5_c_im[j],
                         s5_d[j], s5_w_glu[j], cd_w_out[j])
    return h2.reshape(bsz, seq, d)
```
